```python
import jax, jax.numpy as jnp
from jax import lax
import numpy as np

D_MODEL = 1024
BATCH = 8
SEQ = 8192
DEPTH = 4

N_MIXERS = 3
CONF_CONV_WIDTH = 31
POOL_WINDOWS = (2, 4, 8, 16)
POOL_GROUPS = len(POOL_WINDOWS)
POOL_GROUP_DIM = D_MODEL // POOL_GROUPS
SHORT_CONV_WIDTH = 3
FFN_CONV_WIDTH = 3
D_FF = 2816
RMS_EPS = 1e-6
LN_EPS = 1e-5
N_A = (DEPTH + N_MIXERS - 1) // N_MIXERS
N_B = (DEPTH + N_MIXERS - 2) // N_MIXERS
N_C = (DEPTH + N_MIXERS - 3) // N_MIXERS

kernel_name = "hybrid_interleaved_conv_pool_shortconv_trunk"


def rmsnorm(x, g):
    x32 = x.astype(jnp.float32)
    y = x32 * lax.rsqrt(jnp.mean(x32 * x32, axis=-1, keepdims=True) + RMS_EPS)
    return y.astype(x.dtype) * g


def layernorm(x, g, b):
    x32 = x.astype(jnp.float32)
    mu = jnp.mean(x32, axis=-1, keepdims=True)
    var = jnp.mean(jnp.square(x32 - mu), axis=-1, keepdims=True)
    y = (x32 - mu) * lax.rsqrt(var + LN_EPS)
    return y.astype(x.dtype) * g + b


def causal_dwconv(x, w):
    k, ch = w.shape
    return lax.conv_general_dilated(
        x, w[:, None, :].astype(x.dtype), window_strides=(1,), padding=[(k - 1, 0)],
        dimension_numbers=("NWC", "WIO", "NWC"), feature_group_count=ch)


def conformer_conv_module(h, w1, b1, dw_w, dw_b, ln_g, ln_b, w2, b2):
    u = h @ w1 + b1
    a, gt = jnp.split(u, 2, axis=-1)
    u = a * jax.nn.sigmoid(gt)
    u = causal_dwconv(u, dw_w) + dw_b
    u = jax.nn.silu(layernorm(u, ln_g, ln_b))
    return u @ w2 + b2


def pooling_mixer(h, group_w, group_b, ch_scale):
    bsz, t_len, d = h.shape
    h32 = h.astype(jnp.float32)
    cs = jnp.cumsum(h32, axis=1)
    pos = jnp.arange(1, t_len + 1, dtype=jnp.float32)[None, :, None]
    means = []
    for g, w in enumerate(POOL_WINDOWS):
        csg = cs[..., g * POOL_GROUP_DIM:(g + 1) * POOL_GROUP_DIM]
        prev = jnp.pad(csg, ((0, 0), (w, 0), (0, 0)))[:, :t_len]
        means.append((csg - prev) / jnp.minimum(pos, float(w)))
    pooled = (jnp.concatenate(means, axis=-1) - h32).astype(h.dtype)
    pooled = pooled.reshape(bsz, t_len, POOL_GROUPS, POOL_GROUP_DIM)
    mixed = jnp.einsum("btgc,gcd->btgd", pooled, group_w).reshape(bsz, t_len, d)
    return (mixed + group_b) * ch_scale


def short_conv_mixer(h, w_in, w_conv, w_out):
    bcx = h @ w_in
    gb, gc, v = jnp.split(bcx, 3, axis=-1)
    return (gb * causal_dwconv(gc * v, w_conv)) @ w_out


def conv_ffn(h, w_up, dw_w, dw_b, w_down):
    u = causal_dwconv(h @ w_up, dw_w) + dw_b
    gt, v = jnp.split(u, 2, axis=-1)
    return (jax.nn.silu(gt) * v) @ w_down


def _fwd_setup_inputs(seed: int = 0) -> dict:
    key = jax.random.key(seed)
    ks = iter(jax.random.split(key, 40))
    d, f = D_MODEL, D_FF

    def nrm(shape, scale):
        return scale * jax.random.normal(next(ks), shape, jnp.float32)

    def gain(shape):
        return 1.0 + nrm(shape, 0.05)

    return {
        "x": nrm((BATCH, SEQ, d), 1.0),
        "c": nrm((BATCH, d), 1.0),
        "mod_w": nrm((DEPTH, d, 6 * d), 0.5 * d ** -0.5),
        "mod_b": nrm((DEPTH, 6 * d), 0.02),
        "norm_pre_mix": gain((DEPTH, d)),
        "norm_post_mix": gain((DEPTH, d)),
        "norm_pre_ffn": gain((DEPTH, d)),
        "norm_post_ffn": gain((DEPTH, d)),
        "a_pw1_w": nrm((N_A, d, 2 * d), d ** -0.5),
        "a_pw1_b": nrm((N_A, 2 * d), 0.02),
        "a_dw_w": nrm((N_A, CONF_CONV_WIDTH, d), CONF_CONV_WIDTH ** -0.5),
        "a_dw_b": nrm((N_A, d), 0.02),
        "a_ln_g": gain((N_A, d)),
        "a_ln_b": nrm((N_A, d), 0.02),
        "a_pw2_w": nrm((N_A, d, d), d ** -0.5),
        "a_pw2_b": nrm((N_A, d), 0.02),
        "b_group_w": nrm((N_B, POOL_GROUPS, POOL_GROUP_DIM, POOL_GROUP_DIM), POOL_GROUP_DIM ** -0.5),
        "b_group_b": nrm((N_B, d), 0.02),
        "b_scale": gain((N_B, d)),
        "c_in_w": nrm((N_C, d, 3 * d), d ** -0.5),
        "c_conv_w": nrm((N_C, SHORT_CONV_WIDTH, d), SHORT_CONV_WIDTH ** -0.5),
        "c_out_w": nrm((N_C, d, d), d ** -0.5),
        "f_up_w": nrm((DEPTH, d, 2 * f), d ** -0.5),
        "f_dw_w": nrm((DEPTH, FFN_CONV_WIDTH, 2 * f), FFN_CONV_WIDTH ** -0.5),
        "f_dw_b": nrm((DEPTH, 2 * f), 0.02),
        "f_down_w": nrm((DEPTH, f, d), f ** -0.5),
    }


def _fwd_reference(x, c, mod_w, mod_b, norm_pre_mix, norm_post_mix, norm_pre_ffn, norm_post_ffn,
              a_pw1_w, a_pw1_b, a_dw_w, a_dw_b, a_ln_g, a_ln_b, a_pw2_w, a_pw2_b,
              b_group_w, b_group_b, b_scale,
              c_in_w, c_conv_w, c_out_w,
              f_up_w, f_dw_w, f_dw_b, f_down_w):
    c_act = jax.nn.silu(c)
    for i in range(DEPTH):
        mod = c_act @ mod_w[i] + mod_b[i]
        sh_m, sc_m, gt_m, sh_f, sc_f, gt_f = [m[:, None, :] for m in jnp.split(mod, 6, axis=-1)]

        h = rmsnorm(x, norm_pre_mix[i]) * (1.0 + sc_m) + sh_m
        kind, slot = i % N_MIXERS, i // N_MIXERS
        if kind == 0:
            y = conformer_conv_module(h, a_pw1_w[slot], a_pw1_b[slot], a_dw_w[slot], a_dw_b[slot],
                                      a_ln_g[slot], a_ln_b[slot], a_pw2_w[slot], a_pw2_b[slot])
        elif kind == 1:
            y = pooling_mixer(h, b_group_w[slot], b_group_b[slot], b_scale[slot])
        else:
            y = short_conv_mixer(h, c_in_w[slot], c_conv_w[slot], c_out_w[slot])
        x = x + gt_m * rmsnorm(y, norm_post_mix[i])

        h = rmsnorm(x, norm_pre_ffn[i]) * (1.0 + sc_f) + sh_f
        y = conv_ffn(h, f_up_w[i], f_dw_w[i], f_dw_b[i], f_down_w[i])
        x = x + gt_f * rmsnorm(y, norm_post_ffn[i])
    return x


import jax as _jax
import jax.numpy as _jnp

TWIN_FORMAT = 'train_step'
FWD_PARAMS = ['x', 'c', 'mod_w', 'mod_b', 'norm_pre_mix', 'norm_post_mix', 'norm_pre_ffn', 'norm_post_ffn', 'a_pw1_w', 'a_pw1_b', 'a_dw_w', 'a_dw_b', 'a_ln_g', 'a_ln_b', 'a_pw2_w', 'a_pw2_b', 'b_group_w', 'b_group_b', 'b_scale', 'c_in_w', 'c_conv_w', 'c_out_w', 'f_up_w', 'f_dw_w', 'f_dw_b', 'f_down_w']
TWIN_WEIGHTS = ['mod_w', 'mod_b', 'norm_pre_mix', 'norm_post_mix', 'norm_pre_ffn', 'norm_post_ffn', 'a_pw1_w', 'a_pw1_b', 'a_dw_w', 'a_dw_b', 'a_ln_g', 'a_ln_b', 'a_pw2_w', 'a_pw2_b', 'b_group_w', 'b_group_b', 'b_scale', 'c_in_w', 'c_conv_w', 'c_out_w', 'f_up_w', 'f_dw_w', 'f_dw_b', 'f_down_w']
TWIN_DIFF_INPUT = 'x'
TWIN_INPUTS = ['x', 'c', 'mod_w', 'mod_b', 'norm_pre_mix', 'norm_post_mix', 'norm_pre_ffn', 'norm_post_ffn', 'a_pw1_w', 'a_pw1_b', 'a_dw_w', 'a_dw_b', 'a_ln_g', 'a_ln_b', 'a_pw2_w', 'a_pw2_b', 'b_group_w', 'b_group_b', 'b_scale', 'c_in_w', 'c_conv_w', 'c_out_w', 'f_up_w', 'f_dw_w', 'f_dw_b', 'f_down_w', 'loss_target', 'm_mod_w', 'm_mod_b', 'm_norm_pre_mix', 'm_norm_post_mix', 'm_norm_pre_ffn', 'm_norm_post_ffn', 'm_a_pw1_w', 'm_a_pw1_b', 'm_a_dw_w', 'm_a_dw_b', 'm_a_ln_g', 'm_a_ln_b', 'm_a_pw2_w', 'm_a_pw2_b', 'm_b_group_w', 'm_b_group_b', 'm_b_scale', 'm_c_in_w', 'm_c_conv_w', 'm_c_out_w', 'm_f_up_w', 'm_f_dw_w', 'm_f_dw_b', 'm_f_down_w', 'v_mod_w', 'v_mod_b', 'v_norm_pre_mix', 'v_norm_post_mix', 'v_norm_pre_ffn', 'v_norm_post_ffn', 'v_a_pw1_w', 'v_a_pw1_b', 'v_a_dw_w', 'v_a_dw_b', 'v_a_ln_g', 'v_a_ln_b', 'v_a_pw2_w', 'v_a_pw2_b', 'v_b_group_w', 'v_b_group_b', 'v_b_scale', 'v_c_in_w', 'v_c_conv_w', 'v_c_out_w', 'v_f_up_w', 'v_f_dw_w', 'v_f_dw_b', 'v_f_down_w']
TWIN_OUTPUTS = ['loss', 'grad_x', 'grad_mod_w', 'grad_mod_b', 'grad_norm_pre_mix', 'grad_norm_post_mix', 'grad_norm_pre_ffn', 'grad_norm_post_ffn', 'grad_a_pw1_w', 'grad_a_pw1_b', 'grad_a_dw_w', 'grad_a_dw_b', 'grad_a_ln_g', 'grad_a_ln_b', 'grad_a_pw2_w', 'grad_a_pw2_b', 'grad_b_group_w', 'grad_b_group_b', 'grad_b_scale', 'grad_c_in_w', 'grad_c_conv_w', 'grad_c_out_w', 'grad_f_up_w', 'grad_f_dw_w', 'grad_f_dw_b', 'grad_f_down_w', 'delta_mod_w', 'delta_mod_b', 'delta_norm_pre_mix', 'delta_norm_post_mix', 'delta_norm_pre_ffn', 'delta_norm_post_ffn', 'delta_a_pw1_w', 'delta_a_pw1_b', 'delta_a_dw_w', 'delta_a_dw_b', 'delta_a_ln_g', 'delta_a_ln_b', 'delta_a_pw2_w', 'delta_a_pw2_b', 'delta_b_group_w', 'delta_b_group_b', 'delta_b_scale', 'delta_c_in_w', 'delta_c_conv_w', 'delta_c_out_w', 'delta_f_up_w', 'delta_f_dw_w', 'delta_f_dw_b', 'delta_f_down_w', 'new_m_mod_w', 'new_m_mod_b', 'new_m_norm_pre_mix', 'new_m_norm_post_mix', 'new_m_norm_pre_ffn', 'new_m_norm_post_ffn', 'new_m_a_pw1_w', 'new_m_a_pw1_b', 'new_m_a_dw_w', 'new_m_a_dw_b', 'new_m_a_ln_g', 'new_m_a_ln_b', 'new_m_a_pw2_w', 'new_m_a_pw2_b', 'new_m_b_group_w', 'new_m_b_group_b', 'new_m_b_scale', 'new_m_c_in_w', 'new_m_c_conv_w', 'new_m_c_out_w', 'new_m_f_up_w', 'new_m_f_dw_w', 'new_m_f_dw_b', 'new_m_f_down_w', 'new_v_mod_w', 'new_v_mod_b', 'new_v_norm_pre_mix', 'new_v_norm_post_mix', 'new_v_norm_pre_ffn', 'new_v_norm_post_ffn', 'new_v_a_pw1_w', 'new_v_a_pw1_b', 'new_v_a_dw_w', 'new_v_a_dw_b', 'new_v_a_ln_g', 'new_v_a_ln_b', 'new_v_a_pw2_w', 'new_v_a_pw2_b', 'new_v_b_group_w', 'new_v_b_group_b', 'new_v_b_scale', 'new_v_c_in_w', 'new_v_c_conv_w', 'new_v_c_out_w', 'new_v_f_up_w', 'new_v_f_dw_w', 'new_v_f_dw_b', 'new_v_f_down_w']
TWIN_LEAF_KINDS = {'loss': 'loss', 'grad_x': 'grad_x', 'grad_mod_w': 'grad_w', 'grad_mod_b': 'grad_w', 'grad_norm_pre_mix': 'grad_w', 'grad_norm_post_mix': 'grad_w', 'grad_norm_pre_ffn': 'grad_w', 'grad_norm_post_ffn': 'grad_w', 'grad_a_pw1_w': 'grad_w', 'grad_a_pw1_b': 'grad_w', 'grad_a_dw_w': 'grad_w', 'grad_a_dw_b': 'grad_w', 'grad_a_ln_g': 'grad_w', 'grad_a_ln_b': 'grad_w', 'grad_a_pw2_w': 'grad_w', 'grad_a_pw2_b': 'grad_w', 'grad_b_group_w': 'grad_w', 'grad_b_group_b': 'grad_w', 'grad_b_scale': 'grad_w', 'grad_c_in_w': 'grad_w', 'grad_c_conv_w': 'grad_w', 'grad_c_out_w': 'grad_w', 'grad_f_up_w': 'grad_w', 'grad_f_dw_w': 'grad_w', 'grad_f_dw_b': 'grad_w', 'grad_f_down_w': 'grad_w', 'delta_mod_w': 'delta_w', 'delta_mod_b': 'delta_w', 'delta_norm_pre_mix': 'delta_w', 'delta_norm_post_mix': 'delta_w', 'delta_norm_pre_ffn': 'delta_w', 'delta_norm_post_ffn': 'delta_w', 'delta_a_pw1_w': 'delta_w', 'delta_a_pw1_b': 'delta_w', 'delta_a_dw_w': 'delta_w', 'delta_a_dw_b': 'delta_w', 'delta_a_ln_g': 'delta_w', 'delta_a_ln_b': 'delta_w', 'delta_a_pw2_w': 'delta_w', 'delta_a_pw2_b': 'delta_w', 'delta_b_group_w': 'delta_w', 'delta_b_group_b': 'delta_w', 'delta_b_scale': 'delta_w', 'delta_c_in_w': 'delta_w', 'delta_c_conv_w': 'delta_w', 'delta_c_out_w': 'delta_w', 'delta_f_up_w': 'delta_w', 'delta_f_dw_w': 'delta_w', 'delta_f_dw_b': 'delta_w', 'delta_f_down_w': 'delta_w', 'new_m_mod_w': 'new_m', 'new_m_mod_b': 'new_m', 'new_m_norm_pre_mix': 'new_m', 'new_m_norm_post_mix': 'new_m', 'new_m_norm_pre_ffn': 'new_m', 'new_m_norm_post_ffn': 'new_m', 'new_m_a_pw1_w': 'new_m', 'new_m_a_pw1_b': 'new_m', 'new_m_a_dw_w': 'new_m', 'new_m_a_dw_b': 'new_m', 'new_m_a_ln_g': 'new_m', 'new_m_a_ln_b': 'new_m', 'new_m_a_pw2_w': 'new_m', 'new_m_a_pw2_b': 'new_m', 'new_m_b_group_w': 'new_m', 'new_m_b_group_b': 'new_m', 'new_m_b_scale': 'new_m', 'new_m_c_in_w': 'new_m', 'new_m_c_conv_w': 'new_m', 'new_m_c_out_w': 'new_m', 'new_m_f_up_w': 'new_m', 'new_m_f_dw_w': 'new_m', 'new_m_f_dw_b': 'new_m', 'new_m_f_down_w': 'new_m', 'new_v_mod_w': 'new_v', 'new_v_mod_b': 'new_v', 'new_v_norm_pre_mix': 'new_v', 'new_v_norm_post_mix': 'new_v', 'new_v_norm_pre_ffn': 'new_v', 'new_v_norm_post_ffn': 'new_v', 'new_v_a_pw1_w': 'new_v', 'new_v_a_pw1_b': 'new_v', 'new_v_a_dw_w': 'new_v', 'new_v_a_dw_b': 'new_v', 'new_v_a_ln_g': 'new_v', 'new_v_a_ln_b': 'new_v', 'new_v_a_pw2_w': 'new_v', 'new_v_a_pw2_b': 'new_v', 'new_v_b_group_w': 'new_v', 'new_v_b_group_b': 'new_v', 'new_v_b_scale': 'new_v', 'new_v_c_in_w': 'new_v', 'new_v_c_conv_w': 'new_v', 'new_v_c_out_w': 'new_v', 'new_v_f_up_w': 'new_v', 'new_v_f_dw_w': 'new_v', 'new_v_f_dw_b': 'new_v', 'new_v_f_down_w': 'new_v'}


def _forward(args):
    return _fwd_reference(*[args[k] for k in FWD_PARAMS])


def _output_shape():
    def fwd():
        inp = _fwd_setup_inputs(0)
        return _fwd_reference(*[inp[k] for k in FWD_PARAMS])
    out = _jax.eval_shape(fwd)
    return out.shape, out.dtype

N_MICROBATCH = 1
ADAM_LR = 0.001
ADAM_B1 = 0.9
ADAM_B2 = 0.999
ADAM_EPS = 1e-08
ADAM_WD = 0.01
ADAM_STEP = 10
PER_EXAMPLE_BATCH_AXIS = {'x': 0, 'c': 0, 'loss_target': 0}
SHARED_INPUTS = []
_WEIGHT_DTYPES = {'mod_w': _jnp.float32, 'mod_b': _jnp.float32, 'norm_pre_mix': _jnp.float32, 'norm_post_mix': _jnp.float32, 'norm_pre_ffn': _jnp.float32, 'norm_post_ffn': _jnp.float32, 'a_pw1_w': _jnp.float32, 'a_pw1_b': _jnp.float32, 'a_dw_w': _jnp.float32, 'a_dw_b': _jnp.float32, 'a_ln_g': _jnp.float32, 'a_ln_b': _jnp.float32, 'a_pw2_w': _jnp.float32, 'a_pw2_b': _jnp.float32, 'b_group_w': _jnp.float32, 'b_group_b': _jnp.float32, 'b_scale': _jnp.float32, 'c_in_w': _jnp.float32, 'c_conv_w': _jnp.float32, 'c_out_w': _jnp.float32, 'f_up_w': _jnp.float32, 'f_dw_w': _jnp.float32, 'f_dw_b': _jnp.float32, 'f_down_w': _jnp.float32}
MOMENT_SCALE = {'mod_w': 2.435112e+00, 'mod_b': 5.303421e+00, 'norm_pre_mix': 2.786402e-01, 'norm_post_mix': 6.713939e+00, 'norm_pre_ffn': 2.231233e-01, 'norm_post_ffn': 6.648940e+00, 'a_pw1_w': 2.155067e-01, 'a_pw1_b': 8.657556e-01, 'a_dw_w': 2.989905e-01, 'a_dw_b': 1.943051e+00, 'a_ln_g': 8.650631e-01, 'a_ln_b': 1.287032e+00, 'a_pw2_w': 5.443599e-01, 'a_pw2_b': 2.620829e+00, 'b_group_w': 4.215510e-01, 'b_group_b': 7.897976e-01, 'b_scale': 2.925893e+00, 'c_in_w': 1.505646e-01, 'c_conv_w': 1.604539e-01, 'c_out_w': 1.639212e-01, 'f_up_w': 1.114132e-01, 'f_dw_w': 1.204596e-01, 'f_dw_b': 2.250521e-01, 'f_down_w': 2.108845e-01}


def _to_microbatches(a, axis):
    t = _jnp.moveaxis(a, axis, 0)
    t = t.reshape((N_MICROBATCH, t.shape[0] // N_MICROBATCH) + t.shape[1:])
    return _jnp.moveaxis(t, 1, axis + 1)


def setup_inputs(seed: int = 0) -> dict:
    inp = _fwd_setup_inputs(seed)
    key = _jax.random.fold_in(_jax.random.key(seed), 7919)
    shape, _ = _output_shape()
    out = dict(inp)
    out["loss_target"] = _jax.random.normal(_jax.random.fold_in(key, 0), shape, _jnp.float32)
    for i, name in enumerate(TWIN_WEIGHTS):
        w = inp[name].astype(_jnp.float32)
        if MOMENT_SCALE is None:
            s = _jnp.sqrt(_jnp.mean(_jnp.square(w)) + 1e-30)
        else:
            s = MOMENT_SCALE[name]
        km, kv = _jax.random.split(_jax.random.fold_in(key, i + 1))
        out[name] = w
        out["m_" + name] = s * _jax.random.normal(km, w.shape, _jnp.float32)
        out["v_" + name] = (s * s) * _jax.random.uniform(kv, w.shape, _jnp.float32, 0.5, 1.5)
    if N_MICROBATCH > 1:
        for name, axis in PER_EXAMPLE_BATCH_AXIS.items():
            out[name] = _to_microbatches(out[name], axis)
    return {'x': out['x'], 'c': out['c'], 'mod_w': out['mod_w'], 'mod_b': out['mod_b'], 'norm_pre_mix': out['norm_pre_mix'], 'norm_post_mix': out['norm_post_mix'], 'norm_pre_ffn': out['norm_pre_ffn'], 'norm_post_ffn': out['norm_post_ffn'], 'a_pw1_w': out['a_pw1_w'], 'a_pw1_b': out['a_pw1_b'], 'a_dw_w': out['a_dw_w'], 'a_dw_b': out['a_dw_b'], 'a_ln_g': out['a_ln_g'], 'a_ln_b': out['a_ln_b'], 'a_pw2_w': out['a_pw2_w'], 'a_pw2_b': out['a_pw2_b'], 'b_group_w': out['b_group_w'], 'b_group_b': out['b_group_b'], 'b_scale': out['b_scale'], 'c_in_w': out['c_in_w'], 'c_conv_w': out['c_conv_w'], 'c_out_w': out['c_out_w'], 'f_up_w': out['f_up_w'], 'f_dw_w': out['f_dw_w'], 'f_dw_b': out['f_dw_b'], 'f_down_w': out['f_down_w'], 'loss_target': out['loss_target'], 'm_mod_w': out['m_mod_w'], 'm_mod_b': out['m_mod_b'], 'm_norm_pre_mix': out['m_norm_pre_mix'], 'm_norm_post_mix': out['m_norm_post_mix'], 'm_norm_pre_ffn': out['m_norm_pre_ffn'], 'm_norm_post_ffn': out['m_norm_post_ffn'], 'm_a_pw1_w': out['m_a_pw1_w'], 'm_a_pw1_b': out['m_a_pw1_b'], 'm_a_dw_w': out['m_a_dw_w'], 'm_a_dw_b': out['m_a_dw_b'], 'm_a_ln_g': out['m_a_ln_g'], 'm_a_ln_b': out['m_a_ln_b'], 'm_a_pw2_w': out['m_a_pw2_w'], 'm_a_pw2_b': out['m_a_pw2_b'], 'm_b_group_w': out['m_b_group_w'], 'm_b_group_b': out['m_b_group_b'], 'm_b_scale': out['m_b_scale'], 'm_c_in_w': out['m_c_in_w'], 'm_c_conv_w': out['m_c_conv_w'], 'm_c_out_w': out['m_c_out_w'], 'm_f_up_w': out['m_f_up_w'], 'm_f_dw_w': out['m_f_dw_w'], 'm_f_dw_b': out['m_f_dw_b'], 'm_f_down_w': out['m_f_down_w'], 'v_mod_w': out['v_mod_w'], 'v_mod_b': out['v_mod_b'], 'v_norm_pre_mix': out['v_norm_pre_mix'], 'v_norm_post_mix': out['v_norm_post_mix'], 'v_norm_pre_ffn': out['v_norm_pre_ffn'], 'v_norm_post_ffn': out['v_norm_post_ffn'], 'v_a_pw1_w': out['v_a_pw1_w'], 'v_a_pw1_b': out['v_a_pw1_b'], 'v_a_dw_w': out['v_a_dw_w'], 'v_a_dw_b': out['v_a_dw_b'], 'v_a_ln_g': out['v_a_ln_g'], 'v_a_ln_b': out['v_a_ln_b'], 'v_a_pw2_w': out['v_a_pw2_w'], 'v_a_pw2_b': out['v_a_pw2_b'], 'v_b_group_w': out['v_b_group_w'], 'v_b_group_b': out['v_b_group_b'], 'v_b_scale': out['v_b_scale'], 'v_c_in_w': out['v_c_in_w'], 'v_c_conv_w': out['v_c_conv_w'], 'v_c_out_w': out['v_c_out_w'], 'v_f_up_w': out['v_f_up_w'], 'v_f_dw_w': out['v_f_dw_w'], 'v_f_dw_b': out['v_f_dw_b'], 'v_f_down_w': out['v_f_down_w']}


def _loss(weights, diff, rest, loss_target):
    with _jax.named_scope("forward"):
        args = {**rest, TWIN_DIFF_INPUT: diff, **{k: w.astype(_WEIGHT_DTYPES[k]) for k, w in weights.items()}}
        y = _forward(args)
    with _jax.named_scope("loss_head"):
        err = _jnp.square(y.astype(_jnp.float32) - loss_target)
        return 0.5 * _jnp.sum(_jnp.mean(err, axis=-1)) if err.ndim else 0.5 * err


def _adamw(w, g, m, v):
    m = ADAM_B1 * m + (1.0 - ADAM_B1) * g
    v = ADAM_B2 * v + (1.0 - ADAM_B2) * _jnp.square(g)
    m_hat = m / (1.0 - ADAM_B1 ** ADAM_STEP)
    v_hat = v / (1.0 - ADAM_B2 ** ADAM_STEP)
    delta = -ADAM_LR * (m_hat / (_jnp.sqrt(v_hat) + ADAM_EPS) + ADAM_WD * w)
    return delta, m, v


def reference(x, c, mod_w, mod_b, norm_pre_mix, norm_post_mix, norm_pre_ffn, norm_post_ffn, a_pw1_w, a_pw1_b, a_dw_w, a_dw_b, a_ln_g, a_ln_b, a_pw2_w, a_pw2_b, b_group_w, b_group_b, b_scale, c_in_w, c_conv_w, c_out_w, f_up_w, f_dw_w, f_dw_b, f_down_w, loss_target, m_mod_w, m_mod_b, m_norm_pre_mix, m_norm_post_mix, m_norm_pre_ffn, m_norm_post_ffn, m_a_pw1_w, m_a_pw1_b, m_a_dw_w, m_a_dw_b, m_a_ln_g, m_a_ln_b, m_a_pw2_w, m_a_pw2_b, m_b_group_w, m_b_group_b, m_b_scale, m_c_in_w, m_c_conv_w, m_c_out_w, m_f_up_w, m_f_dw_w, m_f_dw_b, m_f_down_w, v_mod_w, v_mod_b, v_norm_pre_mix, v_norm_post_mix, v_norm_pre_ffn, v_norm_post_ffn, v_a_pw1_w, v_a_pw1_b, v_a_dw_w, v_a_dw_b, v_a_ln_g, v_a_ln_b, v_a_pw2_w, v_a_pw2_b, v_b_group_w, v_b_group_b, v_b_scale, v_c_in_w, v_c_conv_w, v_c_out_w, v_f_up_w, v_f_dw_w, v_f_dw_b, v_f_down_w):
    given = dict(x=x, c=c, mod_w=mod_w, mod_b=mod_b, norm_pre_mix=norm_pre_mix, norm_post_mix=norm_post_mix, norm_pre_ffn=norm_pre_ffn, norm_post_ffn=norm_post_ffn, a_pw1_w=a_pw1_w, a_pw1_b=a_pw1_b, a_dw_w=a_dw_w, a_dw_b=a_dw_b, a_ln_g=a_ln_g, a_ln_b=a_ln_b, a_pw2_w=a_pw2_w, a_pw2_b=a_pw2_b, b_group_w=b_group_w, b_group_b=b_group_b, b_scale=b_scale, c_in_w=c_in_w, c_conv_w=c_conv_w, c_out_w=c_out_w, f_up_w=f_up_w, f_dw_w=f_dw_w, f_dw_b=f_dw_b, f_down_w=f_down_w, loss_target=loss_target, m_mod_w=m_mod_w, m_mod_b=m_mod_b, m_norm_pre_mix=m_norm_pre_mix, m_norm_post_mix=m_norm_post_mix, m_norm_pre_ffn=m_norm_pre_ffn, m_norm_post_ffn=m_norm_post_ffn, m_a_pw1_w=m_a_pw1_w, m_a_pw1_b=m_a_pw1_b, m_a_dw_w=m_a_dw_w, m_a_dw_b=m_a_dw_b, m_a_ln_g=m_a_ln_g, m_a_ln_b=m_a_ln_b, m_a_pw2_w=m_a_pw2_w, m_a_pw2_b=m_a_pw2_b, m_b_group_w=m_b_group_w, m_b_group_b=m_b_group_b, m_b_scale=m_b_scale, m_c_in_w=m_c_in_w, m_c_conv_w=m_c_conv_w, m_c_out_w=m_c_out_w, m_f_up_w=m_f_up_w, m_f_dw_w=m_f_dw_w, m_f_dw_b=m_f_dw_b, m_f_down_w=m_f_down_w, v_mod_w=v_mod_w, v_mod_b=v_mod_b, v_norm_pre_mix=v_norm_pre_mix, v_norm_post_mix=v_norm_post_mix, v_norm_pre_ffn=v_norm_pre_ffn, v_norm_post_ffn=v_norm_post_ffn, v_a_pw1_w=v_a_pw1_w, v_a_pw1_b=v_a_pw1_b, v_a_dw_w=v_a_dw_w, v_a_dw_b=v_a_dw_b, v_a_ln_g=v_a_ln_g, v_a_ln_b=v_a_ln_b, v_a_pw2_w=v_a_pw2_w, v_a_pw2_b=v_a_pw2_b, v_b_group_w=v_b_group_w, v_b_group_b=v_b_group_b, v_b_scale=v_b_scale, v_c_in_w=v_c_in_w, v_c_conv_w=v_c_conv_w, v_c_out_w=v_c_out_w, v_f_up_w=v_f_up_w, v_f_dw_w=v_f_dw_w, v_f_dw_b=v_f_dw_b, v_f_down_w=v_f_down_w)
    weights = {n: given[n] for n in TWIN_WEIGHTS}
    shared = {n: given[n] for n in SHARED_INPUTS}
    per_example = {n: given[n] for n in ['x', 'c']}
    grad_fn = _jax.value_and_grad(_loss, argnums=(0, 1))

    def one_microbatch(ex, loss_target):
        ex = dict(ex)
        diff = ex.pop(TWIN_DIFF_INPUT)
        return grad_fn(weights, diff, {**shared, **ex}, loss_target)

    if N_MICROBATCH == 1:
        loss, (grad_w, grad_x) = one_microbatch(per_example, given["loss_target"])
    else:
        def body(carry, xs):
            loss_sum, grad_sum = carry
            l_k, (gw_k, gx_k) = one_microbatch(xs[0], xs[1])
            with _jax.named_scope("update"):
                return (loss_sum + l_k, _jax.tree.map(_jnp.add, grad_sum, gw_k)), gx_k

        init = (_jnp.zeros((), _jnp.float32), _jax.tree.map(_jnp.zeros_like, weights))
        (loss, grad_w), grad_x = _jax.lax.scan(body, init, (per_example, given["loss_target"]))
    with _jax.named_scope("update"):
        delta_w, new_m, new_v = {}, {}, {}
        for n in TWIN_WEIGHTS:
            delta_w[n], new_m[n], new_v[n] = _adamw(weights[n], grad_w[n], given["m_" + n], given["v_" + n])
    return (loss, grad_x, *[grad_w[n] for n in TWIN_WEIGHTS], *[delta_w[n] for n in TWIN_WEIGHTS],
            *[new_m[n] for n in TWIN_WEIGHTS], *[new_v[n] for n in TWIN_WEIGHTS])
```

```python
import functools

import jax
import jax.numpy as jnp
from jax import lax
from jax.experimental import pallas as pl
from jax.experimental.pallas import tpu as pltpu

F32 = jnp.float32
BF16 = jnp.bfloat16

DEPTH = 4
N_MIXERS = 3
CONF_CONV_WIDTH = 31
POOL_WINDOWS = (2, 4, 8, 16)
RMS_EPS = 1e-6
LN_EPS = 1e-5
ADAM_LR = 0.001
ADAM_B1 = 0.9
ADAM_B2 = 0.999
ADAM_EPS = 1e-08
ADAM_WD = 0.01
ADAM_STEP = 10

TM_ROW = 512
TM_CONV = 128
TK_TOKENS = 512
SUBLANES = 8
LANES = 128
NSHARD = 4
NDEV = 8
HALO_A = 32
HALO_POOL = 16
HALO_3 = 8
VMEM_LIMIT = 56 * 1024 * 1024

_PAR = "parallel"
_ARB = "arbitrary"


def _cparams(*sem):
    return pltpu.CompilerParams(dimension_semantics=sem, vmem_limit_bytes=VMEM_LIMIT)


def _pick(n, prefs):
    for p in prefs:
        if p <= n and n % p == 0:
            return p
    return n


def _row_spec(tm, width):
    return pl.BlockSpec((tm, width), lambda i: (i, 0))


def _vec_spec(rows, width):
    return pl.BlockSpec((rows, width), lambda i: (0, 0))


def _prev_spec(tm, hb, width):
    return pl.BlockSpec((hb, width), lambda i: (jnp.maximum(i * (tm // hb) - 1, 0), 0))


def _next_spec(tm, hb, width, total):
    last = total // hb - 1
    return pl.BlockSpec((hb, width), lambda i: (jnp.minimum((i + 1) * (tm // hb), last), 0))


def _sum8(v):
    r, c = v.shape
    return jnp.sum(v.reshape(r // SUBLANES, SUBLANES, c), axis=0)


def _rms(x):
    r = lax.rsqrt(jnp.mean(x * x, axis=-1, keepdims=True) + RMS_EPS)
    return x * r, r


def _rms_bwd(dy, xn, r):
    return r * (dy - xn * jnp.mean(dy * xn, axis=-1, keepdims=True))


def _sigmoid(x):
    return 1.0 / (1.0 + jnp.exp(-x))


_DIMS = {"nn": ((1,), (0,)), "nt": ((1,), (1,)), "tn": ((0,), (0,))}


def _mm(a, b, *, mode, name, out_dtype=F32, bias=None, tm=512, tn=512, tk=None, layer=None):
    sharded = layer is not None
    if mode == "nn":
        m, k = a.shape
        n = NSHARD * b.shape[3] if sharded else b.shape[1]
    elif mode == "nt":
        m, k = a.shape
        n = b.shape[2] if sharded else b.shape[0]
    else:
        (k, m), (_, n) = a.shape, b.shape
    ns = n // NSHARD
    ks = k // NSHARD
    tm = _pick(m, (tm, 256, 128))
    if sharded and mode != "nt":
        tn = _pick(ns, (1408, 768, 512, 256, 128))
    else:
        tn = _pick(n, (tn, 1408, 512, 256, 128))
    if sharded and mode == "nt":
        tk = _pick(ks, (1408, 768, 512, 256, 128))
    else:
        tk = _pick(k, (tk or k, 2816, 1024, 512, 256, 128))
    nk = k // tk
    per_n = ns // tn if sharded and mode != "nt" else 1
    per_k = ks // tk if sharded and mode == "nt" else 1
    dims = (_DIMS[mode], ((), ()))

    def body(*refs):
        a_ref, b_ref = refs[0], refs[1]
        bias_ref = refs[2] if bias is not None else None
        o_ref = refs[3] if bias is not None else refs[2]
        part = lax.dot_general(a_ref[...].astype(BF16), b_ref[...].astype(BF16), dims,
                               preferred_element_type=F32)

        def finish(r):
            if bias_ref is not None:
                r = r + bias_ref[...]
            o_ref[...] = r.astype(out_dtype)

        if nk == 1:
            finish(part)
        else:
            acc_ref = refs[-1]
            kk = pl.program_id(2)

            @pl.when(kk == 0)
            def _():
                acc_ref[...] = part

            @pl.when(kk > 0)
            def _():
                acc_ref[...] += part

            @pl.when(kk == nk - 1)
            def _():
                finish(acc_ref[...])

    out_spec = pl.BlockSpec((tm, tn), lambda i, j, kk: (i, j))
    out_shape = jax.ShapeDtypeStruct((m, n), out_dtype)
    if mode == "nn":
        a_spec = pl.BlockSpec((tm, tk), lambda i, j, kk: (i, kk))
        if sharded:
            b_spec = pl.BlockSpec((None, None, tk, tn), lambda i, j, kk: (j // per_n, layer, kk, j % per_n))
        else:
            b_spec = pl.BlockSpec((tk, tn), lambda i, j, kk: (kk, j))
    elif mode == "nt":
        a_spec = pl.BlockSpec((tm, tk), lambda i, j, kk: (i, kk))
        if sharded:
            b_spec = pl.BlockSpec((None, None, tn, tk), lambda i, j, kk: (kk // per_k, layer, j, kk % per_k))
        else:
            b_spec = pl.BlockSpec((tn, tk), lambda i, j, kk: (j, kk))
    else:
        a_spec = pl.BlockSpec((tk, tm), lambda i, j, kk: (kk, i))
        b_spec = pl.BlockSpec((tk, tn), lambda i, j, kk: (kk, j))
        if sharded:
            out_spec = pl.BlockSpec((None, tm, tn), lambda i, j, kk: (j // per_n, i, j % per_n))
            out_shape = jax.ShapeDtypeStruct((NSHARD, m, ns), out_dtype)
    in_specs = [a_spec, b_spec]
    args = [a, b]
    if bias is not None:
        in_specs.append(pl.BlockSpec((1, tn), lambda i, j, kk: (0, j)))
        args.append(bias)
    return pl.pallas_call(
        body, grid=(m // tm, n // tn, nk), in_specs=in_specs, out_specs=out_spec, out_shape=out_shape,
        scratch_shapes=[pltpu.VMEM((tm, tn), F32)] if nk > 1 else [],
        compiler_params=_cparams(_PAR, _PAR, _ARB), name=name)(*args)


def _mm_group(a, b, *, mode, name, out_dtype=F32, tm=512):
    t = a.shape[0]
    tm = _pick(t, (tm, 256, 128))
    nt_ = t // tm
    g = len(POOL_WINDOWS)
    gd = a.shape[1] // g
    dims = (_DIMS[mode], ((), ()))

    if mode == "tn":
        def body(a_ref, b_ref, o_ref, acc_ref):
            kk = pl.program_id(1)
            part = lax.dot_general(a_ref[...].astype(BF16), b_ref[...].astype(BF16), dims,
                                   preferred_element_type=F32)

            @pl.when(kk == 0)
            def _():
                acc_ref[...] = part

            @pl.when(kk > 0)
            def _():
                acc_ref[...] += part

            @pl.when(kk == nt_ - 1)
            def _():
                o_ref[...] = acc_ref[...]

        return pl.pallas_call(
            body, grid=(g, nt_),
            in_specs=[pl.BlockSpec((tm, gd), lambda gi, kk: (kk, gi)),
                      pl.BlockSpec((tm, gd), lambda gi, kk: (kk, gi))],
            out_specs=pl.BlockSpec((None, gd, gd), lambda gi, kk: (gi, 0, 0)),
            out_shape=jax.ShapeDtypeStruct((g, gd, gd), F32),
            scratch_shapes=[pltpu.VMEM((gd, gd), F32)],
            compiler_params=_cparams(_PAR, _ARB), name=name)(a, b)

    def body(a_ref, b_ref, o_ref):
        o_ref[...] = lax.dot_general(a_ref[...].astype(BF16), b_ref[...].astype(BF16), dims,
                                     preferred_element_type=F32).astype(out_dtype)

    return pl.pallas_call(
        body, grid=(nt_, g),
        in_specs=[pl.BlockSpec((tm, gd), lambda i, gi: (i, gi)),
                  pl.BlockSpec((None, gd, gd), lambda i, gi: (gi, 0, 0))],
        out_specs=pl.BlockSpec((tm, gd), lambda i, gi: (i, gi)),
        out_shape=jax.ShapeDtypeStruct((t, g * gd), out_dtype),
        compiler_params=_cparams(_PAR, _PAR), name=name)(a, b)


def _pre(x, gpre, sc, sh):
    xn, r = _rms(x)
    return (xn * gpre) * (1.0 + sc) + sh, xn, r


def _fwd_first(x, gpre, sc, sh, *, h_dtype, name):
    t, d = x.shape
    tm = _pick(t, (TM_ROW, 256, 128))

    def body(x_ref, gpre_ref, sc_ref, sh_ref, h_ref):
        h, _, _ = _pre(x_ref[...], gpre_ref[...], sc_ref[...], sh_ref[...])
        h_ref[...] = h.astype(h_dtype)

    return pl.pallas_call(
        body, grid=(t // tm,),
        in_specs=[_row_spec(tm, d)] + [_vec_spec(1, d)] * 3,
        out_specs=_row_spec(tm, d), out_shape=jax.ShapeDtypeStruct((t, d), h_dtype),
        compiler_params=_cparams(_PAR), name=name)(x, gpre, sc, sh)


def _fwd_mid(x, y, gpost, gt, gpre, sc, sh, *, h_dtype, name):
    t, d = x.shape
    tm = _pick(t, (TM_ROW, 256, 128))

    def body(x_ref, y_ref, gpost_ref, gt_ref, gpre_ref, sc_ref, sh_ref, xn_ref, h_ref):
        yn, _ = _rms(y_ref[...])
        x_new = x_ref[...] + gt_ref[...] * (yn * gpost_ref[...])
        xn_ref[...] = x_new
        h, _, _ = _pre(x_new, gpre_ref[...], sc_ref[...], sh_ref[...])
        h_ref[...] = h.astype(h_dtype)

    return pl.pallas_call(
        body, grid=(t // tm,),
        in_specs=[_row_spec(tm, d)] * 2 + [_vec_spec(1, d)] * 5,
        out_specs=[_row_spec(tm, d)] * 2,
        out_shape=[jax.ShapeDtypeStruct((t, d), F32), jax.ShapeDtypeStruct((t, d), h_dtype)],
        compiler_params=_cparams(_PAR), name=name)(x, y, gpost, gt, gpre, sc, sh)


def _post_bwd(dx, y, gpost, gt):
    yn, r2 = _rms(y)
    dyn = dx * (gt * gpost)
    dy = _rms_bwd(dyn, yn, r2)
    return dy, dx * yn


def _last_fwd_bwd(x, y, target, gpost, gt, *, name):
    t, d = x.shape
    tm = _pick(t, (TM_ROW, 256, 128))
    n = t // tm

    def body(x_ref, y_ref, tg_ref, gpost_ref, gt_ref, dx_ref, dy_ref, dgpost_ref, dgt_ref, sdy_ref,
             loss_ref, qa, sa, la):
        i = pl.program_id(0)

        @pl.when(i == 0)
        def _():
            qa[...] = jnp.zeros_like(qa)
            sa[...] = jnp.zeros_like(sa)
            la[...] = jnp.zeros_like(la)

        yv = y_ref[...]
        yn, r2 = _rms(yv)
        gt_v, gpost_v = gt_ref[...], gpost_ref[...]
        err = x_ref[...] + gt_v * (yn * gpost_v) - tg_ref[...]
        la[...] += _sum8(err * err)
        dx = err * (1.0 / d)
        dx_ref[...] = dx
        dy = _rms_bwd(dx * (gt_v * gpost_v), yn, r2)
        dy_ref[...] = dy.astype(dy_ref.dtype)
        qa[...] += _sum8(dx * yn)
        sa[...] += _sum8(dy)

        @pl.when(i == n - 1)
        def _():
            q = jnp.sum(qa[...], axis=0, keepdims=True)
            dgpost_ref[...] = gt_v * q
            dgt_ref[...] = gpost_v * q
            sdy_ref[...] = jnp.sum(sa[...], axis=0, keepdims=True)
            tot = jnp.sum(jnp.sum(la[...], axis=0, keepdims=True), axis=1, keepdims=True)
            loss_ref[...] = tot * (0.5 / d)

    return pl.pallas_call(
        body, grid=(n,),
        in_specs=[_row_spec(tm, d)] * 3 + [_vec_spec(1, d)] * 2,
        out_specs=[_row_spec(tm, d)] * 2 + [_vec_spec(1, d)] * 3 + [_vec_spec(1, 1)],
        out_shape=[jax.ShapeDtypeStruct((t, d), F32), jax.ShapeDtypeStruct((t, d), BF16)]
        + [jax.ShapeDtypeStruct((1, d), F32)] * 3 + [jax.ShapeDtypeStruct((1, 1), F32)],
        scratch_shapes=[pltpu.VMEM((SUBLANES, d), F32)] * 3,
        compiler_params=_cparams(_ARB), name=name)(x, y, target, gpost, gt)


def _bwd_mid(dx_new, dh, x_in, gpre, sc, y_prev, gpost_p, gt_p, *, dy_dtype, name):
    t, d = x_in.shape
    tm = _pick(t, (TM_ROW, 256, 128))
    n = t // tm

    def body(dxn_ref, dh_ref, x_ref, gpre_ref, sc_ref, y_ref, gpost_ref, gt_ref,
             dx_ref, dy_ref, dsh_ref, dsc_ref, dgpre_ref, dgpost_ref, dgt_ref, sdy_ref, a1, a2, aq, asd):
        i = pl.program_id(0)

        @pl.when(i == 0)
        def _():
            for a in (a1, a2, aq, asd):
                a[...] = jnp.zeros_like(a)

        dh_v = dh_ref[...]
        xn, r = _rms(x_ref[...])
        dx = dxn_ref[...] + _rms_bwd(dh_v * ((1.0 + sc_ref[...]) * gpre_ref[...]), xn, r)
        dx_ref[...] = dx
        a1[...] += _sum8(dh_v)
        a2[...] += _sum8(dh_v * xn)
        dy, dxyn = _post_bwd(dx, y_ref[...], gpost_ref[...], gt_ref[...])
        dy_ref[...] = dy.astype(dy_dtype)
        aq[...] += _sum8(dxyn)
        asd[...] += _sum8(dy)

        @pl.when(i == n - 1)
        def _():
            s2 = jnp.sum(a2[...], axis=0, keepdims=True)
            q = jnp.sum(aq[...], axis=0, keepdims=True)
            dsh_ref[...] = jnp.sum(a1[...], axis=0, keepdims=True)
            dsc_ref[...] = gpre_ref[...] * s2
            dgpre_ref[...] = (1.0 + sc_ref[...]) * s2
            dgpost_ref[...] = gt_ref[...] * q
            dgt_ref[...] = gpost_ref[...] * q
            sdy_ref[...] = jnp.sum(asd[...], axis=0, keepdims=True)

    return pl.pallas_call(
        body, grid=(n,),
        in_specs=[_row_spec(tm, d)] * 3 + [_vec_spec(1, d)] * 2 + [_row_spec(tm, d)] + [_vec_spec(1, d)] * 2,
        out_specs=[_row_spec(tm, d)] * 2 + [_vec_spec(1, d)] * 6,
        out_shape=[jax.ShapeDtypeStruct((t, d), F32), jax.ShapeDtypeStruct((t, d), dy_dtype)]
        + [jax.ShapeDtypeStruct((1, d), F32)] * 6,
        scratch_shapes=[pltpu.VMEM((SUBLANES, d), F32)] * 4,
        compiler_params=_cparams(_ARB), name=name)(dx_new, dh, x_in, gpre, sc, y_prev, gpost_p, gt_p)


def _bwd_first(dx_new, dh, x_in, gpre, sc, *, name):
    t, d = x_in.shape
    tm = _pick(t, (TM_ROW, 256, 128))
    n = t // tm

    def body(dxn_ref, dh_ref, x_ref, gpre_ref, sc_ref, dx_ref, dsh_ref, dsc_ref, dgpre_ref, a1, a2):
        i = pl.program_id(0)

        @pl.when(i == 0)
        def _():
            a1[...] = jnp.zeros_like(a1)
            a2[...] = jnp.zeros_like(a2)

        dh_v = dh_ref[...]
        xn, r = _rms(x_ref[...])
        dx_ref[...] = dxn_ref[...] + _rms_bwd(dh_v * ((1.0 + sc_ref[...]) * gpre_ref[...]), xn, r)
        a1[...] += _sum8(dh_v)
        a2[...] += _sum8(dh_v * xn)

        @pl.when(i == n - 1)
        def _():
            s2 = jnp.sum(a2[...], axis=0, keepdims=True)
            dsh_ref[...] = jnp.sum(a1[...], axis=0, keepdims=True)
            dsc_ref[...] = gpre_ref[...] * s2
            dgpre_ref[...] = (1.0 + sc_ref[...]) * s2

    return pl.pallas_call(
        body, grid=(n,),
        in_specs=[_row_spec(tm, d)] * 3 + [_vec_spec(1, d)] * 2,
        out_specs=[_row_spec(tm, d)] + [_vec_spec(1, d)] * 3,
        out_shape=[jax.ShapeDtypeStruct((t, d), F32)] + [jax.ShapeDtypeStruct((1, d), F32)] * 3,
        scratch_shapes=[pltpu.VMEM((SUBLANES, d), F32)] * 2,
        compiler_params=_cparams(_ARB), name=name)(dx_new, dh, x_in, gpre, sc)


def _conv3_rows(buf, w_ref, rows, first):
    out = buf[pl.ds(first, rows), :] * w_ref[pl.ds(0, 1), :]
    for k in (1, 2):
        out = out + buf[pl.ds(first + k, rows), :] * w_ref[pl.ds(k, 1), :]
    return out


def _ffn_gate_fwd(u, w, b, *, name):
    t, f2 = u.shape
    f = f2 // 2
    tm = _pick(t, (TM_CONV,))
    hb = HALO_3

    def body(u_ref, up_ref, w_ref, b_ref, a_ref, buf):
        i = pl.program_id(0)
        buf[pl.ds(hb, tm), :] = u_ref[...]
        buf[pl.ds(0, hb), :] = jnp.where(i > 0, up_ref[...], 0.0)
        v = _conv3_rows(buf, w_ref, tm, hb - 2) + b_ref[...]
        vg = v[:, :f]
        a_ref[...] = (vg * _sigmoid(vg) * v[:, f:]).astype(BF16)

    return pl.pallas_call(
        body, grid=(t // tm,),
        in_specs=[_row_spec(tm, f2), _prev_spec(tm, hb, f2), _vec_spec(3, f2), _vec_spec(1, f2)],
        out_specs=_row_spec(tm, f), out_shape=jax.ShapeDtypeStruct((t, f), BF16),
        scratch_shapes=[pltpu.VMEM((tm + hb, f2), F32)],
        compiler_params=_cparams(_PAR), name=name)(u, u, w, b)


def _ffn_gate_bwd(u, da, w, b, *, name):
    t, f2 = u.shape
    f = f2 // 2
    tm = _pick(t, (TM_CONV,))
    hb = HALO_3
    n = t // tm
    ext = tm + hb

    def body(u_ref, up_ref, un_ref, da_ref, dan_ref, w_ref, b_ref, du_ref, dw_ref, db_ref,
             ubuf, dvbuf, wacc, bacc):
        i = pl.program_id(0)

        @pl.when(i == 0)
        def _():
            wacc[...] = jnp.zeros_like(wacc)
            bacc[...] = jnp.zeros_like(bacc)

        ubuf[pl.ds(0, hb), :] = jnp.where(i > 0, up_ref[...], 0.0)
        ubuf[pl.ds(hb, tm), :] = u_ref[...]
        ubuf[pl.ds(hb + tm, hb), :] = un_ref[...]
        v = _conv3_rows(ubuf, w_ref, ext, hb - 2) + b_ref[...]
        vg, vv = v[:, :f], v[:, f:]
        sg = _sigmoid(vg)
        dan = jnp.where(i < n - 1, dan_ref[...], 0.0)
        dvbuf[pl.ds(0, tm), pl.ds(0, f)] = da_ref[...] * vv[:tm] * (sg[:tm] * (1.0 + vg[:tm] * (1.0 - sg[:tm])))
        dvbuf[pl.ds(0, tm), pl.ds(f, f)] = da_ref[...] * (vg[:tm] * sg[:tm])
        dvbuf[pl.ds(tm, hb), pl.ds(0, f)] = dan * vv[tm:] * (sg[tm:] * (1.0 + vg[tm:] * (1.0 - sg[tm:])))
        dvbuf[pl.ds(tm, hb), pl.ds(f, f)] = dan * (vg[tm:] * sg[tm:])
        du = dvbuf[pl.ds(0, tm), :] * w_ref[pl.ds(2, 1), :]
        du = du + dvbuf[pl.ds(1, tm), :] * w_ref[pl.ds(1, 1), :]
        du = du + dvbuf[pl.ds(2, tm), :] * w_ref[pl.ds(0, 1), :]
        du_ref[...] = du.astype(BF16)
        dv = dvbuf[pl.ds(0, tm), :]
        bacc[...] += _sum8(dv)
        for k in range(3):
            wacc[k] += _sum8(dv * ubuf[pl.ds(hb - 2 + k, tm), :])

        @pl.when(i == n - 1)
        def _():
            db_ref[...] = jnp.sum(bacc[...], axis=0, keepdims=True)
            dw_ref[...] = jnp.sum(wacc[...], axis=1)

    return pl.pallas_call(
        body, grid=(n,),
        in_specs=[_row_spec(tm, f2), _prev_spec(tm, hb, f2), _next_spec(tm, hb, f2, t),
                  _row_spec(tm, f), _next_spec(tm, hb, f, t), _vec_spec(3, f2), _vec_spec(1, f2)],
        out_specs=[_row_spec(tm, f2), _vec_spec(3, f2), _vec_spec(1, f2)],
        out_shape=[jax.ShapeDtypeStruct((t, f2), BF16), jax.ShapeDtypeStruct((3, f2), F32),
                   jax.ShapeDtypeStruct((1, f2), F32)],
        scratch_shapes=[pltpu.VMEM((tm + 2 * hb, f2), F32), pltpu.VMEM((ext, f2), F32),
                        pltpu.VMEM((3, SUBLANES, f2), F32), pltpu.VMEM((SUBLANES, f2), F32)],
        compiler_params=_cparams(_ARB), name=name)(u, u, u, da, da, w, b)


def _glu(u, b1, d):
    return (u[:, :d] + b1[:, :d]) * _sigmoid(u[:, d:] + b1[:, d:])


def _fill_glu_buf(buf, u_ref, up_ref, b1_ref, i, tm, d):
    b1 = b1_ref[...]
    buf[pl.ds(0, HALO_A), :] = jnp.where(i > 0, _glu(up_ref[...], b1, d), 0.0)
    buf[pl.ds(HALO_A, tm), :] = _glu(u_ref[...], b1, d)


def _conv31(buf, w_ref, tm):
    first = HALO_A - (CONF_CONV_WIDTH - 1)
    out = buf[pl.ds(first, tm), :] * w_ref[pl.ds(0, 1), :]
    for k in range(1, CONF_CONV_WIDTH):
        out = out + buf[pl.ds(first + k, tm), :] * w_ref[pl.ds(k, 1), :]
    return out


def _layernorm_parts(x):
    mu = jnp.mean(x, axis=-1, keepdims=True)
    xc = x - mu
    rstd = lax.rsqrt(jnp.mean(xc * xc, axis=-1, keepdims=True) + LN_EPS)
    return xc * rstd, rstd


def _a_fwd(u1, b1, dww, dwb, lng, lnb, *, name):
    t, d2 = u1.shape
    d = d2 // 2
    tm = _pick(t, (TM_CONV,))

    def body(u_ref, up_ref, b1_ref, w_ref, wb_ref, g_ref, bb_ref, o_ref, buf):
        i = pl.program_id(0)
        _fill_glu_buf(buf, u_ref, up_ref, b1_ref, i, tm, d)
        u3 = _conv31(buf, w_ref, tm) + wb_ref[...]
        xhat, _ = _layernorm_parts(u3)
        u4 = xhat * g_ref[...] + bb_ref[...]
        o_ref[...] = (u4 * _sigmoid(u4)).astype(BF16)

    return pl.pallas_call(
        body, grid=(t // tm,),
        in_specs=[_row_spec(tm, d2), _prev_spec(tm, HALO_A, d2), _vec_spec(1, d2),
                  _vec_spec(CONF_CONV_WIDTH, d)] + [_vec_spec(1, d)] * 3,
        out_specs=_row_spec(tm, d), out_shape=jax.ShapeDtypeStruct((t, d), BF16),
        scratch_shapes=[pltpu.VMEM((tm + HALO_A, d), F32)],
        compiler_params=_cparams(_PAR), name=name)(u1, u1, b1, dww, dwb, lng, lnb)


def _a_bwd_norm(u1, du5, b1, dww, dwb, lng, lnb, *, name):
    t, d2 = u1.shape
    d = d2 // 2
    tm = _pick(t, (TM_CONV,))
    n = t // tm

    def body(u_ref, up_ref, du5_ref, b1_ref, w_ref, wb_ref, g_ref, bb_ref,
             du3_ref, dg_ref, db_ref, dwb_ref, buf, ag, ab, aw):
        i = pl.program_id(0)

        @pl.when(i == 0)
        def _():
            for a in (ag, ab, aw):
                a[...] = jnp.zeros_like(a)

        _fill_glu_buf(buf, u_ref, up_ref, b1_ref, i, tm, d)
        u3 = _conv31(buf, w_ref, tm) + wb_ref[...]
        xhat, rstd = _layernorm_parts(u3)
        g = g_ref[...]
        u4 = xhat * g + bb_ref[...]
        sg = _sigmoid(u4)
        du4 = du5_ref[...] * (sg * (1.0 + u4 * (1.0 - sg)))
        dxh = du4 * g
        du3 = rstd * (dxh - jnp.mean(dxh, axis=-1, keepdims=True)
                      - xhat * jnp.mean(dxh * xhat, axis=-1, keepdims=True))
        du3_ref[...] = du3
        ag[...] += _sum8(du4 * xhat)
        ab[...] += _sum8(du4)
        aw[...] += _sum8(du3)

        @pl.when(i == n - 1)
        def _():
            for a, o in ((ag, dg_ref), (ab, db_ref), (aw, dwb_ref)):
                o[...] = jnp.sum(a[...], axis=0, keepdims=True)

    return pl.pallas_call(
        body, grid=(n,),
        in_specs=[_row_spec(tm, d2), _prev_spec(tm, HALO_A, d2), _row_spec(tm, d), _vec_spec(1, d2),
                  _vec_spec(CONF_CONV_WIDTH, d)] + [_vec_spec(1, d)] * 3,
        out_specs=[_row_spec(tm, d)] + [_vec_spec(1, d)] * 3,
        out_shape=[jax.ShapeDtypeStruct((t, d), F32)] + [jax.ShapeDtypeStruct((1, d), F32)] * 3,
        scratch_shapes=[pltpu.VMEM((tm + HALO_A, d), F32)] + [pltpu.VMEM((SUBLANES, d), F32)] * 3,
        compiler_params=_cparams(_ARB), name=name)(u1, u1, du5, b1, dww, dwb, lng, lnb)


def _a_bwd_conv(u1, du3, b1, dww, *, name):
    t, d2 = u1.shape
    d = d2 // 2
    tm = _pick(t, (TM_CONV,))
    n = t // tm
    kw = CONF_CONV_WIDTH

    def body(u_ref, up_ref, g3_ref, g3n_ref, b1_ref, w_ref, du1_ref, dw_ref, db1_ref,
             buf, gbuf, wacc, bacc):
        i = pl.program_id(0)

        @pl.when(i == 0)
        def _():
            wacc[...] = jnp.zeros_like(wacc)
            bacc[...] = jnp.zeros_like(bacc)

        _fill_glu_buf(buf, u_ref, up_ref, b1_ref, i, tm, d)
        g3 = g3_ref[...]
        gbuf[pl.ds(0, tm), :] = g3
        gbuf[pl.ds(tm, HALO_A), :] = jnp.where(i < n - 1, g3n_ref[...], 0.0)
        du2 = gbuf[pl.ds(kw - 1, tm), :] * w_ref[pl.ds(0, 1), :]
        for k in range(1, kw):
            du2 = du2 + gbuf[pl.ds(kw - 1 - k, tm), :] * w_ref[pl.ds(k, 1), :]
        first = HALO_A - (kw - 1)
        for k in range(kw):
            wacc[k] += _sum8(g3 * buf[pl.ds(first + k, tm), :])
        b1 = b1_ref[...]
        uv = u_ref[...]
        av = uv[:, :d] + b1[:, :d]
        sg = _sigmoid(uv[:, d:] + b1[:, d:])
        da = du2 * sg
        dg = du2 * av * (sg * (1.0 - sg))
        du1_ref[:, pl.ds(0, d)] = da.astype(BF16)
        du1_ref[:, pl.ds(d, d)] = dg.astype(BF16)
        bacc[:, pl.ds(0, d)] += _sum8(da)
        bacc[:, pl.ds(d, d)] += _sum8(dg)

        @pl.when(i == n - 1)
        def _():
            dw_ref[...] = jnp.sum(wacc[...], axis=1)
            db1_ref[...] = jnp.sum(bacc[...], axis=0, keepdims=True)

    return pl.pallas_call(
        body, grid=(n,),
        in_specs=[_row_spec(tm, d2), _prev_spec(tm, HALO_A, d2), _row_spec(tm, d),
                  _next_spec(tm, HALO_A, d, t), _vec_spec(1, d2), _vec_spec(kw, d)],
        out_specs=[_row_spec(tm, d2), _vec_spec(kw, d), _vec_spec(1, d2)],
        out_shape=[jax.ShapeDtypeStruct((t, d2), BF16), jax.ShapeDtypeStruct((kw, d), F32),
                   jax.ShapeDtypeStruct((1, d2), F32)],
        scratch_shapes=[pltpu.VMEM((tm + HALO_A, d), F32), pltpu.VMEM((tm + HALO_A, d), F32),
                        pltpu.VMEM((kw, SUBLANES, d), F32), pltpu.VMEM((SUBLANES, d2), F32)],
        compiler_params=_cparams(_ARB), name=name)(u1, u1, du3, du3, b1, dww)


def _pool_counts(i, tm, w):
    pos = (i * tm + lax.broadcasted_iota(jnp.int32, (tm, 1), 0) + 1).astype(F32)
    return jnp.minimum(pos, float(w))


def _b_pool_fwd(h, *, name):
    t, d = h.shape
    gd = d // len(POOL_WINDOWS)
    tm = _pick(t, (TM_CONV,))
    hb = HALO_POOL

    def body(h_ref, hp_ref, o_ref, buf):
        i = pl.program_id(0)
        buf[pl.ds(0, hb), :] = jnp.where(i > 0, hp_ref[...], 0.0)
        buf[pl.ds(hb, tm), :] = h_ref[...]
        for g, w in enumerate(POOL_WINDOWS):
            cols = pl.ds(g * gd, gd)
            cur = buf[pl.ds(hb, tm), cols]
            s = cur
            for j in range(1, w):
                s = s + buf[pl.ds(hb - j, tm), cols]
            o_ref[:, cols] = (s / _pool_counts(i, tm, w) - cur).astype(BF16)

    return pl.pallas_call(
        body, grid=(t // tm,),
        in_specs=[_row_spec(tm, d), _prev_spec(tm, hb, d)],
        out_specs=_row_spec(tm, d), out_shape=jax.ShapeDtypeStruct((t, d), BF16),
        scratch_shapes=[pltpu.VMEM((tm + hb, d), F32)],
        compiler_params=_cparams(_PAR), name=name)(h, h)


def _b_pool_bwd(dp, *, name):
    t, d = dp.shape
    gd = d // len(POOL_WINDOWS)
    tm = _pick(t, (TM_CONV,))
    hb = HALO_POOL
    n = t // tm

    def body(dp_ref, dpn_ref, o_ref, buf):
        i = pl.program_id(0)
        for g, w in enumerate(POOL_WINDOWS):
            cols = pl.ds(g * gd, gd)
            buf[pl.ds(0, tm), cols] = dp_ref[:, cols] / _pool_counts(i, tm, w)
            buf[pl.ds(tm, hb), cols] = jnp.where(i < n - 1, dpn_ref[:, cols] * (1.0 / w), 0.0)
            s = buf[pl.ds(0, tm), cols]
            for j in range(1, w):
                s = s + buf[pl.ds(j, tm), cols]
            o_ref[:, cols] = s - dp_ref[:, cols]

    return pl.pallas_call(
        body, grid=(n,),
        in_specs=[_row_spec(tm, d), _next_spec(tm, hb, d, t)],
        out_specs=_row_spec(tm, d), out_shape=jax.ShapeDtypeStruct((t, d), F32),
        scratch_shapes=[pltpu.VMEM((tm + hb, d), F32)],
        compiler_params=_cparams(_PAR), name=name)(dp, dp)


def _b_affine_fwd(mixed, gb, scale, *, name):
    t, d = mixed.shape
    tm = _pick(t, (TM_ROW, 256, 128))

    def body(m_ref, gb_ref, s_ref, o_ref):
        o_ref[...] = (m_ref[...] + gb_ref[...]) * s_ref[...]

    return pl.pallas_call(
        body, grid=(t // tm,), in_specs=[_row_spec(tm, d)] + [_vec_spec(1, d)] * 2,
        out_specs=_row_spec(tm, d), out_shape=jax.ShapeDtypeStruct((t, d), F32),
        compiler_params=_cparams(_PAR), name=name)(mixed, gb, scale)


def _b_affine_bwd(dy, mixed, gb, scale, *, name):
    t, d = mixed.shape
    tm = _pick(t, (TM_ROW, 256, 128))
    n = t // tm

    def body(dy_ref, m_ref, gb_ref, s_ref, dm_ref, ds_ref, dgb_ref, a1, a2):
        i = pl.program_id(0)

        @pl.when(i == 0)
        def _():
            a1[...] = jnp.zeros_like(a1)
            a2[...] = jnp.zeros_like(a2)

        dy_v = dy_ref[...]
        dm_ref[...] = (dy_v * s_ref[...]).astype(BF16)
        a1[...] += _sum8(dy_v * (m_ref[...] + gb_ref[...]))
        a2[...] += _sum8(dy_v)

        @pl.when(i == n - 1)
        def _():
            ds_ref[...] = jnp.sum(a1[...], axis=0, keepdims=True)
            dgb_ref[...] = jnp.sum(a2[...], axis=0, keepdims=True) * s_ref[...]

    return pl.pallas_call(
        body, grid=(n,), in_specs=[_row_spec(tm, d)] * 2 + [_vec_spec(1, d)] * 2,
        out_specs=[_row_spec(tm, d)] + [_vec_spec(1, d)] * 2,
        out_shape=[jax.ShapeDtypeStruct((t, d), BF16)] + [jax.ShapeDtypeStruct((1, d), F32)] * 2,
        scratch_shapes=[pltpu.VMEM((SUBLANES, d), F32)] * 2,
        compiler_params=_cparams(_ARB), name=name)(dy, mixed, gb, scale)


def _c_gate_fwd(bcx, wc, *, name):
    t, d3 = bcx.shape
    d = d3 // 3
    tm = _pick(t, (TM_CONV,))
    hb = HALO_3

    def body(x_ref, xp_ref, w_ref, z_ref, buf):
        i = pl.program_id(0)
        xv, xp = x_ref[...], xp_ref[...]
        buf[pl.ds(0, hb), :] = jnp.where(i > 0, xp[:, d:2 * d] * xp[:, 2 * d:], 0.0)
        buf[pl.ds(hb, tm), :] = xv[:, d:2 * d] * xv[:, 2 * d:]
        z_ref[...] = (xv[:, :d] * _conv3_rows(buf, w_ref, tm, hb - 2)).astype(BF16)

    return pl.pallas_call(
        body, grid=(t // tm,),
        in_specs=[_row_spec(tm, d3), _prev_spec(tm, hb, d3), _vec_spec(3, d)],
        out_specs=_row_spec(tm, d), out_shape=jax.ShapeDtypeStruct((t, d), BF16),
        scratch_shapes=[pltpu.VMEM((tm + hb, d), F32)],
        compiler_params=_cparams(_PAR), name=name)(bcx, bcx, wc)


def _c_gate_bwd(bcx, dz, wc, *, name):
    t, d3 = bcx.shape
    d = d3 // 3
    tm = _pick(t, (TM_CONV,))
    hb = HALO_3
    n = t // tm

    def body(x_ref, xp_ref, xn_ref, dz_ref, dzn_ref, w_ref, o_ref, dw_ref, pbuf, qbuf, wacc):
        i = pl.program_id(0)

        @pl.when(i == 0)
        def _():
            wacc[...] = jnp.zeros_like(wacc)

        xv, xp, xnx = x_ref[...], xp_ref[...], xn_ref[...]
        gbv, gcv, vv = xv[:, :d], xv[:, d:2 * d], xv[:, 2 * d:]
        pbuf[pl.ds(0, hb), :] = jnp.where(i > 0, xp[:, d:2 * d] * xp[:, 2 * d:], 0.0)
        pbuf[pl.ds(hb, tm), :] = gcv * vv
        q = _conv3_rows(pbuf, w_ref, tm, hb - 2)
        dz_v = dz_ref[...]
        dq = dz_v * gbv
        qbuf[pl.ds(0, tm), :] = dq
        qbuf[pl.ds(tm, hb), :] = jnp.where(i < n - 1, dzn_ref[...] * xnx[:, :d], 0.0)
        dp = qbuf[pl.ds(0, tm), :] * w_ref[pl.ds(2, 1), :]
        dp = dp + qbuf[pl.ds(1, tm), :] * w_ref[pl.ds(1, 1), :]
        dp = dp + qbuf[pl.ds(2, tm), :] * w_ref[pl.ds(0, 1), :]
        o_ref[:, pl.ds(0, d)] = (dz_v * q).astype(BF16)
        o_ref[:, pl.ds(d, d)] = (dp * vv).astype(BF16)
        o_ref[:, pl.ds(2 * d, d)] = (dp * gcv).astype(BF16)
        for k in range(3):
            wacc[k] += _sum8(dq * pbuf[pl.ds(hb - 2 + k, tm), :])

        @pl.when(i == n - 1)
        def _():
            dw_ref[...] = jnp.sum(wacc[...], axis=1)

    return pl.pallas_call(
        body, grid=(n,),
        in_specs=[_row_spec(tm, d3), _prev_spec(tm, hb, d3), _next_spec(tm, hb, d3, t),
                  _row_spec(tm, d), _next_spec(tm, hb, d, t), _vec_spec(3, d)],
        out_specs=[_row_spec(tm, d3), _vec_spec(3, d)],
        out_shape=[jax.ShapeDtypeStruct((t, d3), BF16), jax.ShapeDtypeStruct((3, d), F32)],
        scratch_shapes=[pltpu.VMEM((tm + hb, d), F32), pltpu.VMEM((tm + hb, d), F32),
                        pltpu.VMEM((3, SUBLANES, d), F32)],
        compiler_params=_cparams(_ARB), name=name)(bcx, bcx, bcx, dz, dz, wc)


def _row(v):
    return v.reshape(1, -1)


def _kind_of(j):
    return "f" if j % 2 else "abc"[(j // 2) % N_MIXERS]


BIG = ("a_pw1_w", "a_pw2_w", "b_group_w", "c_in_w", "c_out_w", "f_up_w", "f_down_w")
COL_SHARDED = ("a_pw1_w", "c_in_w", "f_up_w")


def _local_step(x, target, mod, p):
    nsub = 2 * DEPTH
    norm_names = (("norm_pre_mix", "norm_post_mix"), ("norm_pre_ffn", "norm_post_ffn"))
    gpre = [_row(p[norm_names[s][0]][i]) for i in range(DEPTH) for s in (0, 1)]
    gpost = [_row(p[norm_names[s][1]][i]) for i in range(DEPTH) for s in (0, 1)]
    sh = [_row(mod[i, 3 * s + 0]) for i in range(DEPTH) for s in (0, 1)]
    sc = [_row(mod[i, 3 * s + 1]) for i in range(DEPTH) for s in (0, 1)]
    gt = [_row(mod[i, 3 * s + 2]) for i in range(DEPTH) for s in (0, 1)]

    def h_dtype(j):
        return F32 if _kind_of(j) == "b" else BF16

    xs, hs, ys, saved = [x], [], [], []

    hs.append(_fwd_first(x, gpre[0], sc[0], sh[0], h_dtype=h_dtype(0), name="fwd_first"))
    for j in range(nsub):
        i, kind = j // 2, _kind_of(j)
        slot = i // N_MIXERS
        h = hs[j]
        tag = f"{kind}{j}"
        if kind == "f":
            u = _mm(h, p["f_up_w"], mode="nn", layer=i, name=f"ffn_up_{tag}")
            a = _ffn_gate_fwd(u, p["f_dw_w"][i], _row(p["f_dw_b"][i]), name=f"ffn_gate_{tag}")
            y = _mm(a, p["f_down_w"][i], mode="nn", name=f"ffn_down_{tag}")
            saved.append((u, a))
        elif kind == "a":
            u1 = _mm(h, p["a_pw1_w"], mode="nn", layer=slot, name=f"a_pw1_{tag}")
            u5 = _a_fwd(u1, _row(p["a_pw1_b"][slot]), p["a_dw_w"][slot], _row(p["a_dw_b"][slot]),
                        _row(p["a_ln_g"][slot]), _row(p["a_ln_b"][slot]), name=f"a_conv_{tag}")
            y = _mm(u5, p["a_pw2_w"][slot], mode="nn", bias=_row(p["a_pw2_b"][slot]), name=f"a_pw2_{tag}")
            saved.append((u1, u5))
        elif kind == "b":
            pooled = _b_pool_fwd(h, name=f"b_pool_{tag}")
            mixed = _mm_group(pooled, p["b_group_w"][slot], mode="nn", name=f"b_mix_{tag}")
            y = _b_affine_fwd(mixed, _row(p["b_group_b"][slot]), _row(p["b_scale"][slot]), name=f"b_aff_{tag}")
            saved.append((pooled, mixed))
        else:
            bcx = _mm(h, p["c_in_w"], mode="nn", layer=slot, name=f"c_in_{tag}")
            z = _c_gate_fwd(bcx, p["c_conv_w"][slot], name=f"c_gate_{tag}")
            y = _mm(z, p["c_out_w"][slot], mode="nn", name=f"c_out_{tag}")
            saved.append((bcx, z))
        ys.append(y)
        if j + 1 < nsub:
            x_new, h_next = _fwd_mid(xs[j], y, gpost[j], gt[j], gpre[j + 1], sc[j + 1], sh[j + 1],
                                     h_dtype=h_dtype(j + 1), name=f"fwd_mid_{j}")
            xs.append(x_new)
            hs.append(h_next)

    n_of = {"a": len([i for i in range(DEPTH) if i % N_MIXERS == 0]),
            "b": len([i for i in range(DEPTH) if i % N_MIXERS == 1]),
            "c": len([i for i in range(DEPTH) if i % N_MIXERS == 2]), "f": DEPTH, "n": DEPTH}
    g = {k: [None] * n_of[k[0]] for k in p}
    dmod = [[None] * 6 for _ in range(DEPTH)]

    last = nsub - 1
    dx, dy, dgpost, dgt, sdy, loss = _last_fwd_bwd(xs[last], ys[last], target, gpost[last], gt[last],
                                                   name="loss_head")
    for j in range(last, -1, -1):
        i, kind = j // 2, _kind_of(j)
        slot = i // N_MIXERS
        sub = j % 2
        tag = f"{kind}{j}"
        g[norm_names[sub][1]][i] = dgpost
        dmod[i][3 * sub + 2] = dgt
        h = hs[j]
        if kind == "f":
            u, a = saved[j]
            da = _mm(dy, p["f_down_w"][i], mode="nt", name=f"ffn_dda_{tag}")
            g["f_down_w"][i] = _mm(a, dy, mode="tn", tk=TK_TOKENS, name=f"ffn_dwdown_{tag}")
            du, dw, db = _ffn_gate_bwd(u, da, p["f_dw_w"][i], _row(p["f_dw_b"][i]), name=f"ffn_gate_bwd_{tag}")
            g["f_dw_w"][i], g["f_dw_b"][i] = dw, db
            dh = _mm(du, p["f_up_w"], mode="nt", layer=i, name=f"ffn_ddh_{tag}")
            g["f_up_w"][i] = _mm(h, du, mode="tn", tk=TK_TOKENS, layer=i, name=f"ffn_dwup_{tag}")
        elif kind == "a":
            u1, u5 = saved[j]
            g["a_pw2_b"][slot] = sdy
            du5 = _mm(dy, p["a_pw2_w"][slot], mode="nt", name=f"a_ddu5_{tag}")
            g["a_pw2_w"][slot] = _mm(u5, dy, mode="tn", tk=TK_TOKENS, name=f"a_dw2_{tag}")
            b1 = _row(p["a_pw1_b"][slot])
            du3, dlg, dlb, ddwb = _a_bwd_norm(u1, du5, b1, p["a_dw_w"][slot], _row(p["a_dw_b"][slot]),
                                              _row(p["a_ln_g"][slot]), _row(p["a_ln_b"][slot]),
                                              name=f"a_bwd_norm_{tag}")
            g["a_ln_g"][slot], g["a_ln_b"][slot], g["a_dw_b"][slot] = dlg, dlb, ddwb
            du1, ddww, db1 = _a_bwd_conv(u1, du3, b1, p["a_dw_w"][slot], name=f"a_bwd_conv_{tag}")
            g["a_dw_w"][slot], g["a_pw1_b"][slot] = ddww, db1
            dh = _mm(du1, p["a_pw1_w"], mode="nt", layer=slot, name=f"a_ddh_{tag}")
            g["a_pw1_w"][slot] = _mm(h, du1, mode="tn", tk=TK_TOKENS, layer=slot, name=f"a_dw1_{tag}")
        elif kind == "b":
            pooled, mixed = saved[j]
            dmixed, dscale, dgb = _b_affine_bwd(dy, mixed, _row(p["b_group_b"][slot]), _row(p["b_scale"][slot]),
                                                name=f"b_aff_bwd_{tag}")
            g["b_scale"][slot], g["b_group_b"][slot] = dscale, dgb
            dpooled = _mm_group(dmixed, p["b_group_w"][slot], mode="nt", name=f"b_dpool_{tag}")
            g["b_group_w"][slot] = _mm_group(pooled, dmixed, mode="tn", tm=TK_TOKENS, name=f"b_dw_{tag}")
            dh = _b_pool_bwd(dpooled, name=f"b_pool_bwd_{tag}")
        else:
            bcx, z = saved[j]
            dz = _mm(dy, p["c_out_w"][slot], mode="nt", name=f"c_ddz_{tag}")
            g["c_out_w"][slot] = _mm(z, dy, mode="tn", tk=TK_TOKENS, name=f"c_dwout_{tag}")
            dbcx, dwc = _c_gate_bwd(bcx, dz, p["c_conv_w"][slot], name=f"c_gate_bwd_{tag}")
            g["c_conv_w"][slot] = dwc
            dh = _mm(dbcx, p["c_in_w"], mode="nt", layer=slot, name=f"c_ddh_{tag}")
            g["c_in_w"][slot] = _mm(h, dbcx, mode="tn", tk=TK_TOKENS, layer=slot, name=f"c_dwin_{tag}")
        if j > 0:
            pj = j - 1
            dy_dtype = F32 if _kind_of(pj) == "b" else BF16
            dx, dy, dsh, dsc, dgpre, dgpost, dgt, sdy = _bwd_mid(
                dx, dh, xs[j], gpre[j], sc[j], ys[pj], gpost[pj], gt[pj], dy_dtype=dy_dtype, name=f"bwd_mid_{j}")
        else:
            dx, dsh, dsc, dgpre = _bwd_first(dx, dh, xs[0], gpre[0], sc[0], name="bwd_first")
        dmod[i][3 * sub + 0] = dsh
        dmod[i][3 * sub + 1] = dsc
        g[norm_names[sub][0]][i] = dgpre

    small = {k: jnp.stack(v).reshape(p[k].shape) for k, v in g.items() if k not in BIG}
    big = {k: v for k, v in g.items() if k in BIG}
    dmod_arr = jnp.stack([jnp.concatenate(r, axis=0) for r in dmod])
    return loss, dx, dmod_arr, small, big


_MESH = pl.DeviceIdType.MESH
_ANY = pl.BlockSpec(memory_space=pl.ANY)
_VMEM = pl.BlockSpec(memory_space=pltpu.VMEM)


def _place():
    return lax.axis_index("x"), lax.axis_index("y"), lax.axis_index("c")


def _other_chips(x, y):
    return [(1 - x, y), (x, 1 - y), (1 - x, 1 - y)]


def _remote(src, dst, send_sem, recv_sem, to):
    return pltpu.make_async_remote_copy(src_ref=src, dst_ref=dst, send_sem=send_sem, recv_sem=recv_sem,
                                        device_id=to, device_id_type=_MESH)


def _all_gather8(blk, *, name):
    r, cdim = blk.shape

    def body(x_ref, out_ref, send_sems, recv_sems, local_sem):
        x, y, c = _place()
        me, sibling = (x, y, c), (x, y, 1 - c)
        chips = _other_chips(x, y)

        def slot(px, py, pc):
            return out_ref.at[4 * px + 2 * py + pc]

        def copy(k, block, to, src=None):
            return _remote(slot(*block) if src is None else src, slot(*block),
                           send_sems.at[k], recv_sems.at[k], to)

        mine = pltpu.make_async_copy(x_ref, slot(*me), local_sem)
        mine.start()
        first = [copy(0, me, sibling, src=x_ref)]
        first += [copy(1 + j, me, (*chip, c), src=x_ref) for j, chip in enumerate(chips)]
        for cp in first:
            cp.start()
        passed = [copy(4 + j, (*chip, c), sibling) for j, chip in enumerate(chips)]
        for j, chip in enumerate(chips):
            copy(1 + j, (*chip, c), me).wait_recv()
            passed[j].start()
        copy(0, sibling, me).wait_recv()
        for j, chip in enumerate(chips):
            copy(4 + j, (*chip, 1 - c), me).wait_recv()
        for cp in first + passed:
            cp.wait_send()
        mine.wait()

    return pl.pallas_call(
        body, out_shape=jax.ShapeDtypeStruct((NDEV, r, cdim), blk.dtype),
        in_specs=[_VMEM], out_specs=_VMEM,
        scratch_shapes=[pltpu.SemaphoreType.DMA((7,)), pltpu.SemaphoreType.DMA((7,)), pltpu.SemaphoreType.DMA],
        compiler_params=pltpu.CompilerParams(vmem_limit_bytes=VMEM_LIMIT), name=name)(blk)


def _gather_dst(kind):
    if kind == "col":
        return lambda s, h: (s, h)
    if kind == "row":
        return lambda s, h: (h, slice(None), s)
    return lambda s, h: (slice(None), s, h)


def _gather_weights(srcs, kinds, out_shapes, *, name):
    nt = len(srcs)

    def body(*refs):
        src_refs, out_refs = refs[:nt], refs[nt:2 * nt]
        send_sems, recv_sems, local_sems = refs[2 * nt:]
        x, y, c = _place()
        sibling = (x, y, 1 - c)
        chips = _other_chips(x, y)
        s_me = 2 * x + y

        def at(k, s, h):
            return out_refs[k].at[_gather_dst(kinds[k])(s, h)]

        local, sends, passed = [], [], []
        for k in range(nt):
            for h in (0, 1):
                cp = pltpu.make_async_copy(src_refs[k].at[h], at(k, s_me, h), local_sems.at[k, h])
                cp.start()
                local.append(cp)
            for j, chip in enumerate(chips):
                cp = _remote(src_refs[k].at[c], at(k, s_me, c), send_sems.at[k, j], recv_sems.at[k, j], (*chip, c))
                cp.start()
                sends.append(cp)
        for k in range(nt):
            for j, (px, py) in enumerate(chips):
                got = at(k, 2 * px + py, c)
                _remote(got, got, send_sems.at[k, j], recv_sems.at[k, j], (px, py, c)).wait_recv()
                cp = _remote(got, got, send_sems.at[k, 3 + j], recv_sems.at[k, 3 + j], sibling)
                cp.start()
                passed.append(cp)
        for k in range(nt):
            for j, (px, py) in enumerate(chips):
                got = at(k, 2 * px + py, 1 - c)
                _remote(got, got, send_sems.at[k, 3 + j], recv_sems.at[k, 3 + j], sibling).wait_recv()
        for cp in sends + passed:
            cp.wait_send()
        for cp in local:
            cp.wait()

    return pl.pallas_call(
        body, out_shape=[jax.ShapeDtypeStruct(s, BF16) for s in out_shapes],
        in_specs=[_ANY] * nt, out_specs=[_ANY] * nt,
        scratch_shapes=[pltpu.SemaphoreType.DMA((nt, 6)), pltpu.SemaphoreType.DMA((nt, 6)),
                        pltpu.SemaphoreType.DMA((nt, 2))],
        name=name)(*srcs)


def _pair_exchange(gs, *, name):
    n = len(gs)

    def body(*refs):
        g_refs, out_refs = refs[:n], refs[n:2 * n]
        send_sems, recv_sems = refs[2 * n:]
        x, y, c = _place()
        sibling = (x, y, 1 - c)
        copies = []
        for k in range(n):
            cp = _remote(g_refs[k].at[:, 1 - c], out_refs[k], send_sems.at[k], recv_sems.at[k], sibling)
            cp.start()
            copies.append(cp)
        for cp in copies:
            cp.wait()

    return pl.pallas_call(
        body, out_shape=[jax.ShapeDtypeStruct((NSHARD,) + g.shape[2:], F32) for g in gs],
        in_specs=[_ANY] * n, out_specs=[_ANY] * n,
        scratch_shapes=[pltpu.SemaphoreType.DMA((n,)), pltpu.SemaphoreType.DMA((n,))],
        name=name)(*gs)


def _pair_sum(g, r1, half, *, name):
    _, _, rh, cdim = g.shape
    tr = _pick(rh, (256, 128, 176, 64, 32, 16, 8))

    def body(half_ref, g_ref, r_ref, o_ref):
        o_ref[...] = g_ref[...] + r_ref[...]

    grid_spec = pltpu.PrefetchScalarGridSpec(
        num_scalar_prefetch=1, grid=(NSHARD, rh // tr),
        in_specs=[pl.BlockSpec((None, None, tr, cdim), lambda s, r, hf: (s, hf[0], r, 0)),
                  pl.BlockSpec((None, tr, cdim), lambda s, r, hf: (s, r, 0))],
        out_specs=pl.BlockSpec((None, tr, cdim), lambda s, r, hf: (s, r, 0)))
    return pl.pallas_call(
        body, grid_spec=grid_spec, out_shape=jax.ShapeDtypeStruct((NSHARD, rh, cdim), F32),
        compiler_params=_cparams(_PAR, _PAR), name=name)(half, g, r1)


def _chip_exchange(ss, *, name):
    n = len(ss)

    def body(*refs):
        s_refs, out_refs = refs[:n], refs[n:2 * n]
        send_sems, recv_sems, local_sems = refs[2 * n:]
        x, y, c = _place()
        chips = _other_chips(x, y)
        s_me = 2 * x + y
        copies = []
        for k in range(n):
            cp = pltpu.make_async_copy(s_refs[k].at[s_me], out_refs[k].at[s_me], local_sems.at[k])
            cp.start()
            copies.append(cp)
            for j, (px, py) in enumerate(chips):
                cp = _remote(s_refs[k].at[2 * px + py], out_refs[k].at[s_me],
                             send_sems.at[k, j], recv_sems.at[k, j], (px, py, c))
                cp.start()
                copies.append(cp)
        for cp in copies:
            cp.wait()

    return pl.pallas_call(
        body, out_shape=[jax.ShapeDtypeStruct(s.shape, F32) for s in ss],
        in_specs=[_ANY] * n, out_specs=[_ANY] * n,
        scratch_shapes=[pltpu.SemaphoreType.DMA((n, 3)), pltpu.SemaphoreType.DMA((n, 3)),
                        pltpu.SemaphoreType.DMA((n,))],
        name=name)(*ss)


def _chip_sum(r3, *, name):
    _, rh, cdim = r3.shape
    tr = _pick(rh, (256, 128, 176, 64, 32, 16, 8))

    def body(r_ref, o_ref):
        o_ref[...] = ((r_ref[0] + r_ref[1]) + r_ref[2]) + r_ref[3]

    return pl.pallas_call(
        body, grid=(rh // tr,),
        in_specs=[pl.BlockSpec((NSHARD, tr, cdim), lambda r: (0, r, 0))],
        out_specs=pl.BlockSpec((tr, cdim), lambda r: (r, 0)),
        out_shape=jax.ShapeDtypeStruct((rh, cdim), F32),
        compiler_params=_cparams(_PAR), name=name)(r3)


def _join_halves(reds, layers_of, *, name):
    n = len(reds)
    nt = len(layers_of)

    def body(*refs):
        r_refs, out_refs = refs[:n], refs[n:n + nt]
        send_sems, recv_sems, local_sems = refs[n + nt:]
        x, y, c = _place()
        sibling = (x, y, 1 - c)
        local, remote = [], []
        for t, ks in enumerate(layers_of):
            for l, k in enumerate(ks):
                cp = pltpu.make_async_copy(r_refs[k], out_refs[t].at[l, c], local_sems.at[k])
                cp.start()
                local.append(cp)
                cp = _remote(r_refs[k], out_refs[t].at[l, c], send_sems.at[k], recv_sems.at[k], sibling)
                cp.start()
                remote.append((cp, t, l, k))
        for cp, t, l, k in remote:
            cp.wait_send()
            got = out_refs[t].at[l, 1 - c]
            _remote(got, got, send_sems.at[k], recv_sems.at[k], sibling).wait_recv()
        for cp in local:
            cp.wait()

    out_shape = [jax.ShapeDtypeStruct((len(ks), 2) + reds[ks[0]].shape, F32) for ks in layers_of]
    return pl.pallas_call(
        body, out_shape=out_shape, in_specs=[_ANY] * n, out_specs=[_ANY] * nt,
        scratch_shapes=[pltpu.SemaphoreType.DMA((n,)), pltpu.SemaphoreType.DMA((n,)),
                        pltpu.SemaphoreType.DMA((n,))],
        name=name)(*reds)


def _sum_devices(g, *, name):
    _, r, cdim = g.shape
    tr = _pick(r, (512, 256, 128, 64, 32, 16, 8))

    def body(g_ref, o_ref):
        acc = g_ref[0]
        for e in range(1, NDEV):
            acc = acc + g_ref[e]
        o_ref[...] = acc

    return pl.pallas_call(
        body, grid=(r // tr,), in_specs=[pl.BlockSpec((NDEV, tr, cdim), lambda i: (0, i, 0))],
        out_specs=pl.BlockSpec((tr, cdim), lambda i: (i, 0)),
        out_shape=jax.ShapeDtypeStruct((r, cdim), F32),
        compiler_params=_cparams(_PAR), name=name)(g)


def _mod_fwd(c_all, mod_w, mod_b_cols, *, name):
    nl, d, n = mod_w.shape
    ne = c_all.shape[0]
    tn = _pick(n, (768, 512, 384, 256, 128))

    def body(c_ref, w_ref, b_ref, o_ref):
        cv = c_ref[...]
        act = (cv * _sigmoid(cv)).astype(BF16)
        o_ref[...] = jnp.dot(act, w_ref[...].astype(BF16), preferred_element_type=F32) + b_ref[...]

    return pl.pallas_call(
        body, grid=(nl, n // tn),
        in_specs=[pl.BlockSpec((ne, d), lambda i, j: (0, 0)),
                  pl.BlockSpec((None, d, tn), lambda i, j: (i, 0, j)),
                  pl.BlockSpec((None, 1, tn), lambda i, j: (i, 0, j))],
        out_specs=pl.BlockSpec((None, ne, tn), lambda i, j: (i, 0, j)),
        out_shape=jax.ShapeDtypeStruct((nl, ne, n), F32),
        compiler_params=_cparams(_PAR, _PAR), name=name)(c_all, mod_w, mod_b_cols)


def _adam_math(w, g, m, v):
    m2 = ADAM_B1 * m + (1.0 - ADAM_B1) * g
    v2 = ADAM_B2 * v + (1.0 - ADAM_B2) * (g * g)
    m_hat = m2 / (1.0 - ADAM_B1 ** ADAM_STEP)
    v_hat = v2 / (1.0 - ADAM_B2 ** ADAM_STEP)
    delta = -ADAM_LR * (m_hat / (jnp.sqrt(v_hat) + ADAM_EPS) + ADAM_WD * w)
    return delta, m2, v2


def _adamw(w, g, m, v, *, name):
    rows, cdim = w.shape
    tr = _pick(rows, tuple(t for t in (512, 256, 128, 64, 32, 16, 8) if t * cdim <= 256 * 1024))

    def body(w_ref, g_ref, m_ref, v_ref, d_ref, mo_ref, vo_ref):
        d_ref[...], mo_ref[...], vo_ref[...] = _adam_math(w_ref[...], g_ref[...], m_ref[...], v_ref[...])

    spec = pl.BlockSpec((tr, cdim), lambda i: (i, 0))
    return pl.pallas_call(
        body, grid=(rows // tr,), in_specs=[spec] * 4, out_specs=[spec] * 3,
        out_shape=[jax.ShapeDtypeStruct((rows, cdim), F32)] * 3,
        compiler_params=_cparams(_PAR), name=name)(w, g, m, v)


def _mod_w_update(c_t, dmod, w, m, v, *, name):
    nl, d, n = w.shape
    ne = c_t.shape[1]
    tr = _pick(d, (128, 64, 32, 16, 8))

    def body(c_ref, dm_ref, w_ref, m_ref, v_ref, g_ref, d_ref, mo_ref, vo_ref):
        cv = c_ref[...]
        act = cv * _sigmoid(cv)
        dm = dm_ref[...]
        g = act[:, 0:1] * dm[0:1, :]
        for e in range(1, ne):
            g = g + act[:, e:e + 1] * dm[e:e + 1, :]
        g_ref[...] = g
        d_ref[...], mo_ref[...], vo_ref[...] = _adam_math(w_ref[...], g, m_ref[...], v_ref[...])

    big = pl.BlockSpec((None, tr, n), lambda i, r: (i, r, 0))
    return pl.pallas_call(
        body, grid=(nl, d // tr),
        in_specs=[pl.BlockSpec((tr, ne), lambda i, r: (r, 0)),
                  pl.BlockSpec((None, ne, n), lambda i, r: (i, 0, 0)), big, big, big],
        out_specs=[big] * 4, out_shape=[jax.ShapeDtypeStruct((nl, d, n), F32)] * 4,
        compiler_params=_cparams(_PAR, _PAR), name=name)(c_t, dmod, w, m, v)


def _pack(arrs):
    flat = jnp.concatenate([a.reshape(-1) for a in arrs])
    tile = SUBLANES * LANES
    pad = (-flat.shape[0]) % tile
    return jnp.pad(flat, (0, pad)).reshape(-1, LANES)


def _unpack(packed, shapes, lead=()):
    flat = packed.reshape(lead + (-1,))
    out, off = [], 0
    for shp in shapes:
        size = 1
        for s in shp:
            size *= s
        out.append(flat[..., off:off + size].reshape(lead + tuple(shp)))
        off += size
    return out


SMALL_SHARD_AXIS = {"a_pw1_b": 1, "a_dw_w": 2, "a_dw_b": 1, "a_ln_g": 1, "a_ln_b": 1, "a_pw2_b": 1,
                    "c_conv_w": 2, "f_dw_w": 2}
SMALL_REPLICATED = ("norm_pre_mix", "norm_post_mix", "norm_pre_ffn", "norm_post_ffn",
                    "b_group_b", "b_scale", "f_dw_b")
WEIGHT_ORDER = ("mod_w", "mod_b", "norm_pre_mix", "norm_post_mix", "norm_pre_ffn", "norm_post_ffn",
                "a_pw1_w", "a_pw1_b", "a_dw_w", "a_dw_b", "a_ln_g", "a_ln_b", "a_pw2_w", "a_pw2_b",
                "b_group_w", "b_group_b", "b_scale", "c_in_w", "c_conv_w", "c_out_w",
                "f_up_w", "f_dw_w", "f_dw_b", "f_down_w")


def _as_layers_rows_cols(name, w):
    if name == "b_group_w":
        return w.reshape(w.shape[1], w.shape[2], w.shape[3])
    return w


def _step(x, c, loss_target, w, m, v):
    xi, yi, ci = _place()
    shard = 2 * xi + yi
    example = 4 * xi + 2 * yi + ci
    d = x.shape[-1]

    small_names = tuple(SMALL_SHARD_AXIS)
    gathered0 = _all_gather8(_pack([c] + [w[k] for k in small_names]), name="gather_small")
    parts = _unpack(gathered0, [c.shape] + [w[k].shape for k in small_names], lead=(NDEV,))
    c_all = parts[0].reshape(NDEV, d)
    p = {}
    for k, part in zip(small_names, parts[1:]):
        p[k] = jnp.concatenate([part[2 * s] for s in range(NSHARD)], axis=SMALL_SHARD_AXIS[k])
    for k in SMALL_REPLICATED:
        p[k] = w[k]

    ncol = w["mod_w"].shape[2]
    mod_b_cols = lax.dynamic_slice_in_dim(w["mod_b"], shard * ncol, ncol, axis=1).reshape(DEPTH, 1, ncol)
    mod_part = _mod_fwd(c_all, w["mod_w"], mod_b_cols, name="mod_fwd")
    gathered1 = _all_gather8(mod_part.reshape(DEPTH * NDEV, ncol), name="gather_mod")
    mod_all = gathered1.reshape(NSHARD, 2, DEPTH, NDEV, ncol)[:, 0]
    mod_mine = lax.dynamic_index_in_dim(mod_all, example, axis=2, keepdims=False)
    mod = jnp.transpose(mod_mine, (1, 0, 2)).reshape(DEPTH, 6, d)

    srcs, kinds, out_shapes = [], [], []
    for k in BIG:
        wk = _as_layers_rows_cols(k, w[k]).astype(BF16)
        nl, r, cdim = wk.shape
        if nl >= 2:
            a, rh = nl // 2, r
        else:
            a, rh = 1, r // 2
        srcs.append(wk.reshape(2, a, rh, cdim))
        if k in COL_SHARDED:
            kinds.append("col")
            out_shapes.append((NSHARD, 2, a, rh, cdim))
        elif nl >= 2:
            kinds.append("row")
            out_shapes.append((2, a, NSHARD, rh, cdim))
        else:
            kinds.append("row1")
            out_shapes.append((1, NSHARD, 2, rh, cdim))
    full = _gather_weights(srcs, kinds, out_shapes, name="gather_weights")
    for k, f in zip(BIG, full):
        nl, r, cdim = _as_layers_rows_cols(k, w[k]).shape
        if k in COL_SHARDED:
            p[k] = f.reshape(NSHARD, nl, r, cdim)
        elif k == "b_group_w":
            p[k] = f.reshape(1, nl, NSHARD * r, cdim)
        else:
            p[k] = f.reshape(nl, NSHARD * r, cdim)

    loss, grad_x, dmod, small, big = _local_step(x[0], loss_target[0], mod, p)

    gs, layers_of = [], []
    for k in BIG:
        ks = []
        for g in big[k]:
            if k == "b_group_w":
                ng, rr, cc = g.shape
                g = jnp.transpose(g.reshape(ng, NSHARD, rr // NSHARD, cc), (1, 0, 2, 3)).reshape(NSHARD, -1, cc)
            elif k not in COL_SHARDED:
                g = g.reshape(NSHARD, g.shape[0] // NSHARD, g.shape[1])
            ks.append(len(gs))
            gs.append(g.reshape(NSHARD, 2, g.shape[1] // 2, g.shape[2]))
        layers_of.append(ks)
    half = ci.reshape(1).astype(jnp.int32)
    r1 = _pair_exchange(gs, name="grad_pair_exchange")
    ss = [_pair_sum(g, r, half, name=f"grad_pair_sum_{i}") for i, (g, r) in enumerate(zip(gs, r1))]
    r3 = _chip_exchange(ss, name="grad_chip_exchange")
    reds = [_chip_sum(r, name=f"grad_chip_sum_{i}") for i, r in enumerate(r3)]
    joined = _join_halves(reds, layers_of, name="grad_join_halves")
    grads = {k: j.reshape(w[k].shape) for k, j in zip(BIG, joined)}

    rep_names = SMALL_REPLICATED
    small_list = [small[k] for k in rep_names] + [small[k] for k in small_names] + [dmod]
    gathered2 = _all_gather8(_pack(small_list), name="gather_small_grads")
    summed = _sum_devices(gathered2, name="sum_small_grads")
    shapes = [s.shape for s in small_list]
    sums = _unpack(summed, shapes)
    for k, s in zip(rep_names, sums[:len(rep_names)]):
        grads[k] = s
    for k, s in zip(small_names, sums[len(rep_names):-1]):
        ax = SMALL_SHARD_AXIS[k]
        grads[k] = lax.dynamic_slice_in_dim(s, shard * w[k].shape[ax], w[k].shape[ax], axis=ax)
    grads["mod_b"] = sums[-1].reshape(w["mod_b"].shape)
    dmod_all = _unpack(gathered2, shapes, lead=(NDEV,))[-1].reshape(NDEV, DEPTH, NSHARD, ncol)
    dmod_cols = jnp.transpose(lax.dynamic_index_in_dim(dmod_all, shard, axis=2, keepdims=False), (1, 0, 2))

    delta, new_m, new_v = {}, {}, {}
    grads["mod_w"], delta["mod_w"], new_m["mod_w"], new_v["mod_w"] = _mod_w_update(
        c_all.T, dmod_cols, w["mod_w"], m["mod_w"], v["mod_w"], name="mod_w_update")
    for k in BIG:
        cdim = w[k].shape[-1]
        outs = _adamw(*[t.reshape(-1, cdim) for t in (w[k], grads[k], m[k], v[k])], name=f"adamw_{k}")
        delta[k], new_m[k], new_v[k] = [o.reshape(w[k].shape) for o in outs]
    rest = ("mod_b",) + rep_names + small_names
    packs = [_pack([t[k] for k in rest]) for t in (w, grads, m, v)]
    outs = _adamw(*packs, name="adamw_small")
    rest_shapes = [w[k].shape for k in rest]
    for dst, o in zip((delta, new_m, new_v), outs):
        for k, t in zip(rest, _unpack(o, rest_shapes)):
            dst[k] = t

    loss_all = lax.psum(loss[0, 0], ("x", "y", "c"))
    return (loss_all, grad_x[None], *[grads[k] for k in WEIGHT_ORDER], *[delta[k] for k in WEIGHT_ORDER],
            *[new_m[k] for k in WEIGHT_ORDER], *[new_v[k] for k in WEIGHT_ORDER])


def kernel(x, c, mod_w, mod_b, norm_pre_mix, norm_post_mix, norm_pre_ffn, norm_post_ffn, a_pw1_w, a_pw1_b, a_dw_w, a_dw_b, a_ln_g, a_ln_b, a_pw2_w, a_pw2_b, b_group_w, b_group_b, b_scale, c_in_w, c_conv_w, c_out_w, f_up_w, f_dw_w, f_dw_b, f_down_w, loss_target, m_mod_w, m_mod_b, m_norm_pre_mix, m_norm_post_mix, m_norm_pre_ffn, m_norm_post_ffn, m_a_pw1_w, m_a_pw1_b, m_a_dw_w, m_a_dw_b, m_a_ln_g, m_a_ln_b, m_a_pw2_w, m_a_pw2_b, m_b_group_w, m_b_group_b, m_b_scale, m_c_in_w, m_c_conv_w, m_c_out_w, m_f_up_w, m_f_dw_w, m_f_dw_b, m_f_down_w, v_mod_w, v_mod_b, v_norm_pre_mix, v_norm_post_mix, v_norm_pre_ffn, v_norm_post_ffn, v_a_pw1_w, v_a_pw1_b, v_a_dw_w, v_a_dw_b, v_a_ln_g, v_a_ln_b, v_a_pw2_w, v_a_pw2_b, v_b_group_w, v_b_group_b, v_b_scale, v_c_in_w, v_c_conv_w, v_c_out_w, v_f_up_w, v_f_dw_w, v_f_dw_b, v_f_down_w):
    given = dict(locals())
    w = {k: given[k] for k in WEIGHT_ORDER}
    m = {k: given["m_" + k] for k in WEIGHT_ORDER}
    v = {k: given["v_" + k] for k in WEIGHT_ORDER}
    return _step(x, c, loss_target, w, m, v)
```

```python
import functools

import jax
import jax.numpy as jnp
from jax import lax
from jax.experimental import pallas as pl
from jax.experimental.pallas import tpu as pltpu

F32 = jnp.float32
BF16 = jnp.bfloat16

DEPTH = 4
N_MIXERS = 3
CONF_CONV_WIDTH = 31
POOL_WINDOWS = (2, 4, 8, 16)
RMS_EPS = 1e-6
LN_EPS = 1e-5
ADAM_LR = 0.001
ADAM_B1 = 0.9
ADAM_B2 = 0.999
ADAM_EPS = 1e-08
ADAM_WD = 0.01
ADAM_STEP = 10

TM_ROW = 512
TM_CONV = 128
TK_TOKENS = 1024
TM_MM = 1024
TN_MM = 1024
SUBLANES = 8
LANES = 128
NSHARD = 4
NDEV = 8
HALO_A = 32
HALO_POOL = 16
HALO_3 = 8
VMEM_LIMIT = 56 * 1024 * 1024

_PAR = "parallel"
_ARB = "arbitrary"


def _cparams(*sem):
    return pltpu.CompilerParams(dimension_semantics=sem, vmem_limit_bytes=VMEM_LIMIT)


def _pick(n, prefs):
    for p in prefs:
        if p <= n and n % p == 0:
            return p
    return n


def _row_spec(tm, width):
    return pl.BlockSpec((tm, width), lambda i: (i, 0))


def _vec_spec(rows, width):
    return pl.BlockSpec((rows, width), lambda i: (0, 0))


def _prev_spec(tm, hb, width):
    return pl.BlockSpec((hb, width), lambda i: (jnp.maximum(i * (tm // hb) - 1, 0), 0))


def _next_spec(tm, hb, width, total):
    last = total // hb - 1
    return pl.BlockSpec((hb, width), lambda i: (jnp.minimum((i + 1) * (tm // hb), last), 0))


def _sum8(v):
    r, c = v.shape
    return jnp.sum(v.reshape(r // SUBLANES, SUBLANES, c), axis=0)


def _rms(x):
    r = lax.rsqrt(jnp.mean(x * x, axis=-1, keepdims=True) + RMS_EPS)
    return x * r, r


def _rms_bwd(dy, xn, r):
    return r * (dy - xn * jnp.mean(dy * xn, axis=-1, keepdims=True))


def _sigmoid(x):
    return 1.0 / (1.0 + jnp.exp(-x))


_DIMS = {"nn": ((1,), (0,)), "nt": ((1,), (1,)), "tn": ((0,), (0,))}


def _mm(a, b, *, mode, name, out_dtype=F32, bias=None, tm=TM_MM, tn=TN_MM, tk=None, layer=None):
    sharded = layer is not None
    if mode == "nn":
        m, k = a.shape
        n = NSHARD * b.shape[3] if sharded else b.shape[1]
    elif mode == "nt":
        m, k = a.shape
        n = b.shape[2] if sharded else b.shape[0]
    else:
        (k, m), (_, n) = a.shape, b.shape
    ns = n // NSHARD
    ks = k // NSHARD
    tm = _pick(m, (tm, 1408, 512, 256, 128))
    if sharded and mode != "nt":
        tn = _pick(ns, (1408, 768, 512, 256, 128))
    else:
        tn = _pick(n, (tn, 1408, 512, 256, 128))
    if sharded and mode == "nt":
        tk = _pick(ks, (1408, 768, 512, 256, 128))
    else:
        tk = _pick(k, (tk or k, 2816, 1024, 512, 256, 128))
    nk = k // tk
    per_n = ns // tn if sharded and mode != "nt" else 1
    per_k = ks // tk if sharded and mode == "nt" else 1
    dims = (_DIMS[mode], ((), ()))

    def split(idx, per):
        return (idx, 0) if per == 1 else (idx // per, idx % per)

    def body(*refs):
        a_ref, b_ref = refs[0], refs[1]
        bias_ref = refs[2] if bias is not None else None
        o_ref = refs[3] if bias is not None else refs[2]
        part = lax.dot_general(a_ref[...].astype(BF16), b_ref[...].astype(BF16), dims,
                               preferred_element_type=F32)

        def finish(r):
            if bias_ref is not None:
                r = r + bias_ref[...]
            o_ref[...] = r.astype(out_dtype)

        if nk == 1:
            finish(part)
        else:
            acc_ref = refs[-1]
            kk = pl.program_id(2)

            @pl.when(kk == 0)
            def _():
                acc_ref[...] = part

            @pl.when(kk > 0)
            def _():
                acc_ref[...] += part

            @pl.when(kk == nk - 1)
            def _():
                finish(acc_ref[...])

    out_spec = pl.BlockSpec((tm, tn), lambda i, j, kk: (i, j))
    out_shape = jax.ShapeDtypeStruct((m, n), out_dtype)
    if mode == "nn":
        a_spec = pl.BlockSpec((tm, tk), lambda i, j, kk: (i, kk))
        if sharded:
            b_spec = pl.BlockSpec((None, None, tk, tn),
                                  lambda i, j, kk: (split(j, per_n)[0], layer, kk, split(j, per_n)[1]))
        else:
            b_spec = pl.BlockSpec((tk, tn), lambda i, j, kk: (kk, j))
    elif mode == "nt":
        a_spec = pl.BlockSpec((tm, tk), lambda i, j, kk: (i, kk))
        if sharded:
            b_spec = pl.BlockSpec((None, None, tn, tk),
                                  lambda i, j, kk: (split(kk, per_k)[0], layer, j, split(kk, per_k)[1]))
        else:
            b_spec = pl.BlockSpec((tn, tk), lambda i, j, kk: (j, kk))
    else:
        a_spec = pl.BlockSpec((tk, tm), lambda i, j, kk: (kk, i))
        b_spec = pl.BlockSpec((tk, tn), lambda i, j, kk: (kk, j))
        if sharded:
            out_spec = pl.BlockSpec((None, tm, tn), lambda i, j, kk: (split(j, per_n)[0], i, split(j, per_n)[1]))
            out_shape = jax.ShapeDtypeStruct((NSHARD, m, ns), out_dtype)
    in_specs = [a_spec, b_spec]
    args = [a, b]
    if bias is not None:
        in_specs.append(pl.BlockSpec((1, tn), lambda i, j, kk: (0, j)))
        args.append(bias)
    return pl.pallas_call(
        body, grid=(m // tm, n // tn, nk), in_specs=in_specs, out_specs=out_spec, out_shape=out_shape,
        scratch_shapes=[pltpu.VMEM((tm, tn), F32)] if nk > 1 else [],
        compiler_params=_cparams(_PAR, _PAR, _ARB), name=name)(*args)


def _mm_group(a, b, *, mode, name, out_dtype=F32, tm=512):
    t = a.shape[0]
    tm = _pick(t, (tm, 256, 128))
    nt_ = t // tm
    g = len(POOL_WINDOWS)
    gd = a.shape[1] // g
    dims = (_DIMS[mode], ((), ()))

    if mode == "tn":
        def body(a_ref, b_ref, o_ref, acc_ref):
            kk = pl.program_id(1)
            part = lax.dot_general(a_ref[...].astype(BF16), b_ref[...].astype(BF16), dims,
                                   preferred_element_type=F32)

            @pl.when(kk == 0)
            def _():
                acc_ref[...] = part

            @pl.when(kk > 0)
            def _():
                acc_ref[...] += part

            @pl.when(kk == nt_ - 1)
            def _():
                o_ref[...] = acc_ref[...]

        return pl.pallas_call(
            body, grid=(g, nt_),
            in_specs=[pl.BlockSpec((tm, gd), lambda gi, kk: (kk, gi)),
                      pl.BlockSpec((tm, gd), lambda gi, kk: (kk, gi))],
            out_specs=pl.BlockSpec((None, gd, gd), lambda gi, kk: (gi, 0, 0)),
            out_shape=jax.ShapeDtypeStruct((g, gd, gd), F32),
            scratch_shapes=[pltpu.VMEM((gd, gd), F32)],
            compiler_params=_cparams(_PAR, _ARB), name=name)(a, b)

    def body(a_ref, b_ref, o_ref):
        o_ref[...] = lax.dot_general(a_ref[...].astype(BF16), b_ref[...].astype(BF16), dims,
                                     preferred_element_type=F32).astype(out_dtype)

    return pl.pallas_call(
        body, grid=(nt_, g),
        in_specs=[pl.BlockSpec((tm, gd), lambda i, gi: (i, gi)),
                  pl.BlockSpec((None, gd, gd), lambda i, gi: (gi, 0, 0))],
        out_specs=pl.BlockSpec((tm, gd), lambda i, gi: (i, gi)),
        out_shape=jax.ShapeDtypeStruct((t, g * gd), out_dtype),
        compiler_params=_cparams(_PAR, _PAR), name=name)(a, b)


def _pre(x, gpre, sc, sh):
    xn, r = _rms(x)
    return (xn * gpre) * (1.0 + sc) + sh, xn, r


def _fwd_first(x, gpre, sc, sh, *, h_dtype, name):
    t, d = x.shape
    tm = _pick(t, (TM_ROW, 256, 128))

    def body(x_ref, gpre_ref, sc_ref, sh_ref, h_ref):
        h, _, _ = _pre(x_ref[...], gpre_ref[...], sc_ref[...], sh_ref[...])
        h_ref[...] = h.astype(h_dtype)

    return pl.pallas_call(
        body, grid=(t // tm,),
        in_specs=[_row_spec(tm, d)] + [_vec_spec(1, d)] * 3,
        out_specs=_row_spec(tm, d), out_shape=jax.ShapeDtypeStruct((t, d), h_dtype),
        compiler_params=_cparams(_PAR), name=name)(x, gpre, sc, sh)


def _fwd_mid(x, y, gpost, gt, gpre, sc, sh, *, h_dtype, name):
    t, d = x.shape
    tm = _pick(t, (TM_ROW, 256, 128))

    def body(x_ref, y_ref, gpost_ref, gt_ref, gpre_ref, sc_ref, sh_ref, xn_ref, h_ref):
        yn, _ = _rms(y_ref[...])
        x_new = x_ref[...] + gt_ref[...] * (yn * gpost_ref[...])
        xn_ref[...] = x_new
        h, _, _ = _pre(x_new, gpre_ref[...], sc_ref[...], sh_ref[...])
        h_ref[...] = h.astype(h_dtype)

    return pl.pallas_call(
        body, grid=(t // tm,),
        in_specs=[_row_spec(tm, d)] * 2 + [_vec_spec(1, d)] * 5,
        out_specs=[_row_spec(tm, d)] * 2,
        out_shape=[jax.ShapeDtypeStruct((t, d), F32), jax.ShapeDtypeStruct((t, d), h_dtype)],
        compiler_params=_cparams(_PAR), name=name)(x, y, gpost, gt, gpre, sc, sh)


def _post_bwd(dx, y, gpost, gt):
    yn, r2 = _rms(y)
    dyn = dx * (gt * gpost)
    dy = _rms_bwd(dyn, yn, r2)
    return dy, dx * yn


def _last_fwd_bwd(x, y, target, gpost, gt, *, name):
    t, d = x.shape
    tm = _pick(t, (TM_ROW, 256, 128))
    n = t // tm

    def body(x_ref, y_ref, tg_ref, gpost_ref, gt_ref, dx_ref, dy_ref, dgpost_ref, dgt_ref, sdy_ref,
             loss_ref, qa, sa, la):
        i = pl.program_id(0)

        @pl.when(i == 0)
        def _():
            qa[...] = jnp.zeros_like(qa)
            sa[...] = jnp.zeros_like(sa)
            la[...] = jnp.zeros_like(la)

        yv = y_ref[...]
        yn, r2 = _rms(yv)
        gt_v, gpost_v = gt_ref[...], gpost_ref[...]
        err = x_ref[...] + gt_v * (yn * gpost_v) - tg_ref[...]
        la[...] += _sum8(err * err)
        dx = err * (1.0 / d)
        dx_ref[...] = dx
        dy = _rms_bwd(dx * (gt_v * gpost_v), yn, r2)
        dy_ref[...] = dy.astype(dy_ref.dtype)
        qa[...] += _sum8(dx * yn)
        sa[...] += _sum8(dy)

        @pl.when(i == n - 1)
        def _():
            q = jnp.sum(qa[...], axis=0, keepdims=True)
            dgpost_ref[...] = gt_v * q
            dgt_ref[...] = gpost_v * q
            sdy_ref[...] = jnp.sum(sa[...], axis=0, keepdims=True)
            tot = jnp.sum(jnp.sum(la[...], axis=0, keepdims=True), axis=1, keepdims=True)
            loss_ref[...] = tot * (0.5 / d)

    return pl.pallas_call(
        body, grid=(n,),
        in_specs=[_row_spec(tm, d)] * 3 + [_vec_spec(1, d)] * 2,
        out_specs=[_row_spec(tm, d)] * 2 + [_vec_spec(1, d)] * 3 + [_vec_spec(1, 1)],
        out_shape=[jax.ShapeDtypeStruct((t, d), F32), jax.ShapeDtypeStruct((t, d), BF16)]
        + [jax.ShapeDtypeStruct((1, d), F32)] * 3 + [jax.ShapeDtypeStruct((1, 1), F32)],
        scratch_shapes=[pltpu.VMEM((SUBLANES, d), F32)] * 3,
        compiler_params=_cparams(_ARB), name=name)(x, y, target, gpost, gt)


def _bwd_mid(dx_new, dh, x_in, gpre, sc, y_prev, gpost_p, gt_p, *, dy_dtype, name):
    t, d = x_in.shape
    tm = _pick(t, (TM_ROW, 256, 128))
    n = t // tm

    def body(dxn_ref, dh_ref, x_ref, gpre_ref, sc_ref, y_ref, gpost_ref, gt_ref,
             dx_ref, dy_ref, dsh_ref, dsc_ref, dgpre_ref, dgpost_ref, dgt_ref, sdy_ref, a1, a2, aq, asd):
        i = pl.program_id(0)

        @pl.when(i == 0)
        def _():
            for a in (a1, a2, aq, asd):
                a[...] = jnp.zeros_like(a)

        dh_v = dh_ref[...]
        xn, r = _rms(x_ref[...])
        dx = dxn_ref[...] + _rms_bwd(dh_v * ((1.0 + sc_ref[...]) * gpre_ref[...]), xn, r)
        dx_ref[...] = dx
        a1[...] += _sum8(dh_v)
        a2[...] += _sum8(dh_v * xn)
        dy, dxyn = _post_bwd(dx, y_ref[...], gpost_ref[...], gt_ref[...])
        dy_ref[...] = dy.astype(dy_dtype)
        aq[...] += _sum8(dxyn)
        asd[...] += _sum8(dy)

        @pl.when(i == n - 1)
        def _():
            s2 = jnp.sum(a2[...], axis=0, keepdims=True)
            q = jnp.sum(aq[...], axis=0, keepdims=True)
            dsh_ref[...] = jnp.sum(a1[...], axis=0, keepdims=True)
            dsc_ref[...] = gpre_ref[...] * s2
            dgpre_ref[...] = (1.0 + sc_ref[...]) * s2
            dgpost_ref[...] = gt_ref[...] * q
            dgt_ref[...] = gpost_ref[...] * q
            sdy_ref[...] = jnp.sum(asd[...], axis=0, keepdims=True)

    return pl.pallas_call(
        body, grid=(n,),
        in_specs=[_row_spec(tm, d)] * 3 + [_vec_spec(1, d)] * 2 + [_row_spec(tm, d)] + [_vec_spec(1, d)] * 2,
        out_specs=[_row_spec(tm, d)] * 2 + [_vec_spec(1, d)] * 6,
        out_shape=[jax.ShapeDtypeStruct((t, d), F32), jax.ShapeDtypeStruct((t, d), dy_dtype)]
        + [jax.ShapeDtypeStruct((1, d), F32)] * 6,
        scratch_shapes=[pltpu.VMEM((SUBLANES, d), F32)] * 4,
        compiler_params=_cparams(_ARB), name=name)(dx_new, dh, x_in, gpre, sc, y_prev, gpost_p, gt_p)


def _bwd_first(dx_new, dh, x_in, gpre, sc, *, name):
    t, d = x_in.shape
    tm = _pick(t, (TM_ROW, 256, 128))
    n = t // tm

    def body(dxn_ref, dh_ref, x_ref, gpre_ref, sc_ref, dx_ref, dsh_ref, dsc_ref, dgpre_ref, a1, a2):
        i = pl.program_id(0)

        @pl.when(i == 0)
        def _():
            a1[...] = jnp.zeros_like(a1)
            a2[...] = jnp.zeros_like(a2)

        dh_v = dh_ref[...]
        xn, r = _rms(x_ref[...])
        dx_ref[...] = dxn_ref[...] + _rms_bwd(dh_v * ((1.0 + sc_ref[...]) * gpre_ref[...]), xn, r)
        a1[...] += _sum8(dh_v)
        a2[...] += _sum8(dh_v * xn)

        @pl.when(i == n - 1)
        def _():
            s2 = jnp.sum(a2[...], axis=0, keepdims=True)
            dsh_ref[...] = jnp.sum(a1[...], axis=0, keepdims=True)
            dsc_ref[...] = gpre_ref[...] * s2
            dgpre_ref[...] = (1.0 + sc_ref[...]) * s2

    return pl.pallas_call(
        body, grid=(n,),
        in_specs=[_row_spec(tm, d)] * 3 + [_vec_spec(1, d)] * 2,
        out_specs=[_row_spec(tm, d)] + [_vec_spec(1, d)] * 3,
        out_shape=[jax.ShapeDtypeStruct((t, d), F32)] + [jax.ShapeDtypeStruct((1, d), F32)] * 3,
        scratch_shapes=[pltpu.VMEM((SUBLANES, d), F32)] * 2,
        compiler_params=_cparams(_ARB), name=name)(dx_new, dh, x_in, gpre, sc)


def _conv3_rows(buf, w_ref, rows, first):
    out = buf[pl.ds(first, rows), :] * w_ref[pl.ds(0, 1), :]
    for k in (1, 2):
        out = out + buf[pl.ds(first + k, rows), :] * w_ref[pl.ds(k, 1), :]
    return out


ROWS_BLK = 16
COLS_BLK = 256


def _bcast_rows(dst, src_ref, first, nrows):
    for k in range(nrows):
        dst[first + k] = jnp.broadcast_to(src_ref[pl.ds(k, 1), :], dst.shape[1:])


def _conv3_blk(buf, wb, first, rows, cols):
    xs = [buf[pl.ds(first + k, rows), cols] for k in range(3)]
    out = xs[0] * wb[0, pl.ds(0, rows), cols]
    out = out + xs[1] * wb[1, pl.ds(0, rows), cols]
    out = out + xs[2] * wb[2, pl.ds(0, rows), cols]
    return out + wb[3, pl.ds(0, rows), cols], xs


def _ffn_gate_fwd(u, w, b, *, name):
    t, f2 = u.shape
    f = f2 // 2
    tm = _pick(t, (TM_CONV,))
    hb = HALO_3
    rb = ROWS_BLK
    cw = _pick(f, (COLS_BLK, LANES))

    def body(u_ref, up_ref, w_ref, b_ref, a_ref, buf, wb):
        i = pl.program_id(0)
        buf[pl.ds(hb, tm), :] = u_ref[...]
        buf[pl.ds(0, hb), :] = jnp.where(i > 0, up_ref[...], 0.0)
        _bcast_rows(wb, w_ref, 0, 3)
        _bcast_rows(wb, b_ref, 3, 1)
        for c0 in range(0, f, cw):
            for r0 in range(0, tm, rb):
                vg, _ = _conv3_blk(buf, wb, hb - 2 + r0, rb, pl.ds(c0, cw))
                vv, _ = _conv3_blk(buf, wb, hb - 2 + r0, rb, pl.ds(f + c0, cw))
                a_ref[pl.ds(r0, rb), pl.ds(c0, cw)] = (vg * _sigmoid(vg) * vv).astype(BF16)

    return pl.pallas_call(
        body, grid=(t // tm,),
        in_specs=[_row_spec(tm, f2), _prev_spec(tm, hb, f2), _vec_spec(3, f2), _vec_spec(1, f2)],
        out_specs=_row_spec(tm, f), out_shape=jax.ShapeDtypeStruct((t, f), BF16),
        scratch_shapes=[pltpu.VMEM((tm + hb, f2), F32), pltpu.VMEM((4, rb, f2), F32)],
        compiler_params=_cparams(_PAR), name=name)(u, u, w, b)


def _ffn_gate_bwd(u, da, w, b, *, name):
    t, f2 = u.shape
    f = f2 // 2
    tm = _pick(t, (TM_CONV,))
    hb = HALO_3
    n = t // tm
    rb = ROWS_BLK
    cw = _pick(f, (COLS_BLK, LANES))
    blocks = [(r0, rb) for r0 in range(0, tm, rb)] + [(tm, hb)]

    def body(u_ref, up_ref, un_ref, da_ref, dan_ref, w_ref, b_ref, du_ref, dw_ref, db_ref,
             ubuf, dvbuf, wb, wacc, bacc):
        i = pl.program_id(0)

        @pl.when(i == 0)
        def _():
            wacc[...] = jnp.zeros_like(wacc)
            bacc[...] = jnp.zeros_like(bacc)

        ubuf[pl.ds(0, hb), :] = jnp.where(i > 0, up_ref[...], 0.0)
        ubuf[pl.ds(hb, tm), :] = u_ref[...]
        ubuf[pl.ds(hb + tm, hb), :] = un_ref[...]
        _bcast_rows(wb, w_ref, 0, 3)
        _bcast_rows(wb, b_ref, 3, 1)
        for c0 in range(0, f, cw):
            gcols, vcols = pl.ds(c0, cw), pl.ds(f + c0, cw)
            for r0, rows in blocks:
                vg, xg = _conv3_blk(ubuf, wb, hb - 2 + r0, rows, gcols)
                vv, xv = _conv3_blk(ubuf, wb, hb - 2 + r0, rows, vcols)
                sg = _sigmoid(vg)
                if r0 < tm:
                    dav = da_ref[pl.ds(r0, rows), gcols]
                else:
                    dav = jnp.where(i < n - 1, dan_ref[:, gcols], 0.0)
                dvg = dav * vv * (sg * (1.0 + vg * (1.0 - sg)))
                dvv = dav * (vg * sg)
                dvbuf[pl.ds(r0, rows), gcols] = dvg
                dvbuf[pl.ds(r0, rows), vcols] = dvv
                if r0 < tm:
                    bacc[:, gcols] += _sum8(dvg)
                    bacc[:, vcols] += _sum8(dvv)
                    for k in range(3):
                        wacc[k, :, gcols] += _sum8(dvg * xg[k])
                        wacc[k, :, vcols] += _sum8(dvv * xv[k])
        for c0 in range(0, f2, cw):
            cols = pl.ds(c0, cw)
            for r0 in range(0, tm, rb):
                du = dvbuf[pl.ds(r0, rb), cols] * wb[2, :, cols]
                du = du + dvbuf[pl.ds(r0 + 1, rb), cols] * wb[1, :, cols]
                du = du + dvbuf[pl.ds(r0 + 2, rb), cols] * wb[0, :, cols]
                du_ref[pl.ds(r0, rb), cols] = du.astype(BF16)

        @pl.when(i == n - 1)
        def _():
            db_ref[...] = jnp.sum(bacc[...], axis=0, keepdims=True)
            dw_ref[...] = jnp.sum(wacc[...], axis=1)

    return pl.pallas_call(
        body, grid=(n,),
        in_specs=[_row_spec(tm, f2), _prev_spec(tm, hb, f2), _next_spec(tm, hb, f2, t),
                  _row_spec(tm, f), _next_spec(tm, hb, f, t), _vec_spec(3, f2), _vec_spec(1, f2)],
        out_specs=[_row_spec(tm, f2), _vec_spec(3, f2), _vec_spec(1, f2)],
        out_shape=[jax.ShapeDtypeStruct((t, f2), BF16), jax.ShapeDtypeStruct((3, f2), F32),
                   jax.ShapeDtypeStruct((1, f2), F32)],
        scratch_shapes=[pltpu.VMEM((tm + 2 * hb, f2), F32), pltpu.VMEM((tm + hb, f2), F32),
                        pltpu.VMEM((4, rb, f2), F32),
                        pltpu.VMEM((3, SUBLANES, f2), F32), pltpu.VMEM((SUBLANES, f2), F32)],
        compiler_params=_cparams(_ARB), name=name)(u, u, u, da, da, w, b)


def _glu(u, b1, d):
    return (u[:, :d] + b1[:, :d]) * _sigmoid(u[:, d:] + b1[:, d:])


def _fill_glu_buf(buf, u_ref, up_ref, b1_ref, i, tm, d):
    b1 = b1_ref[...]
    buf[pl.ds(0, HALO_A), :] = jnp.where(i > 0, _glu(up_ref[...], b1, d), 0.0)
    buf[pl.ds(HALO_A, tm), :] = _glu(u_ref[...], b1, d)


def _conv31(buf, w_ref, tm):
    first = HALO_A - (CONF_CONV_WIDTH - 1)
    out = buf[pl.ds(first, tm), :] * w_ref[pl.ds(0, 1), :]
    for k in range(1, CONF_CONV_WIDTH):
        out = out + buf[pl.ds(first + k, tm), :] * w_ref[pl.ds(k, 1), :]
    return out


def _layernorm_parts(x):
    mu = jnp.mean(x, axis=-1, keepdims=True)
    xc = x - mu
    rstd = lax.rsqrt(jnp.mean(xc * xc, axis=-1, keepdims=True) + LN_EPS)
    return xc * rstd, rstd


def _a_fwd(u1, b1, dww, dwb, lng, lnb, *, name):
    t, d2 = u1.shape
    d = d2 // 2
    tm = _pick(t, (TM_CONV,))

    def body(u_ref, up_ref, b1_ref, w_ref, wb_ref, g_ref, bb_ref, o_ref, buf):
        i = pl.program_id(0)
        _fill_glu_buf(buf, u_ref, up_ref, b1_ref, i, tm, d)
        u3 = _conv31(buf, w_ref, tm) + wb_ref[...]
        xhat, _ = _layernorm_parts(u3)
        u4 = xhat * g_ref[...] + bb_ref[...]
        o_ref[...] = (u4 * _sigmoid(u4)).astype(BF16)

    return pl.pallas_call(
        body, grid=(t // tm,),
        in_specs=[_row_spec(tm, d2), _prev_spec(tm, HALO_A, d2), _vec_spec(1, d2),
                  _vec_spec(CONF_CONV_WIDTH, d)] + [_vec_spec(1, d)] * 3,
        out_specs=_row_spec(tm, d), out_shape=jax.ShapeDtypeStruct((t, d), BF16),
        scratch_shapes=[pltpu.VMEM((tm + HALO_A, d), F32)],
        compiler_params=_cparams(_PAR), name=name)(u1, u1, b1, dww, dwb, lng, lnb)


def _a_bwd_norm(u1, du5, b1, dww, dwb, lng, lnb, *, name):
    t, d2 = u1.shape
    d = d2 // 2
    tm = _pick(t, (TM_CONV,))
    n = t // tm

    def body(u_ref, up_ref, du5_ref, b1_ref, w_ref, wb_ref, g_ref, bb_ref,
             du3_ref, dg_ref, db_ref, dwb_ref, buf, ag, ab, aw):
        i = pl.program_id(0)

        @pl.when(i == 0)
        def _():
            for a in (ag, ab, aw):
                a[...] = jnp.zeros_like(a)

        _fill_glu_buf(buf, u_ref, up_ref, b1_ref, i, tm, d)
        u3 = _conv31(buf, w_ref, tm) + wb_ref[...]
        xhat, rstd = _layernorm_parts(u3)
        g = g_ref[...]
        u4 = xhat * g + bb_ref[...]
        sg = _sigmoid(u4)
        du4 = du5_ref[...] * (sg * (1.0 + u4 * (1.0 - sg)))
        dxh = du4 * g
        du3 = rstd * (dxh - jnp.mean(dxh, axis=-1, keepdims=True)
                      - xhat * jnp.mean(dxh * xhat, axis=-1, keepdims=True))
        du3_ref[...] = du3
        ag[...] += _sum8(du4 * xhat)
        ab[...] += _sum8(du4)
        aw[...] += _sum8(du3)

        @pl.when(i == n - 1)
        def _():
            for a, o in ((ag, dg_ref), (ab, db_ref), (aw, dwb_ref)):
                o[...] = jnp.sum(a[...], axis=0, keepdims=True)

    return pl.pallas_call(
        body, grid=(n,),
        in_specs=[_row_spec(tm, d2), _prev_spec(tm, HALO_A, d2), _row_spec(tm, d), _vec_spec(1, d2),
                  _vec_spec(CONF_CONV_WIDTH, d)] + [_vec_spec(1, d)] * 3,
        out_specs=[_row_spec(tm, d)] + [_vec_spec(1, d)] * 3,
        out_shape=[jax.ShapeDtypeStruct((t, d), F32)] + [jax.ShapeDtypeStruct((1, d), F32)] * 3,
        scratch_shapes=[pltpu.VMEM((tm + HALO_A, d), F32)] + [pltpu.VMEM((SUBLANES, d), F32)] * 3,
        compiler_params=_cparams(_ARB), name=name)(u1, u1, du5, b1, dww, dwb, lng, lnb)


def _a_bwd_conv(u1, du3, b1, dww, *, name):
    t, d2 = u1.shape
    d = d2 // 2
    tm = _pick(t, (TM_CONV,))
    n = t // tm
    kw = CONF_CONV_WIDTH

    def body(u_ref, up_ref, g3_ref, g3n_ref, b1_ref, w_ref, du1_ref, dw_ref, db1_ref,
             buf, gbuf, wacc, bacc):
        i = pl.program_id(0)

        @pl.when(i == 0)
        def _():
            wacc[...] = jnp.zeros_like(wacc)
            bacc[...] = jnp.zeros_like(bacc)

        _fill_glu_buf(buf, u_ref, up_ref, b1_ref, i, tm, d)
        g3 = g3_ref[...]
        gbuf[pl.ds(0, tm), :] = g3
        gbuf[pl.ds(tm, HALO_A), :] = jnp.where(i < n - 1, g3n_ref[...], 0.0)
        du2 = gbuf[pl.ds(kw - 1, tm), :] * w_ref[pl.ds(0, 1), :]
        for k in range(1, kw):
            du2 = du2 + gbuf[pl.ds(kw - 1 - k, tm), :] * w_ref[pl.ds(k, 1), :]
        first = HALO_A - (kw - 1)
        for k in range(kw):
            wacc[k] += _sum8(g3 * buf[pl.ds(first + k, tm), :])
        b1 = b1_ref[...]
        uv = u_ref[...]
        av = uv[:, :d] + b1[:, :d]
        sg = _sigmoid(uv[:, d:] + b1[:, d:])
        da = du2 * sg
        dg = du2 * av * (sg * (1.0 - sg))
        du1_ref[:, pl.ds(0, d)] = da.astype(BF16)
        du1_ref[:, pl.ds(d, d)] = dg.astype(BF16)
        bacc[:, pl.ds(0, d)] += _sum8(da)
        bacc[:, pl.ds(d, d)] += _sum8(dg)

        @pl.when(i == n - 1)
        def _():
            dw_ref[...] = jnp.sum(wacc[...], axis=1)
            db1_ref[...] = jnp.sum(bacc[...], axis=0, keepdims=True)

    return pl.pallas_call(
        body, grid=(n,),
        in_specs=[_row_spec(tm, d2), _prev_spec(tm, HALO_A, d2), _row_spec(tm, d),
                  _next_spec(tm, HALO_A, d, t), _vec_spec(1, d2), _vec_spec(kw, d)],
        out_specs=[_row_spec(tm, d2), _vec_spec(kw, d), _vec_spec(1, d2)],
        out_shape=[jax.ShapeDtypeStruct((t, d2), BF16), jax.ShapeDtypeStruct((kw, d), F32),
                   jax.ShapeDtypeStruct((1, d2), F32)],
        scratch_shapes=[pltpu.VMEM((tm + HALO_A, d), F32), pltpu.VMEM((tm + HALO_A, d), F32),
                        pltpu.VMEM((kw, SUBLANES, d), F32), pltpu.VMEM((SUBLANES, d2), F32)],
        compiler_params=_cparams(_ARB), name=name)(u1, u1, du3, du3, b1, dww)


def _pool_counts(i, tm, w):
    pos = (i * tm + lax.broadcasted_iota(jnp.int32, (tm, 1), 0) + 1).astype(F32)
    return jnp.minimum(pos, float(w))


def _b_pool_fwd(h, *, name):
    t, d = h.shape
    gd = d // len(POOL_WINDOWS)
    tm = _pick(t, (TM_CONV,))
    hb = HALO_POOL

    def body(h_ref, hp_ref, o_ref, buf):
        i = pl.program_id(0)
        buf[pl.ds(0, hb), :] = jnp.where(i > 0, hp_ref[...], 0.0)
        buf[pl.ds(hb, tm), :] = h_ref[...]
        for g, w in enumerate(POOL_WINDOWS):
            cols = pl.ds(g * gd, gd)
            cur = buf[pl.ds(hb, tm), cols]
            s = cur
            for j in range(1, w):
                s = s + buf[pl.ds(hb - j, tm), cols]
            o_ref[:, cols] = (s / _pool_counts(i, tm, w) - cur).astype(BF16)

    return pl.pallas_call(
        body, grid=(t // tm,),
        in_specs=[_row_spec(tm, d), _prev_spec(tm, hb, d)],
        out_specs=_row_spec(tm, d), out_shape=jax.ShapeDtypeStruct((t, d), BF16),
        scratch_shapes=[pltpu.VMEM((tm + hb, d), F32)],
        compiler_params=_cparams(_PAR), name=name)(h, h)


def _b_pool_bwd(dp, *, name):
    t, d = dp.shape
    gd = d // len(POOL_WINDOWS)
    tm = _pick(t, (TM_CONV,))
    hb = HALO_POOL
    n = t // tm

    def body(dp_ref, dpn_ref, o_ref, buf):
        i = pl.program_id(0)
        for g, w in enumerate(POOL_WINDOWS):
            cols = pl.ds(g * gd, gd)
            buf[pl.ds(0, tm), cols] = dp_ref[:, cols] / _pool_counts(i, tm, w)
            buf[pl.ds(tm, hb), cols] = jnp.where(i < n - 1, dpn_ref[:, cols] * (1.0 / w), 0.0)
            s = buf[pl.ds(0, tm), cols]
            for j in range(1, w):
                s = s + buf[pl.ds(j, tm), cols]
            o_ref[:, cols] = s - dp_ref[:, cols]

    return pl.pallas_call(
        body, grid=(n,),
        in_specs=[_row_spec(tm, d), _next_spec(tm, hb, d, t)],
        out_specs=_row_spec(tm, d), out_shape=jax.ShapeDtypeStruct((t, d), F32),
        scratch_shapes=[pltpu.VMEM((tm + hb, d), F32)],
        compiler_params=_cparams(_PAR), name=name)(dp, dp)


def _b_affine_fwd(mixed, gb, scale, *, name):
    t, d = mixed.shape
    tm = _pick(t, (TM_ROW, 256, 128))

    def body(m_ref, gb_ref, s_ref, o_ref):
        o_ref[...] = (m_ref[...] + gb_ref[...]) * s_ref[...]

    return pl.pallas_call(
        body, grid=(t // tm,), in_specs=[_row_spec(tm, d)] + [_vec_spec(1, d)] * 2,
        out_specs=_row_spec(tm, d), out_shape=jax.ShapeDtypeStruct((t, d), F32),
        compiler_params=_cparams(_PAR), name=name)(mixed, gb, scale)


def _b_affine_bwd(dy, mixed, gb, scale, *, name):
    t, d = mixed.shape
    tm = _pick(t, (TM_ROW, 256, 128))
    n = t // tm

    def body(dy_ref, m_ref, gb_ref, s_ref, dm_ref, ds_ref, dgb_ref, a1, a2):
        i = pl.program_id(0)

        @pl.when(i == 0)
        def _():
            a1[...] = jnp.zeros_like(a1)
            a2[...] = jnp.zeros_like(a2)

        dy_v = dy_ref[...]
        dm_ref[...] = (dy_v * s_ref[...]).astype(BF16)
        a1[...] += _sum8(dy_v * (m_ref[...] + gb_ref[...]))
        a2[...] += _sum8(dy_v)

        @pl.when(i == n - 1)
        def _():
            ds_ref[...] = jnp.sum(a1[...], axis=0, keepdims=True)
            dgb_ref[...] = jnp.sum(a2[...], axis=0, keepdims=True) * s_ref[...]

    return pl.pallas_call(
        body, grid=(n,), in_specs=[_row_spec(tm, d)] * 2 + [_vec_spec(1, d)] * 2,
        out_specs=[_row_spec(tm, d)] + [_vec_spec(1, d)] * 2,
        out_shape=[jax.ShapeDtypeStruct((t, d), BF16)] + [jax.ShapeDtypeStruct((1, d), F32)] * 2,
        scratch_shapes=[pltpu.VMEM((SUBLANES, d), F32)] * 2,
        compiler_params=_cparams(_ARB), name=name)(dy, mixed, gb, scale)


def _c_gate_fwd(bcx, wc, *, name):
    t, d3 = bcx.shape
    d = d3 // 3
    tm = _pick(t, (TM_CONV,))
    hb = HALO_3

    def body(x_ref, xp_ref, w_ref, z_ref, buf):
        i = pl.program_id(0)
        xv, xp = x_ref[...], xp_ref[...]
        buf[pl.ds(0, hb), :] = jnp.where(i > 0, xp[:, d:2 * d] * xp[:, 2 * d:], 0.0)
        buf[pl.ds(hb, tm), :] = xv[:, d:2 * d] * xv[:, 2 * d:]
        z_ref[...] = (xv[:, :d] * _conv3_rows(buf, w_ref, tm, hb - 2)).astype(BF16)

    return pl.pallas_call(
        body, grid=(t // tm,),
        in_specs=[_row_spec(tm, d3), _prev_spec(tm, hb, d3), _vec_spec(3, d)],
        out_specs=_row_spec(tm, d), out_shape=jax.ShapeDtypeStruct((t, d), BF16),
        scratch_shapes=[pltpu.VMEM((tm + hb, d), F32)],
        compiler_params=_cparams(_PAR), name=name)(bcx, bcx, wc)


def _c_gate_bwd(bcx, dz, wc, *, name):
    t, d3 = bcx.shape
    d = d3 // 3
    tm = _pick(t, (TM_CONV,))
    hb = HALO_3
    n = t // tm

    def body(x_ref, xp_ref, xn_ref, dz_ref, dzn_ref, w_ref, o_ref, dw_ref, pbuf, qbuf, wacc):
        i = pl.program_id(0)

        @pl.when(i == 0)
        def _():
            wacc[...] = jnp.zeros_like(wacc)

        xv, xp, xnx = x_ref[...], xp_ref[...], xn_ref[...]
        gbv, gcv, vv = xv[:, :d], xv[:, d:2 * d], xv[:, 2 * d:]
        pbuf[pl.ds(0, hb), :] = jnp.where(i > 0, xp[:, d:2 * d] * xp[:, 2 * d:], 0.0)
        pbuf[pl.ds(hb, tm), :] = gcv * vv
        q = _conv3_rows(pbuf, w_ref, tm, hb - 2)
        dz_v = dz_ref[...]
        dq = dz_v * gbv
        qbuf[pl.ds(0, tm), :] = dq
        qbuf[pl.ds(tm, hb), :] = jnp.where(i < n - 1, dzn_ref[...] * xnx[:, :d], 0.0)
        dp = qbuf[pl.ds(0, tm), :] * w_ref[pl.ds(2, 1), :]
        dp = dp + qbuf[pl.ds(1, tm), :] * w_ref[pl.ds(1, 1), :]
        dp = dp + qbuf[pl.ds(2, tm), :] * w_ref[pl.ds(0, 1), :]
        o_ref[:, pl.ds(0, d)] = (dz_v * q).astype(BF16)
        o_ref[:, pl.ds(d, d)] = (dp * vv).astype(BF16)
        o_ref[:, pl.ds(2 * d, d)] = (dp * gcv).astype(BF16)
        for k in range(3):
            wacc[k] += _sum8(dq * pbuf[pl.ds(hb - 2 + k, tm), :])

        @pl.when(i == n - 1)
        def _():
            dw_ref[...] = jnp.sum(wacc[...], axis=1)

    return pl.pallas_call(
        body, grid=(n,),
        in_specs=[_row_spec(tm, d3), _prev_spec(tm, hb, d3), _next_spec(tm, hb, d3, t),
                  _row_spec(tm, d), _next_spec(tm, hb, d, t), _vec_spec(3, d)],
        out_specs=[_row_spec(tm, d3), _vec_spec(3, d)],
        out_shape=[jax.ShapeDtypeStruct((t, d3), BF16), jax.ShapeDtypeStruct((3, d), F32)],
        scratch_shapes=[pltpu.VMEM((tm + hb, d), F32), pltpu.VMEM((tm + hb, d), F32),
                        pltpu.VMEM((3, SUBLANES, d), F32)],
        compiler_params=_cparams(_ARB), name=name)(bcx, bcx, bcx, dz, dz, wc)


def _row(v):
    return v.reshape(1, -1)


def _kind_of(j):
    return "f" if j % 2 else "abc"[(j // 2) % N_MIXERS]


BIG = ("a_pw1_w", "a_pw2_w", "b_group_w", "c_in_w", "c_out_w", "f_up_w", "f_down_w")
COL_SHARDED = ("a_pw1_w", "c_in_w", "f_up_w")


def _local_step(x, target, mod, p):
    nsub = 2 * DEPTH
    norm_names = (("norm_pre_mix", "norm_post_mix"), ("norm_pre_ffn", "norm_post_ffn"))
    gpre = [_row(p[norm_names[s][0]][i]) for i in range(DEPTH) for s in (0, 1)]
    gpost = [_row(p[norm_names[s][1]][i]) for i in range(DEPTH) for s in (0, 1)]
    sh = [_row(mod[i, 3 * s + 0]) for i in range(DEPTH) for s in (0, 1)]
    sc = [_row(mod[i, 3 * s + 1]) for i in range(DEPTH) for s in (0, 1)]
    gt = [_row(mod[i, 3 * s + 2]) for i in range(DEPTH) for s in (0, 1)]

    def h_dtype(j):
        return F32 if _kind_of(j) == "b" else BF16

    xs, hs, ys, saved = [x], [], [], []

    hs.append(_fwd_first(x, gpre[0], sc[0], sh[0], h_dtype=h_dtype(0), name="fwd_first"))
    for j in range(nsub):
        i, kind = j // 2, _kind_of(j)
        slot = i // N_MIXERS
        h = hs[j]
        tag = f"{kind}{j}"
        if kind == "f":
            u = _mm(h, p["f_up_w"], mode="nn", layer=i, name=f"ffn_up_{tag}")
            a = _ffn_gate_fwd(u, p["f_dw_w"][i], _row(p["f_dw_b"][i]), name=f"ffn_gate_{tag}")
            y = _mm(a, p["f_down_w"][i], mode="nn", name=f"ffn_down_{tag}")
            saved.append((u, a))
        elif kind == "a":
            u1 = _mm(h, p["a_pw1_w"], mode="nn", layer=slot, name=f"a_pw1_{tag}")
            u5 = _a_fwd(u1, _row(p["a_pw1_b"][slot]), p["a_dw_w"][slot], _row(p["a_dw_b"][slot]),
                        _row(p["a_ln_g"][slot]), _row(p["a_ln_b"][slot]), name=f"a_conv_{tag}")
            y = _mm(u5, p["a_pw2_w"][slot], mode="nn", bias=_row(p["a_pw2_b"][slot]), name=f"a_pw2_{tag}")
            saved.append((u1, u5))
        elif kind == "b":
            pooled = _b_pool_fwd(h, name=f"b_pool_{tag}")
            mixed = _mm_group(pooled, p["b_group_w"][slot], mode="nn", name=f"b_mix_{tag}")
            y = _b_affine_fwd(mixed, _row(p["b_group_b"][slot]), _row(p["b_scale"][slot]), name=f"b_aff_{tag}")
            saved.append((pooled, mixed))
        else:
            bcx = _mm(h, p["c_in_w"], mode="nn", layer=slot, name=f"c_in_{tag}")
            z = _c_gate_fwd(bcx, p["c_conv_w"][slot], name=f"c_gate_{tag}")
            y = _mm(z, p["c_out_w"][slot], mode="nn", name=f"c_out_{tag}")
            saved.append((bcx, z))
        ys.append(y)
        if j + 1 < nsub:
            x_new, h_next = _fwd_mid(xs[j], y, gpost[j], gt[j], gpre[j + 1], sc[j + 1], sh[j + 1],
                                     h_dtype=h_dtype(j + 1), name=f"fwd_mid_{j}")
            xs.append(x_new)
            hs.append(h_next)

    n_of = {"a": len([i for i in range(DEPTH) if i % N_MIXERS == 0]),
            "b": len([i for i in range(DEPTH) if i % N_MIXERS == 1]),
            "c": len([i for i in range(DEPTH) if i % N_MIXERS == 2]), "f": DEPTH, "n": DEPTH}
    g = {k: [None] * n_of[k[0]] for k in p}
    dmod = [[None] * 6 for _ in range(DEPTH)]

    last = nsub - 1
    dx, dy, dgpost, dgt, sdy, loss = _last_fwd_bwd(xs[last], ys[last], target, gpost[last], gt[last],
                                                   name="loss_head")
    for j in range(last, -1, -1):
        i, kind = j // 2, _kind_of(j)
        slot = i // N_MIXERS
        sub = j % 2
        tag = f"{kind}{j}"
        g[norm_names[sub][1]][i] = dgpost
        dmod[i][3 * sub + 2] = dgt
        h = hs[j]
        if kind == "f":
            u, a = saved[j]
            da = _mm(dy, p["f_down_w"][i], mode="nt", name=f"ffn_dda_{tag}")
            g["f_down_w"][i] = _mm(a, dy, mode="tn", tk=TK_TOKENS, name=f"ffn_dwdown_{tag}")
            du, dw, db = _ffn_gate_bwd(u, da, p["f_dw_w"][i], _row(p["f_dw_b"][i]), name=f"ffn_gate_bwd_{tag}")
            g["f_dw_w"][i], g["f_dw_b"][i] = dw, db
            dh = _mm(du, p["f_up_w"], mode="nt", layer=i, name=f"ffn_ddh_{tag}")
            g["f_up_w"][i] = _mm(h, du, mode="tn", tk=TK_TOKENS, layer=i, name=f"ffn_dwup_{tag}")
        elif kind == "a":
            u1, u5 = saved[j]
            g["a_pw2_b"][slot] = sdy
            du5 = _mm(dy, p["a_pw2_w"][slot], mode="nt", name=f"a_ddu5_{tag}")
            g["a_pw2_w"][slot] = _mm(u5, dy, mode="tn", tk=TK_TOKENS, name=f"a_dw2_{tag}")
            b1 = _row(p["a_pw1_b"][slot])
            du3, dlg, dlb, ddwb = _a_bwd_norm(u1, du5, b1, p["a_dw_w"][slot], _row(p["a_dw_b"][slot]),
                                              _row(p["a_ln_g"][slot]), _row(p["a_ln_b"][slot]),
                                              name=f"a_bwd_norm_{tag}")
            g["a_ln_g"][slot], g["a_ln_b"][slot], g["a_dw_b"][slot] = dlg, dlb, ddwb
            du1, ddww, db1 = _a_bwd_conv(u1, du3, b1, p["a_dw_w"][slot], name=f"a_bwd_conv_{tag}")
            g["a_dw_w"][slot], g["a_pw1_b"][slot] = ddww, db1
            dh = _mm(du1, p["a_pw1_w"], mode="nt", layer=slot, name=f"a_ddh_{tag}")
            g["a_pw1_w"][slot] = _mm(h, du1, mode="tn", tk=TK_TOKENS, layer=slot, name=f"a_dw1_{tag}")
        elif kind == "b":
            pooled, mixed = saved[j]
            dmixed, dscale, dgb = _b_affine_bwd(dy, mixed, _row(p["b_group_b"][slot]), _row(p["b_scale"][slot]),
                                                name=f"b_aff_bwd_{tag}")
            g["b_scale"][slot], g["b_group_b"][slot] = dscale, dgb
            dpooled = _mm_group(dmixed, p["b_group_w"][slot], mode="nt", name=f"b_dpool_{tag}")
            g["b_group_w"][slot] = _mm_group(pooled, dmixed, mode="tn", tm=TK_TOKENS, name=f"b_dw_{tag}")
            dh = _b_pool_bwd(dpooled, name=f"b_pool_bwd_{tag}")
        else:
            bcx, z = saved[j]
            dz = _mm(dy, p["c_out_w"][slot], mode="nt", name=f"c_ddz_{tag}")
            g["c_out_w"][slot] = _mm(z, dy, mode="tn", tk=TK_TOKENS, name=f"c_dwout_{tag}")
            dbcx, dwc = _c_gate_bwd(bcx, dz, p["c_conv_w"][slot], name=f"c_gate_bwd_{tag}")
            g["c_conv_w"][slot] = dwc
            dh = _mm(dbcx, p["c_in_w"], mode="nt", layer=slot, name=f"c_ddh_{tag}")
            g["c_in_w"][slot] = _mm(h, dbcx, mode="tn", tk=TK_TOKENS, layer=slot, name=f"c_dwin_{tag}")
        if j > 0:
            pj = j - 1
            dy_dtype = F32 if _kind_of(pj) == "b" else BF16
            dx, dy, dsh, dsc, dgpre, dgpost, dgt, sdy = _bwd_mid(
                dx, dh, xs[j], gpre[j], sc[j], ys[pj], gpost[pj], gt[pj], dy_dtype=dy_dtype, name=f"bwd_mid_{j}")
        else:
            dx, dsh, dsc, dgpre = _bwd_first(dx, dh, xs[0], gpre[0], sc[0], name="bwd_first")
        dmod[i][3 * sub + 0] = dsh
        dmod[i][3 * sub + 1] = dsc
        g[norm_names[sub][0]][i] = dgpre

    small = {k: jnp.stack(v).reshape(p[k].shape) for k, v in g.items() if k not in BIG}
    big = {k: v for k, v in g.items() if k in BIG}
    dmod_arr = jnp.stack([jnp.concatenate(r, axis=0) for r in dmod])
    return loss, dx, dmod_arr, small, big


_MESH = pl.DeviceIdType.MESH
_ANY = pl.BlockSpec(memory_space=pl.ANY)
_VMEM = pl.BlockSpec(memory_space=pltpu.VMEM)


def _place():
    return lax.axis_index("x"), lax.axis_index("y"), lax.axis_index("c")


def _other_chips(x, y):
    return [(1 - x, y), (x, 1 - y), (1 - x, 1 - y)]


def _remote(src, dst, send_sem, recv_sem, to):
    return pltpu.make_async_remote_copy(src_ref=src, dst_ref=dst, send_sem=send_sem, recv_sem=recv_sem,
                                        device_id=to, device_id_type=_MESH)


def _all_gather8(blk, *, name):
    r, cdim = blk.shape

    def body(x_ref, out_ref, send_sems, recv_sems, local_sem):
        x, y, c = _place()
        me, sibling = (x, y, c), (x, y, 1 - c)
        chips = _other_chips(x, y)

        def slot(px, py, pc):
            return out_ref.at[4 * px + 2 * py + pc]

        def copy(k, block, to, src=None):
            return _remote(slot(*block) if src is None else src, slot(*block),
                           send_sems.at[k], recv_sems.at[k], to)

        mine = pltpu.make_async_copy(x_ref, slot(*me), local_sem)
        mine.start()
        first = [copy(0, me, sibling, src=x_ref)]
        first += [copy(1 + j, me, (*chip, c), src=x_ref) for j, chip in enumerate(chips)]
        for cp in first:
            cp.start()
        passed = [copy(4 + j, (*chip, c), sibling) for j, chip in enumerate(chips)]
        for j, chip in enumerate(chips):
            copy(1 + j, (*chip, c), me).wait_recv()
            passed[j].start()
        copy(0, sibling, me).wait_recv()
        for j, chip in enumerate(chips):
            copy(4 + j, (*chip, 1 - c), me).wait_recv()
        for cp in first + passed:
            cp.wait_send()
        mine.wait()

    return pl.pallas_call(
        body, out_shape=jax.ShapeDtypeStruct((NDEV, r, cdim), blk.dtype),
        in_specs=[_VMEM], out_specs=_VMEM,
        scratch_shapes=[pltpu.SemaphoreType.DMA((7,)), pltpu.SemaphoreType.DMA((7,)), pltpu.SemaphoreType.DMA],
        compiler_params=pltpu.CompilerParams(vmem_limit_bytes=VMEM_LIMIT), name=name)(blk)


def _gather_dst(kind):
    if kind == "col":
        return lambda s, h: (s, h)
    if kind == "row":
        return lambda s, h: (h, slice(None), s)
    return lambda s, h: (slice(None), s, h)


def _cast_into_gathered(src, kind, out_shape, shard, *, name):
    _, a, rh, cdim = src.shape
    tr = _pick(rh, (512, 256, 128, 64, 32, 16))
    if kind == "col":
        out_idx = lambda h, ai, r, s: (s[0], h, ai, r, 0)
    elif kind == "row":
        out_idx = lambda h, ai, r, s: (h, ai, s[0], r, 0)
    else:
        out_idx = lambda h, ai, r, s: (ai, s[0], h, r, 0)

    def body(s_ref, x_ref, o_ref):
        o_ref[...] = x_ref[...].astype(BF16)

    grid_spec = pltpu.PrefetchScalarGridSpec(
        num_scalar_prefetch=1, grid=(2, a, rh // tr),
        in_specs=[pl.BlockSpec((None, None, tr, cdim), lambda h, ai, r, s: (h, ai, r, 0))],
        out_specs=pl.BlockSpec((None, None, None, tr, cdim), out_idx))
    return pl.pallas_call(
        body, grid_spec=grid_spec, out_shape=jax.ShapeDtypeStruct(out_shape, BF16),
        compiler_params=_cparams(_PAR, _PAR, _PAR), name=name)(shard, src)


def _gather_weights(bufs, kinds, *, name):
    nt = len(bufs)

    def body(*refs):
        out_refs = refs[nt:2 * nt]
        send_sems, recv_sems = refs[2 * nt:]
        x, y, c = _place()
        sibling = (x, y, 1 - c)
        chips = _other_chips(x, y)
        s_me = 2 * x + y

        def at(k, s, h):
            return out_refs[k].at[_gather_dst(kinds[k])(s, h)]

        sends, passed = [], []
        for k in range(nt):
            for j, chip in enumerate(chips):
                cp = _remote(at(k, s_me, c), at(k, s_me, c), send_sems.at[k, j], recv_sems.at[k, j], (*chip, c))
                cp.start()
                sends.append(cp)
        for k in range(nt):
            for j, (px, py) in enumerate(chips):
                got = at(k, 2 * px + py, c)
                _remote(got, got, send_sems.at[k, j], recv_sems.at[k, j], (px, py, c)).wait_recv()
                cp = _remote(got, got, send_sems.at[k, 3 + j], recv_sems.at[k, 3 + j], sibling)
                cp.start()
                passed.append(cp)
        for k in range(nt):
            for j, (px, py) in enumerate(chips):
                got = at(k, 2 * px + py, 1 - c)
                _remote(got, got, send_sems.at[k, 3 + j], recv_sems.at[k, 3 + j], sibling).wait_recv()
        for cp in sends + passed:
            cp.wait_send()

    return pl.pallas_call(
        body, out_shape=[jax.ShapeDtypeStruct(b.shape, BF16) for b in bufs],
        in_specs=[_ANY] * nt, out_specs=[_ANY] * nt, input_output_aliases={k: k for k in range(nt)},
        scratch_shapes=[pltpu.SemaphoreType.DMA((nt, 6)), pltpu.SemaphoreType.DMA((nt, 6))],
        name=name)(*bufs)


def _pair_exchange(gs, layers_of, *, name):
    n, nt = len(gs), len(layers_of)

    def body(*refs):
        g_refs, out_refs = refs[:n], refs[n:n + nt]
        send_sems, recv_sems = refs[n + nt:]
        x, y, c = _place()
        sibling = (x, y, 1 - c)
        copies = []
        for t, ks in enumerate(layers_of):
            for l, k in enumerate(ks):
                cp = _remote(g_refs[k].at[:, 1 - c], out_refs[t].at[l], send_sems.at[k], recv_sems.at[k], sibling)
                cp.start()
                copies.append(cp)
        for cp in copies:
            cp.wait()

    out_shape = [jax.ShapeDtypeStruct((len(ks), NSHARD) + gs[ks[0]].shape[2:], F32) for ks in layers_of]
    return pl.pallas_call(
        body, out_shape=out_shape, in_specs=[_ANY] * n, out_specs=[_ANY] * nt,
        scratch_shapes=[pltpu.SemaphoreType.DMA((n,)), pltpu.SemaphoreType.DMA((n,))],
        name=name)(*gs)


def _pair_sum(g, r1, s_acc, layer, half, *, name):
    _, _, rh, cdim = g.shape
    tr = _pick(rh, (256, 128, 176, 64, 32, 16))

    def body(half_ref, g_ref, r_ref, s_in_ref, o_ref):
        o_ref[...] = (g_ref[...] + r_ref[...]).astype(BF16)

    grid_spec = pltpu.PrefetchScalarGridSpec(
        num_scalar_prefetch=1, grid=(NSHARD, rh // tr),
        in_specs=[pl.BlockSpec((None, None, tr, cdim), lambda s, r, hf: (s, hf[0], r, 0)),
                  pl.BlockSpec((None, None, tr, cdim), lambda s, r, hf: (layer, s, r, 0)),
                  _ANY],
        out_specs=pl.BlockSpec((None, None, tr, cdim), lambda s, r, hf: (layer, s, r, 0)))
    return pl.pallas_call(
        body, grid_spec=grid_spec, out_shape=jax.ShapeDtypeStruct(s_acc.shape, BF16),
        input_output_aliases={3: 0},
        compiler_params=_cparams(_PAR, _PAR), name=name)(half, g, r1, s_acc)


def _chip_exchange(ss, *, name):
    nt = len(ss)

    def body(*refs):
        s_refs, out_refs = refs[:nt], refs[nt:2 * nt]
        send_sems, recv_sems = refs[2 * nt:]
        x, y, c = _place()
        copies = []
        for t in range(nt):
            for j, (px, py) in enumerate(_other_chips(x, y)):
                cp = _remote(s_refs[t].at[:, 2 * px + py], out_refs[t].at[j],
                             send_sems.at[t, j], recv_sems.at[t, j], (px, py, c))
                cp.start()
                copies.append(cp)
        for cp in copies:
            cp.wait()

    out_shape = [jax.ShapeDtypeStruct((3, s.shape[0]) + s.shape[2:], BF16) for s in ss]
    return pl.pallas_call(
        body, out_shape=out_shape, in_specs=[_ANY] * nt, out_specs=[_ANY] * nt,
        scratch_shapes=[pltpu.SemaphoreType.DMA((nt, 3)), pltpu.SemaphoreType.DMA((nt, 3))],
        name=name)(*ss)


def _chip_sum(s_t, r3_t, place, *, name):
    nl, _, rh, cdim = s_t.shape
    tr = _pick(rh, (256, 128, 176, 64, 32, 16))

    def body(pz, s_ref, r_ref, o_ref):
        acc = s_ref[...].astype(F32) + r_ref[0].astype(F32)
        o_ref[...] = (acc + r_ref[1].astype(F32)) + r_ref[2].astype(F32)

    grid_spec = pltpu.PrefetchScalarGridSpec(
        num_scalar_prefetch=1, grid=(nl, rh // tr),
        in_specs=[pl.BlockSpec((None, None, tr, cdim), lambda l, r, pz: (l, pz[0], r, 0)),
                  pl.BlockSpec((3, None, tr, cdim), lambda l, r, pz: (0, l, r, 0))],
        out_specs=pl.BlockSpec((None, None, tr, cdim), lambda l, r, pz: (l, pz[1], r, 0)))
    return pl.pallas_call(
        body, grid_spec=grid_spec, out_shape=jax.ShapeDtypeStruct((nl, 2, rh, cdim), F32),
        compiler_params=_cparams(_PAR, _PAR), name=name)(place, s_t, r3_t)


def _join_halves(reds, *, name):
    nt = len(reds)

    def body(*refs):
        out_refs = refs[nt:2 * nt]
        send_sems, recv_sems = refs[2 * nt:]
        x, y, c = _place()
        sibling = (x, y, 1 - c)
        copies = []
        for t in range(nt):
            cp = _remote(out_refs[t].at[:, c], out_refs[t].at[:, c], send_sems.at[t], recv_sems.at[t], sibling)
            cp.start()
            copies.append(cp)
        for t, cp in enumerate(copies):
            cp.wait_send()
            got = out_refs[t].at[:, 1 - c]
            _remote(got, got, send_sems.at[t], recv_sems.at[t], sibling).wait_recv()

    return pl.pallas_call(
        body, out_shape=[jax.ShapeDtypeStruct(r.shape, F32) for r in reds],
        in_specs=[_ANY] * nt, out_specs=[_ANY] * nt, input_output_aliases={t: t for t in range(nt)},
        scratch_shapes=[pltpu.SemaphoreType.DMA((nt,)), pltpu.SemaphoreType.DMA((nt,))],
        name=name)(*reds)


def _sum_devices(g, *, name):
    _, r, cdim = g.shape
    tr = _pick(r, (512, 256, 128, 64, 32, 16, 8))

    def body(g_ref, o_ref):
        acc = g_ref[0]
        for e in range(1, NDEV):
            acc = acc + g_ref[e]
        o_ref[...] = acc

    return pl.pallas_call(
        body, grid=(r // tr,), in_specs=[pl.BlockSpec((NDEV, tr, cdim), lambda i: (0, i, 0))],
        out_specs=pl.BlockSpec((tr, cdim), lambda i: (i, 0)),
        out_shape=jax.ShapeDtypeStruct((r, cdim), F32),
        compiler_params=_cparams(_PAR), name=name)(g)


def _mod_fwd(c_all, mod_w, mod_b_cols, *, name):
    nl, d, n = mod_w.shape
    ne = c_all.shape[0]
    tn = _pick(n, (768, 512, 384, 256, 128))

    def body(c_ref, w_ref, b_ref, o_ref):
        cv = c_ref[...]
        act = (cv * _sigmoid(cv)).astype(BF16)
        o_ref[...] = jnp.dot(act, w_ref[...].astype(BF16), preferred_element_type=F32) + b_ref[...]

    return pl.pallas_call(
        body, grid=(nl, n // tn),
        in_specs=[pl.BlockSpec((ne, d), lambda i, j: (0, 0)),
                  pl.BlockSpec((None, d, tn), lambda i, j: (i, 0, j)),
                  pl.BlockSpec((None, 1, tn), lambda i, j: (i, 0, j))],
        out_specs=pl.BlockSpec((None, ne, tn), lambda i, j: (i, 0, j)),
        out_shape=jax.ShapeDtypeStruct((nl, ne, n), F32),
        compiler_params=_cparams(_PAR, _PAR), name=name)(c_all, mod_w, mod_b_cols)


def _adam_math(w, g, m, v):
    m2 = ADAM_B1 * m + (1.0 - ADAM_B1) * g
    v2 = ADAM_B2 * v + (1.0 - ADAM_B2) * (g * g)
    m_hat = m2 / (1.0 - ADAM_B1 ** ADAM_STEP)
    v_hat = v2 / (1.0 - ADAM_B2 ** ADAM_STEP)
    delta = -ADAM_LR * (m_hat / (jnp.sqrt(v_hat) + ADAM_EPS) + ADAM_WD * w)
    return delta, m2, v2


def _adamw(w, g, m, v, *, name):
    rows, cdim = w.shape
    tr = _pick(rows, tuple(t for t in (512, 256, 128, 64, 32, 16, 8) if t * cdim <= 256 * 1024))

    def body(w_ref, g_ref, m_ref, v_ref, d_ref, mo_ref, vo_ref):
        d_ref[...], mo_ref[...], vo_ref[...] = _adam_math(w_ref[...], g_ref[...], m_ref[...], v_ref[...])

    spec = pl.BlockSpec((tr, cdim), lambda i: (i, 0))
    return pl.pallas_call(
        body, grid=(rows // tr,), in_specs=[spec] * 4, out_specs=[spec] * 3,
        out_shape=[jax.ShapeDtypeStruct((rows, cdim), F32)] * 3,
        compiler_params=_cparams(_PAR), name=name)(w, g, m, v)


def _mod_w_update(c_t, dmod, w, m, v, *, name):
    nl, d, n = w.shape
    ne = c_t.shape[1]
    tr = _pick(d, (128, 64, 32, 16, 8))

    def body(c_ref, dm_ref, w_ref, m_ref, v_ref, g_ref, d_ref, mo_ref, vo_ref):
        cv = c_ref[...]
        act = cv * _sigmoid(cv)
        dm = dm_ref[...]
        g = act[:, 0:1] * dm[0:1, :]
        for e in range(1, ne):
            g = g + act[:, e:e + 1] * dm[e:e + 1, :]
        g_ref[...] = g
        d_ref[...], mo_ref[...], vo_ref[...] = _adam_math(w_ref[...], g, m_ref[...], v_ref[...])

    big = pl.BlockSpec((None, tr, n), lambda i, r: (i, r, 0))
    return pl.pallas_call(
        body, grid=(nl, d // tr),
        in_specs=[pl.BlockSpec((tr, ne), lambda i, r: (r, 0)),
                  pl.BlockSpec((None, ne, n), lambda i, r: (i, 0, 0)), big, big, big],
        out_specs=[big] * 4, out_shape=[jax.ShapeDtypeStruct((nl, d, n), F32)] * 4,
        compiler_params=_cparams(_PAR, _PAR), name=name)(c_t, dmod, w, m, v)


def _pack(arrs):
    flat = jnp.concatenate([a.reshape(-1) for a in arrs])
    tile = SUBLANES * LANES
    pad = (-flat.shape[0]) % tile
    return jnp.pad(flat, (0, pad)).reshape(-1, LANES)


def _unpack(packed, shapes, lead=()):
    flat = packed.reshape(lead + (-1,))
    out, off = [], 0
    for shp in shapes:
        size = 1
        for s in shp:
            size *= s
        out.append(flat[..., off:off + size].reshape(lead + tuple(shp)))
        off += size
    return out


SMALL_SHARD_AXIS = {"a_pw1_b": 1, "a_dw_w": 2, "a_dw_b": 1, "a_ln_g": 1, "a_ln_b": 1, "a_pw2_b": 1,
                    "c_conv_w": 2, "f_dw_w": 2}
SMALL_REPLICATED = ("norm_pre_mix", "norm_post_mix", "norm_pre_ffn", "norm_post_ffn",
                    "b_group_b", "b_scale", "f_dw_b")
WEIGHT_ORDER = ("mod_w", "mod_b", "norm_pre_mix", "norm_post_mix", "norm_pre_ffn", "norm_post_ffn",
                "a_pw1_w", "a_pw1_b", "a_dw_w", "a_dw_b", "a_ln_g", "a_ln_b", "a_pw2_w", "a_pw2_b",
                "b_group_w", "b_group_b", "b_scale", "c_in_w", "c_conv_w", "c_out_w",
                "f_up_w", "f_dw_w", "f_dw_b", "f_down_w")


def _as_layers_rows_cols(name, w):
    if name == "b_group_w":
        return w.reshape(w.shape[1], w.shape[2], w.shape[3])
    return w


def _step(x, c, loss_target, w, m, v):
    xi, yi, ci = _place()
    shard = 2 * xi + yi
    example = 4 * xi + 2 * yi + ci
    d = x.shape[-1]

    small_names = tuple(SMALL_SHARD_AXIS)
    gathered0 = _all_gather8(_pack([c] + [w[k] for k in small_names]), name="gather_small")
    parts = _unpack(gathered0, [c.shape] + [w[k].shape for k in small_names], lead=(NDEV,))
    c_all = parts[0].reshape(NDEV, d)
    p = {}
    for k, part in zip(small_names, parts[1:]):
        p[k] = jnp.concatenate([part[2 * s] for s in range(NSHARD)], axis=SMALL_SHARD_AXIS[k])
    for k in SMALL_REPLICATED:
        p[k] = w[k]

    ncol = w["mod_w"].shape[2]
    mod_b_cols = lax.dynamic_slice_in_dim(w["mod_b"], shard * ncol, ncol, axis=1).reshape(DEPTH, 1, ncol)
    mod_part = _mod_fwd(c_all, w["mod_w"], mod_b_cols, name="mod_fwd")
    gathered1 = _all_gather8(mod_part.reshape(DEPTH * NDEV, ncol), name="gather_mod")
    mod_all = gathered1.reshape(NSHARD, 2, DEPTH, NDEV, ncol)[:, 0]
    mod_mine = lax.dynamic_index_in_dim(mod_all, example, axis=2, keepdims=False)
    mod = jnp.transpose(mod_mine, (1, 0, 2)).reshape(DEPTH, 6, d)

    shard_arr = shard.reshape(1).astype(jnp.int32)
    bufs, kinds = [], []
    for k in BIG:
        wk = _as_layers_rows_cols(k, w[k])
        nl, r, cdim = wk.shape
        if nl >= 2:
            a, rh = nl // 2, r
        else:
            a, rh = 1, r // 2
        if k in COL_SHARDED:
            kind, out_shape = "col", (NSHARD, 2, a, rh, cdim)
        elif nl >= 2:
            kind, out_shape = "row", (2, a, NSHARD, rh, cdim)
        else:
            kind, out_shape = "row1", (1, NSHARD, 2, rh, cdim)
        kinds.append(kind)
        bufs.append(_cast_into_gathered(wk.reshape(2, a, rh, cdim), kind, out_shape, shard_arr, name=f"cast_{k}"))
    full = _gather_weights(bufs, kinds, name="gather_weights")
    for k, f in zip(BIG, full):
        nl, r, cdim = _as_layers_rows_cols(k, w[k]).shape
        if k in COL_SHARDED:
            p[k] = f.reshape(NSHARD, nl, r, cdim)
        elif k == "b_group_w":
            p[k] = f.reshape(1, nl, NSHARD * r, cdim)
        else:
            p[k] = f.reshape(nl, NSHARD * r, cdim)

    loss, grad_x, dmod, small, big = _local_step(x[0], loss_target[0], mod, p)

    gs, layers_of = [], []
    for k in BIG:
        ks = []
        for g in big[k]:
            if k == "b_group_w":
                ng, rr, cc = g.shape
                g = jnp.transpose(g.reshape(ng, NSHARD, rr // NSHARD, cc), (1, 0, 2, 3)).reshape(NSHARD, -1, cc)
            elif k not in COL_SHARDED:
                g = g.reshape(NSHARD, g.shape[0] // NSHARD, g.shape[1])
            ks.append(len(gs))
            gs.append(g.reshape(NSHARD, 2, g.shape[1] // 2, g.shape[2]))
        layers_of.append(ks)
    half = ci.reshape(1).astype(jnp.int32)
    place = jnp.stack([shard, ci]).astype(jnp.int32)
    r1 = _pair_exchange(gs, layers_of, name="grad_pair_exchange")
    ss = []
    for k, ks, r1_t in zip(BIG, layers_of, r1):
        s_t = lax.empty(r1_t.shape, BF16)
        for l, i in enumerate(ks):
            s_t = _pair_sum(gs[i], r1_t, s_t, l, half, name=f"grad_pair_sum_{k}_{l}")
        ss.append(s_t)
    r3 = _chip_exchange(ss, name="grad_chip_exchange")
    reds = [_chip_sum(s_t, r3_t, place, name=f"grad_chip_sum_{k}") for k, s_t, r3_t in zip(BIG, ss, r3)]
    joined = _join_halves(reds, name="grad_join_halves")
    grads = {k: j.reshape(w[k].shape) for k, j in zip(BIG, joined)}

    rep_names = SMALL_REPLICATED
    small_list = [small[k] for k in rep_names] + [small[k] for k in small_names] + [dmod]
    gathered2 = _all_gather8(_pack(small_list), name="gather_small_grads")
    summed = _sum_devices(gathered2, name="sum_small_grads")
    shapes = [s.shape for s in small_list]
    sums = _unpack(summed, shapes)
    for k, s in zip(rep_names, sums[:len(rep_names)]):
        grads[k] = s
    for k, s in zip(small_names, sums[len(rep_names):-1]):
        ax = SMALL_SHARD_AXIS[k]
        grads[k] = lax.dynamic_slice_in_dim(s, shard * w[k].shape[ax], w[k].shape[ax], axis=ax)
    grads["mod_b"] = sums[-1].reshape(w["mod_b"].shape)
    dmod_all = _unpack(gathered2, shapes, lead=(NDEV,))[-1].reshape(NDEV, DEPTH, NSHARD, ncol)
    dmod_cols = jnp.transpose(lax.dynamic_index_in_dim(dmod_all, shard, axis=2, keepdims=False), (1, 0, 2))

    delta, new_m, new_v = {}, {}, {}
    grads["mod_w"], delta["mod_w"], new_m["mod_w"], new_v["mod_w"] = _mod_w_update(
        c_all.T, dmod_cols, w["mod_w"], m["mod_w"], v["mod_w"], name="mod_w_update")
    for k in BIG:
        cdim = w[k].shape[-1]
        outs = _adamw(*[t.reshape(-1, cdim) for t in (w[k], grads[k], m[k], v[k])], name=f"adamw_{k}")
        delta[k], new_m[k], new_v[k] = [o.reshape(w[k].shape) for o in outs]
    rest = ("mod_b",) + rep_names + small_names
    packs = [_pack([t[k] for k in rest]) for t in (w, grads, m, v)]
    outs = _adamw(*packs, name="adamw_small")
    rest_shapes = [w[k].shape for k in rest]
    for dst, o in zip((delta, new_m, new_v), outs):
        for k, t in zip(rest, _unpack(o, rest_shapes)):
            dst[k] = t

    loss_all = lax.psum(loss[0, 0], ("x", "y", "c"))
    return (loss_all, grad_x[None], *[grads[k] for k in WEIGHT_ORDER], *[delta[k] for k in WEIGHT_ORDER],
            *[new_m[k] for k in WEIGHT_ORDER], *[new_v[k] for k in WEIGHT_ORDER])


def kernel(x, c, mod_w, mod_b, norm_pre_mix, norm_post_mix, norm_pre_ffn, norm_post_ffn, a_pw1_w, a_pw1_b, a_dw_w, a_dw_b, a_ln_g, a_ln_b, a_pw2_w, a_pw2_b, b_group_w, b_group_b, b_scale, c_in_w, c_conv_w, c_out_w, f_up_w, f_dw_w, f_dw_b, f_down_w, loss_target, m_mod_w, m_mod_b, m_norm_pre_mix, m_norm_post_mix, m_norm_pre_ffn, m_norm_post_ffn, m_a_pw1_w, m_a_pw1_b, m_a_dw_w, m_a_dw_b, m_a_ln_g, m_a_ln_b, m_a_pw2_w, m_a_pw2_b, m_b_group_w, m_b_group_b, m_b_scale, m_c_in_w, m_c_conv_w, m_c_out_w, m_f_up_w, m_f_dw_w, m_f_dw_b, m_f_down_w, v_mod_w, v_mod_b, v_norm_pre_mix, v_norm_post_mix, v_norm_pre_ffn, v_norm_post_ffn, v_a_pw1_w, v_a_pw1_b, v_a_dw_w, v_a_dw_b, v_a_ln_g, v_a_ln_b, v_a_pw2_w, v_a_pw2_b, v_b_group_w, v_b_group_b, v_b_scale, v_c_in_w, v_c_conv_w, v_c_out_w, v_f_up_w, v_f_dw_w, v_f_dw_b, v_f_down_w):
    given = dict(locals())
    w = {k: given[k] for k in WEIGHT_ORDER}
    m = {k: given["m_" + k] for k in WEIGHT_ORDER}
    v = {k: given["v_" + k] for k in WEIGHT_ORDER}
    return _step(x, c, loss_target, w, m, v)
```

```python
import functools

import jax
import jax.numpy as jnp
from jax import lax
from jax.experimental import pallas as pl
from jax.experimental.pallas import tpu as pltpu

F32 = jnp.float32
BF16 = jnp.bfloat16

DEPTH = 4
N_MIXERS = 3
CONF_CONV_WIDTH = 31
POOL_WINDOWS = (2, 4, 8, 16)
RMS_EPS = 1e-6
LN_EPS = 1e-5
ADAM_LR = 0.001
ADAM_B1 = 0.9
ADAM_B2 = 0.999
ADAM_EPS = 1e-08
ADAM_WD = 0.01
ADAM_STEP = 10

TM_ROW = 512
TM_CONV = 128
TK_TOKENS = 1024
TM_MM = 1024
TN_MM = 1024
SUBLANES = 8
LANES = 128
NSHARD = 4
NDEV = 8
HALO_A = 32
HALO_POOL = 16
HALO_3 = 8
VMEM_LIMIT = 56 * 1024 * 1024

_PAR = "parallel"
_ARB = "arbitrary"


def _cparams(*sem):
    return pltpu.CompilerParams(dimension_semantics=sem, vmem_limit_bytes=VMEM_LIMIT)


def _pick(n, prefs):
    for p in prefs:
        if p <= n and n % p == 0:
            return p
    return n


def _row_spec(tm, width):
    return pl.BlockSpec((tm, width), lambda i: (i, 0))


def _vec_spec(rows, width):
    return pl.BlockSpec((rows, width), lambda i: (0, 0))


def _prev_spec(tm, hb, width):
    return pl.BlockSpec((hb, width), lambda i: (jnp.maximum(i * (tm // hb) - 1, 0), 0))


def _next_spec(tm, hb, width, total):
    last = total // hb - 1
    return pl.BlockSpec((hb, width), lambda i: (jnp.minimum((i + 1) * (tm // hb), last), 0))


def _sum8(v):
    r, c = v.shape
    return jnp.sum(v.reshape(r // SUBLANES, SUBLANES, c), axis=0)


def _rms(x):
    r = lax.rsqrt(jnp.mean(x * x, axis=-1, keepdims=True) + RMS_EPS)
    return x * r, r


def _rms_bwd(dy, xn, r):
    return r * (dy - xn * jnp.mean(dy * xn, axis=-1, keepdims=True))


def _sigmoid(x):
    return 1.0 / (1.0 + jnp.exp(-x))


_DIMS = {"nn": ((1,), (0,)), "nt": ((1,), (1,)), "tn": ((0,), (0,))}


def _mm(a, b, *, mode, name, out_dtype=F32, bias=None, tm=TM_MM, tn=TN_MM, tk=None, layer=None):
    sharded = layer is not None
    if mode == "nn":
        m, k = a.shape
        n = NSHARD * b.shape[3] if sharded else b.shape[1]
    elif mode == "nt":
        m, k = a.shape
        n = b.shape[2] if sharded else b.shape[0]
    else:
        (k, m), (_, n) = a.shape, b.shape
    ns = n // NSHARD
    ks = k // NSHARD
    tm = _pick(m, (tm, 1408, 512, 256, 128))
    if sharded and mode != "nt":
        tn = _pick(ns, (1408, 768, 512, 256, 128))
    else:
        tn = _pick(n, (tn, 1408, 512, 256, 128))
    if sharded and mode == "nt":
        tk = _pick(ks, (1408, 768, 512, 256, 128))
    else:
        tk = _pick(k, (tk or k, 2816, 1024, 512, 256, 128))
    nk = k // tk
    per_n = ns // tn if sharded and mode != "nt" else 1
    per_k = ks // tk if sharded and mode == "nt" else 1
    dims = (_DIMS[mode], ((), ()))

    def split(idx, per):
        return (idx, 0) if per == 1 else (idx // per, idx % per)

    def body(*refs):
        a_ref, b_ref = refs[0], refs[1]
        bias_ref = refs[2] if bias is not None else None
        o_ref = refs[3] if bias is not None else refs[2]
        part = lax.dot_general(a_ref[...].astype(BF16), b_ref[...].astype(BF16), dims,
                               preferred_element_type=F32)

        def finish(r):
            if bias_ref is not None:
                r = r + bias_ref[...]
            o_ref[...] = r.astype(out_dtype)

        if nk == 1:
            finish(part)
        else:
            acc_ref = refs[-1]
            kk = pl.program_id(2)

            @pl.when(kk == 0)
            def _():
                acc_ref[...] = part

            @pl.when(kk > 0)
            def _():
                acc_ref[...] += part

            @pl.when(kk == nk - 1)
            def _():
                finish(acc_ref[...])

    out_spec = pl.BlockSpec((tm, tn), lambda i, j, kk: (i, j))
    out_shape = jax.ShapeDtypeStruct((m, n), out_dtype)
    if mode == "nn":
        a_spec = pl.BlockSpec((tm, tk), lambda i, j, kk: (i, kk))
        if sharded:
            b_spec = pl.BlockSpec((None, None, tk, tn),
                                  lambda i, j, kk: (split(j, per_n)[0], layer, kk, split(j, per_n)[1]))
        else:
            b_spec = pl.BlockSpec((tk, tn), lambda i, j, kk: (kk, j))
    elif mode == "nt":
        a_spec = pl.BlockSpec((tm, tk), lambda i, j, kk: (i, kk))
        if sharded:
            b_spec = pl.BlockSpec((None, None, tn, tk),
                                  lambda i, j, kk: (split(kk, per_k)[0], layer, j, split(kk, per_k)[1]))
        else:
            b_spec = pl.BlockSpec((tn, tk), lambda i, j, kk: (j, kk))
    else:
        a_spec = pl.BlockSpec((tk, tm), lambda i, j, kk: (kk, i))
        b_spec = pl.BlockSpec((tk, tn), lambda i, j, kk: (kk, j))
        if sharded:
            out_spec = pl.BlockSpec((None, tm, tn), lambda i, j, kk: (split(j, per_n)[0], i, split(j, per_n)[1]))
            out_shape = jax.ShapeDtypeStruct((NSHARD, m, ns), out_dtype)
    in_specs = [a_spec, b_spec]
    args = [a, b]
    if bias is not None:
        in_specs.append(pl.BlockSpec((1, tn), lambda i, j, kk: (0, j)))
        args.append(bias)
    return pl.pallas_call(
        body, grid=(m // tm, n // tn, nk), in_specs=in_specs, out_specs=out_spec, out_shape=out_shape,
        scratch_shapes=[pltpu.VMEM((tm, tn), F32)] if nk > 1 else [],
        compiler_params=_cparams(_PAR, _PAR, _ARB), name=name)(*args)


def _mm_group(a, b, *, mode, name, out_dtype=F32, tm=512):
    t = a.shape[0]
    tm = _pick(t, (tm, 256, 128))
    nt_ = t // tm
    g = len(POOL_WINDOWS)
    gd = a.shape[1] // g
    dims = (_DIMS[mode], ((), ()))

    if mode == "tn":
        def body(a_ref, b_ref, o_ref, acc_ref):
            kk = pl.program_id(1)
            part = lax.dot_general(a_ref[...].astype(BF16), b_ref[...].astype(BF16), dims,
                                   preferred_element_type=F32)

            @pl.when(kk == 0)
            def _():
                acc_ref[...] = part

            @pl.when(kk > 0)
            def _():
                acc_ref[...] += part

            @pl.when(kk == nt_ - 1)
            def _():
                o_ref[...] = acc_ref[...]

        return pl.pallas_call(
            body, grid=(g, nt_),
            in_specs=[pl.BlockSpec((tm, gd), lambda gi, kk: (kk, gi)),
                      pl.BlockSpec((tm, gd), lambda gi, kk: (kk, gi))],
            out_specs=pl.BlockSpec((None, gd, gd), lambda gi, kk: (gi, 0, 0)),
            out_shape=jax.ShapeDtypeStruct((g, gd, gd), F32),
            scratch_shapes=[pltpu.VMEM((gd, gd), F32)],
            compiler_params=_cparams(_PAR, _ARB), name=name)(a, b)

    def body(a_ref, b_ref, o_ref):
        o_ref[...] = lax.dot_general(a_ref[...].astype(BF16), b_ref[...].astype(BF16), dims,
                                     preferred_element_type=F32).astype(out_dtype)

    return pl.pallas_call(
        body, grid=(nt_, g),
        in_specs=[pl.BlockSpec((tm, gd), lambda i, gi: (i, gi)),
                  pl.BlockSpec((None, gd, gd), lambda i, gi: (gi, 0, 0))],
        out_specs=pl.BlockSpec((tm, gd), lambda i, gi: (i, gi)),
        out_shape=jax.ShapeDtypeStruct((t, g * gd), out_dtype),
        compiler_params=_cparams(_PAR, _PAR), name=name)(a, b)


def _pre(x, gpre, sc, sh):
    xn, r = _rms(x)
    return (xn * gpre) * (1.0 + sc) + sh, xn, r


def _fwd_first(x, gpre, sc, sh, *, h_dtype, name):
    t, d = x.shape
    tm = _pick(t, (TM_ROW, 256, 128))

    def body(x_ref, gpre_ref, sc_ref, sh_ref, h_ref):
        h, _, _ = _pre(x_ref[...], gpre_ref[...], sc_ref[...], sh_ref[...])
        h_ref[...] = h.astype(h_dtype)

    return pl.pallas_call(
        body, grid=(t // tm,),
        in_specs=[_row_spec(tm, d)] + [_vec_spec(1, d)] * 3,
        out_specs=_row_spec(tm, d), out_shape=jax.ShapeDtypeStruct((t, d), h_dtype),
        compiler_params=_cparams(_PAR), name=name)(x, gpre, sc, sh)


def _fwd_mid(x, y, gpost, gt, gpre, sc, sh, *, h_dtype, name):
    t, d = x.shape
    tm = _pick(t, (TM_ROW, 256, 128))

    def body(x_ref, y_ref, gpost_ref, gt_ref, gpre_ref, sc_ref, sh_ref, xn_ref, h_ref):
        yn, _ = _rms(y_ref[...])
        x_new = x_ref[...] + gt_ref[...] * (yn * gpost_ref[...])
        xn_ref[...] = x_new
        h, _, _ = _pre(x_new, gpre_ref[...], sc_ref[...], sh_ref[...])
        h_ref[...] = h.astype(h_dtype)

    return pl.pallas_call(
        body, grid=(t // tm,),
        in_specs=[_row_spec(tm, d)] * 2 + [_vec_spec(1, d)] * 5,
        out_specs=[_row_spec(tm, d)] * 2,
        out_shape=[jax.ShapeDtypeStruct((t, d), F32), jax.ShapeDtypeStruct((t, d), h_dtype)],
        compiler_params=_cparams(_PAR), name=name)(x, y, gpost, gt, gpre, sc, sh)


def _post_bwd(dx, y, gpost, gt):
    yn, r2 = _rms(y)
    dyn = dx * (gt * gpost)
    dy = _rms_bwd(dyn, yn, r2)
    return dy, dx * yn


def _last_fwd_bwd(x, y, target, gpost, gt, *, name):
    t, d = x.shape
    tm = _pick(t, (TM_ROW, 256, 128))
    n = t // tm

    def body(x_ref, y_ref, tg_ref, gpost_ref, gt_ref, dx_ref, dy_ref, dgpost_ref, dgt_ref, sdy_ref,
             loss_ref, qa, sa, la):
        i = pl.program_id(0)

        @pl.when(i == 0)
        def _():
            qa[...] = jnp.zeros_like(qa)
            sa[...] = jnp.zeros_like(sa)
            la[...] = jnp.zeros_like(la)

        yv = y_ref[...]
        yn, r2 = _rms(yv)
        gt_v, gpost_v = gt_ref[...], gpost_ref[...]
        err = x_ref[...] + gt_v * (yn * gpost_v) - tg_ref[...]
        la[...] += _sum8(err * err)
        dx = err * (1.0 / d)
        dx_ref[...] = dx
        dy = _rms_bwd(dx * (gt_v * gpost_v), yn, r2)
        dy_ref[...] = dy.astype(dy_ref.dtype)
        qa[...] += _sum8(dx * yn)
        sa[...] += _sum8(dy)

        @pl.when(i == n - 1)
        def _():
            q = jnp.sum(qa[...], axis=0, keepdims=True)
            dgpost_ref[...] = gt_v * q
            dgt_ref[...] = gpost_v * q
            sdy_ref[...] = jnp.sum(sa[...], axis=0, keepdims=True)
            tot = jnp.sum(jnp.sum(la[...], axis=0, keepdims=True), axis=1, keepdims=True)
            loss_ref[...] = tot * (0.5 / d)

    return pl.pallas_call(
        body, grid=(n,),
        in_specs=[_row_spec(tm, d)] * 3 + [_vec_spec(1, d)] * 2,
        out_specs=[_row_spec(tm, d)] * 2 + [_vec_spec(1, d)] * 3 + [_vec_spec(1, 1)],
        out_shape=[jax.ShapeDtypeStruct((t, d), F32), jax.ShapeDtypeStruct((t, d), BF16)]
        + [jax.ShapeDtypeStruct((1, d), F32)] * 3 + [jax.ShapeDtypeStruct((1, 1), F32)],
        scratch_shapes=[pltpu.VMEM((SUBLANES, d), F32)] * 3,
        compiler_params=_cparams(_ARB), name=name)(x, y, target, gpost, gt)


def _bwd_mid(dx_new, dh, x_in, gpre, sc, y_prev, gpost_p, gt_p, *, dy_dtype, name):
    t, d = x_in.shape
    tm = _pick(t, (TM_ROW, 256, 128))
    n = t // tm

    def body(dxn_ref, dh_ref, x_ref, gpre_ref, sc_ref, y_ref, gpost_ref, gt_ref,
             dx_ref, dy_ref, dsh_ref, dsc_ref, dgpre_ref, dgpost_ref, dgt_ref, sdy_ref, a1, a2, aq, asd):
        i = pl.program_id(0)

        @pl.when(i == 0)
        def _():
            for a in (a1, a2, aq, asd):
                a[...] = jnp.zeros_like(a)

        dh_v = dh_ref[...]
        xn, r = _rms(x_ref[...])
        dx = dxn_ref[...] + _rms_bwd(dh_v * ((1.0 + sc_ref[...]) * gpre_ref[...]), xn, r)
        dx_ref[...] = dx
        a1[...] += _sum8(dh_v)
        a2[...] += _sum8(dh_v * xn)
        dy, dxyn = _post_bwd(dx, y_ref[...], gpost_ref[...], gt_ref[...])
        dy_ref[...] = dy.astype(dy_dtype)
        aq[...] += _sum8(dxyn)
        asd[...] += _sum8(dy)

        @pl.when(i == n - 1)
        def _():
            s2 = jnp.sum(a2[...], axis=0, keepdims=True)
            q = jnp.sum(aq[...], axis=0, keepdims=True)
            dsh_ref[...] = jnp.sum(a1[...], axis=0, keepdims=True)
            dsc_ref[...] = gpre_ref[...] * s2
            dgpre_ref[...] = (1.0 + sc_ref[...]) * s2
            dgpost_ref[...] = gt_ref[...] * q
            dgt_ref[...] = gpost_ref[...] * q
            sdy_ref[...] = jnp.sum(asd[...], axis=0, keepdims=True)

    return pl.pallas_call(
        body, grid=(n,),
        in_specs=[_row_spec(tm, d)] * 3 + [_vec_spec(1, d)] * 2 + [_row_spec(tm, d)] + [_vec_spec(1, d)] * 2,
        out_specs=[_row_spec(tm, d)] * 2 + [_vec_spec(1, d)] * 6,
        out_shape=[jax.ShapeDtypeStruct((t, d), F32), jax.ShapeDtypeStruct((t, d), dy_dtype)]
        + [jax.ShapeDtypeStruct((1, d), F32)] * 6,
        scratch_shapes=[pltpu.VMEM((SUBLANES, d), F32)] * 4,
        compiler_params=_cparams(_ARB), name=name)(dx_new, dh, x_in, gpre, sc, y_prev, gpost_p, gt_p)


def _bwd_first(dx_new, dh, x_in, gpre, sc, *, name):
    t, d = x_in.shape
    tm = _pick(t, (TM_ROW, 256, 128))
    n = t // tm

    def body(dxn_ref, dh_ref, x_ref, gpre_ref, sc_ref, dx_ref, dsh_ref, dsc_ref, dgpre_ref, a1, a2):
        i = pl.program_id(0)

        @pl.when(i == 0)
        def _():
            a1[...] = jnp.zeros_like(a1)
            a2[...] = jnp.zeros_like(a2)

        dh_v = dh_ref[...]
        xn, r = _rms(x_ref[...])
        dx_ref[...] = dxn_ref[...] + _rms_bwd(dh_v * ((1.0 + sc_ref[...]) * gpre_ref[...]), xn, r)
        a1[...] += _sum8(dh_v)
        a2[...] += _sum8(dh_v * xn)

        @pl.when(i == n - 1)
        def _():
            s2 = jnp.sum(a2[...], axis=0, keepdims=True)
            dsh_ref[...] = jnp.sum(a1[...], axis=0, keepdims=True)
            dsc_ref[...] = gpre_ref[...] * s2
            dgpre_ref[...] = (1.0 + sc_ref[...]) * s2

    return pl.pallas_call(
        body, grid=(n,),
        in_specs=[_row_spec(tm, d)] * 3 + [_vec_spec(1, d)] * 2,
        out_specs=[_row_spec(tm, d)] + [_vec_spec(1, d)] * 3,
        out_shape=[jax.ShapeDtypeStruct((t, d), F32)] + [jax.ShapeDtypeStruct((1, d), F32)] * 3,
        scratch_shapes=[pltpu.VMEM((SUBLANES, d), F32)] * 2,
        compiler_params=_cparams(_ARB), name=name)(dx_new, dh, x_in, gpre, sc)


def _conv3_rows(buf, w_ref, rows, first):
    out = buf[pl.ds(first, rows), :] * w_ref[pl.ds(0, 1), :]
    for k in (1, 2):
        out = out + buf[pl.ds(first + k, rows), :] * w_ref[pl.ds(k, 1), :]
    return out


ROWS_BLK = 16
COLS_BLK = 256


def _bcast_rows(dst, src_ref, first, nrows):
    for k in range(nrows):
        dst[first + k] = jnp.broadcast_to(src_ref[pl.ds(k, 1), :], dst.shape[1:])


def _conv3_blk(buf, wb, first, rows, cols):
    xs = [buf[pl.ds(first + k, rows), cols] for k in range(3)]
    out = xs[0] * wb[0, pl.ds(0, rows), cols]
    out = out + xs[1] * wb[1, pl.ds(0, rows), cols]
    out = out + xs[2] * wb[2, pl.ds(0, rows), cols]
    return out + wb[3, pl.ds(0, rows), cols], xs


def _ffn_gate_fwd(u, w, b, *, name):
    t, f2 = u.shape
    f = f2 // 2
    tm = _pick(t, (TM_CONV,))
    hb = HALO_3
    rb = ROWS_BLK
    cw = _pick(f, (COLS_BLK, LANES))

    def body(u_ref, up_ref, w_ref, b_ref, a_ref, v_ref, buf, wb):
        i = pl.program_id(0)
        buf[pl.ds(hb, tm), :] = u_ref[...]
        buf[pl.ds(0, hb), :] = jnp.where(i > 0, up_ref[...], 0.0)
        _bcast_rows(wb, w_ref, 0, 3)
        _bcast_rows(wb, b_ref, 3, 1)
        for c0 in range(0, f, cw):
            gcols, vcols = pl.ds(c0, cw), pl.ds(f + c0, cw)
            for r0 in range(0, tm, rb):
                rows = pl.ds(r0, rb)
                vg, _ = _conv3_blk(buf, wb, hb - 2 + r0, rb, gcols)
                vv, _ = _conv3_blk(buf, wb, hb - 2 + r0, rb, vcols)
                v_ref[rows, gcols] = vg
                v_ref[rows, vcols] = vv
                a_ref[rows, gcols] = (vg * _sigmoid(vg) * vv).astype(BF16)

    return pl.pallas_call(
        body, grid=(t // tm,),
        in_specs=[_row_spec(tm, f2), _prev_spec(tm, hb, f2), _vec_spec(3, f2), _vec_spec(1, f2)],
        out_specs=[_row_spec(tm, f), _row_spec(tm, f2)],
        out_shape=[jax.ShapeDtypeStruct((t, f), BF16), jax.ShapeDtypeStruct((t, f2), F32)],
        scratch_shapes=[pltpu.VMEM((tm + hb, f2), F32), pltpu.VMEM((4, rb, f2), F32)],
        compiler_params=_cparams(_PAR), name=name)(u, u, w, b)


def _ffn_gate_bwd(u, v, da, w, *, name):
    t, f2 = u.shape
    f = f2 // 2
    tm = _pick(t, (TM_CONV,))
    hb = HALO_3
    n = t // tm
    rb = ROWS_BLK
    cw = _pick(f, (COLS_BLK, LANES))
    blocks = [(r0, rb) for r0 in range(0, tm, rb)] + [(tm, hb)]

    def body(u_ref, v_ref, vn_ref, da_ref, dan_ref, w_ref, du_ref, dw_ref, db_ref, dvbuf, wb, wacc, bacc):
        i = pl.program_id(0)

        @pl.when(i == 0)
        def _():
            wacc[...] = jnp.zeros_like(wacc)
            bacc[...] = jnp.zeros_like(bacc)

        _bcast_rows(wb, w_ref, 0, 3)
        for c0 in range(0, f, cw):
            gcols, vcols = pl.ds(c0, cw), pl.ds(f + c0, cw)
            for r0, rows in blocks:
                if r0 < tm:
                    vg, vv = v_ref[pl.ds(r0, rows), gcols], v_ref[pl.ds(r0, rows), vcols]
                    dav = da_ref[pl.ds(r0, rows), gcols]
                else:
                    vg, vv = vn_ref[:, gcols], vn_ref[:, vcols]
                    dav = jnp.where(i < n - 1, dan_ref[:, gcols], 0.0)
                sg = _sigmoid(vg)
                dvg = dav * vv * (sg * (1.0 + vg * (1.0 - sg)))
                dvv = dav * (vg * sg)
                dvbuf[pl.ds(r0, rows), gcols] = dvg
                dvbuf[pl.ds(r0, rows), vcols] = dvv
                if r0 < tm:
                    bacc[:, gcols] += _sum8(dvg)
                    bacc[:, vcols] += _sum8(dvv)
        for c0 in range(0, f2, cw):
            cols = pl.ds(c0, cw)
            for r0 in range(0, tm, rb):
                uv = u_ref[pl.ds(r0, rb), cols]
                du = None
                for k in range(3):
                    dvk = dvbuf[pl.ds(r0 + 2 - k, rb), cols]
                    term = dvk * wb[k, :, cols]
                    du = term if du is None else du + term
                    wacc[k, :, cols] += _sum8(uv * dvk)
                du_ref[pl.ds(r0, rb), cols] = du.astype(BF16)

        @pl.when(i == n - 1)
        def _():
            db_ref[...] = jnp.sum(bacc[...], axis=0, keepdims=True)
            dw_ref[...] = jnp.sum(wacc[...], axis=1)

    return pl.pallas_call(
        body, grid=(n,),
        in_specs=[_row_spec(tm, f2), _row_spec(tm, f2), _next_spec(tm, hb, f2, t),
                  _row_spec(tm, f), _next_spec(tm, hb, f, t), _vec_spec(3, f2)],
        out_specs=[_row_spec(tm, f2), _vec_spec(3, f2), _vec_spec(1, f2)],
        out_shape=[jax.ShapeDtypeStruct((t, f2), BF16), jax.ShapeDtypeStruct((3, f2), F32),
                   jax.ShapeDtypeStruct((1, f2), F32)],
        scratch_shapes=[pltpu.VMEM((tm + hb, f2), F32), pltpu.VMEM((3, rb, f2), F32),
                        pltpu.VMEM((3, SUBLANES, f2), F32), pltpu.VMEM((SUBLANES, f2), F32)],
        compiler_params=_cparams(_ARB), name=name)(u, v, v, da, da, w)


def _glu(u, b1, d):
    return (u[:, :d] + b1[:, :d]) * _sigmoid(u[:, d:] + b1[:, d:])


ROWS_TAPS = 32
ROWS_NORM = 16


def _fill_glu_buf(buf, u_ref, up_ref, b1_ref, i, tm, d):
    cw = _pick(d, (COLS_BLK, LANES))
    for c0 in range(0, d, cw):
        b1 = jnp.concatenate([b1_ref[:, pl.ds(c0, cw)], b1_ref[:, pl.ds(d + c0, cw)]], axis=1)
        up = jnp.concatenate([up_ref[:, pl.ds(c0, cw)], up_ref[:, pl.ds(d + c0, cw)]], axis=1)
        buf[pl.ds(0, HALO_A), pl.ds(c0, cw)] = jnp.where(i > 0, _glu(up, b1, cw), 0.0)
        for r0 in range(0, tm, ROWS_TAPS):
            rows = pl.ds(r0, ROWS_TAPS)
            uv = jnp.concatenate([u_ref[rows, pl.ds(c0, cw)], u_ref[rows, pl.ds(d + c0, cw)]], axis=1)
            buf[pl.ds(HALO_A + r0, ROWS_TAPS), pl.ds(c0, cw)] = _glu(uv, b1, cw)


def _taps31(buf, r0, cols, offs, use):
    nv = ROWS_TAPS // SUBLANES
    span = {}
    for o in offs:
        span[o % SUBLANES] = max(span.get(o % SUBLANES, 0), ROWS_TAPS + SUBLANES * (o // SUBLANES))
    shifted = {b: buf[pl.ds(r0 + b, span[b]), cols] for b in span}
    for k, o in enumerate(offs):
        b, a = o % SUBLANES, o // SUBLANES
        use(k, [shifted[b][SUBLANES * (a + v):SUBLANES * (a + v + 1)] for v in range(nv)])


def _conv_taps_blk(buf, wb, r0, cols, offs):
    nv = ROWS_TAPS // SUBLANES
    acc = [None] * nv

    def use(k, rows):
        wk = wb[k, :, cols]
        for v in range(nv):
            term = rows[v] * wk
            acc[v] = term if acc[v] is None else acc[v] + term

    _taps31(buf, r0, cols, offs, use)
    return jnp.concatenate(acc, axis=0)


def _layernorm_parts(x):
    mu = jnp.mean(x, axis=-1, keepdims=True)
    xc = x - mu
    rstd = lax.rsqrt(jnp.mean(xc * xc, axis=-1, keepdims=True) + LN_EPS)
    return xc * rstd, rstd


_FWD_OFFS = tuple(HALO_A - (CONF_CONV_WIDTH - 1) + k for k in range(CONF_CONV_WIDTH))
_BWD_OFFS = tuple(CONF_CONV_WIDTH - 1 - k for k in range(CONF_CONV_WIDTH))


def _a_fwd(u1, b1, dww, dwb, lng, lnb, *, name):
    t, d2 = u1.shape
    d = d2 // 2
    tm = _pick(t, (TM_CONV,))

    def body(u_ref, up_ref, b1_ref, w_ref, wbias_ref, g_ref, bb_ref, o_ref, u3_ref, buf, wb):
        i = pl.program_id(0)
        _fill_glu_buf(buf, u_ref, up_ref, b1_ref, i, tm, d)
        _bcast_rows(wb, w_ref, 0, CONF_CONV_WIDTH)
        for c0 in range(0, d, LANES):
            cols = pl.ds(c0, LANES)
            for r0 in range(0, tm, ROWS_TAPS):
                u3_ref[pl.ds(r0, ROWS_TAPS), cols] = (_conv_taps_blk(buf, wb, r0, cols, _FWD_OFFS)
                                                      + wbias_ref[:, cols])
        for r0 in range(0, tm, ROWS_NORM):
            rows = pl.ds(r0, ROWS_NORM)
            xhat, _ = _layernorm_parts(u3_ref[rows, :])
            u4 = xhat * g_ref[...] + bb_ref[...]
            o_ref[rows, :] = (u4 * _sigmoid(u4)).astype(BF16)

    return pl.pallas_call(
        body, grid=(t // tm,),
        in_specs=[_row_spec(tm, d2), _prev_spec(tm, HALO_A, d2), _vec_spec(1, d2),
                  _vec_spec(CONF_CONV_WIDTH, d)] + [_vec_spec(1, d)] * 3,
        out_specs=[_row_spec(tm, d)] * 2,
        out_shape=[jax.ShapeDtypeStruct((t, d), BF16), jax.ShapeDtypeStruct((t, d), F32)],
        scratch_shapes=[pltpu.VMEM((tm + HALO_A, d), F32), pltpu.VMEM((CONF_CONV_WIDTH, SUBLANES, d), F32)],
        compiler_params=_cparams(_PAR), name=name)(u1, u1, b1, dww, dwb, lng, lnb)


def _a_bwd_norm(u3, du5, lng, lnb, *, name):
    t, d = u3.shape
    tm = _pick(t, (TM_ROW, 256, 128))
    n = t // tm

    def body(u3_ref, du5_ref, g_ref, bb_ref, du3_ref, dg_ref, db_ref, dwb_ref, ag, ab, aw):
        i = pl.program_id(0)

        @pl.when(i == 0)
        def _():
            for a in (ag, ab, aw):
                a[...] = jnp.zeros_like(a)

        g = g_ref[...]
        for r0 in range(0, tm, ROWS_NORM):
            rows = pl.ds(r0, ROWS_NORM)
            xhat, rstd = _layernorm_parts(u3_ref[rows, :])
            u4 = xhat * g + bb_ref[...]
            sg = _sigmoid(u4)
            du4 = du5_ref[rows, :] * (sg * (1.0 + u4 * (1.0 - sg)))
            dxh = du4 * g
            du3 = rstd * (dxh - jnp.mean(dxh, axis=-1, keepdims=True)
                          - xhat * jnp.mean(dxh * xhat, axis=-1, keepdims=True))
            du3_ref[rows, :] = du3
            ag[...] += _sum8(du4 * xhat)
            ab[...] += _sum8(du4)
            aw[...] += _sum8(du3)

        @pl.when(i == n - 1)
        def _():
            for a, o in ((ag, dg_ref), (ab, db_ref), (aw, dwb_ref)):
                o[...] = jnp.sum(a[...], axis=0, keepdims=True)

    return pl.pallas_call(
        body, grid=(n,),
        in_specs=[_row_spec(tm, d)] * 2 + [_vec_spec(1, d)] * 2,
        out_specs=[_row_spec(tm, d)] + [_vec_spec(1, d)] * 3,
        out_shape=[jax.ShapeDtypeStruct((t, d), F32)] + [jax.ShapeDtypeStruct((1, d), F32)] * 3,
        scratch_shapes=[pltpu.VMEM((SUBLANES, d), F32)] * 3,
        compiler_params=_cparams(_ARB), name=name)(u3, du5, lng, lnb)


def _a_bwd_conv(u1, du3, b1, dww, *, name):
    t, d2 = u1.shape
    d = d2 // 2
    tm = _pick(t, (TM_CONV,))
    n = t // tm
    kw = CONF_CONV_WIDTH
    nv = ROWS_TAPS // SUBLANES

    def body(u_ref, up_ref, g3_ref, g3n_ref, b1_ref, w_ref, du1_ref, dw_ref, db1_ref,
             buf, gbuf, wb, wacc, bacc):
        i = pl.program_id(0)

        @pl.when(i == 0)
        def _():
            wacc[...] = jnp.zeros_like(wacc)
            bacc[...] = jnp.zeros_like(bacc)

        _fill_glu_buf(buf, u_ref, up_ref, b1_ref, i, tm, d)
        _bcast_rows(wb, w_ref, 0, kw)
        gbuf[pl.ds(0, tm), :] = g3_ref[...]
        gbuf[pl.ds(tm, HALO_A), :] = jnp.where(i < n - 1, g3n_ref[...], 0.0)
        for c0 in range(0, d, LANES):
            cols, gcols = pl.ds(c0, LANES), pl.ds(d + c0, LANES)
            for r0 in range(0, tm, ROWS_TAPS):
                rows = pl.ds(r0, ROWS_TAPS)
                u2 = [buf[pl.ds(HALO_A + r0 + SUBLANES * v, SUBLANES), cols] for v in range(nv)]
                acc = [None] * nv

                def use(k, gs):
                    wk = wb[k, :, cols]
                    part = None
                    for v in range(nv):
                        term = gs[v] * wk
                        acc[v] = term if acc[v] is None else acc[v] + term
                        prod = u2[v] * gs[v]
                        part = prod if part is None else part + prod
                    wacc[k, :, cols] += part

                _taps31(gbuf, r0, cols, _BWD_OFFS, use)
                du2 = jnp.concatenate(acc, axis=0)
                av = u_ref[rows, cols] + b1_ref[:, cols]
                sg = _sigmoid(u_ref[rows, gcols] + b1_ref[:, gcols])
                da = du2 * sg
                dg = du2 * av * (sg * (1.0 - sg))
                du1_ref[rows, cols] = da.astype(BF16)
                du1_ref[rows, gcols] = dg.astype(BF16)
                bacc[:, cols] += _sum8(da)
                bacc[:, gcols] += _sum8(dg)

        @pl.when(i == n - 1)
        def _():
            dw_ref[...] = jnp.sum(wacc[...], axis=1)
            db1_ref[...] = jnp.sum(bacc[...], axis=0, keepdims=True)

    return pl.pallas_call(
        body, grid=(n,),
        in_specs=[_row_spec(tm, d2), _prev_spec(tm, HALO_A, d2), _row_spec(tm, d),
                  _next_spec(tm, HALO_A, d, t), _vec_spec(1, d2), _vec_spec(kw, d)],
        out_specs=[_row_spec(tm, d2), _vec_spec(kw, d), _vec_spec(1, d2)],
        out_shape=[jax.ShapeDtypeStruct((t, d2), BF16), jax.ShapeDtypeStruct((kw, d), F32),
                   jax.ShapeDtypeStruct((1, d2), F32)],
        scratch_shapes=[pltpu.VMEM((tm + HALO_A, d), F32), pltpu.VMEM((tm + HALO_A, d), F32),
                        pltpu.VMEM((kw, SUBLANES, d), F32),
                        pltpu.VMEM((kw, SUBLANES, d), F32), pltpu.VMEM((SUBLANES, d2), F32)],
        compiler_params=_cparams(_ARB), name=name)(u1, u1, du3, du3, b1, dww)


def _pool_counts(i, tm, w):
    pos = (i * tm + lax.broadcasted_iota(jnp.int32, (tm, 1), 0) + 1).astype(F32)
    return jnp.minimum(pos, float(w))


def _b_pool_fwd(h, *, name):
    t, d = h.shape
    gd = d // len(POOL_WINDOWS)
    tm = _pick(t, (TM_CONV,))
    hb = HALO_POOL

    def body(h_ref, hp_ref, o_ref, buf):
        i = pl.program_id(0)
        buf[pl.ds(0, hb), :] = jnp.where(i > 0, hp_ref[...], 0.0)
        buf[pl.ds(hb, tm), :] = h_ref[...]
        for g, w in enumerate(POOL_WINDOWS):
            cols = pl.ds(g * gd, gd)
            cur = buf[pl.ds(hb, tm), cols]
            s = cur
            for j in range(1, w):
                s = s + buf[pl.ds(hb - j, tm), cols]
            o_ref[:, cols] = (s / _pool_counts(i, tm, w) - cur).astype(BF16)

    return pl.pallas_call(
        body, grid=(t // tm,),
        in_specs=[_row_spec(tm, d), _prev_spec(tm, hb, d)],
        out_specs=_row_spec(tm, d), out_shape=jax.ShapeDtypeStruct((t, d), BF16),
        scratch_shapes=[pltpu.VMEM((tm + hb, d), F32)],
        compiler_params=_cparams(_PAR), name=name)(h, h)


def _b_pool_bwd(dp, *, name):
    t, d = dp.shape
    gd = d // len(POOL_WINDOWS)
    tm = _pick(t, (TM_CONV,))
    hb = HALO_POOL
    n = t // tm

    def body(dp_ref, dpn_ref, o_ref, buf):
        i = pl.program_id(0)
        for g, w in enumerate(POOL_WINDOWS):
            cols = pl.ds(g * gd, gd)
            buf[pl.ds(0, tm), cols] = dp_ref[:, cols] / _pool_counts(i, tm, w)
            buf[pl.ds(tm, hb), cols] = jnp.where(i < n - 1, dpn_ref[:, cols] * (1.0 / w), 0.0)
            s = buf[pl.ds(0, tm), cols]
            for j in range(1, w):
                s = s + buf[pl.ds(j, tm), cols]
            o_ref[:, cols] = s - dp_ref[:, cols]

    return pl.pallas_call(
        body, grid=(n,),
        in_specs=[_row_spec(tm, d), _next_spec(tm, hb, d, t)],
        out_specs=_row_spec(tm, d), out_shape=jax.ShapeDtypeStruct((t, d), F32),
        scratch_shapes=[pltpu.VMEM((tm + hb, d), F32)],
        compiler_params=_cparams(_PAR), name=name)(dp, dp)


def _b_affine_fwd(mixed, gb, scale, *, name):
    t, d = mixed.shape
    tm = _pick(t, (TM_ROW, 256, 128))

    def body(m_ref, gb_ref, s_ref, o_ref):
        o_ref[...] = (m_ref[...] + gb_ref[...]) * s_ref[...]

    return pl.pallas_call(
        body, grid=(t // tm,), in_specs=[_row_spec(tm, d)] + [_vec_spec(1, d)] * 2,
        out_specs=_row_spec(tm, d), out_shape=jax.ShapeDtypeStruct((t, d), F32),
        compiler_params=_cparams(_PAR), name=name)(mixed, gb, scale)


def _b_affine_bwd(dy, mixed, gb, scale, *, name):
    t, d = mixed.shape
    tm = _pick(t, (TM_ROW, 256, 128))
    n = t // tm

    def body(dy_ref, m_ref, gb_ref, s_ref, dm_ref, ds_ref, dgb_ref, a1, a2):
        i = pl.program_id(0)

        @pl.when(i == 0)
        def _():
            a1[...] = jnp.zeros_like(a1)
            a2[...] = jnp.zeros_like(a2)

        dy_v = dy_ref[...]
        dm_ref[...] = (dy_v * s_ref[...]).astype(BF16)
        a1[...] += _sum8(dy_v * (m_ref[...] + gb_ref[...]))
        a2[...] += _sum8(dy_v)

        @pl.when(i == n - 1)
        def _():
            ds_ref[...] = jnp.sum(a1[...], axis=0, keepdims=True)
            dgb_ref[...] = jnp.sum(a2[...], axis=0, keepdims=True) * s_ref[...]

    return pl.pallas_call(
        body, grid=(n,), in_specs=[_row_spec(tm, d)] * 2 + [_vec_spec(1, d)] * 2,
        out_specs=[_row_spec(tm, d)] + [_vec_spec(1, d)] * 2,
        out_shape=[jax.ShapeDtypeStruct((t, d), BF16)] + [jax.ShapeDtypeStruct((1, d), F32)] * 2,
        scratch_shapes=[pltpu.VMEM((SUBLANES, d), F32)] * 2,
        compiler_params=_cparams(_ARB), name=name)(dy, mixed, gb, scale)


def _c_gate_fwd(bcx, wc, *, name):
    t, d3 = bcx.shape
    d = d3 // 3
    tm = _pick(t, (TM_CONV,))
    hb = HALO_3

    def body(x_ref, xp_ref, w_ref, z_ref, buf):
        i = pl.program_id(0)
        xv, xp = x_ref[...], xp_ref[...]
        buf[pl.ds(0, hb), :] = jnp.where(i > 0, xp[:, d:2 * d] * xp[:, 2 * d:], 0.0)
        buf[pl.ds(hb, tm), :] = xv[:, d:2 * d] * xv[:, 2 * d:]
        z_ref[...] = (xv[:, :d] * _conv3_rows(buf, w_ref, tm, hb - 2)).astype(BF16)

    return pl.pallas_call(
        body, grid=(t // tm,),
        in_specs=[_row_spec(tm, d3), _prev_spec(tm, hb, d3), _vec_spec(3, d)],
        out_specs=_row_spec(tm, d), out_shape=jax.ShapeDtypeStruct((t, d), BF16),
        scratch_shapes=[pltpu.VMEM((tm + hb, d), F32)],
        compiler_params=_cparams(_PAR), name=name)(bcx, bcx, wc)


def _c_gate_bwd(bcx, dz, wc, *, name):
    t, d3 = bcx.shape
    d = d3 // 3
    tm = _pick(t, (TM_CONV,))
    hb = HALO_3
    n = t // tm

    def body(x_ref, xp_ref, xn_ref, dz_ref, dzn_ref, w_ref, o_ref, dw_ref, pbuf, qbuf, wacc):
        i = pl.program_id(0)

        @pl.when(i == 0)
        def _():
            wacc[...] = jnp.zeros_like(wacc)

        xv, xp, xnx = x_ref[...], xp_ref[...], xn_ref[...]
        gbv, gcv, vv = xv[:, :d], xv[:, d:2 * d], xv[:, 2 * d:]
        pbuf[pl.ds(0, hb), :] = jnp.where(i > 0, xp[:, d:2 * d] * xp[:, 2 * d:], 0.0)
        pbuf[pl.ds(hb, tm), :] = gcv * vv
        q = _conv3_rows(pbuf, w_ref, tm, hb - 2)
        dz_v = dz_ref[...]
        dq = dz_v * gbv
        qbuf[pl.ds(0, tm), :] = dq
        qbuf[pl.ds(tm, hb), :] = jnp.where(i < n - 1, dzn_ref[...] * xnx[:, :d], 0.0)
        dp = qbuf[pl.ds(0, tm), :] * w_ref[pl.ds(2, 1), :]
        dp = dp + qbuf[pl.ds(1, tm), :] * w_ref[pl.ds(1, 1), :]
        dp = dp + qbuf[pl.ds(2, tm), :] * w_ref[pl.ds(0, 1), :]
        o_ref[:, pl.ds(0, d)] = (dz_v * q).astype(BF16)
        o_ref[:, pl.ds(d, d)] = (dp * vv).astype(BF16)
        o_ref[:, pl.ds(2 * d, d)] = (dp * gcv).astype(BF16)
        for k in range(3):
            wacc[k] += _sum8(dq * pbuf[pl.ds(hb - 2 + k, tm), :])

        @pl.when(i == n - 1)
        def _():
            dw_ref[...] = jnp.sum(wacc[...], axis=1)

    return pl.pallas_call(
        body, grid=(n,),
        in_specs=[_row_spec(tm, d3), _prev_spec(tm, hb, d3), _next_spec(tm, hb, d3, t),
                  _row_spec(tm, d), _next_spec(tm, hb, d, t), _vec_spec(3, d)],
        out_specs=[_row_spec(tm, d3), _vec_spec(3, d)],
        out_shape=[jax.ShapeDtypeStruct((t, d3), BF16), jax.ShapeDtypeStruct((3, d), F32)],
        scratch_shapes=[pltpu.VMEM((tm + hb, d), F32), pltpu.VMEM((tm + hb, d), F32),
                        pltpu.VMEM((3, SUBLANES, d), F32)],
        compiler_params=_cparams(_ARB), name=name)(bcx, bcx, bcx, dz, dz, wc)


def _row(v):
    return v.reshape(1, -1)


def _kind_of(j):
    return "f" if j % 2 else "abc"[(j // 2) % N_MIXERS]


BIG = ("a_pw1_w", "a_pw2_w", "b_group_w", "c_in_w", "c_out_w", "f_up_w", "f_down_w")
COL_SHARDED = ("a_pw1_w", "c_in_w", "f_up_w")


def _local_step(x, target, mod, p):
    nsub = 2 * DEPTH
    norm_names = (("norm_pre_mix", "norm_post_mix"), ("norm_pre_ffn", "norm_post_ffn"))
    gpre = [_row(p[norm_names[s][0]][i]) for i in range(DEPTH) for s in (0, 1)]
    gpost = [_row(p[norm_names[s][1]][i]) for i in range(DEPTH) for s in (0, 1)]
    sh = [_row(mod[i, 3 * s + 0]) for i in range(DEPTH) for s in (0, 1)]
    sc = [_row(mod[i, 3 * s + 1]) for i in range(DEPTH) for s in (0, 1)]
    gt = [_row(mod[i, 3 * s + 2]) for i in range(DEPTH) for s in (0, 1)]

    def h_dtype(j):
        return F32 if _kind_of(j) == "b" else BF16

    xs, hs, ys, saved = [x], [], [], []

    hs.append(_fwd_first(x, gpre[0], sc[0], sh[0], h_dtype=h_dtype(0), name="fwd_first"))
    for j in range(nsub):
        i, kind = j // 2, _kind_of(j)
        slot = i // N_MIXERS
        h = hs[j]
        tag = f"{kind}{j}"
        if kind == "f":
            u = _mm(h, p["f_up_w"], mode="nn", layer=i, name=f"ffn_up_{tag}")
            a, vpre = _ffn_gate_fwd(u, p["f_dw_w"][i], _row(p["f_dw_b"][i]), name=f"ffn_gate_{tag}")
            y = _mm(a, p["f_down_w"][i], mode="nn", name=f"ffn_down_{tag}")
            saved.append((u, a, vpre))
        elif kind == "a":
            u1 = _mm(h, p["a_pw1_w"], mode="nn", layer=slot, name=f"a_pw1_{tag}")
            u5, u3 = _a_fwd(u1, _row(p["a_pw1_b"][slot]), p["a_dw_w"][slot], _row(p["a_dw_b"][slot]),
                            _row(p["a_ln_g"][slot]), _row(p["a_ln_b"][slot]), name=f"a_conv_{tag}")
            y = _mm(u5, p["a_pw2_w"][slot], mode="nn", bias=_row(p["a_pw2_b"][slot]), name=f"a_pw2_{tag}")
            saved.append((u1, u5, u3))
        elif kind == "b":
            pooled = _b_pool_fwd(h, name=f"b_pool_{tag}")
            mixed = _mm_group(pooled, p["b_group_w"][slot], mode="nn", name=f"b_mix_{tag}")
            y = _b_affine_fwd(mixed, _row(p["b_group_b"][slot]), _row(p["b_scale"][slot]), name=f"b_aff_{tag}")
            saved.append((pooled, mixed))
        else:
            bcx = _mm(h, p["c_in_w"], mode="nn", layer=slot, name=f"c_in_{tag}")
            z = _c_gate_fwd(bcx, p["c_conv_w"][slot], name=f"c_gate_{tag}")
            y = _mm(z, p["c_out_w"][slot], mode="nn", name=f"c_out_{tag}")
            saved.append((bcx, z))
        ys.append(y)
        if j + 1 < nsub:
            x_new, h_next = _fwd_mid(xs[j], y, gpost[j], gt[j], gpre[j + 1], sc[j + 1], sh[j + 1],
                                     h_dtype=h_dtype(j + 1), name=f"fwd_mid_{j}")
            xs.append(x_new)
            hs.append(h_next)

    n_of = {"a": len([i for i in range(DEPTH) if i % N_MIXERS == 0]),
            "b": len([i for i in range(DEPTH) if i % N_MIXERS == 1]),
            "c": len([i for i in range(DEPTH) if i % N_MIXERS == 2]), "f": DEPTH, "n": DEPTH}
    g = {k: [None] * n_of[k[0]] for k in p}
    dmod = [[None] * 6 for _ in range(DEPTH)]

    last = nsub - 1
    dx, dy, dgpost, dgt, sdy, loss = _last_fwd_bwd(xs[last], ys[last], target, gpost[last], gt[last],
                                                   name="loss_head")
    for j in range(last, -1, -1):
        i, kind = j // 2, _kind_of(j)
        slot = i // N_MIXERS
        sub = j % 2
        tag = f"{kind}{j}"
        g[norm_names[sub][1]][i] = dgpost
        dmod[i][3 * sub + 2] = dgt
        h = hs[j]
        if kind == "f":
            u, a, vpre = saved[j]
            da = _mm(dy, p["f_down_w"][i], mode="nt", name=f"ffn_dda_{tag}")
            g["f_down_w"][i] = _mm(a, dy, mode="tn", tk=TK_TOKENS, name=f"ffn_dwdown_{tag}")
            du, dw, db = _ffn_gate_bwd(u, vpre, da, p["f_dw_w"][i], name=f"ffn_gate_bwd_{tag}")
            g["f_dw_w"][i], g["f_dw_b"][i] = dw, db
            dh = _mm(du, p["f_up_w"], mode="nt", layer=i, name=f"ffn_ddh_{tag}")
            g["f_up_w"][i] = _mm(h, du, mode="tn", tk=TK_TOKENS, layer=i, name=f"ffn_dwup_{tag}")
        elif kind == "a":
            u1, u5, u3 = saved[j]
            g["a_pw2_b"][slot] = sdy
            du5 = _mm(dy, p["a_pw2_w"][slot], mode="nt", name=f"a_ddu5_{tag}")
            g["a_pw2_w"][slot] = _mm(u5, dy, mode="tn", tk=TK_TOKENS, name=f"a_dw2_{tag}")
            b1 = _row(p["a_pw1_b"][slot])
            du3, dlg, dlb, ddwb = _a_bwd_norm(u3, du5, _row(p["a_ln_g"][slot]), _row(p["a_ln_b"][slot]),
                                              name=f"a_bwd_norm_{tag}")
            g["a_ln_g"][slot], g["a_ln_b"][slot], g["a_dw_b"][slot] = dlg, dlb, ddwb
            du1, ddww, db1 = _a_bwd_conv(u1, du3, b1, p["a_dw_w"][slot], name=f"a_bwd_conv_{tag}")
            g["a_dw_w"][slot], g["a_pw1_b"][slot] = ddww, db1
            dh = _mm(du1, p["a_pw1_w"], mode="nt", layer=slot, name=f"a_ddh_{tag}")
            g["a_pw1_w"][slot] = _mm(h, du1, mode="tn", tk=TK_TOKENS, layer=slot, name=f"a_dw1_{tag}")
        elif kind == "b":
            pooled, mixed = saved[j]
            dmixed, dscale, dgb = _b_affine_bwd(dy, mixed, _row(p["b_group_b"][slot]), _row(p["b_scale"][slot]),
                                                name=f"b_aff_bwd_{tag}")
            g["b_scale"][slot], g["b_group_b"][slot] = dscale, dgb
            dpooled = _mm_group(dmixed, p["b_group_w"][slot], mode="nt", name=f"b_dpool_{tag}")
            g["b_group_w"][slot] = _mm_group(pooled, dmixed, mode="tn", tm=TK_TOKENS, name=f"b_dw_{tag}")
            dh = _b_pool_bwd(dpooled, name=f"b_pool_bwd_{tag}")
        else:
            bcx, z = saved[j]
            dz = _mm(dy, p["c_out_w"][slot], mode="nt", name=f"c_ddz_{tag}")
            g["c_out_w"][slot] = _mm(z, dy, mode="tn", tk=TK_TOKENS, name=f"c_dwout_{tag}")
            dbcx, dwc = _c_gate_bwd(bcx, dz, p["c_conv_w"][slot], name=f"c_gate_bwd_{tag}")
            g["c_conv_w"][slot] = dwc
            dh = _mm(dbcx, p["c_in_w"], mode="nt", layer=slot, name=f"c_ddh_{tag}")
            g["c_in_w"][slot] = _mm(h, dbcx, mode="tn", tk=TK_TOKENS, layer=slot, name=f"c_dwin_{tag}")
        if j > 0:
            pj = j - 1
            dy_dtype = F32 if _kind_of(pj) == "b" else BF16
            dx, dy, dsh, dsc, dgpre, dgpost, dgt, sdy = _bwd_mid(
                dx, dh, xs[j], gpre[j], sc[j], ys[pj], gpost[pj], gt[pj], dy_dtype=dy_dtype, name=f"bwd_mid_{j}")
        else:
            dx, dsh, dsc, dgpre = _bwd_first(dx, dh, xs[0], gpre[0], sc[0], name="bwd_first")
        dmod[i][3 * sub + 0] = dsh
        dmod[i][3 * sub + 1] = dsc
        g[norm_names[sub][0]][i] = dgpre

    small = {k: jnp.stack(v).reshape(p[k].shape) for k, v in g.items() if k not in BIG}
    big = {k: v for k, v in g.items() if k in BIG}
    dmod_arr = jnp.stack([jnp.concatenate(r, axis=0) for r in dmod])
    return loss, dx, dmod_arr, small, big


_MESH = pl.DeviceIdType.MESH
_ANY = pl.BlockSpec(memory_space=pl.ANY)
_VMEM = pl.BlockSpec(memory_space=pltpu.VMEM)


def _place():
    return lax.axis_index("x"), lax.axis_index("y"), lax.axis_index("c")


def _other_chips(x, y):
    return [(1 - x, y), (x, 1 - y), (1 - x, 1 - y)]


def _remote(src, dst, send_sem, recv_sem, to):
    return pltpu.make_async_remote_copy(src_ref=src, dst_ref=dst, send_sem=send_sem, recv_sem=recv_sem,
                                        device_id=to, device_id_type=_MESH)


def _all_gather8(blk, *, name):
    r, cdim = blk.shape

    def body(x_ref, out_ref, send_sems, recv_sems, local_sem):
        x, y, c = _place()
        me, sibling = (x, y, c), (x, y, 1 - c)
        chips = _other_chips(x, y)

        def slot(px, py, pc):
            return out_ref.at[4 * px + 2 * py + pc]

        def copy(k, block, to, src=None):
            return _remote(slot(*block) if src is None else src, slot(*block),
                           send_sems.at[k], recv_sems.at[k], to)

        mine = pltpu.make_async_copy(x_ref, slot(*me), local_sem)
        mine.start()
        first = [copy(0, me, sibling, src=x_ref)]
        first += [copy(1 + j, me, (*chip, c), src=x_ref) for j, chip in enumerate(chips)]
        for cp in first:
            cp.start()
        passed = [copy(4 + j, (*chip, c), sibling) for j, chip in enumerate(chips)]
        for j, chip in enumerate(chips):
            copy(1 + j, (*chip, c), me).wait_recv()
            passed[j].start()
        copy(0, sibling, me).wait_recv()
        for j, chip in enumerate(chips):
            copy(4 + j, (*chip, 1 - c), me).wait_recv()
        for cp in first + passed:
            cp.wait_send()
        mine.wait()

    return pl.pallas_call(
        body, out_shape=jax.ShapeDtypeStruct((NDEV, r, cdim), blk.dtype),
        in_specs=[_VMEM], out_specs=_VMEM,
        scratch_shapes=[pltpu.SemaphoreType.DMA((7,)), pltpu.SemaphoreType.DMA((7,)), pltpu.SemaphoreType.DMA],
        compiler_params=pltpu.CompilerParams(vmem_limit_bytes=VMEM_LIMIT), name=name)(blk)


def _gather_dst(kind):
    if kind == "col":
        return lambda s, h: (s, h)
    if kind == "row":
        return lambda s, h: (h, slice(None), s)
    return lambda s, h: (slice(None), s, h)


def _cast_into_gathered(src, kind, out_shape, shard, *, name):
    _, a, rh, cdim = src.shape
    tr = _pick(rh, (512, 256, 128, 64, 32, 16))
    if kind == "col":
        out_idx = lambda h, ai, r, s: (s[0], h, ai, r, 0)
    elif kind == "row":
        out_idx = lambda h, ai, r, s: (h, ai, s[0], r, 0)
    else:
        out_idx = lambda h, ai, r, s: (ai, s[0], h, r, 0)

    def body(s_ref, x_ref, o_ref):
        o_ref[...] = x_ref[...].astype(BF16)

    grid_spec = pltpu.PrefetchScalarGridSpec(
        num_scalar_prefetch=1, grid=(2, a, rh // tr),
        in_specs=[pl.BlockSpec((None, None, tr, cdim), lambda h, ai, r, s: (h, ai, r, 0))],
        out_specs=pl.BlockSpec((None, None, None, tr, cdim), out_idx))
    return pl.pallas_call(
        body, grid_spec=grid_spec, out_shape=jax.ShapeDtypeStruct(out_shape, BF16),
        compiler_params=_cparams(_PAR, _PAR, _PAR), name=name)(shard, src)


def _gather_weights(bufs, kinds, *, name):
    nt = len(bufs)

    def body(*refs):
        out_refs = refs[nt:2 * nt]
        send_sems, recv_sems = refs[2 * nt:]
        x, y, c = _place()
        sibling = (x, y, 1 - c)
        chips = _other_chips(x, y)
        s_me = 2 * x + y

        def at(k, s, h):
            return out_refs[k].at[_gather_dst(kinds[k])(s, h)]

        sends, passed = [], []
        for k in range(nt):
            for j, chip in enumerate(chips):
                cp = _remote(at(k, s_me, c), at(k, s_me, c), send_sems.at[k, j], recv_sems.at[k, j], (*chip, c))
                cp.start()
                sends.append(cp)
        for k in range(nt):
            for j, (px, py) in enumerate(chips):
                got = at(k, 2 * px + py, c)
                _remote(got, got, send_sems.at[k, j], recv_sems.at[k, j], (px, py, c)).wait_recv()
                cp = _remote(got, got, send_sems.at[k, 3 + j], recv_sems.at[k, 3 + j], sibling)
                cp.start()
                passed.append(cp)
        for k in range(nt):
            for j, (px, py) in enumerate(chips):
                got = at(k, 2 * px + py, 1 - c)
                _remote(got, got, send_sems.at[k, 3 + j], recv_sems.at[k, 3 + j], sibling).wait_recv()
        for cp in sends + passed:
            cp.wait_send()

    return pl.pallas_call(
        body, out_shape=[jax.ShapeDtypeStruct(b.shape, BF16) for b in bufs],
        in_specs=[_ANY] * nt, out_specs=[_ANY] * nt, input_output_aliases={k: k for k in range(nt)},
        scratch_shapes=[pltpu.SemaphoreType.DMA((nt, 6)), pltpu.SemaphoreType.DMA((nt, 6))],
        name=name)(*bufs)


def _pair_exchange(gs, layers_of, *, name):
    n, nt = len(gs), len(layers_of)

    def body(*refs):
        g_refs, out_refs = refs[:n], refs[n:n + nt]
        send_sems, recv_sems = refs[n + nt:]
        x, y, c = _place()
        sibling = (x, y, 1 - c)
        copies = []
        for t, ks in enumerate(layers_of):
            for l, k in enumerate(ks):
                cp = _remote(g_refs[k].at[:, 1 - c], out_refs[t].at[l], send_sems.at[k], recv_sems.at[k], sibling)
                cp.start()
                copies.append(cp)
        for cp in copies:
            cp.wait()

    out_shape = [jax.ShapeDtypeStruct((len(ks), NSHARD) + gs[ks[0]].shape[2:], F32) for ks in layers_of]
    return pl.pallas_call(
        body, out_shape=out_shape, in_specs=[_ANY] * n, out_specs=[_ANY] * nt,
        scratch_shapes=[pltpu.SemaphoreType.DMA((n,)), pltpu.SemaphoreType.DMA((n,))],
        name=name)(*gs)


def _pair_sum(g, r1, s_acc, layer, half, *, name):
    _, _, rh, cdim = g.shape
    tr = _pick(rh, (256, 128, 176, 64, 32, 16))

    def body(half_ref, g_ref, r_ref, s_in_ref, o_ref):
        o_ref[...] = (g_ref[...] + r_ref[...]).astype(BF16)

    grid_spec = pltpu.PrefetchScalarGridSpec(
        num_scalar_prefetch=1, grid=(NSHARD, rh // tr),
        in_specs=[pl.BlockSpec((None, None, tr, cdim), lambda s, r, hf: (s, hf[0], r, 0)),
                  pl.BlockSpec((None, None, tr, cdim), lambda s, r, hf: (layer, s, r, 0)),
                  _ANY],
        out_specs=pl.BlockSpec((None, None, tr, cdim), lambda s, r, hf: (layer, s, r, 0)))
    return pl.pallas_call(
        body, grid_spec=grid_spec, out_shape=jax.ShapeDtypeStruct(s_acc.shape, BF16),
        input_output_aliases={3: 0},
        compiler_params=_cparams(_PAR, _PAR), name=name)(half, g, r1, s_acc)


def _chip_exchange(ss, *, name):
    nt = len(ss)

    def body(*refs):
        s_refs, out_refs = refs[:nt], refs[nt:2 * nt]
        send_sems, recv_sems = refs[2 * nt:]
        x, y, c = _place()
        copies = []
        for t in range(nt):
            for j, (px, py) in enumerate(_other_chips(x, y)):
                cp = _remote(s_refs[t].at[:, 2 * px + py], out_refs[t].at[j],
                             send_sems.at[t, j], recv_sems.at[t, j], (px, py, c))
                cp.start()
                copies.append(cp)
        for cp in copies:
            cp.wait()

    out_shape = [jax.ShapeDtypeStruct((3, s.shape[0]) + s.shape[2:], BF16) for s in ss]
    return pl.pallas_call(
        body, out_shape=out_shape, in_specs=[_ANY] * nt, out_specs=[_ANY] * nt,
        scratch_shapes=[pltpu.SemaphoreType.DMA((nt, 3)), pltpu.SemaphoreType.DMA((nt, 3))],
        name=name)(*ss)


def _chip_sum(s_t, r3_t, place, *, name):
    nl, _, rh, cdim = s_t.shape
    tr = _pick(rh, (256, 128, 176, 64, 32, 16))

    def body(pz, s_ref, r_ref, o_ref):
        acc = s_ref[...].astype(F32) + r_ref[0].astype(F32)
        o_ref[...] = (acc + r_ref[1].astype(F32)) + r_ref[2].astype(F32)

    grid_spec = pltpu.PrefetchScalarGridSpec(
        num_scalar_prefetch=1, grid=(nl, rh // tr),
        in_specs=[pl.BlockSpec((None, None, tr, cdim), lambda l, r, pz: (l, pz[0], r, 0)),
                  pl.BlockSpec((3, None, tr, cdim), lambda l, r, pz: (0, l, r, 0))],
        out_specs=pl.BlockSpec((None, None, tr, cdim), lambda l, r, pz: (l, pz[1], r, 0)))
    return pl.pallas_call(
        body, grid_spec=grid_spec, out_shape=jax.ShapeDtypeStruct((nl, 2, rh, cdim), F32),
        compiler_params=_cparams(_PAR, _PAR), name=name)(place, s_t, r3_t)


def _join_halves(reds, *, name):
    nt = len(reds)

    def body(*refs):
        out_refs = refs[nt:2 * nt]
        send_sems, recv_sems = refs[2 * nt:]
        x, y, c = _place()
        sibling = (x, y, 1 - c)
        copies = []
        for t in range(nt):
            cp = _remote(out_refs[t].at[:, c], out_refs[t].at[:, c], send_sems.at[t], recv_sems.at[t], sibling)
            cp.start()
            copies.append(cp)
        for t, cp in enumerate(copies):
            cp.wait_send()
            got = out_refs[t].at[:, 1 - c]
            _remote(got, got, send_sems.at[t], recv_sems.at[t], sibling).wait_recv()

    return pl.pallas_call(
        body, out_shape=[jax.ShapeDtypeStruct(r.shape, F32) for r in reds],
        in_specs=[_ANY] * nt, out_specs=[_ANY] * nt, input_output_aliases={t: t for t in range(nt)},
        scratch_shapes=[pltpu.SemaphoreType.DMA((nt,)), pltpu.SemaphoreType.DMA((nt,))],
        name=name)(*reds)


def _sum_devices(g, *, name):
    _, r, cdim = g.shape
    tr = _pick(r, (512, 256, 128, 64, 32, 16, 8))

    def body(g_ref, o_ref):
        acc = g_ref[0]
        for e in range(1, NDEV):
            acc = acc + g_ref[e]
        o_ref[...] = acc

    return pl.pallas_call(
        body, grid=(r // tr,), in_specs=[pl.BlockSpec((NDEV, tr, cdim), lambda i: (0, i, 0))],
        out_specs=pl.BlockSpec((tr, cdim), lambda i: (i, 0)),
        out_shape=jax.ShapeDtypeStruct((r, cdim), F32),
        compiler_params=_cparams(_PAR), name=name)(g)


def _mod_fwd(c_all, mod_w, mod_b_cols, *, name):
    nl, d, n = mod_w.shape
    ne = c_all.shape[0]
    tn = _pick(n, (768, 512, 384, 256, 128))

    def body(c_ref, w_ref, b_ref, o_ref):
        cv = c_ref[...]
        act = (cv * _sigmoid(cv)).astype(BF16)
        o_ref[...] = jnp.dot(act, w_ref[...].astype(BF16), preferred_element_type=F32) + b_ref[...]

    return pl.pallas_call(
        body, grid=(nl, n // tn),
        in_specs=[pl.BlockSpec((ne, d), lambda i, j: (0, 0)),
                  pl.BlockSpec((None, d, tn), lambda i, j: (i, 0, j)),
                  pl.BlockSpec((None, 1, tn), lambda i, j: (i, 0, j))],
        out_specs=pl.BlockSpec((None, ne, tn), lambda i, j: (i, 0, j)),
        out_shape=jax.ShapeDtypeStruct((nl, ne, n), F32),
        compiler_params=_cparams(_PAR, _PAR), name=name)(c_all, mod_w, mod_b_cols)


def _adam_math(w, g, m, v):
    m2 = ADAM_B1 * m + (1.0 - ADAM_B1) * g
    v2 = ADAM_B2 * v + (1.0 - ADAM_B2) * (g * g)
    m_hat = m2 / (1.0 - ADAM_B1 ** ADAM_STEP)
    v_hat = v2 / (1.0 - ADAM_B2 ** ADAM_STEP)
    delta = -ADAM_LR * (m_hat / (jnp.sqrt(v_hat) + ADAM_EPS) + ADAM_WD * w)
    return delta, m2, v2


def _adamw(w, g, m, v, *, name):
    rows, cdim = w.shape
    tr = _pick(rows, tuple(t for t in (512, 256, 128, 64, 32, 16, 8) if t * cdim <= 256 * 1024))

    def body(w_ref, g_ref, m_ref, v_ref, d_ref, mo_ref, vo_ref):
        d_ref[...], mo_ref[...], vo_ref[...] = _adam_math(w_ref[...], g_ref[...], m_ref[...], v_ref[...])

    spec = pl.BlockSpec((tr, cdim), lambda i: (i, 0))
    return pl.pallas_call(
        body, grid=(rows // tr,), in_specs=[spec] * 4, out_specs=[spec] * 3,
        out_shape=[jax.ShapeDtypeStruct((rows, cdim), F32)] * 3,
        compiler_params=_cparams(_PAR), name=name)(w, g, m, v)


def _mod_w_update(c_t, dmod, w, m, v, *, name):
    nl, d, n = w.shape
    ne = c_t.shape[1]
    tr = _pick(d, (128, 64, 32, 16, 8))

    def body(c_ref, dm_ref, w_ref, m_ref, v_ref, g_ref, d_ref, mo_ref, vo_ref):
        cv = c_ref[...]
        act = cv * _sigmoid(cv)
        dm = dm_ref[...]
        g = act[:, 0:1] * dm[0:1, :]
        for e in range(1, ne):
            g = g + act[:, e:e + 1] * dm[e:e + 1, :]
        g_ref[...] = g
        d_ref[...], mo_ref[...], vo_ref[...] = _adam_math(w_ref[...], g, m_ref[...], v_ref[...])

    big = pl.BlockSpec((None, tr, n), lambda i, r: (i, r, 0))
    return pl.pallas_call(
        body, grid=(nl, d // tr),
        in_specs=[pl.BlockSpec((tr, ne), lambda i, r: (r, 0)),
                  pl.BlockSpec((None, ne, n), lambda i, r: (i, 0, 0)), big, big, big],
        out_specs=[big] * 4, out_shape=[jax.ShapeDtypeStruct((nl, d, n), F32)] * 4,
        compiler_params=_cparams(_PAR, _PAR), name=name)(c_t, dmod, w, m, v)


PACK_ROWS = 256


def _pack(arrs):
    flat = jnp.concatenate([a.reshape(-1) for a in arrs])
    tile = PACK_ROWS * LANES
    pad = (-flat.shape[0]) % tile
    return jnp.pad(flat, (0, pad)).reshape(-1, LANES)


def _unpack(packed, shapes, lead=()):
    flat = packed.reshape(lead + (-1,))
    out, off = [], 0
    for shp in shapes:
        size = 1
        for s in shp:
            size *= s
        out.append(flat[..., off:off + size].reshape(lead + tuple(shp)))
        off += size
    return out


SMALL_SHARD_AXIS = {"a_pw1_b": 1, "a_dw_w": 2, "a_dw_b": 1, "a_ln_g": 1, "a_ln_b": 1, "a_pw2_b": 1,
                    "c_conv_w": 2, "f_dw_w": 2}
SMALL_REPLICATED = ("norm_pre_mix", "norm_post_mix", "norm_pre_ffn", "norm_post_ffn",
                    "b_group_b", "b_scale", "f_dw_b")
WEIGHT_ORDER = ("mod_w", "mod_b", "norm_pre_mix", "norm_post_mix", "norm_pre_ffn", "norm_post_ffn",
                "a_pw1_w", "a_pw1_b", "a_dw_w", "a_dw_b", "a_ln_g", "a_ln_b", "a_pw2_w", "a_pw2_b",
                "b_group_w", "b_group_b", "b_scale", "c_in_w", "c_conv_w", "c_out_w",
                "f_up_w", "f_dw_w", "f_dw_b", "f_down_w")


def _as_layers_rows_cols(name, w):
    if name == "b_group_w":
        return w.reshape(w.shape[1], w.shape[2], w.shape[3])
    return w


def _step(x, c, loss_target, w, m, v):
    xi, yi, ci = _place()
    shard = 2 * xi + yi
    example = 4 * xi + 2 * yi + ci
    d = x.shape[-1]

    small_names = tuple(SMALL_SHARD_AXIS)
    gathered0 = _all_gather8(_pack([c] + [w[k] for k in small_names]), name="gather_small")
    parts = _unpack(gathered0, [c.shape] + [w[k].shape for k in small_names], lead=(NDEV,))
    c_all = parts[0].reshape(NDEV, d)
    p = {}
    for k, part in zip(small_names, parts[1:]):
        p[k] = jnp.concatenate([part[2 * s] for s in range(NSHARD)], axis=SMALL_SHARD_AXIS[k])
    for k in SMALL_REPLICATED:
        p[k] = w[k]

    ncol = w["mod_w"].shape[2]
    mod_b_cols = lax.dynamic_slice_in_dim(w["mod_b"], shard * ncol, ncol, axis=1).reshape(DEPTH, 1, ncol)
    mod_part = _mod_fwd(c_all, w["mod_w"], mod_b_cols, name="mod_fwd")
    gathered1 = _all_gather8(mod_part.reshape(DEPTH * NDEV, ncol), name="gather_mod")
    mod_all = gathered1.reshape(NSHARD, 2, DEPTH, NDEV, ncol)[:, 0]
    mod_mine = lax.dynamic_index_in_dim(mod_all, example, axis=2, keepdims=False)
    mod = jnp.transpose(mod_mine, (1, 0, 2)).reshape(DEPTH, 6, d)

    shard_arr = shard.reshape(1).astype(jnp.int32)
    bufs, kinds = [], []
    for k in BIG:
        wk = _as_layers_rows_cols(k, w[k])
        nl, r, cdim = wk.shape
        if nl >= 2:
            a, rh = nl // 2, r
        else:
            a, rh = 1, r // 2
        if k in COL_SHARDED:
            kind, out_shape = "col", (NSHARD, 2, a, rh, cdim)
        elif nl >= 2:
            kind, out_shape = "row", (2, a, NSHARD, rh, cdim)
        else:
            kind, out_shape = "row1", (1, NSHARD, 2, rh, cdim)
        kinds.append(kind)
        bufs.append(_cast_into_gathered(wk.reshape(2, a, rh, cdim), kind, out_shape, shard_arr, name=f"cast_{k}"))
    full = _gather_weights(bufs, kinds, name="gather_weights")
    for k, f in zip(BIG, full):
        nl, r, cdim = _as_layers_rows_cols(k, w[k]).shape
        if k in COL_SHARDED:
            p[k] = f.reshape(NSHARD, nl, r, cdim)
        elif k == "b_group_w":
            p[k] = f.reshape(1, nl, NSHARD * r, cdim)
        else:
            p[k] = f.reshape(nl, NSHARD * r, cdim)

    loss, grad_x, dmod, small, big = _local_step(x[0], loss_target[0], mod, p)

    gs, layers_of = [], []
    for k in BIG:
        ks = []
        for g in big[k]:
            if k == "b_group_w":
                ng, rr, cc = g.shape
                g = jnp.transpose(g.reshape(ng, NSHARD, rr // NSHARD, cc), (1, 0, 2, 3)).reshape(NSHARD, -1, cc)
            elif k not in COL_SHARDED:
                g = g.reshape(NSHARD, g.shape[0] // NSHARD, g.shape[1])
            ks.append(len(gs))
            gs.append(g.reshape(NSHARD, 2, g.shape[1] // 2, g.shape[2]))
        layers_of.append(ks)
    half = ci.reshape(1).astype(jnp.int32)
    place = jnp.stack([shard, ci]).astype(jnp.int32)
    r1 = _pair_exchange(gs, layers_of, name="grad_pair_exchange")
    ss = []
    for k, ks, r1_t in zip(BIG, layers_of, r1):
        s_t = lax.empty(r1_t.shape, BF16)
        for l, i in enumerate(ks):
            s_t = _pair_sum(gs[i], r1_t, s_t, l, half, name=f"grad_pair_sum_{k}_{l}")
        ss.append(s_t)
    r3 = _chip_exchange(ss, name="grad_chip_exchange")
    reds = [_chip_sum(s_t, r3_t, place, name=f"grad_chip_sum_{k}") for k, s_t, r3_t in zip(BIG, ss, r3)]
    joined = _join_halves(reds, name="grad_join_halves")
    grads = {k: j.reshape(w[k].shape) for k, j in zip(BIG, joined)}

    rep_names = SMALL_REPLICATED
    small_list = [small[k] for k in rep_names] + [small[k] for k in small_names] + [dmod]
    gathered2 = _all_gather8(_pack(small_list), name="gather_small_grads")
    summed = _sum_devices(gathered2, name="sum_small_grads")
    shapes = [s.shape for s in small_list]
    sums = _unpack(summed, shapes)
    for k, s in zip(rep_names, sums[:len(rep_names)]):
        grads[k] = s
    for k, s in zip(small_names, sums[len(rep_names):-1]):
        ax = SMALL_SHARD_AXIS[k]
        grads[k] = lax.dynamic_slice_in_dim(s, shard * w[k].shape[ax], w[k].shape[ax], axis=ax)
    grads["mod_b"] = sums[-1].reshape(w["mod_b"].shape)
    dmod_all = _unpack(gathered2, shapes, lead=(NDEV,))[-1].reshape(NDEV, DEPTH, NSHARD, ncol)
    dmod_cols = jnp.transpose(lax.dynamic_index_in_dim(dmod_all, shard, axis=2, keepdims=False), (1, 0, 2))

    delta, new_m, new_v = {}, {}, {}
    grads["mod_w"], delta["mod_w"], new_m["mod_w"], new_v["mod_w"] = _mod_w_update(
        c_all.T, dmod_cols, w["mod_w"], m["mod_w"], v["mod_w"], name="mod_w_update")
    for k in BIG:
        cdim = w[k].shape[-1]
        outs = _adamw(*[t.reshape(-1, cdim) for t in (w[k], grads[k], m[k], v[k])], name=f"adamw_{k}")
        delta[k], new_m[k], new_v[k] = [o.reshape(w[k].shape) for o in outs]
    rest = ("mod_b",) + rep_names + small_names
    packs = [_pack([t[k] for k in rest]) for t in (w, grads, m, v)]
    outs = _adamw(*packs, name="adamw_small")
    rest_shapes = [w[k].shape for k in rest]
    for dst, o in zip((delta, new_m, new_v), outs):
        for k, t in zip(rest, _unpack(o, rest_shapes)):
            dst[k] = t

    loss_all = lax.psum(loss[0, 0], ("x", "y", "c"))
    return (loss_all, grad_x[None], *[grads[k] for k in WEIGHT_ORDER], *[delta[k] for k in WEIGHT_ORDER],
            *[new_m[k] for k in WEIGHT_ORDER], *[new_v[k] for k in WEIGHT_ORDER])


def kernel(x, c, mod_w, mod_b, norm_pre_mix, norm_post_mix, norm_pre_ffn, norm_post_ffn, a_pw1_w, a_pw1_b, a_dw_w, a_dw_b, a_ln_g, a_ln_b, a_pw2_w, a_pw2_b, b_group_w, b_group_b, b_scale, c_in_w, c_conv_w, c_out_w, f_up_w, f_dw_w, f_dw_b, f_down_w, loss_target, m_mod_w, m_mod_b, m_norm_pre_mix, m_norm_post_mix, m_norm_pre_ffn, m_norm_post_ffn, m_a_pw1_w, m_a_pw1_b, m_a_dw_w, m_a_dw_b, m_a_ln_g, m_a_ln_b, m_a_pw2_w, m_a_pw2_b, m_b_group_w, m_b_group_b, m_b_scale, m_c_in_w, m_c_conv_w, m_c_out_w, m_f_up_w, m_f_dw_w, m_f_dw_b, m_f_down_w, v_mod_w, v_mod_b, v_norm_pre_mix, v_norm_post_mix, v_norm_pre_ffn, v_norm_post_ffn, v_a_pw1_w, v_a_pw1_b, v_a_dw_w, v_a_dw_b, v_a_ln_g, v_a_ln_b, v_a_pw2_w, v_a_pw2_b, v_b_group_w, v_b_group_b, v_b_scale, v_c_in_w, v_c_conv_w, v_c_out_w, v_f_up_w, v_f_dw_w, v_f_dw_b, v_f_down_w):
    given = dict(locals())
    w = {k: given[k] for k in WEIGHT_ORDER}
    m = {k: given["m_" + k] for k in WEIGHT_ORDER}
    v = {k: given["v_" + k] for k in WEIGHT_ORDER}
    return _step(x, c, loss_target, w, m, v)
```

```python
import functools

import jax
import jax.numpy as jnp
from jax import lax
from jax.experimental import pallas as pl
from jax.experimental.pallas import tpu as pltpu

F32 = jnp.float32
BF16 = jnp.bfloat16

DEPTH = 4
N_MIXERS = 3
CONF_CONV_WIDTH = 31
POOL_WINDOWS = (2, 4, 8, 16)
RMS_EPS = 1e-6
LN_EPS = 1e-5
ADAM_LR = 0.001
ADAM_B1 = 0.9
ADAM_B2 = 0.999
ADAM_EPS = 1e-08
ADAM_WD = 0.01
ADAM_STEP = 10

TM_ROW = 512
TM_CONV = 128
TK_TOKENS = 1024
TM_MM = 1024
TN_MM = 1024
SUBLANES = 8
LANES = 128
NSHARD = 4
NDEV = 8
HALO_A = 32
HALO_POOL = 16
HALO_3 = 8
VMEM_LIMIT = 56 * 1024 * 1024

_PAR = "parallel"
_ARB = "arbitrary"


def _cparams(*sem):
    return pltpu.CompilerParams(dimension_semantics=sem, vmem_limit_bytes=VMEM_LIMIT)


def _pick(n, prefs):
    for p in prefs:
        if p <= n and n % p == 0:
            return p
    return n


def _row_spec(tm, width):
    return pl.BlockSpec((tm, width), lambda i: (i, 0))


def _vec_spec(rows, width):
    return pl.BlockSpec((rows, width), lambda i: (0, 0))


def _prev_spec(tm, hb, width):
    return pl.BlockSpec((hb, width), lambda i: (jnp.maximum(i * (tm // hb) - 1, 0), 0))


def _next_spec(tm, hb, width, total):
    last = total // hb - 1
    return pl.BlockSpec((hb, width), lambda i: (jnp.minimum((i + 1) * (tm // hb), last), 0))


def _sum8(v):
    r, c = v.shape
    return jnp.sum(v.reshape(r // SUBLANES, SUBLANES, c), axis=0)


def _rms(x):
    r = lax.rsqrt(jnp.mean(x * x, axis=-1, keepdims=True) + RMS_EPS)
    return x * r, r


def _rms_bwd(dy, xn, r):
    return r * (dy - xn * jnp.mean(dy * xn, axis=-1, keepdims=True))


def _sigmoid(x):
    return 1.0 / (1.0 + jnp.exp(-x))


_DIMS = {"nn": ((1,), (0,)), "nt": ((1,), (1,)), "tn": ((0,), (0,))}


def _mm(a, b, *, mode, name, out_dtype=F32, bias=None, tm=TM_MM, tn=TN_MM, tk=None, layer=None):
    sharded = layer is not None
    if mode == "nn":
        m, k = a.shape
        n = NSHARD * b.shape[3] if sharded else b.shape[1]
    elif mode == "nt":
        m, k = a.shape
        n = b.shape[2] if sharded else b.shape[0]
    else:
        (k, m), (_, n) = a.shape, b.shape
    ns = n // NSHARD
    ks = k // NSHARD
    tm = _pick(m, (tm, 1408, 512, 256, 128))
    if sharded and mode != "nt":
        tn = _pick(ns, (1408, 768, 512, 256, 128))
    else:
        tn = _pick(n, (tn, 1408, 512, 256, 128))
    if sharded and mode == "nt":
        tk = _pick(ks, (1408, 768, 512, 256, 128))
    else:
        tk = _pick(k, (tk or k, 2816, 1024, 512, 256, 128))
    nk = k // tk
    per_n = ns // tn if sharded and mode != "nt" else 1
    per_k = ks // tk if sharded and mode == "nt" else 1
    dims = (_DIMS[mode], ((), ()))

    def split(idx, per):
        return (idx, 0) if per == 1 else (idx // per, idx % per)

    def body(*refs):
        a_ref, b_ref = refs[0], refs[1]
        bias_ref = refs[2] if bias is not None else None
        o_ref = refs[3] if bias is not None else refs[2]
        part = lax.dot_general(a_ref[...].astype(BF16), b_ref[...].astype(BF16), dims,
                               preferred_element_type=F32)

        def finish(r):
            if bias_ref is not None:
                r = r + bias_ref[...]
            o_ref[...] = r.astype(out_dtype)

        if nk == 1:
            finish(part)
        else:
            acc_ref = refs[-1]
            kk = pl.program_id(2)

            @pl.when(kk == 0)
            def _():
                acc_ref[...] = part

            @pl.when(kk > 0)
            def _():
                acc_ref[...] += part

            @pl.when(kk == nk - 1)
            def _():
                finish(acc_ref[...])

    out_spec = pl.BlockSpec((tm, tn), lambda i, j, kk: (i, j))
    out_shape = jax.ShapeDtypeStruct((m, n), out_dtype)
    if mode == "nn":
        a_spec = pl.BlockSpec((tm, tk), lambda i, j, kk: (i, kk))
        if sharded:
            b_spec = pl.BlockSpec((None, None, tk, tn),
                                  lambda i, j, kk: (split(j, per_n)[0], layer, kk, split(j, per_n)[1]))
        else:
            b_spec = pl.BlockSpec((tk, tn), lambda i, j, kk: (kk, j))
    elif mode == "nt":
        a_spec = pl.BlockSpec((tm, tk), lambda i, j, kk: (i, kk))
        if sharded:
            b_spec = pl.BlockSpec((None, None, tn, tk),
                                  lambda i, j, kk: (split(kk, per_k)[0], layer, j, split(kk, per_k)[1]))
        else:
            b_spec = pl.BlockSpec((tn, tk), lambda i, j, kk: (j, kk))
    else:
        a_spec = pl.BlockSpec((tk, tm), lambda i, j, kk: (kk, i))
        b_spec = pl.BlockSpec((tk, tn), lambda i, j, kk: (kk, j))
        if sharded:
            out_spec = pl.BlockSpec((None, tm, tn), lambda i, j, kk: (split(j, per_n)[0], i, split(j, per_n)[1]))
            out_shape = jax.ShapeDtypeStruct((NSHARD, m, ns), out_dtype)
    in_specs = [a_spec, b_spec]
    args = [a, b]
    if bias is not None:
        in_specs.append(pl.BlockSpec((1, tn), lambda i, j, kk: (0, j)))
        args.append(bias)
    return pl.pallas_call(
        body, grid=(m // tm, n // tn, nk), in_specs=in_specs, out_specs=out_spec, out_shape=out_shape,
        scratch_shapes=[pltpu.VMEM((tm, tn), F32)] if nk > 1 else [],
        compiler_params=_cparams(_PAR, _PAR, _ARB), name=name)(*args)


def _mm_group(a, b, *, mode, name, out_dtype=F32, tm=2048):
    t = a.shape[0]
    tm = _pick(t, (tm, 1024, 512, 256, 128))
    nt_ = t // tm
    g = len(POOL_WINDOWS)
    gd = a.shape[1] // g
    dims = (_DIMS[mode], ((), ()))

    if mode == "tn":
        def body(a_ref, b_ref, o_ref, acc_ref):
            kk = pl.program_id(1)
            part = lax.dot_general(a_ref[...].astype(BF16), b_ref[...].astype(BF16), dims,
                                   preferred_element_type=F32)

            @pl.when(kk == 0)
            def _():
                acc_ref[...] = part

            @pl.when(kk > 0)
            def _():
                acc_ref[...] += part

            @pl.when(kk == nt_ - 1)
            def _():
                o_ref[...] = acc_ref[...]

        return pl.pallas_call(
            body, grid=(g, nt_),
            in_specs=[pl.BlockSpec((tm, gd), lambda gi, kk: (kk, gi)),
                      pl.BlockSpec((tm, gd), lambda gi, kk: (kk, gi))],
            out_specs=pl.BlockSpec((None, gd, gd), lambda gi, kk: (gi, 0, 0)),
            out_shape=jax.ShapeDtypeStruct((g, gd, gd), F32),
            scratch_shapes=[pltpu.VMEM((gd, gd), F32)],
            compiler_params=_cparams(_PAR, _ARB), name=name)(a, b)

    def body(a_ref, b_ref, o_ref):
        o_ref[...] = lax.dot_general(a_ref[...].astype(BF16), b_ref[...].astype(BF16), dims,
                                     preferred_element_type=F32).astype(out_dtype)

    return pl.pallas_call(
        body, grid=(nt_, g),
        in_specs=[pl.BlockSpec((tm, gd), lambda i, gi: (i, gi)),
                  pl.BlockSpec((None, gd, gd), lambda i, gi: (gi, 0, 0))],
        out_specs=pl.BlockSpec((tm, gd), lambda i, gi: (i, gi)),
        out_shape=jax.ShapeDtypeStruct((t, g * gd), out_dtype),
        compiler_params=_cparams(_PAR, _PAR), name=name)(a, b)


def _pre(x, gpre, sc, sh):
    xn, r = _rms(x)
    return (xn * gpre) * (1.0 + sc) + sh, xn, r


def _fwd_first(x, gpre, sc, sh, *, h_dtype, name):
    t, d = x.shape
    tm = _pick(t, (TM_ROW, 256, 128))

    def body(x_ref, gpre_ref, sc_ref, sh_ref, h_ref):
        h, _, _ = _pre(x_ref[...], gpre_ref[...], sc_ref[...], sh_ref[...])
        h_ref[...] = h.astype(h_dtype)

    return pl.pallas_call(
        body, grid=(t // tm,),
        in_specs=[_row_spec(tm, d)] + [_vec_spec(1, d)] * 3,
        out_specs=_row_spec(tm, d), out_shape=jax.ShapeDtypeStruct((t, d), h_dtype),
        compiler_params=_cparams(_PAR), name=name)(x, gpre, sc, sh)


def _fwd_mid(x, y, gpost, gt, gpre, sc, sh, *, h_dtype, name):
    t, d = x.shape
    tm = _pick(t, (TM_ROW, 256, 128))

    def body(x_ref, y_ref, gpost_ref, gt_ref, gpre_ref, sc_ref, sh_ref, xn_ref, h_ref):
        yn, _ = _rms(y_ref[...])
        x_new = x_ref[...] + gt_ref[...] * (yn * gpost_ref[...])
        xn_ref[...] = x_new
        h, _, _ = _pre(x_new, gpre_ref[...], sc_ref[...], sh_ref[...])
        h_ref[...] = h.astype(h_dtype)

    return pl.pallas_call(
        body, grid=(t // tm,),
        in_specs=[_row_spec(tm, d)] * 2 + [_vec_spec(1, d)] * 5,
        out_specs=[_row_spec(tm, d)] * 2,
        out_shape=[jax.ShapeDtypeStruct((t, d), F32), jax.ShapeDtypeStruct((t, d), h_dtype)],
        compiler_params=_cparams(_PAR), name=name)(x, y, gpost, gt, gpre, sc, sh)


def _post_bwd(dx, y, gpost, gt):
    yn, r2 = _rms(y)
    dyn = dx * (gt * gpost)
    dy = _rms_bwd(dyn, yn, r2)
    return dy, dx * yn


def _last_fwd_bwd(x, y, target, gpost, gt, *, name):
    t, d = x.shape
    tm = _pick(t, (TM_ROW, 256, 128))
    n = t // tm

    def body(x_ref, y_ref, tg_ref, gpost_ref, gt_ref, dx_ref, dy_ref, dgpost_ref, dgt_ref, sdy_ref,
             loss_ref, qa, sa, la):
        i = pl.program_id(0)

        @pl.when(i == 0)
        def _():
            qa[...] = jnp.zeros_like(qa)
            sa[...] = jnp.zeros_like(sa)
            la[...] = jnp.zeros_like(la)

        yv = y_ref[...]
        yn, r2 = _rms(yv)
        gt_v, gpost_v = gt_ref[...], gpost_ref[...]
        err = x_ref[...] + gt_v * (yn * gpost_v) - tg_ref[...]
        la[...] += _sum8(err * err)
        dx = err * (1.0 / d)
        dx_ref[...] = dx
        dy = _rms_bwd(dx * (gt_v * gpost_v), yn, r2)
        dy_ref[...] = dy.astype(dy_ref.dtype)
        qa[...] += _sum8(dx * yn)
        sa[...] += _sum8(dy)

        @pl.when(i == n - 1)
        def _():
            q = jnp.sum(qa[...], axis=0, keepdims=True)
            dgpost_ref[...] = gt_v * q
            dgt_ref[...] = gpost_v * q
            sdy_ref[...] = jnp.sum(sa[...], axis=0, keepdims=True)
            tot = jnp.sum(jnp.sum(la[...], axis=0, keepdims=True), axis=1, keepdims=True)
            loss_ref[...] = tot * (0.5 / d)

    return pl.pallas_call(
        body, grid=(n,),
        in_specs=[_row_spec(tm, d)] * 3 + [_vec_spec(1, d)] * 2,
        out_specs=[_row_spec(tm, d)] * 2 + [_vec_spec(1, d)] * 3 + [_vec_spec(1, 1)],
        out_shape=[jax.ShapeDtypeStruct((t, d), F32), jax.ShapeDtypeStruct((t, d), BF16)]
        + [jax.ShapeDtypeStruct((1, d), F32)] * 3 + [jax.ShapeDtypeStruct((1, 1), F32)],
        scratch_shapes=[pltpu.VMEM((SUBLANES, d), F32)] * 3,
        compiler_params=_cparams(_ARB), name=name)(x, y, target, gpost, gt)


def _bwd_mid(dx_new, dh, x_in, gpre, sc, y_prev, gpost_p, gt_p, *, dy_dtype, name):
    t, d = x_in.shape
    tm = _pick(t, (TM_ROW, 256, 128))
    n = t // tm

    def body(dxn_ref, dh_ref, x_ref, gpre_ref, sc_ref, y_ref, gpost_ref, gt_ref,
             dx_ref, dy_ref, dsh_ref, dsc_ref, dgpre_ref, dgpost_ref, dgt_ref, sdy_ref, a1, a2, aq, asd):
        i = pl.program_id(0)

        @pl.when(i == 0)
        def _():
            for a in (a1, a2, aq, asd):
                a[...] = jnp.zeros_like(a)

        dh_v = dh_ref[...]
        xn, r = _rms(x_ref[...])
        dx = dxn_ref[...] + _rms_bwd(dh_v * ((1.0 + sc_ref[...]) * gpre_ref[...]), xn, r)
        dx_ref[...] = dx
        a1[...] += _sum8(dh_v)
        a2[...] += _sum8(dh_v * xn)
        dy, dxyn = _post_bwd(dx, y_ref[...], gpost_ref[...], gt_ref[...])
        dy_ref[...] = dy.astype(dy_dtype)
        aq[...] += _sum8(dxyn)
        asd[...] += _sum8(dy)

        @pl.when(i == n - 1)
        def _():
            s2 = jnp.sum(a2[...], axis=0, keepdims=True)
            q = jnp.sum(aq[...], axis=0, keepdims=True)
            dsh_ref[...] = jnp.sum(a1[...], axis=0, keepdims=True)
            dsc_ref[...] = gpre_ref[...] * s2
            dgpre_ref[...] = (1.0 + sc_ref[...]) * s2
            dgpost_ref[...] = gt_ref[...] * q
            dgt_ref[...] = gpost_ref[...] * q
            sdy_ref[...] = jnp.sum(asd[...], axis=0, keepdims=True)

    return pl.pallas_call(
        body, grid=(n,),
        in_specs=[_row_spec(tm, d)] * 3 + [_vec_spec(1, d)] * 2 + [_row_spec(tm, d)] + [_vec_spec(1, d)] * 2,
        out_specs=[_row_spec(tm, d)] * 2 + [_vec_spec(1, d)] * 6,
        out_shape=[jax.ShapeDtypeStruct((t, d), F32), jax.ShapeDtypeStruct((t, d), dy_dtype)]
        + [jax.ShapeDtypeStruct((1, d), F32)] * 6,
        scratch_shapes=[pltpu.VMEM((SUBLANES, d), F32)] * 4,
        compiler_params=_cparams(_ARB), name=name)(dx_new, dh, x_in, gpre, sc, y_prev, gpost_p, gt_p)


def _bwd_first(dx_new, dh, x_in, gpre, sc, *, name):
    t, d = x_in.shape
    tm = _pick(t, (TM_ROW, 256, 128))
    n = t // tm

    def body(dxn_ref, dh_ref, x_ref, gpre_ref, sc_ref, dx_ref, dsh_ref, dsc_ref, dgpre_ref, a1, a2):
        i = pl.program_id(0)

        @pl.when(i == 0)
        def _():
            a1[...] = jnp.zeros_like(a1)
            a2[...] = jnp.zeros_like(a2)

        dh_v = dh_ref[...]
        xn, r = _rms(x_ref[...])
        dx_ref[...] = dxn_ref[...] + _rms_bwd(dh_v * ((1.0 + sc_ref[...]) * gpre_ref[...]), xn, r)
        a1[...] += _sum8(dh_v)
        a2[...] += _sum8(dh_v * xn)

        @pl.when(i == n - 1)
        def _():
            s2 = jnp.sum(a2[...], axis=0, keepdims=True)
            dsh_ref[...] = jnp.sum(a1[...], axis=0, keepdims=True)
            dsc_ref[...] = gpre_ref[...] * s2
            dgpre_ref[...] = (1.0 + sc_ref[...]) * s2

    return pl.pallas_call(
        body, grid=(n,),
        in_specs=[_row_spec(tm, d)] * 3 + [_vec_spec(1, d)] * 2,
        out_specs=[_row_spec(tm, d)] + [_vec_spec(1, d)] * 3,
        out_shape=[jax.ShapeDtypeStruct((t, d), F32)] + [jax.ShapeDtypeStruct((1, d), F32)] * 3,
        scratch_shapes=[pltpu.VMEM((SUBLANES, d), F32)] * 2,
        compiler_params=_cparams(_ARB), name=name)(dx_new, dh, x_in, gpre, sc)


def _conv3_rows(buf, w_ref, rows, first):
    out = buf[pl.ds(first, rows), :] * w_ref[pl.ds(0, 1), :]
    for k in (1, 2):
        out = out + buf[pl.ds(first + k, rows), :] * w_ref[pl.ds(k, 1), :]
    return out


ROWS_BLK = 16
COLS_BLK = 256


def _bcast_rows(dst, src_ref, first, nrows):
    for k in range(nrows):
        dst[first + k] = jnp.broadcast_to(src_ref[pl.ds(k, 1), :], dst.shape[1:])


def _shifted_rows(x, off, rows):
    if off % SUBLANES == 0:
        return x[off:off + rows]
    return pltpu.roll(x, x.shape[0] - off, axis=0)[:rows]


def _conv3_blk(buf, wb, first, rows, cols):
    base = first - first % SUBLANES
    window = buf[pl.ds(base, rows + SUBLANES), cols]
    xs = [_shifted_rows(window, first - base + k, rows) for k in range(3)]
    out = xs[0] * wb[0, pl.ds(0, rows), cols]
    out = out + xs[1] * wb[1, pl.ds(0, rows), cols]
    out = out + xs[2] * wb[2, pl.ds(0, rows), cols]
    return out + wb[3, pl.ds(0, rows), cols], xs


def _ffn_gate_fwd(u, w, b, *, name):
    t, f2 = u.shape
    f = f2 // 2
    tm = _pick(t, (TM_CONV,))
    hb = HALO_3
    rb = ROWS_BLK
    cw = _pick(f, (COLS_BLK, LANES))

    def body(u_ref, up_ref, w_ref, b_ref, a_ref, v_ref, buf, wb):
        i = pl.program_id(0)
        buf[pl.ds(hb, tm), :] = u_ref[...]
        buf[pl.ds(0, hb), :] = jnp.where(i > 0, up_ref[...], 0.0)
        _bcast_rows(wb, w_ref, 0, 3)
        _bcast_rows(wb, b_ref, 3, 1)
        for c0 in range(0, f, cw):
            gcols, vcols = pl.ds(c0, cw), pl.ds(f + c0, cw)
            for r0 in range(0, tm, rb):
                rows = pl.ds(r0, rb)
                vg, _ = _conv3_blk(buf, wb, hb - 2 + r0, rb, gcols)
                vv, _ = _conv3_blk(buf, wb, hb - 2 + r0, rb, vcols)
                v_ref[rows, gcols] = vg
                v_ref[rows, vcols] = vv
                a_ref[rows, gcols] = (vg * _sigmoid(vg) * vv).astype(BF16)

    return pl.pallas_call(
        body, grid=(t // tm,),
        in_specs=[_row_spec(tm, f2), _prev_spec(tm, hb, f2), _vec_spec(3, f2), _vec_spec(1, f2)],
        out_specs=[_row_spec(tm, f), _row_spec(tm, f2)],
        out_shape=[jax.ShapeDtypeStruct((t, f), BF16), jax.ShapeDtypeStruct((t, f2), F32)],
        scratch_shapes=[pltpu.VMEM((tm + hb, f2), F32), pltpu.VMEM((4, rb, f2), F32)],
        compiler_params=_cparams(_PAR), name=name)(u, u, w, b)


def _ffn_gate_bwd(u, v, da, w, *, name):
    t, f2 = u.shape
    f = f2 // 2
    tm = _pick(t, (TM_CONV,))
    hb = HALO_3
    n = t // tm
    rb = ROWS_BLK
    cw = _pick(f, (COLS_BLK, LANES))
    blocks = [(r0, rb) for r0 in range(0, tm, rb)] + [(tm, hb)]

    def body(u_ref, v_ref, vn_ref, da_ref, dan_ref, w_ref, du_ref, dw_ref, db_ref, dvbuf, wb, wacc, bacc):
        i = pl.program_id(0)

        @pl.when(i == 0)
        def _():
            wacc[...] = jnp.zeros_like(wacc)
            bacc[...] = jnp.zeros_like(bacc)

        _bcast_rows(wb, w_ref, 0, 3)
        for c0 in range(0, f, cw):
            gcols, vcols = pl.ds(c0, cw), pl.ds(f + c0, cw)
            for r0, rows in blocks:
                if r0 < tm:
                    vg, vv = v_ref[pl.ds(r0, rows), gcols], v_ref[pl.ds(r0, rows), vcols]
                    dav = da_ref[pl.ds(r0, rows), gcols]
                else:
                    vg, vv = vn_ref[:, gcols], vn_ref[:, vcols]
                    dav = jnp.where(i < n - 1, dan_ref[:, gcols], 0.0)
                sg = _sigmoid(vg)
                dvg = dav * vv * (sg * (1.0 + vg * (1.0 - sg)))
                dvv = dav * (vg * sg)
                dvbuf[pl.ds(r0, rows), gcols] = dvg
                dvbuf[pl.ds(r0, rows), vcols] = dvv
                if r0 < tm:
                    bacc[:, gcols] += _sum8(dvg)
                    bacc[:, vcols] += _sum8(dvv)
        for c0 in range(0, f2, cw):
            cols = pl.ds(c0, cw)
            for r0 in range(0, tm, rb):
                uv = u_ref[pl.ds(r0, rb), cols]
                window = dvbuf[pl.ds(r0, rb + SUBLANES), cols]
                du = None
                for k in range(3):
                    dvk = _shifted_rows(window, 2 - k, rb)
                    term = dvk * wb[k, :, cols]
                    du = term if du is None else du + term
                    wacc[k, :, cols] += _sum8(uv * dvk)
                du_ref[pl.ds(r0, rb), cols] = du.astype(BF16)

        @pl.when(i == n - 1)
        def _():
            db_ref[...] = jnp.sum(bacc[...], axis=0, keepdims=True)
            dw_ref[...] = jnp.sum(wacc[...], axis=1)

    return pl.pallas_call(
        body, grid=(n,),
        in_specs=[_row_spec(tm, f2), _row_spec(tm, f2), _next_spec(tm, hb, f2, t),
                  _row_spec(tm, f), _next_spec(tm, hb, f, t), _vec_spec(3, f2)],
        out_specs=[_row_spec(tm, f2), _vec_spec(3, f2), _vec_spec(1, f2)],
        out_shape=[jax.ShapeDtypeStruct((t, f2), BF16), jax.ShapeDtypeStruct((3, f2), F32),
                   jax.ShapeDtypeStruct((1, f2), F32)],
        scratch_shapes=[pltpu.VMEM((tm + hb, f2), F32), pltpu.VMEM((3, rb, f2), F32),
                        pltpu.VMEM((3, SUBLANES, f2), F32), pltpu.VMEM((SUBLANES, f2), F32)],
        compiler_params=_cparams(_ARB), name=name)(u, v, v, da, da, w)


def _glu(u, b1, d):
    return (u[:, :d] + b1[:, :d]) * _sigmoid(u[:, d:] + b1[:, d:])


ROWS_TAPS = 32
ROWS_NORM = 16


def _fill_glu_buf(buf, u_ref, up_ref, b1_ref, i, tm, d):
    cw = _pick(d, (COLS_BLK, LANES))
    for c0 in range(0, d, cw):
        b1 = jnp.concatenate([b1_ref[:, pl.ds(c0, cw)], b1_ref[:, pl.ds(d + c0, cw)]], axis=1)
        up = jnp.concatenate([up_ref[:, pl.ds(c0, cw)], up_ref[:, pl.ds(d + c0, cw)]], axis=1)
        buf[pl.ds(0, HALO_A), pl.ds(c0, cw)] = jnp.where(i > 0, _glu(up, b1, cw), 0.0)
        for r0 in range(0, tm, ROWS_TAPS):
            rows = pl.ds(r0, ROWS_TAPS)
            uv = jnp.concatenate([u_ref[rows, pl.ds(c0, cw)], u_ref[rows, pl.ds(d + c0, cw)]], axis=1)
            buf[pl.ds(HALO_A + r0, ROWS_TAPS), pl.ds(c0, cw)] = _glu(uv, b1, cw)


def _taps31(buf, r0, cols, offs, use):
    nv = ROWS_TAPS // SUBLANES
    nrows = ROWS_TAPS + SUBLANES * (-(-max(offs) // SUBLANES))
    window = buf[pl.ds(r0, nrows), cols]
    shifted = {b: _shifted_rows(window, b, nrows - SUBLANES) if b else window
               for b in sorted({o % SUBLANES for o in offs})}
    for k, o in enumerate(offs):
        b, a = o % SUBLANES, o // SUBLANES
        use(k, [shifted[b][SUBLANES * (a + v):SUBLANES * (a + v + 1)] for v in range(nv)])


def _conv_taps_blk(buf, wb, r0, cols, offs):
    nv = ROWS_TAPS // SUBLANES
    acc = [None] * nv

    def use(k, rows):
        wk = wb[k, :, cols]
        for v in range(nv):
            term = rows[v] * wk
            acc[v] = term if acc[v] is None else acc[v] + term

    _taps31(buf, r0, cols, offs, use)
    return jnp.concatenate(acc, axis=0)


def _layernorm_parts(x):
    mu = jnp.mean(x, axis=-1, keepdims=True)
    xc = x - mu
    rstd = lax.rsqrt(jnp.mean(xc * xc, axis=-1, keepdims=True) + LN_EPS)
    return xc * rstd, rstd


_FWD_OFFS = tuple(HALO_A - (CONF_CONV_WIDTH - 1) + k for k in range(CONF_CONV_WIDTH))
_BWD_OFFS = tuple(CONF_CONV_WIDTH - 1 - k for k in range(CONF_CONV_WIDTH))


def _a_fwd(u1, b1, dww, dwb, lng, lnb, *, name):
    t, d2 = u1.shape
    d = d2 // 2
    tm = _pick(t, (TM_CONV,))

    def body(u_ref, up_ref, b1_ref, w_ref, wbias_ref, g_ref, bb_ref, o_ref, u3_ref, buf, wb):
        i = pl.program_id(0)
        _fill_glu_buf(buf, u_ref, up_ref, b1_ref, i, tm, d)
        _bcast_rows(wb, w_ref, 0, CONF_CONV_WIDTH)
        for c0 in range(0, d, LANES):
            cols = pl.ds(c0, LANES)
            for r0 in range(0, tm, ROWS_TAPS):
                u3_ref[pl.ds(r0, ROWS_TAPS), cols] = (_conv_taps_blk(buf, wb, r0, cols, _FWD_OFFS)
                                                      + wbias_ref[:, cols])
        for r0 in range(0, tm, ROWS_NORM):
            rows = pl.ds(r0, ROWS_NORM)
            xhat, _ = _layernorm_parts(u3_ref[rows, :])
            u4 = xhat * g_ref[...] + bb_ref[...]
            o_ref[rows, :] = (u4 * _sigmoid(u4)).astype(BF16)

    return pl.pallas_call(
        body, grid=(t // tm,),
        in_specs=[_row_spec(tm, d2), _prev_spec(tm, HALO_A, d2), _vec_spec(1, d2),
                  _vec_spec(CONF_CONV_WIDTH, d)] + [_vec_spec(1, d)] * 3,
        out_specs=[_row_spec(tm, d)] * 2,
        out_shape=[jax.ShapeDtypeStruct((t, d), BF16), jax.ShapeDtypeStruct((t, d), F32)],
        scratch_shapes=[pltpu.VMEM((tm + HALO_A, d), F32), pltpu.VMEM((CONF_CONV_WIDTH, SUBLANES, d), F32)],
        compiler_params=_cparams(_PAR), name=name)(u1, u1, b1, dww, dwb, lng, lnb)


def _a_bwd_norm(u3, du5, lng, lnb, *, name):
    t, d = u3.shape
    tm = _pick(t, (TM_ROW, 256, 128))
    n = t // tm

    def body(u3_ref, du5_ref, g_ref, bb_ref, du3_ref, dg_ref, db_ref, dwb_ref, ag, ab, aw):
        i = pl.program_id(0)

        @pl.when(i == 0)
        def _():
            for a in (ag, ab, aw):
                a[...] = jnp.zeros_like(a)

        g = g_ref[...]
        for r0 in range(0, tm, ROWS_NORM):
            rows = pl.ds(r0, ROWS_NORM)
            xhat, rstd = _layernorm_parts(u3_ref[rows, :])
            u4 = xhat * g + bb_ref[...]
            sg = _sigmoid(u4)
            du4 = du5_ref[rows, :] * (sg * (1.0 + u4 * (1.0 - sg)))
            dxh = du4 * g
            du3 = rstd * (dxh - jnp.mean(dxh, axis=-1, keepdims=True)
                          - xhat * jnp.mean(dxh * xhat, axis=-1, keepdims=True))
            du3_ref[rows, :] = du3
            ag[...] += _sum8(du4 * xhat)
            ab[...] += _sum8(du4)
            aw[...] += _sum8(du3)

        @pl.when(i == n - 1)
        def _():
            for a, o in ((ag, dg_ref), (ab, db_ref), (aw, dwb_ref)):
                o[...] = jnp.sum(a[...], axis=0, keepdims=True)

    return pl.pallas_call(
        body, grid=(n,),
        in_specs=[_row_spec(tm, d)] * 2 + [_vec_spec(1, d)] * 2,
        out_specs=[_row_spec(tm, d)] + [_vec_spec(1, d)] * 3,
        out_shape=[jax.ShapeDtypeStruct((t, d), F32)] + [jax.ShapeDtypeStruct((1, d), F32)] * 3,
        scratch_shapes=[pltpu.VMEM((SUBLANES, d), F32)] * 3,
        compiler_params=_cparams(_ARB), name=name)(u3, du5, lng, lnb)


def _a_bwd_conv(u1, du3, b1, dww, *, name):
    t, d2 = u1.shape
    d = d2 // 2
    tm = _pick(t, (TM_CONV,))
    n = t // tm
    kw = CONF_CONV_WIDTH
    nv = ROWS_TAPS // SUBLANES

    def body(u_ref, up_ref, g3_ref, g3n_ref, b1_ref, w_ref, du1_ref, dw_ref, db1_ref,
             buf, gbuf, wb, wacc, bacc):
        i = pl.program_id(0)

        @pl.when(i == 0)
        def _():
            wacc[...] = jnp.zeros_like(wacc)
            bacc[...] = jnp.zeros_like(bacc)

        _fill_glu_buf(buf, u_ref, up_ref, b1_ref, i, tm, d)
        _bcast_rows(wb, w_ref, 0, kw)
        gbuf[pl.ds(0, tm), :] = g3_ref[...]
        gbuf[pl.ds(tm, HALO_A), :] = jnp.where(i < n - 1, g3n_ref[...], 0.0)
        for c0 in range(0, d, LANES):
            cols, gcols = pl.ds(c0, LANES), pl.ds(d + c0, LANES)
            for r0 in range(0, tm, ROWS_TAPS):
                rows = pl.ds(r0, ROWS_TAPS)
                u2 = [buf[pl.ds(HALO_A + r0 + SUBLANES * v, SUBLANES), cols] for v in range(nv)]
                acc = [None] * nv

                def use(k, gs):
                    wk = wb[k, :, cols]
                    part = None
                    for v in range(nv):
                        term = gs[v] * wk
                        acc[v] = term if acc[v] is None else acc[v] + term
                        prod = u2[v] * gs[v]
                        part = prod if part is None else part + prod
                    wacc[k, :, cols] += part

                _taps31(gbuf, r0, cols, _BWD_OFFS, use)
                du2 = jnp.concatenate(acc, axis=0)
                av = u_ref[rows, cols] + b1_ref[:, cols]
                sg = _sigmoid(u_ref[rows, gcols] + b1_ref[:, gcols])
                da = du2 * sg
                dg = du2 * av * (sg * (1.0 - sg))
                du1_ref[rows, cols] = da.astype(BF16)
                du1_ref[rows, gcols] = dg.astype(BF16)
                bacc[:, cols] += _sum8(da)
                bacc[:, gcols] += _sum8(dg)

        @pl.when(i == n - 1)
        def _():
            dw_ref[...] = jnp.sum(wacc[...], axis=1)
            db1_ref[...] = jnp.sum(bacc[...], axis=0, keepdims=True)

    return pl.pallas_call(
        body, grid=(n,),
        in_specs=[_row_spec(tm, d2), _prev_spec(tm, HALO_A, d2), _row_spec(tm, d),
                  _next_spec(tm, HALO_A, d, t), _vec_spec(1, d2), _vec_spec(kw, d)],
        out_specs=[_row_spec(tm, d2), _vec_spec(kw, d), _vec_spec(1, d2)],
        out_shape=[jax.ShapeDtypeStruct((t, d2), BF16), jax.ShapeDtypeStruct((kw, d), F32),
                   jax.ShapeDtypeStruct((1, d2), F32)],
        scratch_shapes=[pltpu.VMEM((tm + HALO_A, d), F32), pltpu.VMEM((tm + HALO_A, d), F32),
                        pltpu.VMEM((kw, SUBLANES, d), F32),
                        pltpu.VMEM((kw, SUBLANES, d), F32), pltpu.VMEM((SUBLANES, d2), F32)],
        compiler_params=_cparams(_ARB), name=name)(u1, u1, du3, du3, b1, dww)


def _pool_counts(i, tm, w):
    pos = (i * tm + lax.broadcasted_iota(jnp.int32, (tm, 1), 0) + 1).astype(F32)
    return jnp.minimum(pos, float(w))


def _b_pool_fwd(h, *, name):
    t, d = h.shape
    gd = d // len(POOL_WINDOWS)
    tm = _pick(t, (TM_CONV,))
    hb = HALO_POOL

    def body(h_ref, hp_ref, o_ref, buf):
        i = pl.program_id(0)
        buf[pl.ds(0, hb), :] = jnp.where(i > 0, hp_ref[...], 0.0)
        buf[pl.ds(hb, tm), :] = h_ref[...]
        for g, w in enumerate(POOL_WINDOWS):
            cols = pl.ds(g * gd, gd)
            cur = buf[pl.ds(hb, tm), cols]
            s = cur
            for j in range(1, w):
                s = s + buf[pl.ds(hb - j, tm), cols]
            o_ref[:, cols] = (s / _pool_counts(i, tm, w) - cur).astype(BF16)

    return pl.pallas_call(
        body, grid=(t // tm,),
        in_specs=[_row_spec(tm, d), _prev_spec(tm, hb, d)],
        out_specs=_row_spec(tm, d), out_shape=jax.ShapeDtypeStruct((t, d), BF16),
        scratch_shapes=[pltpu.VMEM((tm + hb, d), F32)],
        compiler_params=_cparams(_PAR), name=name)(h, h)


def _b_pool_bwd(dp, *, name):
    t, d = dp.shape
    gd = d // len(POOL_WINDOWS)
    tm = _pick(t, (TM_CONV,))
    hb = HALO_POOL
    n = t // tm

    def body(dp_ref, dpn_ref, o_ref, buf):
        i = pl.program_id(0)
        for g, w in enumerate(POOL_WINDOWS):
            cols = pl.ds(g * gd, gd)
            buf[pl.ds(0, tm), cols] = dp_ref[:, cols] / _pool_counts(i, tm, w)
            buf[pl.ds(tm, hb), cols] = jnp.where(i < n - 1, dpn_ref[:, cols] * (1.0 / w), 0.0)
            s = buf[pl.ds(0, tm), cols]
            for j in range(1, w):
                s = s + buf[pl.ds(j, tm), cols]
            o_ref[:, cols] = s - dp_ref[:, cols]

    return pl.pallas_call(
        body, grid=(n,),
        in_specs=[_row_spec(tm, d), _next_spec(tm, hb, d, t)],
        out_specs=_row_spec(tm, d), out_shape=jax.ShapeDtypeStruct((t, d), F32),
        scratch_shapes=[pltpu.VMEM((tm + hb, d), F32)],
        compiler_params=_cparams(_PAR), name=name)(dp, dp)


def _b_affine_fwd(mixed, gb, scale, *, name):
    t, d = mixed.shape
    tm = _pick(t, (TM_ROW, 256, 128))

    def body(m_ref, gb_ref, s_ref, o_ref):
        o_ref[...] = (m_ref[...] + gb_ref[...]) * s_ref[...]

    return pl.pallas_call(
        body, grid=(t // tm,), in_specs=[_row_spec(tm, d)] + [_vec_spec(1, d)] * 2,
        out_specs=_row_spec(tm, d), out_shape=jax.ShapeDtypeStruct((t, d), F32),
        compiler_params=_cparams(_PAR), name=name)(mixed, gb, scale)


def _b_affine_bwd(dy, mixed, gb, scale, *, name):
    t, d = mixed.shape
    tm = _pick(t, (TM_ROW, 256, 128))
    n = t // tm

    def body(dy_ref, m_ref, gb_ref, s_ref, dm_ref, ds_ref, dgb_ref, a1, a2):
        i = pl.program_id(0)

        @pl.when(i == 0)
        def _():
            a1[...] = jnp.zeros_like(a1)
            a2[...] = jnp.zeros_like(a2)

        dy_v = dy_ref[...]
        dm_ref[...] = (dy_v * s_ref[...]).astype(BF16)
        a1[...] += _sum8(dy_v * (m_ref[...] + gb_ref[...]))
        a2[...] += _sum8(dy_v)

        @pl.when(i == n - 1)
        def _():
            ds_ref[...] = jnp.sum(a1[...], axis=0, keepdims=True)
            dgb_ref[...] = jnp.sum(a2[...], axis=0, keepdims=True) * s_ref[...]

    return pl.pallas_call(
        body, grid=(n,), in_specs=[_row_spec(tm, d)] * 2 + [_vec_spec(1, d)] * 2,
        out_specs=[_row_spec(tm, d)] + [_vec_spec(1, d)] * 2,
        out_shape=[jax.ShapeDtypeStruct((t, d), BF16)] + [jax.ShapeDtypeStruct((1, d), F32)] * 2,
        scratch_shapes=[pltpu.VMEM((SUBLANES, d), F32)] * 2,
        compiler_params=_cparams(_ARB), name=name)(dy, mixed, gb, scale)


def _c_gate_fwd(bcx, wc, *, name):
    t, d3 = bcx.shape
    d = d3 // 3
    tm = _pick(t, (TM_CONV,))
    hb = HALO_3

    def body(x_ref, xp_ref, w_ref, z_ref, buf):
        i = pl.program_id(0)
        xv, xp = x_ref[...], xp_ref[...]
        buf[pl.ds(0, hb), :] = jnp.where(i > 0, xp[:, d:2 * d] * xp[:, 2 * d:], 0.0)
        buf[pl.ds(hb, tm), :] = xv[:, d:2 * d] * xv[:, 2 * d:]
        z_ref[...] = (xv[:, :d] * _conv3_rows(buf, w_ref, tm, hb - 2)).astype(BF16)

    return pl.pallas_call(
        body, grid=(t // tm,),
        in_specs=[_row_spec(tm, d3), _prev_spec(tm, hb, d3), _vec_spec(3, d)],
        out_specs=_row_spec(tm, d), out_shape=jax.ShapeDtypeStruct((t, d), BF16),
        scratch_shapes=[pltpu.VMEM((tm + hb, d), F32)],
        compiler_params=_cparams(_PAR), name=name)(bcx, bcx, wc)


def _c_gate_bwd(bcx, dz, wc, *, name):
    t, d3 = bcx.shape
    d = d3 // 3
    tm = _pick(t, (TM_CONV,))
    hb = HALO_3
    n = t // tm

    def body(x_ref, xp_ref, xn_ref, dz_ref, dzn_ref, w_ref, o_ref, dw_ref, pbuf, qbuf, wacc):
        i = pl.program_id(0)

        @pl.when(i == 0)
        def _():
            wacc[...] = jnp.zeros_like(wacc)

        xv, xp, xnx = x_ref[...], xp_ref[...], xn_ref[...]
        gbv, gcv, vv = xv[:, :d], xv[:, d:2 * d], xv[:, 2 * d:]
        pbuf[pl.ds(0, hb), :] = jnp.where(i > 0, xp[:, d:2 * d] * xp[:, 2 * d:], 0.0)
        pbuf[pl.ds(hb, tm), :] = gcv * vv
        q = _conv3_rows(pbuf, w_ref, tm, hb - 2)
        dz_v = dz_ref[...]
        dq = dz_v * gbv
        qbuf[pl.ds(0, tm), :] = dq
        qbuf[pl.ds(tm, hb), :] = jnp.where(i < n - 1, dzn_ref[...] * xnx[:, :d], 0.0)
        dp = qbuf[pl.ds(0, tm), :] * w_ref[pl.ds(2, 1), :]
        dp = dp + qbuf[pl.ds(1, tm), :] * w_ref[pl.ds(1, 1), :]
        dp = dp + qbuf[pl.ds(2, tm), :] * w_ref[pl.ds(0, 1), :]
        o_ref[:, pl.ds(0, d)] = (dz_v * q).astype(BF16)
        o_ref[:, pl.ds(d, d)] = (dp * vv).astype(BF16)
        o_ref[:, pl.ds(2 * d, d)] = (dp * gcv).astype(BF16)
        for k in range(3):
            wacc[k] += _sum8(dq * pbuf[pl.ds(hb - 2 + k, tm), :])

        @pl.when(i == n - 1)
        def _():
            dw_ref[...] = jnp.sum(wacc[...], axis=1)

    return pl.pallas_call(
        body, grid=(n,),
        in_specs=[_row_spec(tm, d3), _prev_spec(tm, hb, d3), _next_spec(tm, hb, d3, t),
                  _row_spec(tm, d), _next_spec(tm, hb, d, t), _vec_spec(3, d)],
        out_specs=[_row_spec(tm, d3), _vec_spec(3, d)],
        out_shape=[jax.ShapeDtypeStruct((t, d3), BF16), jax.ShapeDtypeStruct((3, d), F32)],
        scratch_shapes=[pltpu.VMEM((tm + hb, d), F32), pltpu.VMEM((tm + hb, d), F32),
                        pltpu.VMEM((3, SUBLANES, d), F32)],
        compiler_params=_cparams(_ARB), name=name)(bcx, bcx, bcx, dz, dz, wc)


def _row(v):
    return v.reshape(1, -1)


def _kind_of(j):
    return "f" if j % 2 else "abc"[(j // 2) % N_MIXERS]


BIG = ("a_pw1_w", "a_pw2_w", "b_group_w", "c_in_w", "c_out_w", "f_up_w", "f_down_w")
COL_SHARDED = ("a_pw1_w", "c_in_w", "f_up_w")


def _local_step(x, target, mod, p):
    nsub = 2 * DEPTH
    norm_names = (("norm_pre_mix", "norm_post_mix"), ("norm_pre_ffn", "norm_post_ffn"))
    gpre = [_row(p[norm_names[s][0]][i]) for i in range(DEPTH) for s in (0, 1)]
    gpost = [_row(p[norm_names[s][1]][i]) for i in range(DEPTH) for s in (0, 1)]
    sh = [_row(mod[i, 3 * s + 0]) for i in range(DEPTH) for s in (0, 1)]
    sc = [_row(mod[i, 3 * s + 1]) for i in range(DEPTH) for s in (0, 1)]
    gt = [_row(mod[i, 3 * s + 2]) for i in range(DEPTH) for s in (0, 1)]

    def h_dtype(j):
        return F32 if _kind_of(j) == "b" else BF16

    xs, hs, ys, saved = [x], [], [], []

    hs.append(_fwd_first(x, gpre[0], sc[0], sh[0], h_dtype=h_dtype(0), name="fwd_first"))
    for j in range(nsub):
        i, kind = j // 2, _kind_of(j)
        slot = i // N_MIXERS
        h = hs[j]
        tag = f"{kind}{j}"
        if kind == "f":
            u = _mm(h, p["f_up_w"], mode="nn", layer=i, name=f"ffn_up_{tag}")
            a, vpre = _ffn_gate_fwd(u, p["f_dw_w"][i], _row(p["f_dw_b"][i]), name=f"ffn_gate_{tag}")
            y = _mm(a, p["f_down_w"][i], mode="nn", name=f"ffn_down_{tag}")
            saved.append((u, a, vpre))
        elif kind == "a":
            u1 = _mm(h, p["a_pw1_w"], mode="nn", layer=slot, name=f"a_pw1_{tag}")
            u5, u3 = _a_fwd(u1, _row(p["a_pw1_b"][slot]), p["a_dw_w"][slot], _row(p["a_dw_b"][slot]),
                            _row(p["a_ln_g"][slot]), _row(p["a_ln_b"][slot]), name=f"a_conv_{tag}")
            y = _mm(u5, p["a_pw2_w"][slot], mode="nn", bias=_row(p["a_pw2_b"][slot]), name=f"a_pw2_{tag}")
            saved.append((u1, u5, u3))
        elif kind == "b":
            pooled = _b_pool_fwd(h, name=f"b_pool_{tag}")
            mixed = _mm_group(pooled, p["b_group_w"][slot], mode="nn", name=f"b_mix_{tag}")
            y = _b_affine_fwd(mixed, _row(p["b_group_b"][slot]), _row(p["b_scale"][slot]), name=f"b_aff_{tag}")
            saved.append((pooled, mixed))
        else:
            bcx = _mm(h, p["c_in_w"], mode="nn", layer=slot, name=f"c_in_{tag}")
            z = _c_gate_fwd(bcx, p["c_conv_w"][slot], name=f"c_gate_{tag}")
            y = _mm(z, p["c_out_w"][slot], mode="nn", name=f"c_out_{tag}")
            saved.append((bcx, z))
        ys.append(y)
        if j + 1 < nsub:
            x_new, h_next = _fwd_mid(xs[j], y, gpost[j], gt[j], gpre[j + 1], sc[j + 1], sh[j + 1],
                                     h_dtype=h_dtype(j + 1), name=f"fwd_mid_{j}")
            xs.append(x_new)
            hs.append(h_next)

    n_of = {"a": len([i for i in range(DEPTH) if i % N_MIXERS == 0]),
            "b": len([i for i in range(DEPTH) if i % N_MIXERS == 1]),
            "c": len([i for i in range(DEPTH) if i % N_MIXERS == 2]), "f": DEPTH, "n": DEPTH}
    g = {k: [None] * n_of[k[0]] for k in p}
    dmod = [[None] * 6 for _ in range(DEPTH)]

    last = nsub - 1
    dx, dy, dgpost, dgt, sdy, loss = _last_fwd_bwd(xs[last], ys[last], target, gpost[last], gt[last],
                                                   name="loss_head")
    for j in range(last, -1, -1):
        i, kind = j // 2, _kind_of(j)
        slot = i // N_MIXERS
        sub = j % 2
        tag = f"{kind}{j}"
        g[norm_names[sub][1]][i] = dgpost
        dmod[i][3 * sub + 2] = dgt
        h = hs[j]
        if kind == "f":
            u, a, vpre = saved[j]
            da = _mm(dy, p["f_down_w"][i], mode="nt", name=f"ffn_dda_{tag}")
            g["f_down_w"][i] = _mm(a, dy, mode="tn", tk=TK_TOKENS, name=f"ffn_dwdown_{tag}")
            du, dw, db = _ffn_gate_bwd(u, vpre, da, p["f_dw_w"][i], name=f"ffn_gate_bwd_{tag}")
            g["f_dw_w"][i], g["f_dw_b"][i] = dw, db
            dh = _mm(du, p["f_up_w"], mode="nt", layer=i, name=f"ffn_ddh_{tag}")
            g["f_up_w"][i] = _mm(h, du, mode="tn", tk=TK_TOKENS, layer=i, name=f"ffn_dwup_{tag}")
        elif kind == "a":
            u1, u5, u3 = saved[j]
            g["a_pw2_b"][slot] = sdy
            du5 = _mm(dy, p["a_pw2_w"][slot], mode="nt", name=f"a_ddu5_{tag}")
            g["a_pw2_w"][slot] = _mm(u5, dy, mode="tn", tk=TK_TOKENS, name=f"a_dw2_{tag}")
            b1 = _row(p["a_pw1_b"][slot])
            du3, dlg, dlb, ddwb = _a_bwd_norm(u3, du5, _row(p["a_ln_g"][slot]), _row(p["a_ln_b"][slot]),
                                              name=f"a_bwd_norm_{tag}")
            g["a_ln_g"][slot], g["a_ln_b"][slot], g["a_dw_b"][slot] = dlg, dlb, ddwb
            du1, ddww, db1 = _a_bwd_conv(u1, du3, b1, p["a_dw_w"][slot], name=f"a_bwd_conv_{tag}")
            g["a_dw_w"][slot], g["a_pw1_b"][slot] = ddww, db1
            dh = _mm(du1, p["a_pw1_w"], mode="nt", layer=slot, name=f"a_ddh_{tag}")
            g["a_pw1_w"][slot] = _mm(h, du1, mode="tn", tk=TK_TOKENS, layer=slot, name=f"a_dw1_{tag}")
        elif kind == "b":
            pooled, mixed = saved[j]
            dmixed, dscale, dgb = _b_affine_bwd(dy, mixed, _row(p["b_group_b"][slot]), _row(p["b_scale"][slot]),
                                                name=f"b_aff_bwd_{tag}")
            g["b_scale"][slot], g["b_group_b"][slot] = dscale, dgb
            dpooled = _mm_group(dmixed, p["b_group_w"][slot], mode="nt", name=f"b_dpool_{tag}")
            g["b_group_w"][slot] = _mm_group(pooled, dmixed, mode="tn", tm=TK_TOKENS, name=f"b_dw_{tag}")
            dh = _b_pool_bwd(dpooled, name=f"b_pool_bwd_{tag}")
        else:
            bcx, z = saved[j]
            dz = _mm(dy, p["c_out_w"][slot], mode="nt", name=f"c_ddz_{tag}")
            g["c_out_w"][slot] = _mm(z, dy, mode="tn", tk=TK_TOKENS, name=f"c_dwout_{tag}")
            dbcx, dwc = _c_gate_bwd(bcx, dz, p["c_conv_w"][slot], name=f"c_gate_bwd_{tag}")
            g["c_conv_w"][slot] = dwc
            dh = _mm(dbcx, p["c_in_w"], mode="nt", layer=slot, name=f"c_ddh_{tag}")
            g["c_in_w"][slot] = _mm(h, dbcx, mode="tn", tk=TK_TOKENS, layer=slot, name=f"c_dwin_{tag}")
        if j > 0:
            pj = j - 1
            dy_dtype = F32 if _kind_of(pj) == "b" else BF16
            dx, dy, dsh, dsc, dgpre, dgpost, dgt, sdy = _bwd_mid(
                dx, dh, xs[j], gpre[j], sc[j], ys[pj], gpost[pj], gt[pj], dy_dtype=dy_dtype, name=f"bwd_mid_{j}")
        else:
            dx, dsh, dsc, dgpre = _bwd_first(dx, dh, xs[0], gpre[0], sc[0], name="bwd_first")
        dmod[i][3 * sub + 0] = dsh
        dmod[i][3 * sub + 1] = dsc
        g[norm_names[sub][0]][i] = dgpre

    small = {k: jnp.stack(v).reshape(p[k].shape) for k, v in g.items() if k not in BIG}
    big = {k: v for k, v in g.items() if k in BIG}
    dmod_arr = jnp.stack([jnp.concatenate(r, axis=0) for r in dmod])
    return loss, dx, dmod_arr, small, big


_MESH = pl.DeviceIdType.MESH
_ANY = pl.BlockSpec(memory_space=pl.ANY)
_VMEM = pl.BlockSpec(memory_space=pltpu.VMEM)


def _place():
    return lax.axis_index("x"), lax.axis_index("y"), lax.axis_index("c")


def _other_chips(x, y):
    return [(1 - x, y), (x, 1 - y), (1 - x, 1 - y)]


def _remote(src, dst, send_sem, recv_sem, to):
    return pltpu.make_async_remote_copy(src_ref=src, dst_ref=dst, send_sem=send_sem, recv_sem=recv_sem,
                                        device_id=to, device_id_type=_MESH)


def _all_gather8(blk, *, name):
    r, cdim = blk.shape

    def body(x_ref, out_ref, send_sems, recv_sems, local_sem):
        x, y, c = _place()
        me, sibling = (x, y, c), (x, y, 1 - c)
        chips = _other_chips(x, y)

        def slot(px, py, pc):
            return out_ref.at[4 * px + 2 * py + pc]

        def copy(k, block, to, src=None):
            return _remote(slot(*block) if src is None else src, slot(*block),
                           send_sems.at[k], recv_sems.at[k], to)

        mine = pltpu.make_async_copy(x_ref, slot(*me), local_sem)
        mine.start()
        first = [copy(0, me, sibling, src=x_ref)]
        first += [copy(1 + j, me, (*chip, c), src=x_ref) for j, chip in enumerate(chips)]
        for cp in first:
            cp.start()
        passed = [copy(4 + j, (*chip, c), sibling) for j, chip in enumerate(chips)]
        for j, chip in enumerate(chips):
            copy(1 + j, (*chip, c), me).wait_recv()
            passed[j].start()
        copy(0, sibling, me).wait_recv()
        for j, chip in enumerate(chips):
            copy(4 + j, (*chip, 1 - c), me).wait_recv()
        for cp in first + passed:
            cp.wait_send()
        mine.wait()

    return pl.pallas_call(
        body, out_shape=jax.ShapeDtypeStruct((NDEV, r, cdim), blk.dtype),
        in_specs=[_VMEM], out_specs=_VMEM,
        scratch_shapes=[pltpu.SemaphoreType.DMA((7,)), pltpu.SemaphoreType.DMA((7,)), pltpu.SemaphoreType.DMA],
        compiler_params=pltpu.CompilerParams(vmem_limit_bytes=VMEM_LIMIT), name=name)(blk)


def _gather_dst(kind):
    if kind == "col":
        return lambda s, h: (s, h)
    if kind == "row":
        return lambda s, h: (h, slice(None), s)
    return lambda s, h: (slice(None), s, h)


def _cast_into_gathered(src, kind, out_shape, shard, *, name):
    _, a, rh, cdim = src.shape
    tr = _pick(rh, (512, 256, 128, 64, 32, 16))
    if kind == "col":
        out_idx = lambda h, ai, r, s: (s[0], h, ai, r, 0)
    elif kind == "row":
        out_idx = lambda h, ai, r, s: (h, ai, s[0], r, 0)
    else:
        out_idx = lambda h, ai, r, s: (ai, s[0], h, r, 0)

    def body(s_ref, x_ref, o_ref):
        o_ref[...] = x_ref[...].astype(BF16)

    grid_spec = pltpu.PrefetchScalarGridSpec(
        num_scalar_prefetch=1, grid=(2, a, rh // tr),
        in_specs=[pl.BlockSpec((None, None, tr, cdim), lambda h, ai, r, s: (h, ai, r, 0))],
        out_specs=pl.BlockSpec((None, None, None, tr, cdim), out_idx))
    return pl.pallas_call(
        body, grid_spec=grid_spec, out_shape=jax.ShapeDtypeStruct(out_shape, BF16),
        compiler_params=_cparams(_PAR, _PAR, _PAR), name=name)(shard, src)


def _gather_weights(bufs, kinds, *, name):
    nt = len(bufs)

    def body(*refs):
        out_refs = refs[nt:2 * nt]
        send_sems, recv_sems = refs[2 * nt:]
        x, y, c = _place()
        sibling = (x, y, 1 - c)
        chips = _other_chips(x, y)
        s_me = 2 * x + y

        def at(k, s, h):
            return out_refs[k].at[_gather_dst(kinds[k])(s, h)]

        sends, passed = [], []
        for k in range(nt):
            for j, chip in enumerate(chips):
                cp = _remote(at(k, s_me, c), at(k, s_me, c), send_sems.at[k, j], recv_sems.at[k, j], (*chip, c))
                cp.start()
                sends.append(cp)
        for k in range(nt):
            for j, (px, py) in enumerate(chips):
                got = at(k, 2 * px + py, c)
                _remote(got, got, send_sems.at[k, j], recv_sems.at[k, j], (px, py, c)).wait_recv()
                cp = _remote(got, got, send_sems.at[k, 3 + j], recv_sems.at[k, 3 + j], sibling)
                cp.start()
                passed.append(cp)
        for k in range(nt):
            for j, (px, py) in enumerate(chips):
                got = at(k, 2 * px + py, 1 - c)
                _remote(got, got, send_sems.at[k, 3 + j], recv_sems.at[k, 3 + j], sibling).wait_recv()
        for cp in sends + passed:
            cp.wait_send()

    return pl.pallas_call(
        body, out_shape=[jax.ShapeDtypeStruct(b.shape, BF16) for b in bufs],
        in_specs=[_ANY] * nt, out_specs=[_ANY] * nt, input_output_aliases={k: k for k in range(nt)},
        scratch_shapes=[pltpu.SemaphoreType.DMA((nt, 6)), pltpu.SemaphoreType.DMA((nt, 6))],
        name=name)(*bufs)


def _pair_exchange(gs, layers_of, *, name):
    n, nt = len(gs), len(layers_of)

    def body(*refs):
        g_refs, out_refs = refs[:n], refs[n:n + nt]
        send_sems, recv_sems = refs[n + nt:]
        x, y, c = _place()
        sibling = (x, y, 1 - c)
        copies = []
        for t, ks in enumerate(layers_of):
            for l, k in enumerate(ks):
                cp = _remote(g_refs[k].at[:, 1 - c], out_refs[t].at[l], send_sems.at[k], recv_sems.at[k], sibling)
                cp.start()
                copies.append(cp)
        for cp in copies:
            cp.wait()

    out_shape = [jax.ShapeDtypeStruct((len(ks), NSHARD) + gs[ks[0]].shape[2:], F32) for ks in layers_of]
    return pl.pallas_call(
        body, out_shape=out_shape, in_specs=[_ANY] * n, out_specs=[_ANY] * nt,
        scratch_shapes=[pltpu.SemaphoreType.DMA((n,)), pltpu.SemaphoreType.DMA((n,))],
        name=name)(*gs)


def _pair_sum(g, r1, s_acc, layer, half, *, name):
    _, _, rh, cdim = g.shape
    tr = _pick(rh, (256, 128, 176, 64, 32, 16))

    def body(half_ref, g_ref, r_ref, s_in_ref, o_ref):
        o_ref[...] = (g_ref[...] + r_ref[...]).astype(BF16)

    grid_spec = pltpu.PrefetchScalarGridSpec(
        num_scalar_prefetch=1, grid=(NSHARD, rh // tr),
        in_specs=[pl.BlockSpec((None, None, tr, cdim), lambda s, r, hf: (s, hf[0], r, 0)),
                  pl.BlockSpec((None, None, tr, cdim), lambda s, r, hf: (layer, s, r, 0)),
                  _ANY],
        out_specs=pl.BlockSpec((None, None, tr, cdim), lambda s, r, hf: (layer, s, r, 0)))
    return pl.pallas_call(
        body, grid_spec=grid_spec, out_shape=jax.ShapeDtypeStruct(s_acc.shape, BF16),
        input_output_aliases={3: 0},
        compiler_params=_cparams(_PAR, _PAR), name=name)(half, g, r1, s_acc)


def _chip_exchange(ss, *, name):
    nt = len(ss)

    def body(*refs):
        s_refs, out_refs = refs[:nt], refs[nt:2 * nt]
        send_sems, recv_sems = refs[2 * nt:]
        x, y, c = _place()
        copies = []
        for t in range(nt):
            for j, (px, py) in enumerate(_other_chips(x, y)):
                cp = _remote(s_refs[t].at[:, 2 * px + py], out_refs[t].at[j],
                             send_sems.at[t, j], recv_sems.at[t, j], (px, py, c))
                cp.start()
                copies.append(cp)
        for cp in copies:
            cp.wait()

    out_shape = [jax.ShapeDtypeStruct((3, s.shape[0]) + s.shape[2:], BF16) for s in ss]
    return pl.pallas_call(
        body, out_shape=out_shape, in_specs=[_ANY] * nt, out_specs=[_ANY] * nt,
        scratch_shapes=[pltpu.SemaphoreType.DMA((nt, 3)), pltpu.SemaphoreType.DMA((nt, 3))],
        name=name)(*ss)


def _chip_sum(s_t, r3_t, place, *, name):
    nl, _, rh, cdim = s_t.shape
    tr = _pick(rh, (256, 128, 176, 64, 32, 16))

    def body(pz, s_ref, r_ref, o_ref):
        acc = s_ref[...].astype(F32) + r_ref[0].astype(F32)
        o_ref[...] = (acc + r_ref[1].astype(F32)) + r_ref[2].astype(F32)

    grid_spec = pltpu.PrefetchScalarGridSpec(
        num_scalar_prefetch=1, grid=(nl, rh // tr),
        in_specs=[pl.BlockSpec((None, None, tr, cdim), lambda l, r, pz: (l, pz[0], r, 0)),
                  pl.BlockSpec((3, None, tr, cdim), lambda l, r, pz: (0, l, r, 0))],
        out_specs=pl.BlockSpec((None, None, tr, cdim), lambda l, r, pz: (l, pz[1], r, 0)))
    return pl.pallas_call(
        body, grid_spec=grid_spec, out_shape=jax.ShapeDtypeStruct((nl, 2, rh, cdim), F32),
        compiler_params=_cparams(_PAR, _PAR), name=name)(place, s_t, r3_t)


def _join_halves(reds, *, name):
    nt = len(reds)

    def body(*refs):
        out_refs = refs[nt:2 * nt]
        send_sems, recv_sems = refs[2 * nt:]
        x, y, c = _place()
        sibling = (x, y, 1 - c)
        copies = []
        for t in range(nt):
            cp = _remote(out_refs[t].at[:, c], out_refs[t].at[:, c], send_sems.at[t], recv_sems.at[t], sibling)
            cp.start()
            copies.append(cp)
        for t, cp in enumerate(copies):
            cp.wait_send()
            got = out_refs[t].at[:, 1 - c]
            _remote(got, got, send_sems.at[t], recv_sems.at[t], sibling).wait_recv()

    return pl.pallas_call(
        body, out_shape=[jax.ShapeDtypeStruct(r.shape, F32) for r in reds],
        in_specs=[_ANY] * nt, out_specs=[_ANY] * nt, input_output_aliases={t: t for t in range(nt)},
        scratch_shapes=[pltpu.SemaphoreType.DMA((nt,)), pltpu.SemaphoreType.DMA((nt,))],
        name=name)(*reds)


def _sum_devices(g, *, name):
    _, r, cdim = g.shape
    tr = _pick(r, (512, 256, 128, 64, 32, 16, 8))

    def body(g_ref, o_ref):
        acc = g_ref[0]
        for e in range(1, NDEV):
            acc = acc + g_ref[e]
        o_ref[...] = acc

    return pl.pallas_call(
        body, grid=(r // tr,), in_specs=[pl.BlockSpec((NDEV, tr, cdim), lambda i: (0, i, 0))],
        out_specs=pl.BlockSpec((tr, cdim), lambda i: (i, 0)),
        out_shape=jax.ShapeDtypeStruct((r, cdim), F32),
        compiler_params=_cparams(_PAR), name=name)(g)


def _mod_fwd(c_all, mod_w, mod_b_cols, *, name):
    nl, d, n = mod_w.shape
    ne = c_all.shape[0]
    tn = _pick(n, (768, 512, 384, 256, 128))

    def body(c_ref, w_ref, b_ref, o_ref):
        cv = c_ref[...]
        act = (cv * _sigmoid(cv)).astype(BF16)
        o_ref[...] = jnp.dot(act, w_ref[...].astype(BF16), preferred_element_type=F32) + b_ref[...]

    return pl.pallas_call(
        body, grid=(nl, n // tn),
        in_specs=[pl.BlockSpec((ne, d), lambda i, j: (0, 0)),
                  pl.BlockSpec((None, d, tn), lambda i, j: (i, 0, j)),
                  pl.BlockSpec((None, 1, tn), lambda i, j: (i, 0, j))],
        out_specs=pl.BlockSpec((None, ne, tn), lambda i, j: (i, 0, j)),
        out_shape=jax.ShapeDtypeStruct((nl, ne, n), F32),
        compiler_params=_cparams(_PAR, _PAR), name=name)(c_all, mod_w, mod_b_cols)


def _adam_math(w, g, m, v):
    m2 = ADAM_B1 * m + (1.0 - ADAM_B1) * g
    v2 = ADAM_B2 * v + (1.0 - ADAM_B2) * (g * g)
    m_hat = m2 / (1.0 - ADAM_B1 ** ADAM_STEP)
    v_hat = v2 / (1.0 - ADAM_B2 ** ADAM_STEP)
    delta = -ADAM_LR * (m_hat / (jnp.sqrt(v_hat) + ADAM_EPS) + ADAM_WD * w)
    return delta, m2, v2


def _adamw(w, g, m, v, *, name):
    rows, cdim = w.shape
    tr = _pick(rows, tuple(t for t in (512, 256, 128, 64, 32, 16, 8) if t * cdim <= 256 * 1024))

    def body(w_ref, g_ref, m_ref, v_ref, d_ref, mo_ref, vo_ref):
        d_ref[...], mo_ref[...], vo_ref[...] = _adam_math(w_ref[...], g_ref[...], m_ref[...], v_ref[...])

    spec = pl.BlockSpec((tr, cdim), lambda i: (i, 0))
    return pl.pallas_call(
        body, grid=(rows // tr,), in_specs=[spec] * 4, out_specs=[spec] * 3,
        out_shape=[jax.ShapeDtypeStruct((rows, cdim), F32)] * 3,
        compiler_params=_cparams(_PAR), name=name)(w, g, m, v)


def _mod_w_update(c_t, dmod, w, m, v, *, name):
    nl, d, n = w.shape
    ne = c_t.shape[1]
    tr = _pick(d, (128, 64, 32, 16, 8))

    def body(c_ref, dm_ref, w_ref, m_ref, v_ref, g_ref, d_ref, mo_ref, vo_ref):
        cv = c_ref[...]
        act = cv * _sigmoid(cv)
        dm = dm_ref[...]
        g = act[:, 0:1] * dm[0:1, :]
        for e in range(1, ne):
            g = g + act[:, e:e + 1] * dm[e:e + 1, :]
        g_ref[...] = g
        d_ref[...], mo_ref[...], vo_ref[...] = _adam_math(w_ref[...], g, m_ref[...], v_ref[...])

    big = pl.BlockSpec((None, tr, n), lambda i, r: (i, r, 0))
    return pl.pallas_call(
        body, grid=(nl, d // tr),
        in_specs=[pl.BlockSpec((tr, ne), lambda i, r: (r, 0)),
                  pl.BlockSpec((None, ne, n), lambda i, r: (i, 0, 0)), big, big, big],
        out_specs=[big] * 4, out_shape=[jax.ShapeDtypeStruct((nl, d, n), F32)] * 4,
        compiler_params=_cparams(_PAR, _PAR), name=name)(c_t, dmod, w, m, v)


PACK_ROWS = 256


def _pack(arrs):
    flat = jnp.concatenate([a.reshape(-1) for a in arrs])
    tile = PACK_ROWS * LANES
    pad = (-flat.shape[0]) % tile
    return jnp.pad(flat, (0, pad)).reshape(-1, LANES)


def _unpack(packed, shapes, lead=()):
    flat = packed.reshape(lead + (-1,))
    out, off = [], 0
    for shp in shapes:
        size = 1
        for s in shp:
            size *= s
        out.append(flat[..., off:off + size].reshape(lead + tuple(shp)))
        off += size
    return out


SMALL_SHARD_AXIS = {"a_pw1_b": 1, "a_dw_w": 2, "a_dw_b": 1, "a_ln_g": 1, "a_ln_b": 1, "a_pw2_b": 1,
                    "c_conv_w": 2, "f_dw_w": 2}
SMALL_REPLICATED = ("norm_pre_mix", "norm_post_mix", "norm_pre_ffn", "norm_post_ffn",
                    "b_group_b", "b_scale", "f_dw_b")
WEIGHT_ORDER = ("mod_w", "mod_b", "norm_pre_mix", "norm_post_mix", "norm_pre_ffn", "norm_post_ffn",
                "a_pw1_w", "a_pw1_b", "a_dw_w", "a_dw_b", "a_ln_g", "a_ln_b", "a_pw2_w", "a_pw2_b",
                "b_group_w", "b_group_b", "b_scale", "c_in_w", "c_conv_w", "c_out_w",
                "f_up_w", "f_dw_w", "f_dw_b", "f_down_w")


def _as_layers_rows_cols(name, w):
    if name == "b_group_w":
        return w.reshape(w.shape[1], w.shape[2], w.shape[3])
    return w


def _step(x, c, loss_target, w, m, v):
    xi, yi, ci = _place()
    shard = 2 * xi + yi
    example = 4 * xi + 2 * yi + ci
    d = x.shape[-1]

    small_names = tuple(SMALL_SHARD_AXIS)
    gathered0 = _all_gather8(_pack([c] + [w[k] for k in small_names]), name="gather_small")
    parts = _unpack(gathered0, [c.shape] + [w[k].shape for k in small_names], lead=(NDEV,))
    c_all = parts[0].reshape(NDEV, d)
    p = {}
    for k, part in zip(small_names, parts[1:]):
        p[k] = jnp.concatenate([part[2 * s] for s in range(NSHARD)], axis=SMALL_SHARD_AXIS[k])
    for k in SMALL_REPLICATED:
        p[k] = w[k]

    ncol = w["mod_w"].shape[2]
    mod_b_cols = lax.dynamic_slice_in_dim(w["mod_b"], shard * ncol, ncol, axis=1).reshape(DEPTH, 1, ncol)
    mod_part = _mod_fwd(c_all, w["mod_w"], mod_b_cols, name="mod_fwd")
    gathered1 = _all_gather8(mod_part.reshape(DEPTH * NDEV, ncol), name="gather_mod")
    mod_all = gathered1.reshape(NSHARD, 2, DEPTH, NDEV, ncol)[:, 0]
    mod_mine = lax.dynamic_index_in_dim(mod_all, example, axis=2, keepdims=False)
    mod = jnp.transpose(mod_mine, (1, 0, 2)).reshape(DEPTH, 6, d)

    shard_arr = shard.reshape(1).astype(jnp.int32)
    bufs, kinds = [], []
    for k in BIG:
        wk = _as_layers_rows_cols(k, w[k])
        nl, r, cdim = wk.shape
        if nl >= 2:
            a, rh = nl // 2, r
        else:
            a, rh = 1, r // 2
        if k in COL_SHARDED:
            kind, out_shape = "col", (NSHARD, 2, a, rh, cdim)
        elif nl >= 2:
            kind, out_shape = "row", (2, a, NSHARD, rh, cdim)
        else:
            kind, out_shape = "row1", (1, NSHARD, 2, rh, cdim)
        kinds.append(kind)
        bufs.append(_cast_into_gathered(wk.reshape(2, a, rh, cdim), kind, out_shape, shard_arr, name=f"cast_{k}"))
    full = _gather_weights(bufs, kinds, name="gather_weights")
    for k, f in zip(BIG, full):
        nl, r, cdim = _as_layers_rows_cols(k, w[k]).shape
        if k in COL_SHARDED:
            p[k] = f.reshape(NSHARD, nl, r, cdim)
        elif k == "b_group_w":
            p[k] = f.reshape(1, nl, NSHARD * r, cdim)
        else:
            p[k] = f.reshape(nl, NSHARD * r, cdim)

    loss, grad_x, dmod, small, big = _local_step(x[0], loss_target[0], mod, p)

    gs, layers_of = [], []
    for k in BIG:
        ks = []
        for g in big[k]:
            if k == "b_group_w":
                ng, rr, cc = g.shape
                g = jnp.transpose(g.reshape(ng, NSHARD, rr // NSHARD, cc), (1, 0, 2, 3)).reshape(NSHARD, -1, cc)
            elif k not in COL_SHARDED:
                g = g.reshape(NSHARD, g.shape[0] // NSHARD, g.shape[1])
            ks.append(len(gs))
            gs.append(g.reshape(NSHARD, 2, g.shape[1] // 2, g.shape[2]))
        layers_of.append(ks)
    half = ci.reshape(1).astype(jnp.int32)
    place = jnp.stack([shard, ci]).astype(jnp.int32)
    r1 = _pair_exchange(gs, layers_of, name="grad_pair_exchange")
    ss = []
    for k, ks, r1_t in zip(BIG, layers_of, r1):
        s_t = lax.empty(r1_t.shape, BF16)
        for l, i in enumerate(ks):
            s_t = _pair_sum(gs[i], r1_t, s_t, l, half, name=f"grad_pair_sum_{k}_{l}")
        ss.append(s_t)
    r3 = _chip_exchange(ss, name="grad_chip_exchange")
    reds = [_chip_sum(s_t, r3_t, place, name=f"grad_chip_sum_{k}") for k, s_t, r3_t in zip(BIG, ss, r3)]
    joined = _join_halves(reds, name="grad_join_halves")
    grads = {k: j.reshape(w[k].shape) for k, j in zip(BIG, joined)}

    rep_names = SMALL_REPLICATED
    small_list = [small[k] for k in rep_names] + [small[k] for k in small_names] + [dmod]
    gathered2 = _all_gather8(_pack(small_list), name="gather_small_grads")
    summed = _sum_devices(gathered2, name="sum_small_grads")
    shapes = [s.shape for s in small_list]
    sums = _unpack(summed, shapes)
    for k, s in zip(rep_names, sums[:len(rep_names)]):
        grads[k] = s
    for k, s in zip(small_names, sums[len(rep_names):-1]):
        ax = SMALL_SHARD_AXIS[k]
        grads[k] = lax.dynamic_slice_in_dim(s, shard * w[k].shape[ax], w[k].shape[ax], axis=ax)
    grads["mod_b"] = sums[-1].reshape(w["mod_b"].shape)
    dmod_all = _unpack(gathered2, shapes, lead=(NDEV,))[-1].reshape(NDEV, DEPTH, NSHARD, ncol)
    dmod_cols = jnp.transpose(lax.dynamic_index_in_dim(dmod_all, shard, axis=2, keepdims=False), (1, 0, 2))

    delta, new_m, new_v = {}, {}, {}
    grads["mod_w"], delta["mod_w"], new_m["mod_w"], new_v["mod_w"] = _mod_w_update(
        c_all.T, dmod_cols, w["mod_w"], m["mod_w"], v["mod_w"], name="mod_w_update")
    for k in BIG:
        cdim = w[k].shape[-1]
        outs = _adamw(*[t.reshape(-1, cdim) for t in (w[k], grads[k], m[k], v[k])], name=f"adamw_{k}")
        delta[k], new_m[k], new_v[k] = [o.reshape(w[k].shape) for o in outs]
    rest = ("mod_b",) + rep_names + small_names
    packs = [_pack([t[k] for k in rest]) for t in (w, grads, m, v)]
    outs = _adamw(*packs, name="adamw_small")
    rest_shapes = [w[k].shape for k in rest]
    for dst, o in zip((delta, new_m, new_v), outs):
        for k, t in zip(rest, _unpack(o, rest_shapes)):
            dst[k] = t

    loss_all = lax.psum(loss[0, 0], ("x", "y", "c"))
    return (loss_all, grad_x[None], *[grads[k] for k in WEIGHT_ORDER], *[delta[k] for k in WEIGHT_ORDER],
            *[new_m[k] for k in WEIGHT_ORDER], *[new_v[k] for k in WEIGHT_ORDER])


def kernel(x, c, mod_w, mod_b, norm_pre_mix, norm_post_mix, norm_pre_ffn, norm_post_ffn, a_pw1_w, a_pw1_b, a_dw_w, a_dw_b, a_ln_g, a_ln_b, a_pw2_w, a_pw2_b, b_group_w, b_group_b, b_scale, c_in_w, c_conv_w, c_out_w, f_up_w, f_dw_w, f_dw_b, f_down_w, loss_target, m_mod_w, m_mod_b, m_norm_pre_mix, m_norm_post_mix, m_norm_pre_ffn, m_norm_post_ffn, m_a_pw1_w, m_a_pw1_b, m_a_dw_w, m_a_dw_b, m_a_ln_g, m_a_ln_b, m_a_pw2_w, m_a_pw2_b, m_b_group_w, m_b_group_b, m_b_scale, m_c_in_w, m_c_conv_w, m_c_out_w, m_f_up_w, m_f_dw_w, m_f_dw_b, m_f_down_w, v_mod_w, v_mod_b, v_norm_pre_mix, v_norm_post_mix, v_norm_pre_ffn, v_norm_post_ffn, v_a_pw1_w, v_a_pw1_b, v_a_dw_w, v_a_dw_b, v_a_ln_g, v_a_ln_b, v_a_pw2_w, v_a_pw2_b, v_b_group_w, v_b_group_b, v_b_scale, v_c_in_w, v_c_conv_w, v_c_out_w, v_f_up_w, v_f_dw_w, v_f_dw_b, v_f_down_w):
    given = dict(locals())
    w = {k: given[k] for k in WEIGHT_ORDER}
    m = {k: given["m_" + k] for k in WEIGHT_ORDER}
    v = {k: given["v_" + k] for k in WEIGHT_ORDER}
    return _step(x, c, loss_target, w, m, v)
```

```python
import functools

import jax
import jax.numpy as jnp
from jax import lax
from jax.experimental import pallas as pl
from jax.experimental.pallas import tpu as pltpu

F32 = jnp.float32
BF16 = jnp.bfloat16

DEPTH = 4
N_MIXERS = 3
CONF_CONV_WIDTH = 31
POOL_WINDOWS = (2, 4, 8, 16)
RMS_EPS = 1e-6
LN_EPS = 1e-5
ADAM_LR = 0.001
ADAM_B1 = 0.9
ADAM_B2 = 0.999
ADAM_EPS = 1e-08
ADAM_WD = 0.01
ADAM_STEP = 10

TM_ROW = 512
TM_CONV = 128
TK_TOKENS = 1024
TM_MM = 1024
TN_MM = 1024
SUBLANES = 8
LANES = 128
NSHARD = 4
NDEV = 8
HALO_A = 32
HALO_POOL = 16
HALO_3 = 8
HALO_BF16 = 16
VMEM_LIMIT = 56 * 1024 * 1024

_PAR = "parallel"
_ARB = "arbitrary"


def _cparams(*sem):
    return pltpu.CompilerParams(dimension_semantics=sem, vmem_limit_bytes=VMEM_LIMIT)


def _pick(n, prefs):
    for p in prefs:
        if p <= n and n % p == 0:
            return p
    return n


def _row_spec(tm, width):
    return pl.BlockSpec((tm, width), lambda i: (i, 0))


def _vec_spec(rows, width):
    return pl.BlockSpec((rows, width), lambda i: (0, 0))


def _prev_spec(tm, hb, width):
    return pl.BlockSpec((hb, width), lambda i: (jnp.maximum(i * (tm // hb) - 1, 0), 0))


def _next_spec(tm, hb, width, total):
    last = total // hb - 1
    return pl.BlockSpec((hb, width), lambda i: (jnp.minimum((i + 1) * (tm // hb), last), 0))


def _sum8(v):
    r, c = v.shape
    return jnp.sum(v.reshape(r // SUBLANES, SUBLANES, c), axis=0)


def _rms(x):
    r = lax.rsqrt(jnp.mean(x * x, axis=-1, keepdims=True) + RMS_EPS)
    return x * r, r


def _rms_bwd(dy, xn, r):
    return r * (dy - xn * jnp.mean(dy * xn, axis=-1, keepdims=True))


def _sigmoid(x):
    return 1.0 / (1.0 + jnp.exp(-x))


_DIMS = {"nn": ((1,), (0,)), "nt": ((1,), (1,)), "tn": ((0,), (0,))}


def _mm(a, b, *, mode, name, out_dtype=F32, bias=None, tm=TM_MM, tn=TN_MM, tk=None, layer=None):
    sharded = layer is not None
    if mode == "nn":
        m, k = a.shape
        n = NSHARD * b.shape[3] if sharded else b.shape[1]
    elif mode == "nt":
        m, k = a.shape
        n = b.shape[2] if sharded else b.shape[0]
    else:
        (k, m), (_, n) = a.shape, b.shape
    ns = n // NSHARD
    ks = k // NSHARD
    tm = _pick(m, (tm, 1408, 512, 256, 128))
    if sharded and mode != "nt":
        tn = _pick(ns, (1408, 768, 512, 256, 128))
    else:
        tn = _pick(n, (tn, 1408, 512, 256, 128))
    if sharded and mode == "nt":
        tk = _pick(ks, (1408, 768, 512, 256, 128))
    else:
        tk = _pick(k, (tk or k, 2816, 1024, 512, 256, 128))
    nk = k // tk
    per_n = ns // tn if sharded and mode != "nt" else 1
    per_k = ks // tk if sharded and mode == "nt" else 1
    dims = (_DIMS[mode], ((), ()))

    def split(idx, per):
        return (idx, 0) if per == 1 else (idx // per, idx % per)

    def body(*refs):
        a_ref, b_ref = refs[0], refs[1]
        bias_ref = refs[2] if bias is not None else None
        o_ref = refs[3] if bias is not None else refs[2]
        part = lax.dot_general(a_ref[...].astype(BF16), b_ref[...].astype(BF16), dims,
                               preferred_element_type=F32)

        def finish(r):
            if bias_ref is not None:
                r = r + bias_ref[...]
            o_ref[...] = r.astype(out_dtype)

        if nk == 1:
            finish(part)
        else:
            acc_ref = refs[-1]
            kk = pl.program_id(2)

            @pl.when(kk == 0)
            def _():
                acc_ref[...] = part

            @pl.when(kk > 0)
            def _():
                acc_ref[...] += part

            @pl.when(kk == nk - 1)
            def _():
                finish(acc_ref[...])

    out_spec = pl.BlockSpec((tm, tn), lambda i, j, kk: (i, j))
    out_shape = jax.ShapeDtypeStruct((m, n), out_dtype)
    if mode == "nn":
        a_spec = pl.BlockSpec((tm, tk), lambda i, j, kk: (i, kk))
        if sharded:
            b_spec = pl.BlockSpec((None, None, tk, tn),
                                  lambda i, j, kk: (split(j, per_n)[0], layer, kk, split(j, per_n)[1]))
        else:
            b_spec = pl.BlockSpec((tk, tn), lambda i, j, kk: (kk, j))
    elif mode == "nt":
        a_spec = pl.BlockSpec((tm, tk), lambda i, j, kk: (i, kk))
        if sharded:
            b_spec = pl.BlockSpec((None, None, tn, tk),
                                  lambda i, j, kk: (split(kk, per_k)[0], layer, j, split(kk, per_k)[1]))
        else:
            b_spec = pl.BlockSpec((tn, tk), lambda i, j, kk: (j, kk))
    else:
        a_spec = pl.BlockSpec((tk, tm), lambda i, j, kk: (kk, i))
        b_spec = pl.BlockSpec((tk, tn), lambda i, j, kk: (kk, j))
        if sharded:
            out_spec = pl.BlockSpec((None, tm, tn), lambda i, j, kk: (split(j, per_n)[0], i, split(j, per_n)[1]))
            out_shape = jax.ShapeDtypeStruct((NSHARD, m, ns), out_dtype)
    in_specs = [a_spec, b_spec]
    args = [a, b]
    if bias is not None:
        in_specs.append(pl.BlockSpec((1, tn), lambda i, j, kk: (0, j)))
        args.append(bias)
    return pl.pallas_call(
        body, grid=(m // tm, n // tn, nk), in_specs=in_specs, out_specs=out_spec, out_shape=out_shape,
        scratch_shapes=[pltpu.VMEM((tm, tn), F32)] if nk > 1 else [],
        compiler_params=_cparams(_PAR, _PAR, _ARB), name=name)(*args)


def _mm_group(a, b, *, mode, name, out_dtype=F32, tm=2048):
    t = a.shape[0]
    tm = _pick(t, (tm, 1024, 512, 256, 128))
    nt_ = t // tm
    g = len(POOL_WINDOWS)
    gd = a.shape[1] // g
    dims = (_DIMS[mode], ((), ()))

    if mode == "tn":
        def body(a_ref, b_ref, o_ref, acc_ref):
            kk = pl.program_id(1)
            part = lax.dot_general(a_ref[...].astype(BF16), b_ref[...].astype(BF16), dims,
                                   preferred_element_type=F32)

            @pl.when(kk == 0)
            def _():
                acc_ref[...] = part

            @pl.when(kk > 0)
            def _():
                acc_ref[...] += part

            @pl.when(kk == nt_ - 1)
            def _():
                o_ref[...] = acc_ref[...]

        return pl.pallas_call(
            body, grid=(g, nt_),
            in_specs=[pl.BlockSpec((tm, gd), lambda gi, kk: (kk, gi)),
                      pl.BlockSpec((tm, gd), lambda gi, kk: (kk, gi))],
            out_specs=pl.BlockSpec((None, gd, gd), lambda gi, kk: (gi, 0, 0)),
            out_shape=jax.ShapeDtypeStruct((g, gd, gd), F32),
            scratch_shapes=[pltpu.VMEM((gd, gd), F32)],
            compiler_params=_cparams(_PAR, _ARB), name=name)(a, b)

    def body(a_ref, b_ref, o_ref):
        o_ref[...] = lax.dot_general(a_ref[...].astype(BF16), b_ref[...].astype(BF16), dims,
                                     preferred_element_type=F32).astype(out_dtype)

    return pl.pallas_call(
        body, grid=(nt_, g),
        in_specs=[pl.BlockSpec((tm, gd), lambda i, gi: (i, gi)),
                  pl.BlockSpec((None, gd, gd), lambda i, gi: (gi, 0, 0))],
        out_specs=pl.BlockSpec((tm, gd), lambda i, gi: (i, gi)),
        out_shape=jax.ShapeDtypeStruct((t, g * gd), out_dtype),
        compiler_params=_cparams(_PAR, _PAR), name=name)(a, b)


def _pre(x, gpre, sc, sh):
    xn, r = _rms(x)
    return (xn * gpre) * (1.0 + sc) + sh, xn, r


def _fwd_first(x, gpre, sc, sh, *, h_dtype, name):
    t, d = x.shape
    tm = _pick(t, (TM_ROW, 256, 128))

    def body(x_ref, gpre_ref, sc_ref, sh_ref, h_ref):
        h, _, _ = _pre(x_ref[...], gpre_ref[...], sc_ref[...], sh_ref[...])
        h_ref[...] = h.astype(h_dtype)

    return pl.pallas_call(
        body, grid=(t // tm,),
        in_specs=[_row_spec(tm, d)] + [_vec_spec(1, d)] * 3,
        out_specs=_row_spec(tm, d), out_shape=jax.ShapeDtypeStruct((t, d), h_dtype),
        compiler_params=_cparams(_PAR), name=name)(x, gpre, sc, sh)


def _fwd_mid(x, y, gpost, gt, gpre, sc, sh, *, h_dtype, name):
    t, d = x.shape
    tm = _pick(t, (TM_ROW, 256, 128))

    def body(x_ref, y_ref, gpost_ref, gt_ref, gpre_ref, sc_ref, sh_ref, xn_ref, h_ref):
        yn, _ = _rms(y_ref[...])
        x_new = x_ref[...] + gt_ref[...] * (yn * gpost_ref[...])
        xn_ref[...] = x_new
        h, _, _ = _pre(x_new, gpre_ref[...], sc_ref[...], sh_ref[...])
        h_ref[...] = h.astype(h_dtype)

    return pl.pallas_call(
        body, grid=(t // tm,),
        in_specs=[_row_spec(tm, d)] * 2 + [_vec_spec(1, d)] * 5,
        out_specs=[_row_spec(tm, d)] * 2,
        out_shape=[jax.ShapeDtypeStruct((t, d), F32), jax.ShapeDtypeStruct((t, d), h_dtype)],
        compiler_params=_cparams(_PAR), name=name)(x, y, gpost, gt, gpre, sc, sh)


def _post_bwd(dx, y, gpost, gt):
    yn, r2 = _rms(y)
    dyn = dx * (gt * gpost)
    dy = _rms_bwd(dyn, yn, r2)
    return dy, dx * yn


def _last_fwd_bwd(x, y, target, gpost, gt, *, name):
    t, d = x.shape
    tm = _pick(t, (TM_ROW, 256, 128))
    n = t // tm

    def body(x_ref, y_ref, tg_ref, gpost_ref, gt_ref, dx_ref, dy_ref, dgpost_ref, dgt_ref, sdy_ref,
             loss_ref, qa, sa, la):
        i = pl.program_id(0)

        @pl.when(i == 0)
        def _():
            qa[...] = jnp.zeros_like(qa)
            sa[...] = jnp.zeros_like(sa)
            la[...] = jnp.zeros_like(la)

        yv = y_ref[...]
        yn, r2 = _rms(yv)
        gt_v, gpost_v = gt_ref[...], gpost_ref[...]
        err = x_ref[...] + gt_v * (yn * gpost_v) - tg_ref[...]
        la[...] += _sum8(err * err)
        dx = err * (1.0 / d)
        dx_ref[...] = dx
        dy = _rms_bwd(dx * (gt_v * gpost_v), yn, r2)
        dy_ref[...] = dy.astype(dy_ref.dtype)
        qa[...] += _sum8(dx * yn)
        sa[...] += _sum8(dy)

        @pl.when(i == n - 1)
        def _():
            q = jnp.sum(qa[...], axis=0, keepdims=True)
            dgpost_ref[...] = gt_v * q
            dgt_ref[...] = gpost_v * q
            sdy_ref[...] = jnp.sum(sa[...], axis=0, keepdims=True)
            tot = jnp.sum(jnp.sum(la[...], axis=0, keepdims=True), axis=1, keepdims=True)
            loss_ref[...] = tot * (0.5 / d)

    return pl.pallas_call(
        body, grid=(n,),
        in_specs=[_row_spec(tm, d)] * 3 + [_vec_spec(1, d)] * 2,
        out_specs=[_row_spec(tm, d)] * 2 + [_vec_spec(1, d)] * 3 + [_vec_spec(1, 1)],
        out_shape=[jax.ShapeDtypeStruct((t, d), F32), jax.ShapeDtypeStruct((t, d), BF16)]
        + [jax.ShapeDtypeStruct((1, d), F32)] * 3 + [jax.ShapeDtypeStruct((1, 1), F32)],
        scratch_shapes=[pltpu.VMEM((SUBLANES, d), F32)] * 3,
        compiler_params=_cparams(_ARB), name=name)(x, y, target, gpost, gt)


def _bwd_mid(dx_new, dh, x_in, gpre, sc, y_prev, gpost_p, gt_p, *, dy_dtype, name):
    t, d = x_in.shape
    tm = _pick(t, (TM_ROW, 256, 128))
    n = t // tm

    def body(dxn_ref, dh_ref, x_ref, gpre_ref, sc_ref, y_ref, gpost_ref, gt_ref,
             dx_ref, dy_ref, dsh_ref, dsc_ref, dgpre_ref, dgpost_ref, dgt_ref, sdy_ref, a1, a2, aq, asd):
        i = pl.program_id(0)

        @pl.when(i == 0)
        def _():
            for a in (a1, a2, aq, asd):
                a[...] = jnp.zeros_like(a)

        dh_v = dh_ref[...]
        xn, r = _rms(x_ref[...])
        dx = dxn_ref[...] + _rms_bwd(dh_v * ((1.0 + sc_ref[...]) * gpre_ref[...]), xn, r)
        dx_ref[...] = dx
        a1[...] += _sum8(dh_v)
        a2[...] += _sum8(dh_v * xn)
        dy, dxyn = _post_bwd(dx, y_ref[...], gpost_ref[...], gt_ref[...])
        dy_ref[...] = dy.astype(dy_dtype)
        aq[...] += _sum8(dxyn)
        asd[...] += _sum8(dy)

        @pl.when(i == n - 1)
        def _():
            s2 = jnp.sum(a2[...], axis=0, keepdims=True)
            q = jnp.sum(aq[...], axis=0, keepdims=True)
            dsh_ref[...] = jnp.sum(a1[...], axis=0, keepdims=True)
            dsc_ref[...] = gpre_ref[...] * s2
            dgpre_ref[...] = (1.0 + sc_ref[...]) * s2
            dgpost_ref[...] = gt_ref[...] * q
            dgt_ref[...] = gpost_ref[...] * q
            sdy_ref[...] = jnp.sum(asd[...], axis=0, keepdims=True)

    return pl.pallas_call(
        body, grid=(n,),
        in_specs=[_row_spec(tm, d)] * 3 + [_vec_spec(1, d)] * 2 + [_row_spec(tm, d)] + [_vec_spec(1, d)] * 2,
        out_specs=[_row_spec(tm, d)] * 2 + [_vec_spec(1, d)] * 6,
        out_shape=[jax.ShapeDtypeStruct((t, d), F32), jax.ShapeDtypeStruct((t, d), dy_dtype)]
        + [jax.ShapeDtypeStruct((1, d), F32)] * 6,
        scratch_shapes=[pltpu.VMEM((SUBLANES, d), F32)] * 4,
        compiler_params=_cparams(_ARB), name=name)(dx_new, dh, x_in, gpre, sc, y_prev, gpost_p, gt_p)


def _bwd_first(dx_new, dh, x_in, gpre, sc, *, name):
    t, d = x_in.shape
    tm = _pick(t, (TM_ROW, 256, 128))
    n = t // tm

    def body(dxn_ref, dh_ref, x_ref, gpre_ref, sc_ref, dx_ref, dsh_ref, dsc_ref, dgpre_ref, a1, a2):
        i = pl.program_id(0)

        @pl.when(i == 0)
        def _():
            a1[...] = jnp.zeros_like(a1)
            a2[...] = jnp.zeros_like(a2)

        dh_v = dh_ref[...]
        xn, r = _rms(x_ref[...])
        dx_ref[...] = dxn_ref[...] + _rms_bwd(dh_v * ((1.0 + sc_ref[...]) * gpre_ref[...]), xn, r)
        a1[...] += _sum8(dh_v)
        a2[...] += _sum8(dh_v * xn)

        @pl.when(i == n - 1)
        def _():
            s2 = jnp.sum(a2[...], axis=0, keepdims=True)
            dsh_ref[...] = jnp.sum(a1[...], axis=0, keepdims=True)
            dsc_ref[...] = gpre_ref[...] * s2
            dgpre_ref[...] = (1.0 + sc_ref[...]) * s2

    return pl.pallas_call(
        body, grid=(n,),
        in_specs=[_row_spec(tm, d)] * 3 + [_vec_spec(1, d)] * 2,
        out_specs=[_row_spec(tm, d)] + [_vec_spec(1, d)] * 3,
        out_shape=[jax.ShapeDtypeStruct((t, d), F32)] + [jax.ShapeDtypeStruct((1, d), F32)] * 3,
        scratch_shapes=[pltpu.VMEM((SUBLANES, d), F32)] * 2,
        compiler_params=_cparams(_ARB), name=name)(dx_new, dh, x_in, gpre, sc)


def _conv3_rows(buf, w_ref, rows, first):
    out = buf[pl.ds(first, rows), :] * w_ref[pl.ds(0, 1), :]
    for k in (1, 2):
        out = out + buf[pl.ds(first + k, rows), :] * w_ref[pl.ds(k, 1), :]
    return out


ROWS_BLK = 16
COLS_BLK = 256


def _bcast_rows(dst, src_ref, first, nrows):
    for k in range(nrows):
        dst[first + k] = jnp.broadcast_to(src_ref[pl.ds(k, 1), :], dst.shape[1:])


def _shifted_rows(x, off, rows):
    if off % SUBLANES == 0:
        return x[off:off + rows]
    return pltpu.roll(x, x.shape[0] - off, axis=0)[:rows]


def _conv3_blk(buf, wb, first, rows, cols):
    base = first - first % SUBLANES
    window = buf[pl.ds(base, rows + SUBLANES), cols]
    xs = [_shifted_rows(window, first - base + k, rows) for k in range(3)]
    out = xs[0] * wb[0, pl.ds(0, rows), cols]
    out = out + xs[1] * wb[1, pl.ds(0, rows), cols]
    out = out + xs[2] * wb[2, pl.ds(0, rows), cols]
    return out + wb[3, pl.ds(0, rows), cols], xs


def _ffn_gate_fwd(u, w, b, *, name):
    t, f2 = u.shape
    f = f2 // 2
    tm = _pick(t, (TM_CONV,))
    hb = HALO_BF16
    rb = ROWS_BLK
    cw = _pick(f, (COLS_BLK, LANES))

    def body(u_ref, up_ref, w_ref, b_ref, a_ref, v_ref, buf, wb):
        i = pl.program_id(0)
        buf[pl.ds(hb, tm), :] = u_ref[...].astype(F32)
        buf[pl.ds(0, hb), :] = jnp.where(i > 0, up_ref[...].astype(F32), 0.0)
        _bcast_rows(wb, w_ref, 0, 3)
        _bcast_rows(wb, b_ref, 3, 1)
        for c0 in range(0, f, cw):
            gcols, vcols = pl.ds(c0, cw), pl.ds(f + c0, cw)
            for r0 in range(0, tm, rb):
                rows = pl.ds(r0, rb)
                vg, _ = _conv3_blk(buf, wb, hb - 2 + r0, rb, gcols)
                vv, _ = _conv3_blk(buf, wb, hb - 2 + r0, rb, vcols)
                v_ref[rows, gcols] = vg.astype(BF16)
                v_ref[rows, vcols] = vv.astype(BF16)
                a_ref[rows, gcols] = (vg * _sigmoid(vg) * vv).astype(BF16)

    return pl.pallas_call(
        body, grid=(t // tm,),
        in_specs=[_row_spec(tm, f2), _prev_spec(tm, hb, f2), _vec_spec(3, f2), _vec_spec(1, f2)],
        out_specs=[_row_spec(tm, f), _row_spec(tm, f2)],
        out_shape=[jax.ShapeDtypeStruct((t, f), BF16), jax.ShapeDtypeStruct((t, f2), BF16)],
        scratch_shapes=[pltpu.VMEM((tm + hb, f2), F32), pltpu.VMEM((4, rb, f2), F32)],
        compiler_params=_cparams(_PAR), name=name)(u, u, w, b)


def _ffn_gate_bwd(u, v, da, w, *, name):
    t, f2 = u.shape
    f = f2 // 2
    tm = _pick(t, (TM_CONV,))
    hb = HALO_BF16
    n = t // tm
    rb = ROWS_BLK
    cw = _pick(f, (COLS_BLK, LANES))
    blocks = [(r0, rb) for r0 in range(0, tm, rb)] + [(tm, hb)]

    def body(u_ref, v_ref, vn_ref, da_ref, dan_ref, w_ref, du_ref, dw_ref, db_ref, dvbuf, wb, wacc, bacc):
        i = pl.program_id(0)

        @pl.when(i == 0)
        def _():
            wacc[...] = jnp.zeros_like(wacc)
            bacc[...] = jnp.zeros_like(bacc)

        _bcast_rows(wb, w_ref, 0, 3)
        for c0 in range(0, f, cw):
            gcols, vcols = pl.ds(c0, cw), pl.ds(f + c0, cw)
            for r0, rows in blocks:
                if r0 < tm:
                    vg, vv = v_ref[pl.ds(r0, rows), gcols], v_ref[pl.ds(r0, rows), vcols]
                    dav = da_ref[pl.ds(r0, rows), gcols]
                else:
                    vg, vv = vn_ref[:, gcols], vn_ref[:, vcols]
                    dav = jnp.where(i < n - 1, dan_ref[:, gcols], 0.0)
                vg, vv = vg.astype(F32), vv.astype(F32)
                sg = _sigmoid(vg)
                dvg = dav * vv * (sg * (1.0 + vg * (1.0 - sg)))
                dvv = dav * (vg * sg)
                dvbuf[pl.ds(r0, rows), gcols] = dvg
                dvbuf[pl.ds(r0, rows), vcols] = dvv
                if r0 < tm:
                    bacc[:, gcols] += _sum8(dvg)
                    bacc[:, vcols] += _sum8(dvv)
        for c0 in range(0, f2, cw):
            cols = pl.ds(c0, cw)
            for r0 in range(0, tm, rb):
                uv = u_ref[pl.ds(r0, rb), cols].astype(F32)
                window = dvbuf[pl.ds(r0, rb + SUBLANES), cols]
                du = None
                for k in range(3):
                    dvk = _shifted_rows(window, 2 - k, rb)
                    term = dvk * wb[k, :, cols]
                    du = term if du is None else du + term
                    wacc[k, :, cols] += _sum8(uv * dvk)
                du_ref[pl.ds(r0, rb), cols] = du.astype(BF16)

        @pl.when(i == n - 1)
        def _():
            db_ref[...] = jnp.sum(bacc[...], axis=0, keepdims=True)
            dw_ref[...] = jnp.sum(wacc[...], axis=1)

    return pl.pallas_call(
        body, grid=(n,),
        in_specs=[_row_spec(tm, f2), _row_spec(tm, f2), _next_spec(tm, hb, f2, t),
                  _row_spec(tm, f), _next_spec(tm, hb, f, t), _vec_spec(3, f2)],
        out_specs=[_row_spec(tm, f2), _vec_spec(3, f2), _vec_spec(1, f2)],
        out_shape=[jax.ShapeDtypeStruct((t, f2), BF16), jax.ShapeDtypeStruct((3, f2), F32),
                   jax.ShapeDtypeStruct((1, f2), F32)],
        scratch_shapes=[pltpu.VMEM((tm + hb, f2), F32), pltpu.VMEM((3, rb, f2), F32),
                        pltpu.VMEM((3, SUBLANES, f2), F32), pltpu.VMEM((SUBLANES, f2), F32)],
        compiler_params=_cparams(_ARB), name=name)(u, v, v, da, da, w)


def _glu(u, b1, d):
    return (u[:, :d] + b1[:, :d]) * _sigmoid(u[:, d:] + b1[:, d:])


ROWS_TAPS = 32
ROWS_NORM = 16


def _fill_glu_buf(buf, u_ref, up_ref, b1_ref, i, tm, d):
    cw = _pick(d, (COLS_BLK, LANES))
    for c0 in range(0, d, cw):
        b1 = jnp.concatenate([b1_ref[:, pl.ds(c0, cw)], b1_ref[:, pl.ds(d + c0, cw)]], axis=1)
        up = jnp.concatenate([up_ref[:, pl.ds(c0, cw)], up_ref[:, pl.ds(d + c0, cw)]], axis=1)
        buf[pl.ds(0, HALO_A), pl.ds(c0, cw)] = jnp.where(i > 0, _glu(up, b1, cw), 0.0)
        for r0 in range(0, tm, ROWS_TAPS):
            rows = pl.ds(r0, ROWS_TAPS)
            uv = jnp.concatenate([u_ref[rows, pl.ds(c0, cw)], u_ref[rows, pl.ds(d + c0, cw)]], axis=1)
            buf[pl.ds(HALO_A + r0, ROWS_TAPS), pl.ds(c0, cw)] = _glu(uv, b1, cw)


def _taps31(buf, r0, cols, offs, use):
    nv = ROWS_TAPS // SUBLANES
    nrows = ROWS_TAPS + SUBLANES * (-(-max(offs) // SUBLANES))
    window = buf[pl.ds(r0, nrows), cols]
    shifted = {b: _shifted_rows(window, b, nrows - SUBLANES) if b else window
               for b in sorted({o % SUBLANES for o in offs})}
    for k, o in enumerate(offs):
        b, a = o % SUBLANES, o // SUBLANES
        use(k, [shifted[b][SUBLANES * (a + v):SUBLANES * (a + v + 1)] for v in range(nv)])


def _conv_taps_blk(buf, wb, r0, cols, offs):
    nv = ROWS_TAPS // SUBLANES
    acc = [None] * nv

    def use(k, rows):
        wk = wb[k, :, cols]
        for v in range(nv):
            term = rows[v] * wk
            acc[v] = term if acc[v] is None else acc[v] + term

    _taps31(buf, r0, cols, offs, use)
    return jnp.concatenate(acc, axis=0)


def _layernorm_parts(x):
    mu = jnp.mean(x, axis=-1, keepdims=True)
    xc = x - mu
    rstd = lax.rsqrt(jnp.mean(xc * xc, axis=-1, keepdims=True) + LN_EPS)
    return xc * rstd, rstd


_FWD_OFFS = tuple(HALO_A - (CONF_CONV_WIDTH - 1) + k for k in range(CONF_CONV_WIDTH))
_BWD_OFFS = tuple(CONF_CONV_WIDTH - 1 - k for k in range(CONF_CONV_WIDTH))


def _a_fwd(u1, b1, dww, dwb, lng, lnb, *, name):
    t, d2 = u1.shape
    d = d2 // 2
    tm = _pick(t, (TM_CONV,))

    def body(u_ref, up_ref, b1_ref, w_ref, wbias_ref, g_ref, bb_ref, o_ref, u3_ref, buf, wb):
        i = pl.program_id(0)
        _fill_glu_buf(buf, u_ref, up_ref, b1_ref, i, tm, d)
        _bcast_rows(wb, w_ref, 0, CONF_CONV_WIDTH)
        for c0 in range(0, d, LANES):
            cols = pl.ds(c0, LANES)
            for r0 in range(0, tm, ROWS_TAPS):
                u3_ref[pl.ds(r0, ROWS_TAPS), cols] = (_conv_taps_blk(buf, wb, r0, cols, _FWD_OFFS)
                                                      + wbias_ref[:, cols])
        for r0 in range(0, tm, ROWS_NORM):
            rows = pl.ds(r0, ROWS_NORM)
            xhat, _ = _layernorm_parts(u3_ref[rows, :])
            u4 = xhat * g_ref[...] + bb_ref[...]
            o_ref[rows, :] = (u4 * _sigmoid(u4)).astype(BF16)

    return pl.pallas_call(
        body, grid=(t // tm,),
        in_specs=[_row_spec(tm, d2), _prev_spec(tm, HALO_A, d2), _vec_spec(1, d2),
                  _vec_spec(CONF_CONV_WIDTH, d)] + [_vec_spec(1, d)] * 3,
        out_specs=[_row_spec(tm, d)] * 2,
        out_shape=[jax.ShapeDtypeStruct((t, d), BF16), jax.ShapeDtypeStruct((t, d), F32)],
        scratch_shapes=[pltpu.VMEM((tm + HALO_A, d), F32), pltpu.VMEM((CONF_CONV_WIDTH, SUBLANES, d), F32)],
        compiler_params=_cparams(_PAR), name=name)(u1, u1, b1, dww, dwb, lng, lnb)


def _a_bwd_norm(u3, du5, lng, lnb, *, name):
    t, d = u3.shape
    tm = _pick(t, (TM_ROW, 256, 128))
    n = t // tm

    def body(u3_ref, du5_ref, g_ref, bb_ref, du3_ref, dg_ref, db_ref, dwb_ref, ag, ab, aw):
        i = pl.program_id(0)

        @pl.when(i == 0)
        def _():
            for a in (ag, ab, aw):
                a[...] = jnp.zeros_like(a)

        g = g_ref[...]
        for r0 in range(0, tm, ROWS_NORM):
            rows = pl.ds(r0, ROWS_NORM)
            xhat, rstd = _layernorm_parts(u3_ref[rows, :])
            u4 = xhat * g + bb_ref[...]
            sg = _sigmoid(u4)
            du4 = du5_ref[rows, :] * (sg * (1.0 + u4 * (1.0 - sg)))
            dxh = du4 * g
            du3 = rstd * (dxh - jnp.mean(dxh, axis=-1, keepdims=True)
                          - xhat * jnp.mean(dxh * xhat, axis=-1, keepdims=True))
            du3_ref[rows, :] = du3
            ag[...] += _sum8(du4 * xhat)
            ab[...] += _sum8(du4)
            aw[...] += _sum8(du3)

        @pl.when(i == n - 1)
        def _():
            for a, o in ((ag, dg_ref), (ab, db_ref), (aw, dwb_ref)):
                o[...] = jnp.sum(a[...], axis=0, keepdims=True)

    return pl.pallas_call(
        body, grid=(n,),
        in_specs=[_row_spec(tm, d)] * 2 + [_vec_spec(1, d)] * 2,
        out_specs=[_row_spec(tm, d)] + [_vec_spec(1, d)] * 3,
        out_shape=[jax.ShapeDtypeStruct((t, d), F32)] + [jax.ShapeDtypeStruct((1, d), F32)] * 3,
        scratch_shapes=[pltpu.VMEM((SUBLANES, d), F32)] * 3,
        compiler_params=_cparams(_ARB), name=name)(u3, du5, lng, lnb)


def _a_bwd_conv(u1, du3, b1, dww, *, name):
    t, d2 = u1.shape
    d = d2 // 2
    tm = _pick(t, (TM_CONV,))
    n = t // tm
    kw = CONF_CONV_WIDTH
    nv = ROWS_TAPS // SUBLANES

    def body(u_ref, up_ref, g3_ref, g3n_ref, b1_ref, w_ref, du1_ref, dw_ref, db1_ref,
             buf, gbuf, wb, wacc, bacc):
        i = pl.program_id(0)

        @pl.when(i == 0)
        def _():
            wacc[...] = jnp.zeros_like(wacc)
            bacc[...] = jnp.zeros_like(bacc)

        _fill_glu_buf(buf, u_ref, up_ref, b1_ref, i, tm, d)
        _bcast_rows(wb, w_ref, 0, kw)
        gbuf[pl.ds(0, tm), :] = g3_ref[...]
        gbuf[pl.ds(tm, HALO_A), :] = jnp.where(i < n - 1, g3n_ref[...], 0.0)
        for c0 in range(0, d, LANES):
            cols, gcols = pl.ds(c0, LANES), pl.ds(d + c0, LANES)
            for r0 in range(0, tm, ROWS_TAPS):
                rows = pl.ds(r0, ROWS_TAPS)
                u2 = [buf[pl.ds(HALO_A + r0 + SUBLANES * v, SUBLANES), cols] for v in range(nv)]
                acc = [None] * nv

                def use(k, gs):
                    wk = wb[k, :, cols]
                    part = None
                    for v in range(nv):
                        term = gs[v] * wk
                        acc[v] = term if acc[v] is None else acc[v] + term
                        prod = u2[v] * gs[v]
                        part = prod if part is None else part + prod
                    wacc[k, :, cols] += part

                _taps31(gbuf, r0, cols, _BWD_OFFS, use)
                du2 = jnp.concatenate(acc, axis=0)
                av = u_ref[rows, cols] + b1_ref[:, cols]
                sg = _sigmoid(u_ref[rows, gcols] + b1_ref[:, gcols])
                da = du2 * sg
                dg = du2 * av * (sg * (1.0 - sg))
                du1_ref[rows, cols] = da.astype(BF16)
                du1_ref[rows, gcols] = dg.astype(BF16)
                bacc[:, cols] += _sum8(da)
                bacc[:, gcols] += _sum8(dg)

        @pl.when(i == n - 1)
        def _():
            dw_ref[...] = jnp.sum(wacc[...], axis=1)
            db1_ref[...] = jnp.sum(bacc[...], axis=0, keepdims=True)

    return pl.pallas_call(
        body, grid=(n,),
        in_specs=[_row_spec(tm, d2), _prev_spec(tm, HALO_A, d2), _row_spec(tm, d),
                  _next_spec(tm, HALO_A, d, t), _vec_spec(1, d2), _vec_spec(kw, d)],
        out_specs=[_row_spec(tm, d2), _vec_spec(kw, d), _vec_spec(1, d2)],
        out_shape=[jax.ShapeDtypeStruct((t, d2), BF16), jax.ShapeDtypeStruct((kw, d), F32),
                   jax.ShapeDtypeStruct((1, d2), F32)],
        scratch_shapes=[pltpu.VMEM((tm + HALO_A, d), F32), pltpu.VMEM((tm + HALO_A, d), F32),
                        pltpu.VMEM((kw, SUBLANES, d), F32),
                        pltpu.VMEM((kw, SUBLANES, d), F32), pltpu.VMEM((SUBLANES, d2), F32)],
        compiler_params=_cparams(_ARB), name=name)(u1, u1, du3, du3, b1, dww)


def _pool_counts(i, tm, w):
    pos = (i * tm + lax.broadcasted_iota(jnp.int32, (tm, 1), 0) + 1).astype(F32)
    return jnp.minimum(pos, float(w))


def _b_pool_fwd(h, *, name):
    t, d = h.shape
    gd = d // len(POOL_WINDOWS)
    tm = _pick(t, (TM_CONV,))
    hb = HALO_POOL

    def body(h_ref, hp_ref, o_ref, buf):
        i = pl.program_id(0)
        buf[pl.ds(0, hb), :] = jnp.where(i > 0, hp_ref[...], 0.0)
        buf[pl.ds(hb, tm), :] = h_ref[...]
        for g, w in enumerate(POOL_WINDOWS):
            cols = pl.ds(g * gd, gd)
            cur = buf[pl.ds(hb, tm), cols]
            s = cur
            for j in range(1, w):
                s = s + buf[pl.ds(hb - j, tm), cols]
            o_ref[:, cols] = (s / _pool_counts(i, tm, w) - cur).astype(BF16)

    return pl.pallas_call(
        body, grid=(t // tm,),
        in_specs=[_row_spec(tm, d), _prev_spec(tm, hb, d)],
        out_specs=_row_spec(tm, d), out_shape=jax.ShapeDtypeStruct((t, d), BF16),
        scratch_shapes=[pltpu.VMEM((tm + hb, d), F32)],
        compiler_params=_cparams(_PAR), name=name)(h, h)


def _b_pool_bwd(dp, *, name):
    t, d = dp.shape
    gd = d // len(POOL_WINDOWS)
    tm = _pick(t, (TM_CONV,))
    hb = HALO_POOL
    n = t // tm

    def body(dp_ref, dpn_ref, o_ref, buf):
        i = pl.program_id(0)
        for g, w in enumerate(POOL_WINDOWS):
            cols = pl.ds(g * gd, gd)
            buf[pl.ds(0, tm), cols] = dp_ref[:, cols] / _pool_counts(i, tm, w)
            buf[pl.ds(tm, hb), cols] = jnp.where(i < n - 1, dpn_ref[:, cols] * (1.0 / w), 0.0)
            s = buf[pl.ds(0, tm), cols]
            for j in range(1, w):
                s = s + buf[pl.ds(j, tm), cols]
            o_ref[:, cols] = s - dp_ref[:, cols]

    return pl.pallas_call(
        body, grid=(n,),
        in_specs=[_row_spec(tm, d), _next_spec(tm, hb, d, t)],
        out_specs=_row_spec(tm, d), out_shape=jax.ShapeDtypeStruct((t, d), F32),
        scratch_shapes=[pltpu.VMEM((tm + hb, d), F32)],
        compiler_params=_cparams(_PAR), name=name)(dp, dp)


def _b_affine_fwd(mixed, gb, scale, *, name):
    t, d = mixed.shape
    tm = _pick(t, (TM_ROW, 256, 128))

    def body(m_ref, gb_ref, s_ref, o_ref):
        o_ref[...] = (m_ref[...] + gb_ref[...]) * s_ref[...]

    return pl.pallas_call(
        body, grid=(t // tm,), in_specs=[_row_spec(tm, d)] + [_vec_spec(1, d)] * 2,
        out_specs=_row_spec(tm, d), out_shape=jax.ShapeDtypeStruct((t, d), F32),
        compiler_params=_cparams(_PAR), name=name)(mixed, gb, scale)


def _b_affine_bwd(dy, mixed, gb, scale, *, name):
    t, d = mixed.shape
    tm = _pick(t, (TM_ROW, 256, 128))
    n = t // tm

    def body(dy_ref, m_ref, gb_ref, s_ref, dm_ref, ds_ref, dgb_ref, a1, a2):
        i = pl.program_id(0)

        @pl.when(i == 0)
        def _():
            a1[...] = jnp.zeros_like(a1)
            a2[...] = jnp.zeros_like(a2)

        dy_v = dy_ref[...]
        dm_ref[...] = (dy_v * s_ref[...]).astype(BF16)
        a1[...] += _sum8(dy_v * (m_ref[...] + gb_ref[...]))
        a2[...] += _sum8(dy_v)

        @pl.when(i == n - 1)
        def _():
            ds_ref[...] = jnp.sum(a1[...], axis=0, keepdims=True)
            dgb_ref[...] = jnp.sum(a2[...], axis=0, keepdims=True) * s_ref[...]

    return pl.pallas_call(
        body, grid=(n,), in_specs=[_row_spec(tm, d)] * 2 + [_vec_spec(1, d)] * 2,
        out_specs=[_row_spec(tm, d)] + [_vec_spec(1, d)] * 2,
        out_shape=[jax.ShapeDtypeStruct((t, d), BF16)] + [jax.ShapeDtypeStruct((1, d), F32)] * 2,
        scratch_shapes=[pltpu.VMEM((SUBLANES, d), F32)] * 2,
        compiler_params=_cparams(_ARB), name=name)(dy, mixed, gb, scale)


def _c_gate_fwd(bcx, wc, *, name):
    t, d3 = bcx.shape
    d = d3 // 3
    tm = _pick(t, (TM_CONV,))
    hb = HALO_3

    rb = ROWS_BLK
    cw = _pick(d, (COLS_BLK, LANES))

    def body(x_ref, xp_ref, w_ref, z_ref, buf, wb):
        i = pl.program_id(0)
        _bcast_rows(wb, w_ref, 0, 3)
        wb[3] = jnp.zeros(wb.shape[1:], F32)
        for c0 in range(0, d, cw):
            cols, ccols, vcols = pl.ds(c0, cw), pl.ds(d + c0, cw), pl.ds(2 * d + c0, cw)
            buf[pl.ds(0, hb), cols] = jnp.where(i > 0, xp_ref[:, ccols] * xp_ref[:, vcols], 0.0)
            for r0 in range(0, tm, rb):
                rows = pl.ds(r0, rb)
                buf[pl.ds(hb + r0, rb), cols] = x_ref[rows, ccols] * x_ref[rows, vcols]
            for r0 in range(0, tm, rb):
                rows = pl.ds(r0, rb)
                q, _ = _conv3_blk(buf, wb, hb - 2 + r0, rb, cols)
                z_ref[rows, cols] = (x_ref[rows, cols] * q).astype(BF16)

    return pl.pallas_call(
        body, grid=(t // tm,),
        in_specs=[_row_spec(tm, d3), _prev_spec(tm, hb, d3), _vec_spec(3, d)],
        out_specs=_row_spec(tm, d), out_shape=jax.ShapeDtypeStruct((t, d), BF16),
        scratch_shapes=[pltpu.VMEM((tm + hb, d), F32), pltpu.VMEM((4, rb, d), F32)],
        compiler_params=_cparams(_PAR), name=name)(bcx, bcx, wc)


def _c_gate_bwd(bcx, dz, wc, *, name):
    t, d3 = bcx.shape
    d = d3 // 3
    tm = _pick(t, (TM_CONV,))
    hb = HALO_3
    n = t // tm

    rb = ROWS_BLK
    cw = _pick(d, (COLS_BLK, LANES))

    def body(x_ref, xp_ref, xn_ref, dz_ref, dzn_ref, w_ref, o_ref, dw_ref, pbuf, qbuf, wb, wacc):
        i = pl.program_id(0)

        @pl.when(i == 0)
        def _():
            wacc[...] = jnp.zeros_like(wacc)

        _bcast_rows(wb, w_ref, 0, 3)
        wb[3] = jnp.zeros(wb.shape[1:], F32)
        for c0 in range(0, d, cw):
            cols, ccols, vcols = pl.ds(c0, cw), pl.ds(d + c0, cw), pl.ds(2 * d + c0, cw)
            pbuf[pl.ds(0, hb), cols] = jnp.where(i > 0, xp_ref[:, ccols] * xp_ref[:, vcols], 0.0)
            qbuf[pl.ds(tm, hb), cols] = jnp.where(i < n - 1, dzn_ref[:, cols] * xn_ref[:, cols], 0.0)
            for r0 in range(0, tm, rb):
                rows = pl.ds(r0, rb)
                pbuf[pl.ds(hb + r0, rb), cols] = x_ref[rows, ccols] * x_ref[rows, vcols]
            for r0 in range(0, tm, rb):
                rows = pl.ds(r0, rb)
                q, _ = _conv3_blk(pbuf, wb, hb - 2 + r0, rb, cols)
                dz_v = dz_ref[rows, cols]
                qbuf[rows, cols] = dz_v * x_ref[rows, cols]
                o_ref[rows, cols] = (dz_v * q).astype(BF16)
            for r0 in range(0, tm, rb):
                rows = pl.ds(r0, rb)
                pv = pbuf[pl.ds(hb + r0, rb), cols]
                window = qbuf[pl.ds(r0, rb + SUBLANES), cols]
                dp = None
                for k in range(3):
                    dqk = _shifted_rows(window, 2 - k, rb)
                    term = dqk * wb[k, :, cols]
                    dp = term if dp is None else dp + term
                    wacc[k, :, cols] += _sum8(pv * dqk)
                o_ref[rows, ccols] = (dp * x_ref[rows, vcols]).astype(BF16)
                o_ref[rows, vcols] = (dp * x_ref[rows, ccols]).astype(BF16)

        @pl.when(i == n - 1)
        def _():
            dw_ref[...] = jnp.sum(wacc[...], axis=1)

    return pl.pallas_call(
        body, grid=(n,),
        in_specs=[_row_spec(tm, d3), _prev_spec(tm, hb, d3), _next_spec(tm, hb, d3, t),
                  _row_spec(tm, d), _next_spec(tm, hb, d, t), _vec_spec(3, d)],
        out_specs=[_row_spec(tm, d3), _vec_spec(3, d)],
        out_shape=[jax.ShapeDtypeStruct((t, d3), BF16), jax.ShapeDtypeStruct((3, d), F32)],
        scratch_shapes=[pltpu.VMEM((tm + hb, d), F32), pltpu.VMEM((tm + hb, d), F32),
                        pltpu.VMEM((4, rb, d), F32), pltpu.VMEM((3, SUBLANES, d), F32)],
        compiler_params=_cparams(_ARB), name=name)(bcx, bcx, bcx, dz, dz, wc)


def _row(v):
    return v.reshape(1, -1)


def _kind_of(j):
    return "f" if j % 2 else "abc"[(j // 2) % N_MIXERS]


BIG = ("a_pw1_w", "a_pw2_w", "b_group_w", "c_in_w", "c_out_w", "f_up_w", "f_down_w")
COL_SHARDED = ("a_pw1_w", "c_in_w", "f_up_w")


def _local_step(x, target, mod, p):
    nsub = 2 * DEPTH
    norm_names = (("norm_pre_mix", "norm_post_mix"), ("norm_pre_ffn", "norm_post_ffn"))
    gpre = [_row(p[norm_names[s][0]][i]) for i in range(DEPTH) for s in (0, 1)]
    gpost = [_row(p[norm_names[s][1]][i]) for i in range(DEPTH) for s in (0, 1)]
    sh = [_row(mod[i, 3 * s + 0]) for i in range(DEPTH) for s in (0, 1)]
    sc = [_row(mod[i, 3 * s + 1]) for i in range(DEPTH) for s in (0, 1)]
    gt = [_row(mod[i, 3 * s + 2]) for i in range(DEPTH) for s in (0, 1)]

    def h_dtype(j):
        return F32 if _kind_of(j) == "b" else BF16

    xs, hs, ys, saved = [x], [], [], []

    hs.append(_fwd_first(x, gpre[0], sc[0], sh[0], h_dtype=h_dtype(0), name="fwd_first"))
    for j in range(nsub):
        i, kind = j // 2, _kind_of(j)
        slot = i // N_MIXERS
        h = hs[j]
        tag = f"{kind}{j}"
        if kind == "f":
            u = _mm(h, p["f_up_w"], mode="nn", layer=i, out_dtype=BF16, name=f"ffn_up_{tag}")
            a, vpre = _ffn_gate_fwd(u, p["f_dw_w"][i], _row(p["f_dw_b"][i]), name=f"ffn_gate_{tag}")
            y = _mm(a, p["f_down_w"][i], mode="nn", name=f"ffn_down_{tag}")
            saved.append((u, a, vpre))
        elif kind == "a":
            u1 = _mm(h, p["a_pw1_w"], mode="nn", layer=slot, name=f"a_pw1_{tag}")
            u5, u3 = _a_fwd(u1, _row(p["a_pw1_b"][slot]), p["a_dw_w"][slot], _row(p["a_dw_b"][slot]),
                            _row(p["a_ln_g"][slot]), _row(p["a_ln_b"][slot]), name=f"a_conv_{tag}")
            y = _mm(u5, p["a_pw2_w"][slot], mode="nn", bias=_row(p["a_pw2_b"][slot]), name=f"a_pw2_{tag}")
            saved.append((u1, u5, u3))
        elif kind == "b":
            pooled = _b_pool_fwd(h, name=f"b_pool_{tag}")
            mixed = _mm_group(pooled, p["b_group_w"][slot], mode="nn", name=f"b_mix_{tag}")
            y = _b_affine_fwd(mixed, _row(p["b_group_b"][slot]), _row(p["b_scale"][slot]), name=f"b_aff_{tag}")
            saved.append((pooled, mixed))
        else:
            bcx = _mm(h, p["c_in_w"], mode="nn", layer=slot, name=f"c_in_{tag}")
            z = _c_gate_fwd(bcx, p["c_conv_w"][slot], name=f"c_gate_{tag}")
            y = _mm(z, p["c_out_w"][slot], mode="nn", name=f"c_out_{tag}")
            saved.append((bcx, z))
        ys.append(y)
        if j + 1 < nsub:
            x_new, h_next = _fwd_mid(xs[j], y, gpost[j], gt[j], gpre[j + 1], sc[j + 1], sh[j + 1],
                                     h_dtype=h_dtype(j + 1), name=f"fwd_mid_{j}")
            xs.append(x_new)
            hs.append(h_next)

    n_of = {"a": len([i for i in range(DEPTH) if i % N_MIXERS == 0]),
            "b": len([i for i in range(DEPTH) if i % N_MIXERS == 1]),
            "c": len([i for i in range(DEPTH) if i % N_MIXERS == 2]), "f": DEPTH, "n": DEPTH}
    g = {k: [None] * n_of[k[0]] for k in p}
    dmod = [[None] * 6 for _ in range(DEPTH)]

    last = nsub - 1
    dx, dy, dgpost, dgt, sdy, loss = _last_fwd_bwd(xs[last], ys[last], target, gpost[last], gt[last],
                                                   name="loss_head")
    for j in range(last, -1, -1):
        i, kind = j // 2, _kind_of(j)
        slot = i // N_MIXERS
        sub = j % 2
        tag = f"{kind}{j}"
        g[norm_names[sub][1]][i] = dgpost
        dmod[i][3 * sub + 2] = dgt
        h = hs[j]
        if kind == "f":
            u, a, vpre = saved[j]
            da = _mm(dy, p["f_down_w"][i], mode="nt", name=f"ffn_dda_{tag}")
            g["f_down_w"][i] = _mm(a, dy, mode="tn", tk=TK_TOKENS, name=f"ffn_dwdown_{tag}")
            du, dw, db = _ffn_gate_bwd(u, vpre, da, p["f_dw_w"][i], name=f"ffn_gate_bwd_{tag}")
            g["f_dw_w"][i], g["f_dw_b"][i] = dw, db
            dh = _mm(du, p["f_up_w"], mode="nt", layer=i, name=f"ffn_ddh_{tag}")
            g["f_up_w"][i] = _mm(h, du, mode="tn", tk=TK_TOKENS, layer=i, name=f"ffn_dwup_{tag}")
        elif kind == "a":
            u1, u5, u3 = saved[j]
            g["a_pw2_b"][slot] = sdy
            du5 = _mm(dy, p["a_pw2_w"][slot], mode="nt", name=f"a_ddu5_{tag}")
            g["a_pw2_w"][slot] = _mm(u5, dy, mode="tn", tk=TK_TOKENS, name=f"a_dw2_{tag}")
            b1 = _row(p["a_pw1_b"][slot])
            du3, dlg, dlb, ddwb = _a_bwd_norm(u3, du5, _row(p["a_ln_g"][slot]), _row(p["a_ln_b"][slot]),
                                              name=f"a_bwd_norm_{tag}")
            g["a_ln_g"][slot], g["a_ln_b"][slot], g["a_dw_b"][slot] = dlg, dlb, ddwb
            du1, ddww, db1 = _a_bwd_conv(u1, du3, b1, p["a_dw_w"][slot], name=f"a_bwd_conv_{tag}")
            g["a_dw_w"][slot], g["a_pw1_b"][slot] = ddww, db1
            dh = _mm(du1, p["a_pw1_w"], mode="nt", layer=slot, name=f"a_ddh_{tag}")
            g["a_pw1_w"][slot] = _mm(h, du1, mode="tn", tk=TK_TOKENS, layer=slot, name=f"a_dw1_{tag}")
        elif kind == "b":
            pooled, mixed = saved[j]
            dmixed, dscale, dgb = _b_affine_bwd(dy, mixed, _row(p["b_group_b"][slot]), _row(p["b_scale"][slot]),
                                                name=f"b_aff_bwd_{tag}")
            g["b_scale"][slot], g["b_group_b"][slot] = dscale, dgb
            dpooled = _mm_group(dmixed, p["b_group_w"][slot], mode="nt", name=f"b_dpool_{tag}")
            g["b_group_w"][slot] = _mm_group(pooled, dmixed, mode="tn", tm=TK_TOKENS, name=f"b_dw_{tag}")
            dh = _b_pool_bwd(dpooled, name=f"b_pool_bwd_{tag}")
        else:
            bcx, z = saved[j]
            dz = _mm(dy, p["c_out_w"][slot], mode="nt", name=f"c_ddz_{tag}")
            g["c_out_w"][slot] = _mm(z, dy, mode="tn", tk=TK_TOKENS, name=f"c_dwout_{tag}")
            dbcx, dwc = _c_gate_bwd(bcx, dz, p["c_conv_w"][slot], name=f"c_gate_bwd_{tag}")
            g["c_conv_w"][slot] = dwc
            dh = _mm(dbcx, p["c_in_w"], mode="nt", layer=slot, name=f"c_ddh_{tag}")
            g["c_in_w"][slot] = _mm(h, dbcx, mode="tn", tk=TK_TOKENS, layer=slot, name=f"c_dwin_{tag}")
        if j > 0:
            pj = j - 1
            dy_dtype = F32 if _kind_of(pj) == "b" else BF16
            dx, dy, dsh, dsc, dgpre, dgpost, dgt, sdy = _bwd_mid(
                dx, dh, xs[j], gpre[j], sc[j], ys[pj], gpost[pj], gt[pj], dy_dtype=dy_dtype, name=f"bwd_mid_{j}")
        else:
            dx, dsh, dsc, dgpre = _bwd_first(dx, dh, xs[0], gpre[0], sc[0], name="bwd_first")
        dmod[i][3 * sub + 0] = dsh
        dmod[i][3 * sub + 1] = dsc
        g[norm_names[sub][0]][i] = dgpre

    small = {k: jnp.stack(v).reshape(p[k].shape) for k, v in g.items() if k not in BIG}
    big = {k: v for k, v in g.items() if k in BIG}
    dmod_arr = jnp.stack([jnp.concatenate(r, axis=0) for r in dmod])
    return loss, dx, dmod_arr, small, big


_MESH = pl.DeviceIdType.MESH
_ANY = pl.BlockSpec(memory_space=pl.ANY)
_VMEM = pl.BlockSpec(memory_space=pltpu.VMEM)


def _place():
    return lax.axis_index("x"), lax.axis_index("y"), lax.axis_index("c")


def _other_chips(x, y):
    return [(1 - x, y), (x, 1 - y), (1 - x, 1 - y)]


def _remote(src, dst, send_sem, recv_sem, to):
    return pltpu.make_async_remote_copy(src_ref=src, dst_ref=dst, send_sem=send_sem, recv_sem=recv_sem,
                                        device_id=to, device_id_type=_MESH)


def _all_gather8(blk, *, name):
    r, cdim = blk.shape

    def body(x_ref, out_ref, send_sems, recv_sems, local_sem):
        x, y, c = _place()
        me, sibling = (x, y, c), (x, y, 1 - c)
        chips = _other_chips(x, y)

        def slot(px, py, pc):
            return out_ref.at[4 * px + 2 * py + pc]

        def copy(k, block, to, src=None):
            return _remote(slot(*block) if src is None else src, slot(*block),
                           send_sems.at[k], recv_sems.at[k], to)

        mine = pltpu.make_async_copy(x_ref, slot(*me), local_sem)
        mine.start()
        first = [copy(0, me, sibling, src=x_ref)]
        first += [copy(1 + j, me, (*chip, c), src=x_ref) for j, chip in enumerate(chips)]
        for cp in first:
            cp.start()
        passed = [copy(4 + j, (*chip, c), sibling) for j, chip in enumerate(chips)]
        for j, chip in enumerate(chips):
            copy(1 + j, (*chip, c), me).wait_recv()
            passed[j].start()
        copy(0, sibling, me).wait_recv()
        for j, chip in enumerate(chips):
            copy(4 + j, (*chip, 1 - c), me).wait_recv()
        for cp in first + passed:
            cp.wait_send()
        mine.wait()

    return pl.pallas_call(
        body, out_shape=jax.ShapeDtypeStruct((NDEV, r, cdim), blk.dtype),
        in_specs=[_VMEM], out_specs=_VMEM,
        scratch_shapes=[pltpu.SemaphoreType.DMA((7,)), pltpu.SemaphoreType.DMA((7,)), pltpu.SemaphoreType.DMA],
        compiler_params=pltpu.CompilerParams(vmem_limit_bytes=VMEM_LIMIT), name=name)(blk)


def _gather_dst(kind):
    if kind == "col":
        return lambda s, h: (s, h)
    if kind == "row":
        return lambda s, h: (h, slice(None), s)
    return lambda s, h: (slice(None), s, h)


def _cast_into_gathered(src, kind, out_shape, shard, *, name):
    _, a, rh, cdim = src.shape
    tr = _pick(rh, (512, 256, 128, 64, 32, 16))
    if kind == "col":
        out_idx = lambda h, ai, r, s: (s[0], h, ai, r, 0)
    elif kind == "row":
        out_idx = lambda h, ai, r, s: (h, ai, s[0], r, 0)
    else:
        out_idx = lambda h, ai, r, s: (ai, s[0], h, r, 0)

    def body(s_ref, x_ref, o_ref):
        o_ref[...] = x_ref[...].astype(BF16)

    grid_spec = pltpu.PrefetchScalarGridSpec(
        num_scalar_prefetch=1, grid=(2, a, rh // tr),
        in_specs=[pl.BlockSpec((None, None, tr, cdim), lambda h, ai, r, s: (h, ai, r, 0))],
        out_specs=pl.BlockSpec((None, None, None, tr, cdim), out_idx))
    return pl.pallas_call(
        body, grid_spec=grid_spec, out_shape=jax.ShapeDtypeStruct(out_shape, BF16),
        compiler_params=_cparams(_PAR, _PAR, _PAR), name=name)(shard, src)


def _gather_weights(bufs, kinds, *, name):
    nt = len(bufs)

    def body(*refs):
        out_refs = refs[nt:2 * nt]
        send_sems, recv_sems = refs[2 * nt:]
        x, y, c = _place()
        sibling = (x, y, 1 - c)
        chips = _other_chips(x, y)
        s_me = 2 * x + y

        def at(k, s, h):
            return out_refs[k].at[_gather_dst(kinds[k])(s, h)]

        sends, passed = [], []
        for k in range(nt):
            for j, chip in enumerate(chips):
                cp = _remote(at(k, s_me, c), at(k, s_me, c), send_sems.at[k, j], recv_sems.at[k, j], (*chip, c))
                cp.start()
                sends.append(cp)
        for k in range(nt):
            for j, (px, py) in enumerate(chips):
                got = at(k, 2 * px + py, c)
                _remote(got, got, send_sems.at[k, j], recv_sems.at[k, j], (px, py, c)).wait_recv()
                cp = _remote(got, got, send_sems.at[k, 3 + j], recv_sems.at[k, 3 + j], sibling)
                cp.start()
                passed.append(cp)
        for k in range(nt):
            for j, (px, py) in enumerate(chips):
                got = at(k, 2 * px + py, 1 - c)
                _remote(got, got, send_sems.at[k, 3 + j], recv_sems.at[k, 3 + j], sibling).wait_recv()
        for cp in sends + passed:
            cp.wait_send()

    return pl.pallas_call(
        body, out_shape=[jax.ShapeDtypeStruct(b.shape, BF16) for b in bufs],
        in_specs=[_ANY] * nt, out_specs=[_ANY] * nt, input_output_aliases={k: k for k in range(nt)},
        scratch_shapes=[pltpu.SemaphoreType.DMA((nt, 6)), pltpu.SemaphoreType.DMA((nt, 6))],
        name=name)(*bufs)


def _pair_exchange(gs, layers_of, *, name):
    n, nt = len(gs), len(layers_of)

    def body(*refs):
        g_refs, out_refs = refs[:n], refs[n:n + nt]
        send_sems, recv_sems = refs[n + nt:]
        x, y, c = _place()
        sibling = (x, y, 1 - c)
        copies = []
        for t, ks in enumerate(layers_of):
            for l, k in enumerate(ks):
                cp = _remote(g_refs[k].at[:, 1 - c], out_refs[t].at[l], send_sems.at[k], recv_sems.at[k], sibling)
                cp.start()
                copies.append(cp)
        for cp in copies:
            cp.wait()

    out_shape = [jax.ShapeDtypeStruct((len(ks), NSHARD) + gs[ks[0]].shape[2:], F32) for ks in layers_of]
    return pl.pallas_call(
        body, out_shape=out_shape, in_specs=[_ANY] * n, out_specs=[_ANY] * nt,
        scratch_shapes=[pltpu.SemaphoreType.DMA((n,)), pltpu.SemaphoreType.DMA((n,))],
        name=name)(*gs)


def _pair_sum(g, r1, s_acc, layer, half, *, name):
    _, _, rh, cdim = g.shape
    tr = _pick(rh, (256, 128, 176, 64, 32, 16))

    def body(half_ref, g_ref, r_ref, s_in_ref, o_ref):
        o_ref[...] = (g_ref[...] + r_ref[...]).astype(BF16)

    grid_spec = pltpu.PrefetchScalarGridSpec(
        num_scalar_prefetch=1, grid=(NSHARD, rh // tr),
        in_specs=[pl.BlockSpec((None, None, tr, cdim), lambda s, r, hf: (s, hf[0], r, 0)),
                  pl.BlockSpec((None, None, tr, cdim), lambda s, r, hf: (layer, s, r, 0)),
                  _ANY],
        out_specs=pl.BlockSpec((None, None, tr, cdim), lambda s, r, hf: (layer, s, r, 0)))
    return pl.pallas_call(
        body, grid_spec=grid_spec, out_shape=jax.ShapeDtypeStruct(s_acc.shape, BF16),
        input_output_aliases={3: 0},
        compiler_params=_cparams(_PAR, _PAR), name=name)(half, g, r1, s_acc)


def _chip_exchange(ss, *, name):
    nt = len(ss)

    def body(*refs):
        s_refs, out_refs = refs[:nt], refs[nt:2 * nt]
        send_sems, recv_sems = refs[2 * nt:]
        x, y, c = _place()
        copies = []
        for t in range(nt):
            for j, (px, py) in enumerate(_other_chips(x, y)):
                cp = _remote(s_refs[t].at[:, 2 * px + py], out_refs[t].at[j],
                             send_sems.at[t, j], recv_sems.at[t, j], (px, py, c))
                cp.start()
                copies.append(cp)
        for cp in copies:
            cp.wait()

    out_shape = [jax.ShapeDtypeStruct((3, s.shape[0]) + s.shape[2:], BF16) for s in ss]
    return pl.pallas_call(
        body, out_shape=out_shape, in_specs=[_ANY] * nt, out_specs=[_ANY] * nt,
        scratch_shapes=[pltpu.SemaphoreType.DMA((nt, 3)), pltpu.SemaphoreType.DMA((nt, 3))],
        name=name)(*ss)


def _chip_sum(s_t, r3_t, place, *, name):
    nl, _, rh, cdim = s_t.shape
    tr = _pick(rh, (256, 128, 176, 64, 32, 16))

    def body(pz, s_ref, r_ref, o_ref):
        acc = s_ref[...].astype(F32) + r_ref[0].astype(F32)
        o_ref[...] = (acc + r_ref[1].astype(F32)) + r_ref[2].astype(F32)

    grid_spec = pltpu.PrefetchScalarGridSpec(
        num_scalar_prefetch=1, grid=(nl, rh // tr),
        in_specs=[pl.BlockSpec((None, None, tr, cdim), lambda l, r, pz: (l, pz[0], r, 0)),
                  pl.BlockSpec((3, None, tr, cdim), lambda l, r, pz: (0, l, r, 0))],
        out_specs=pl.BlockSpec((None, None, tr, cdim), lambda l, r, pz: (l, pz[1], r, 0)))
    return pl.pallas_call(
        body, grid_spec=grid_spec, out_shape=jax.ShapeDtypeStruct((nl, 2, rh, cdim), F32),
        compiler_params=_cparams(_PAR, _PAR), name=name)(place, s_t, r3_t)


def _join_halves(reds, *, name):
    nt = len(reds)

    def body(*refs):
        out_refs = refs[nt:2 * nt]
        send_sems, recv_sems = refs[2 * nt:]
        x, y, c = _place()
        sibling = (x, y, 1 - c)
        copies = []
        for t in range(nt):
            cp = _remote(out_refs[t].at[:, c], out_refs[t].at[:, c], send_sems.at[t], recv_sems.at[t], sibling)
            cp.start()
            copies.append(cp)
        for t, cp in enumerate(copies):
            cp.wait_send()
            got = out_refs[t].at[:, 1 - c]
            _remote(got, got, send_sems.at[t], recv_sems.at[t], sibling).wait_recv()

    return pl.pallas_call(
        body, out_shape=[jax.ShapeDtypeStruct(r.shape, F32) for r in reds],
        in_specs=[_ANY] * nt, out_specs=[_ANY] * nt, input_output_aliases={t: t for t in range(nt)},
        scratch_shapes=[pltpu.SemaphoreType.DMA((nt,)), pltpu.SemaphoreType.DMA((nt,))],
        name=name)(*reds)


def _sum_devices(g, *, name):
    _, r, cdim = g.shape
    tr = _pick(r, (512, 256, 128, 64, 32, 16, 8))

    def body(g_ref, o_ref):
        acc = g_ref[0]
        for e in range(1, NDEV):
            acc = acc + g_ref[e]
        o_ref[...] = acc

    return pl.pallas_call(
        body, grid=(r // tr,), in_specs=[pl.BlockSpec((NDEV, tr, cdim), lambda i: (0, i, 0))],
        out_specs=pl.BlockSpec((tr, cdim), lambda i: (i, 0)),
        out_shape=jax.ShapeDtypeStruct((r, cdim), F32),
        compiler_params=_cparams(_PAR), name=name)(g)


def _mod_fwd(c_all, mod_w, mod_b_cols, *, name):
    nl, d, n = mod_w.shape
    ne = c_all.shape[0]
    tn = _pick(n, (768, 512, 384, 256, 128))

    def body(c_ref, w_ref, b_ref, o_ref):
        cv = c_ref[...]
        act = (cv * _sigmoid(cv)).astype(BF16)
        o_ref[...] = jnp.dot(act, w_ref[...].astype(BF16), preferred_element_type=F32) + b_ref[...]

    return pl.pallas_call(
        body, grid=(nl, n // tn),
        in_specs=[pl.BlockSpec((ne, d), lambda i, j: (0, 0)),
                  pl.BlockSpec((None, d, tn), lambda i, j: (i, 0, j)),
                  pl.BlockSpec((None, 1, tn), lambda i, j: (i, 0, j))],
        out_specs=pl.BlockSpec((None, ne, tn), lambda i, j: (i, 0, j)),
        out_shape=jax.ShapeDtypeStruct((nl, ne, n), F32),
        compiler_params=_cparams(_PAR, _PAR), name=name)(c_all, mod_w, mod_b_cols)


def _adam_math(w, g, m, v):
    m2 = ADAM_B1 * m + (1.0 - ADAM_B1) * g
    v2 = ADAM_B2 * v + (1.0 - ADAM_B2) * (g * g)
    m_hat = m2 / (1.0 - ADAM_B1 ** ADAM_STEP)
    v_hat = v2 / (1.0 - ADAM_B2 ** ADAM_STEP)
    delta = -ADAM_LR * (m_hat / (jnp.sqrt(v_hat) + ADAM_EPS) + ADAM_WD * w)
    return delta, m2, v2


def _adamw(w, g, m, v, *, name):
    rows, cdim = w.shape
    tr = _pick(rows, tuple(t for t in (512, 256, 128, 64, 32, 16, 8) if t * cdim <= 256 * 1024))

    def body(w_ref, g_ref, m_ref, v_ref, d_ref, mo_ref, vo_ref):
        d_ref[...], mo_ref[...], vo_ref[...] = _adam_math(w_ref[...], g_ref[...], m_ref[...], v_ref[...])

    spec = pl.BlockSpec((tr, cdim), lambda i: (i, 0))
    return pl.pallas_call(
        body, grid=(rows // tr,), in_specs=[spec] * 4, out_specs=[spec] * 3,
        out_shape=[jax.ShapeDtypeStruct((rows, cdim), F32)] * 3,
        compiler_params=_cparams(_PAR), name=name)(w, g, m, v)


def _mod_w_update(c_t, dmod, w, m, v, *, name):
    nl, d, n = w.shape
    ne = c_t.shape[1]
    tr = _pick(d, (128, 64, 32, 16, 8))

    def body(c_ref, dm_ref, w_ref, m_ref, v_ref, g_ref, d_ref, mo_ref, vo_ref):
        cv = c_ref[...]
        act = cv * _sigmoid(cv)
        dm = dm_ref[...]
        g = act[:, 0:1] * dm[0:1, :]
        for e in range(1, ne):
            g = g + act[:, e:e + 1] * dm[e:e + 1, :]
        g_ref[...] = g
        d_ref[...], mo_ref[...], vo_ref[...] = _adam_math(w_ref[...], g, m_ref[...], v_ref[...])

    big = pl.BlockSpec((None, tr, n), lambda i, r: (i, r, 0))
    return pl.pallas_call(
        body, grid=(nl, d // tr),
        in_specs=[pl.BlockSpec((tr, ne), lambda i, r: (r, 0)),
                  pl.BlockSpec((None, ne, n), lambda i, r: (i, 0, 0)), big, big, big],
        out_specs=[big] * 4, out_shape=[jax.ShapeDtypeStruct((nl, d, n), F32)] * 4,
        compiler_params=_cparams(_PAR, _PAR), name=name)(c_t, dmod, w, m, v)


PACK_ROWS = 256


def _pack(arrs):
    flat = jnp.concatenate([a.reshape(-1) for a in arrs])
    tile = PACK_ROWS * LANES
    pad = (-flat.shape[0]) % tile
    return jnp.pad(flat, (0, pad)).reshape(-1, LANES)


def _unpack(packed, shapes, lead=()):
    flat = packed.reshape(lead + (-1,))
    out, off = [], 0
    for shp in shapes:
        size = 1
        for s in shp:
            size *= s
        out.append(flat[..., off:off + size].reshape(lead + tuple(shp)))
        off += size
    return out


SMALL_SHARD_AXIS = {"a_pw1_b": 1, "a_dw_w": 2, "a_dw_b": 1, "a_ln_g": 1, "a_ln_b": 1, "a_pw2_b": 1,
                    "c_conv_w": 2, "f_dw_w": 2}
SMALL_REPLICATED = ("norm_pre_mix", "norm_post_mix", "norm_pre_ffn", "norm_post_ffn",
                    "b_group_b", "b_scale", "f_dw_b")
WEIGHT_ORDER = ("mod_w", "mod_b", "norm_pre_mix", "norm_post_mix", "norm_pre_ffn", "norm_post_ffn",
                "a_pw1_w", "a_pw1_b", "a_dw_w", "a_dw_b", "a_ln_g", "a_ln_b", "a_pw2_w", "a_pw2_b",
                "b_group_w", "b_group_b", "b_scale", "c_in_w", "c_conv_w", "c_out_w",
                "f_up_w", "f_dw_w", "f_dw_b", "f_down_w")


def _as_layers_rows_cols(name, w):
    if name == "b_group_w":
        return w.reshape(w.shape[1], w.shape[2], w.shape[3])
    return w


def _step(x, c, loss_target, w, m, v):
    xi, yi, ci = _place()
    shard = 2 * xi + yi
    example = 4 * xi + 2 * yi + ci
    d = x.shape[-1]

    small_names = tuple(SMALL_SHARD_AXIS)
    gathered0 = _all_gather8(_pack([c] + [w[k] for k in small_names]), name="gather_small")
    parts = _unpack(gathered0, [c.shape] + [w[k].shape for k in small_names], lead=(NDEV,))
    c_all = parts[0].reshape(NDEV, d)
    p = {}
    for k, part in zip(small_names, parts[1:]):
        p[k] = jnp.concatenate([part[2 * s] for s in range(NSHARD)], axis=SMALL_SHARD_AXIS[k])
    for k in SMALL_REPLICATED:
        p[k] = w[k]

    ncol = w["mod_w"].shape[2]
    mod_b_cols = lax.dynamic_slice_in_dim(w["mod_b"], shard * ncol, ncol, axis=1).reshape(DEPTH, 1, ncol)
    mod_part = _mod_fwd(c_all, w["mod_w"], mod_b_cols, name="mod_fwd")
    gathered1 = _all_gather8(mod_part.reshape(DEPTH * NDEV, ncol), name="gather_mod")
    mod_all = gathered1.reshape(NSHARD, 2, DEPTH, NDEV, ncol)[:, 0]
    mod_mine = lax.dynamic_index_in_dim(mod_all, example, axis=2, keepdims=False)
    mod = jnp.transpose(mod_mine, (1, 0, 2)).reshape(DEPTH, 6, d)

    shard_arr = shard.reshape(1).astype(jnp.int32)
    bufs, kinds = [], []
    for k in BIG:
        wk = _as_layers_rows_cols(k, w[k])
        nl, r, cdim = wk.shape
        if nl >= 2:
            a, rh = nl // 2, r
        else:
            a, rh = 1, r // 2
        if k in COL_SHARDED:
            kind, out_shape = "col", (NSHARD, 2, a, rh, cdim)
        elif nl >= 2:
            kind, out_shape = "row", (2, a, NSHARD, rh, cdim)
        else:
            kind, out_shape = "row1", (1, NSHARD, 2, rh, cdim)
        kinds.append(kind)
        bufs.append(_cast_into_gathered(wk.reshape(2, a, rh, cdim), kind, out_shape, shard_arr, name=f"cast_{k}"))
    full = _gather_weights(bufs, kinds, name="gather_weights")
    for k, f in zip(BIG, full):
        nl, r, cdim = _as_layers_rows_cols(k, w[k]).shape
        if k in COL_SHARDED:
            p[k] = f.reshape(NSHARD, nl, r, cdim)
        elif k == "b_group_w":
            p[k] = f.reshape(1, nl, NSHARD * r, cdim)
        else:
            p[k] = f.reshape(nl, NSHARD * r, cdim)

    loss, grad_x, dmod, small, big = _local_step(x[0], loss_target[0], mod, p)

    gs, layers_of = [], []
    for k in BIG:
        ks = []
        for g in big[k]:
            if k == "b_group_w":
                ng, rr, cc = g.shape
                g = jnp.transpose(g.reshape(ng, NSHARD, rr // NSHARD, cc), (1, 0, 2, 3)).reshape(NSHARD, -1, cc)
            elif k not in COL_SHARDED:
                g = g.reshape(NSHARD, g.shape[0] // NSHARD, g.shape[1])
            ks.append(len(gs))
            gs.append(g.reshape(NSHARD, 2, g.shape[1] // 2, g.shape[2]))
        layers_of.append(ks)
    half = ci.reshape(1).astype(jnp.int32)
    place = jnp.stack([shard, ci]).astype(jnp.int32)
    r1 = _pair_exchange(gs, layers_of, name="grad_pair_exchange")
    ss = []
    for k, ks, r1_t in zip(BIG, layers_of, r1):
        s_t = lax.empty(r1_t.shape, BF16)
        for l, i in enumerate(ks):
            s_t = _pair_sum(gs[i], r1_t, s_t, l, half, name=f"grad_pair_sum_{k}_{l}")
        ss.append(s_t)
    r3 = _chip_exchange(ss, name="grad_chip_exchange")
    reds = [_chip_sum(s_t, r3_t, place, name=f"grad_chip_sum_{k}") for k, s_t, r3_t in zip(BIG, ss, r3)]
    joined = _join_halves(reds, name="grad_join_halves")
    grads = {k: j.reshape(w[k].shape) for k, j in zip(BIG, joined)}

    rep_names = SMALL_REPLICATED
    small_list = [small[k] for k in rep_names] + [small[k] for k in small_names] + [dmod]
    gathered2 = _all_gather8(_pack(small_list), name="gather_small_grads")
    summed = _sum_devices(gathered2, name="sum_small_grads")
    shapes = [s.shape for s in small_list]
    sums = _unpack(summed, shapes)
    for k, s in zip(rep_names, sums[:len(rep_names)]):
        grads[k] = s
    for k, s in zip(small_names, sums[len(rep_names):-1]):
        ax = SMALL_SHARD_AXIS[k]
        grads[k] = lax.dynamic_slice_in_dim(s, shard * w[k].shape[ax], w[k].shape[ax], axis=ax)
    grads["mod_b"] = sums[-1].reshape(w["mod_b"].shape)
    dmod_all = _unpack(gathered2, shapes, lead=(NDEV,))[-1].reshape(NDEV, DEPTH, NSHARD, ncol)
    dmod_cols = jnp.transpose(lax.dynamic_index_in_dim(dmod_all, shard, axis=2, keepdims=False), (1, 0, 2))

    delta, new_m, new_v = {}, {}, {}
    grads["mod_w"], delta["mod_w"], new_m["mod_w"], new_v["mod_w"] = _mod_w_update(
        c_all.T, dmod_cols, w["mod_w"], m["mod_w"], v["mod_w"], name="mod_w_update")
    for k in BIG:
        cdim = w[k].shape[-1]
        outs = _adamw(*[t.reshape(-1, cdim) for t in (w[k], grads[k], m[k], v[k])], name=f"adamw_{k}")
        delta[k], new_m[k], new_v[k] = [o.reshape(w[k].shape) for o in outs]
    rest = ("mod_b",) + rep_names + small_names
    packs = [_pack([t[k] for k in rest]) for t in (w, grads, m, v)]
    outs = _adamw(*packs, name="adamw_small")
    rest_shapes = [w[k].shape for k in rest]
    for dst, o in zip((delta, new_m, new_v), outs):
        for k, t in zip(rest, _unpack(o, rest_shapes)):
            dst[k] = t

    loss_all = lax.psum(loss[0, 0], ("x", "y", "c"))
    return (loss_all, grad_x[None], *[grads[k] for k in WEIGHT_ORDER], *[delta[k] for k in WEIGHT_ORDER],
            *[new_m[k] for k in WEIGHT_ORDER], *[new_v[k] for k in WEIGHT_ORDER])


def kernel(x, c, mod_w, mod_b, norm_pre_mix, norm_post_mix, norm_pre_ffn, norm_post_ffn, a_pw1_w, a_pw1_b, a_dw_w, a_dw_b, a_ln_g, a_ln_b, a_pw2_w, a_pw2_b, b_group_w, b_group_b, b_scale, c_in_w, c_conv_w, c_out_w, f_up_w, f_dw_w, f_dw_b, f_down_w, loss_target, m_mod_w, m_mod_b, m_norm_pre_mix, m_norm_post_mix, m_norm_pre_ffn, m_norm_post_ffn, m_a_pw1_w, m_a_pw1_b, m_a_dw_w, m_a_dw_b, m_a_ln_g, m_a_ln_b, m_a_pw2_w, m_a_pw2_b, m_b_group_w, m_b_group_b, m_b_scale, m_c_in_w, m_c_conv_w, m_c_out_w, m_f_up_w, m_f_dw_w, m_f_dw_b, m_f_down_w, v_mod_w, v_mod_b, v_norm_pre_mix, v_norm_post_mix, v_norm_pre_ffn, v_norm_post_ffn, v_a_pw1_w, v_a_pw1_b, v_a_dw_w, v_a_dw_b, v_a_ln_g, v_a_ln_b, v_a_pw2_w, v_a_pw2_b, v_b_group_w, v_b_group_b, v_b_scale, v_c_in_w, v_c_conv_w, v_c_out_w, v_f_up_w, v_f_dw_w, v_f_dw_b, v_f_down_w):
    given = dict(locals())
    w = {k: given[k] for k in WEIGHT_ORDER}
    m = {k: given["m_" + k] for k in WEIGHT_ORDER}
    v = {k: given["v_" + k] for k in WEIGHT_ORDER}
    return _step(x, c, loss_target, w, m, v)
```

```python
import functools

import jax
import jax.numpy as jnp
from jax import lax
from jax.experimental import pallas as pl
from jax.experimental.pallas import tpu as pltpu

F32 = jnp.float32
BF16 = jnp.bfloat16

DEPTH = 4
N_MIXERS = 3
CONF_CONV_WIDTH = 31
POOL_WINDOWS = (2, 4, 8, 16)
RMS_EPS = 1e-6
LN_EPS = 1e-5
ADAM_LR = 0.001
ADAM_B1 = 0.9
ADAM_B2 = 0.999
ADAM_EPS = 1e-08
ADAM_WD = 0.01
ADAM_STEP = 10

TM_ROW = 512
TM_CONV = 128
TK_TOKENS = 2048
TM_MM = 1024
TN_MM = 1024
SUBLANES = 8
LANES = 128
NSHARD = 4
NDEV = 8
HALO_A = 32
HALO_POOL = 16
HALO_3 = 8
HALO_BF16 = 16
VMEM_LIMIT = 56 * 1024 * 1024

_PAR = "parallel"
_ARB = "arbitrary"


def _cparams(*sem):
    return pltpu.CompilerParams(dimension_semantics=sem, vmem_limit_bytes=VMEM_LIMIT)


def _pick(n, prefs):
    for p in prefs:
        if p <= n and n % p == 0:
            return p
    return n


def _row_spec(tm, width):
    return pl.BlockSpec((tm, width), lambda i: (i, 0))


def _vec_spec(rows, width):
    return pl.BlockSpec((rows, width), lambda i: (0, 0))


def _prev_spec(tm, hb, width):
    return pl.BlockSpec((hb, width), lambda i: (jnp.maximum(i * (tm // hb) - 1, 0), 0))


def _next_spec(tm, hb, width, total):
    last = total // hb - 1
    return pl.BlockSpec((hb, width), lambda i: (jnp.minimum((i + 1) * (tm // hb), last), 0))


def _sum8(v):
    r, c = v.shape
    return jnp.sum(v.reshape(r // SUBLANES, SUBLANES, c), axis=0)


def _rms(x):
    r = lax.rsqrt(jnp.mean(x * x, axis=-1, keepdims=True) + RMS_EPS)
    return x * r, r


def _rms_bwd(dy, xn, r):
    return r * (dy - xn * jnp.mean(dy * xn, axis=-1, keepdims=True))


def _sigmoid(x):
    return 1.0 / (1.0 + jnp.exp(-x))


_DIMS = {"nn": ((1,), (0,)), "nt": ((1,), (1,)), "tn": ((0,), (0,))}


def _mm(a, b, *, mode, name, out_dtype=F32, bias=None, tm=TM_MM, tn=TN_MM, tk=None, layer=None):
    sharded = layer is not None
    if mode == "nn":
        m, k = a.shape
        n = NSHARD * b.shape[3] if sharded else b.shape[1]
    elif mode == "nt":
        m, k = a.shape
        n = b.shape[2] if sharded else b.shape[0]
    else:
        (k, m), (_, n) = a.shape, b.shape
    ns = n // NSHARD
    ks = k // NSHARD
    tm = _pick(m, (tm, 1408, 512, 256, 128))
    if sharded and mode != "nt":
        tn = _pick(ns, (1408, 768, 512, 256, 128))
    else:
        tn = _pick(n, (tn, 1408, 512, 256, 128))
    if sharded and mode == "nt":
        tk = _pick(ks, (1408, 768, 512, 256, 128))
    else:
        tk = _pick(k, (tk or k, 2816, 1024, 512, 256, 128))
    nk = k // tk
    per_n = ns // tn if sharded and mode != "nt" else 1
    per_k = ks // tk if sharded and mode == "nt" else 1
    dims = (_DIMS[mode], ((), ()))

    def split(idx, per):
        return (idx, 0) if per == 1 else (idx // per, idx % per)

    def body(*refs):
        a_ref, b_ref = refs[0], refs[1]
        bias_ref = refs[2] if bias is not None else None
        o_ref = refs[3] if bias is not None else refs[2]
        part = lax.dot_general(a_ref[...].astype(BF16), b_ref[...].astype(BF16), dims,
                               preferred_element_type=F32)

        def finish(r):
            if bias_ref is not None:
                r = r + bias_ref[...]
            o_ref[...] = r.astype(out_dtype)

        if nk == 1:
            finish(part)
        else:
            acc_ref = refs[-1]
            kk = pl.program_id(2)

            @pl.when(kk == 0)
            def _():
                acc_ref[...] = part

            @pl.when(kk > 0)
            def _():
                acc_ref[...] += part

            @pl.when(kk == nk - 1)
            def _():
                finish(acc_ref[...])

    out_spec = pl.BlockSpec((tm, tn), lambda i, j, kk: (i, j))
    out_shape = jax.ShapeDtypeStruct((m, n), out_dtype)
    if mode == "nn":
        a_spec = pl.BlockSpec((tm, tk), lambda i, j, kk: (i, kk))
        if sharded:
            b_spec = pl.BlockSpec((None, None, tk, tn),
                                  lambda i, j, kk: (split(j, per_n)[0], layer, kk, split(j, per_n)[1]))
        else:
            b_spec = pl.BlockSpec((tk, tn), lambda i, j, kk: (kk, j))
    elif mode == "nt":
        a_spec = pl.BlockSpec((tm, tk), lambda i, j, kk: (i, kk))
        if sharded:
            b_spec = pl.BlockSpec((None, None, tn, tk),
                                  lambda i, j, kk: (split(kk, per_k)[0], layer, j, split(kk, per_k)[1]))
        else:
            b_spec = pl.BlockSpec((tn, tk), lambda i, j, kk: (j, kk))
    else:
        a_spec = pl.BlockSpec((tk, tm), lambda i, j, kk: (kk, i))
        b_spec = pl.BlockSpec((tk, tn), lambda i, j, kk: (kk, j))
        if sharded:
            out_spec = pl.BlockSpec((None, tm, tn), lambda i, j, kk: (split(j, per_n)[0], i, split(j, per_n)[1]))
            out_shape = jax.ShapeDtypeStruct((NSHARD, m, ns), out_dtype)
    in_specs = [a_spec, b_spec]
    args = [a, b]
    if bias is not None:
        in_specs.append(pl.BlockSpec((1, tn), lambda i, j, kk: (0, j)))
        args.append(bias)
    return pl.pallas_call(
        body, grid=(m // tm, n // tn, nk), in_specs=in_specs, out_specs=out_spec, out_shape=out_shape,
        scratch_shapes=[pltpu.VMEM((tm, tn), F32)] if nk > 1 else [],
        compiler_params=_cparams(_PAR, _PAR, _ARB), name=name)(*args)


def _mm_group(a, b, *, mode, name, out_dtype=F32, tm=2048):
    t = a.shape[0]
    tm = _pick(t, (tm, 1024, 512, 256, 128))
    nt_ = t // tm
    g = len(POOL_WINDOWS)
    gd = a.shape[1] // g
    dims = (_DIMS[mode], ((), ()))

    if mode == "tn":
        def body(a_ref, b_ref, o_ref, acc_ref):
            kk = pl.program_id(1)
            part = lax.dot_general(a_ref[...].astype(BF16), b_ref[...].astype(BF16), dims,
                                   preferred_element_type=F32)

            @pl.when(kk == 0)
            def _():
                acc_ref[...] = part

            @pl.when(kk > 0)
            def _():
                acc_ref[...] += part

            @pl.when(kk == nt_ - 1)
            def _():
                o_ref[...] = acc_ref[...]

        return pl.pallas_call(
            body, grid=(g, nt_),
            in_specs=[pl.BlockSpec((tm, gd), lambda gi, kk: (kk, gi)),
                      pl.BlockSpec((tm, gd), lambda gi, kk: (kk, gi))],
            out_specs=pl.BlockSpec((None, gd, gd), lambda gi, kk: (gi, 0, 0)),
            out_shape=jax.ShapeDtypeStruct((g, gd, gd), F32),
            scratch_shapes=[pltpu.VMEM((gd, gd), F32)],
            compiler_params=_cparams(_PAR, _ARB), name=name)(a, b)

    def body(a_ref, b_ref, o_ref):
        o_ref[...] = lax.dot_general(a_ref[...].astype(BF16), b_ref[...].astype(BF16), dims,
                                     preferred_element_type=F32).astype(out_dtype)

    return pl.pallas_call(
        body, grid=(nt_, g),
        in_specs=[pl.BlockSpec((tm, gd), lambda i, gi: (i, gi)),
                  pl.BlockSpec((None, gd, gd), lambda i, gi: (gi, 0, 0))],
        out_specs=pl.BlockSpec((tm, gd), lambda i, gi: (i, gi)),
        out_shape=jax.ShapeDtypeStruct((t, g * gd), out_dtype),
        compiler_params=_cparams(_PAR, _PAR), name=name)(a, b)


def _pre(x, gpre, sc, sh):
    xn, r = _rms(x)
    return (xn * gpre) * (1.0 + sc) + sh, xn, r


def _fwd_first(x, gpre, sc, sh, *, h_dtype, name):
    t, d = x.shape
    tm = _pick(t, (TM_ROW, 256, 128))

    def body(x_ref, gpre_ref, sc_ref, sh_ref, h_ref):
        h, _, _ = _pre(x_ref[...], gpre_ref[...], sc_ref[...], sh_ref[...])
        h_ref[...] = h.astype(h_dtype)

    return pl.pallas_call(
        body, grid=(t // tm,),
        in_specs=[_row_spec(tm, d)] + [_vec_spec(1, d)] * 3,
        out_specs=_row_spec(tm, d), out_shape=jax.ShapeDtypeStruct((t, d), h_dtype),
        compiler_params=_cparams(_PAR), name=name)(x, gpre, sc, sh)


def _fwd_mid(x, y, gpost, gt, gpre, sc, sh, *, h_dtype, name):
    t, d = x.shape
    tm = _pick(t, (TM_ROW, 256, 128))

    def body(x_ref, y_ref, gpost_ref, gt_ref, gpre_ref, sc_ref, sh_ref, xn_ref, h_ref):
        yn, _ = _rms(y_ref[...].astype(F32))
        x_new = x_ref[...] + gt_ref[...] * (yn * gpost_ref[...])
        xn_ref[...] = x_new
        h, _, _ = _pre(x_new, gpre_ref[...], sc_ref[...], sh_ref[...])
        h_ref[...] = h.astype(h_dtype)

    return pl.pallas_call(
        body, grid=(t // tm,),
        in_specs=[_row_spec(tm, d)] * 2 + [_vec_spec(1, d)] * 5,
        out_specs=[_row_spec(tm, d)] * 2,
        out_shape=[jax.ShapeDtypeStruct((t, d), F32), jax.ShapeDtypeStruct((t, d), h_dtype)],
        compiler_params=_cparams(_PAR), name=name)(x, y, gpost, gt, gpre, sc, sh)


def _post_bwd(dx, y, gpost, gt):
    yn, r2 = _rms(y)
    dyn = dx * (gt * gpost)
    dy = _rms_bwd(dyn, yn, r2)
    return dy, dx * yn


def _last_fwd_bwd(x, y, target, gpost, gt, *, name):
    t, d = x.shape
    tm = _pick(t, (TM_ROW, 256, 128))
    n = t // tm

    def body(x_ref, y_ref, tg_ref, gpost_ref, gt_ref, dx_ref, dy_ref, dgpost_ref, dgt_ref, sdy_ref,
             loss_ref, qa, sa, la):
        i = pl.program_id(0)

        @pl.when(i == 0)
        def _():
            qa[...] = jnp.zeros_like(qa)
            sa[...] = jnp.zeros_like(sa)
            la[...] = jnp.zeros_like(la)

        yv = y_ref[...].astype(F32)
        yn, r2 = _rms(yv)
        gt_v, gpost_v = gt_ref[...], gpost_ref[...]
        err = x_ref[...] + gt_v * (yn * gpost_v) - tg_ref[...]
        la[...] += _sum8(err * err)
        dx = err * (1.0 / d)
        dx_ref[...] = dx
        dy = _rms_bwd(dx * (gt_v * gpost_v), yn, r2)
        dy_ref[...] = dy.astype(dy_ref.dtype)
        qa[...] += _sum8(dx * yn)
        sa[...] += _sum8(dy)

        @pl.when(i == n - 1)
        def _():
            q = jnp.sum(qa[...], axis=0, keepdims=True)
            dgpost_ref[...] = gt_v * q
            dgt_ref[...] = gpost_v * q
            sdy_ref[...] = jnp.sum(sa[...], axis=0, keepdims=True)
            tot = jnp.sum(jnp.sum(la[...], axis=0, keepdims=True), axis=1, keepdims=True)
            loss_ref[...] = tot * (0.5 / d)

    return pl.pallas_call(
        body, grid=(n,),
        in_specs=[_row_spec(tm, d)] * 3 + [_vec_spec(1, d)] * 2,
        out_specs=[_row_spec(tm, d)] * 2 + [_vec_spec(1, d)] * 3 + [_vec_spec(1, 1)],
        out_shape=[jax.ShapeDtypeStruct((t, d), F32), jax.ShapeDtypeStruct((t, d), BF16)]
        + [jax.ShapeDtypeStruct((1, d), F32)] * 3 + [jax.ShapeDtypeStruct((1, 1), F32)],
        scratch_shapes=[pltpu.VMEM((SUBLANES, d), F32)] * 3,
        compiler_params=_cparams(_ARB), name=name)(x, y, target, gpost, gt)


def _bwd_mid(dx_new, dh, x_in, gpre, sc, y_prev, gpost_p, gt_p, *, dy_dtype, name):
    t, d = x_in.shape
    tm = _pick(t, (TM_ROW, 256, 128))
    n = t // tm

    def body(dxn_ref, dh_ref, x_ref, gpre_ref, sc_ref, y_ref, gpost_ref, gt_ref,
             dx_ref, dy_ref, dsh_ref, dsc_ref, dgpre_ref, dgpost_ref, dgt_ref, sdy_ref, a1, a2, aq, asd):
        i = pl.program_id(0)

        @pl.when(i == 0)
        def _():
            for a in (a1, a2, aq, asd):
                a[...] = jnp.zeros_like(a)

        dh_v = dh_ref[...]
        xn, r = _rms(x_ref[...])
        dx = dxn_ref[...] + _rms_bwd(dh_v * ((1.0 + sc_ref[...]) * gpre_ref[...]), xn, r)
        dx_ref[...] = dx
        a1[...] += _sum8(dh_v)
        a2[...] += _sum8(dh_v * xn)
        dy, dxyn = _post_bwd(dx, y_ref[...].astype(F32), gpost_ref[...], gt_ref[...])
        dy_ref[...] = dy.astype(dy_dtype)
        aq[...] += _sum8(dxyn)
        asd[...] += _sum8(dy)

        @pl.when(i == n - 1)
        def _():
            s2 = jnp.sum(a2[...], axis=0, keepdims=True)
            q = jnp.sum(aq[...], axis=0, keepdims=True)
            dsh_ref[...] = jnp.sum(a1[...], axis=0, keepdims=True)
            dsc_ref[...] = gpre_ref[...] * s2
            dgpre_ref[...] = (1.0 + sc_ref[...]) * s2
            dgpost_ref[...] = gt_ref[...] * q
            dgt_ref[...] = gpost_ref[...] * q
            sdy_ref[...] = jnp.sum(asd[...], axis=0, keepdims=True)

    return pl.pallas_call(
        body, grid=(n,),
        in_specs=[_row_spec(tm, d)] * 3 + [_vec_spec(1, d)] * 2 + [_row_spec(tm, d)] + [_vec_spec(1, d)] * 2,
        out_specs=[_row_spec(tm, d)] * 2 + [_vec_spec(1, d)] * 6,
        out_shape=[jax.ShapeDtypeStruct((t, d), F32), jax.ShapeDtypeStruct((t, d), dy_dtype)]
        + [jax.ShapeDtypeStruct((1, d), F32)] * 6,
        scratch_shapes=[pltpu.VMEM((SUBLANES, d), F32)] * 4,
        compiler_params=_cparams(_ARB), name=name)(dx_new, dh, x_in, gpre, sc, y_prev, gpost_p, gt_p)


def _bwd_first(dx_new, dh, x_in, gpre, sc, *, name):
    t, d = x_in.shape
    tm = _pick(t, (TM_ROW, 256, 128))
    n = t // tm

    def body(dxn_ref, dh_ref, x_ref, gpre_ref, sc_ref, dx_ref, dsh_ref, dsc_ref, dgpre_ref, a1, a2):
        i = pl.program_id(0)

        @pl.when(i == 0)
        def _():
            a1[...] = jnp.zeros_like(a1)
            a2[...] = jnp.zeros_like(a2)

        dh_v = dh_ref[...]
        xn, r = _rms(x_ref[...])
        dx_ref[...] = dxn_ref[...] + _rms_bwd(dh_v * ((1.0 + sc_ref[...]) * gpre_ref[...]), xn, r)
        a1[...] += _sum8(dh_v)
        a2[...] += _sum8(dh_v * xn)

        @pl.when(i == n - 1)
        def _():
            s2 = jnp.sum(a2[...], axis=0, keepdims=True)
            dsh_ref[...] = jnp.sum(a1[...], axis=0, keepdims=True)
            dsc_ref[...] = gpre_ref[...] * s2
            dgpre_ref[...] = (1.0 + sc_ref[...]) * s2

    return pl.pallas_call(
        body, grid=(n,),
        in_specs=[_row_spec(tm, d)] * 3 + [_vec_spec(1, d)] * 2,
        out_specs=[_row_spec(tm, d)] + [_vec_spec(1, d)] * 3,
        out_shape=[jax.ShapeDtypeStruct((t, d), F32)] + [jax.ShapeDtypeStruct((1, d), F32)] * 3,
        scratch_shapes=[pltpu.VMEM((SUBLANES, d), F32)] * 2,
        compiler_params=_cparams(_ARB), name=name)(dx_new, dh, x_in, gpre, sc)


def _conv3_rows(buf, w_ref, rows, first):
    out = buf[pl.ds(first, rows), :] * w_ref[pl.ds(0, 1), :]
    for k in (1, 2):
        out = out + buf[pl.ds(first + k, rows), :] * w_ref[pl.ds(k, 1), :]
    return out


ROWS_BLK = 16
COLS_BLK = 256


def _bcast_rows(dst, src_ref, first, nrows):
    for k in range(nrows):
        dst[first + k] = jnp.broadcast_to(src_ref[pl.ds(k, 1), :], dst.shape[1:])


def _shifted_rows(x, off, rows):
    if off % SUBLANES == 0:
        return x[off:off + rows]
    return pltpu.roll(x, x.shape[0] - off, axis=0)[:rows]


def _conv3_blk(buf, wb, first, rows, cols):
    base = first - first % SUBLANES
    window = buf[pl.ds(base, rows + SUBLANES), cols]
    xs = [_shifted_rows(window, first - base + k, rows) for k in range(3)]
    out = xs[0] * wb[0, pl.ds(0, rows), cols]
    out = out + xs[1] * wb[1, pl.ds(0, rows), cols]
    out = out + xs[2] * wb[2, pl.ds(0, rows), cols]
    return out + wb[3, pl.ds(0, rows), cols], xs


def _ffn_gate_fwd(u, w, b, *, name):
    t, f2 = u.shape
    f = f2 // 2
    tm = _pick(t, (TM_CONV,))
    hb = HALO_BF16
    rb = ROWS_BLK
    cw = _pick(f, (COLS_BLK, LANES))

    def body(u_ref, up_ref, w_ref, b_ref, a_ref, v_ref, buf, wb):
        i = pl.program_id(0)
        buf[pl.ds(hb, tm), :] = u_ref[...].astype(F32)
        buf[pl.ds(0, hb), :] = jnp.where(i > 0, up_ref[...].astype(F32), 0.0)
        _bcast_rows(wb, w_ref, 0, 3)
        _bcast_rows(wb, b_ref, 3, 1)
        for c0 in range(0, f, cw):
            gcols, vcols = pl.ds(c0, cw), pl.ds(f + c0, cw)
            for r0 in range(0, tm, rb):
                rows = pl.ds(r0, rb)
                vg, _ = _conv3_blk(buf, wb, hb - 2 + r0, rb, gcols)
                vv, _ = _conv3_blk(buf, wb, hb - 2 + r0, rb, vcols)
                v_ref[rows, gcols] = vg.astype(BF16)
                v_ref[rows, vcols] = vv.astype(BF16)
                a_ref[rows, gcols] = (vg * _sigmoid(vg) * vv).astype(BF16)

    return pl.pallas_call(
        body, grid=(t // tm,),
        in_specs=[_row_spec(tm, f2), _prev_spec(tm, hb, f2), _vec_spec(3, f2), _vec_spec(1, f2)],
        out_specs=[_row_spec(tm, f), _row_spec(tm, f2)],
        out_shape=[jax.ShapeDtypeStruct((t, f), BF16), jax.ShapeDtypeStruct((t, f2), BF16)],
        scratch_shapes=[pltpu.VMEM((tm + hb, f2), F32), pltpu.VMEM((4, rb, f2), F32)],
        compiler_params=_cparams(_PAR), name=name)(u, u, w, b)


def _ffn_gate_bwd(u, v, da, w, *, name):
    t, f2 = u.shape
    f = f2 // 2
    tm = _pick(t, (TM_CONV,))
    hb = HALO_BF16
    n = t // tm
    rb = ROWS_BLK
    cw = _pick(f, (COLS_BLK, LANES))
    blocks = [(r0, rb) for r0 in range(0, tm, rb)] + [(tm, hb)]

    def body(u_ref, v_ref, vn_ref, da_ref, dan_ref, w_ref, du_ref, dw_ref, db_ref, dvbuf, wb, wacc, bacc):
        i = pl.program_id(0)

        @pl.when(i == 0)
        def _():
            wacc[...] = jnp.zeros_like(wacc)
            bacc[...] = jnp.zeros_like(bacc)

        _bcast_rows(wb, w_ref, 0, 3)
        for c0 in range(0, f, cw):
            gcols, vcols = pl.ds(c0, cw), pl.ds(f + c0, cw)
            for r0, rows in blocks:
                if r0 < tm:
                    vg, vv = v_ref[pl.ds(r0, rows), gcols], v_ref[pl.ds(r0, rows), vcols]
                    dav = da_ref[pl.ds(r0, rows), gcols].astype(F32)
                else:
                    vg, vv = vn_ref[:, gcols], vn_ref[:, vcols]
                    dav = jnp.where(i < n - 1, dan_ref[:, gcols].astype(F32), 0.0)
                vg, vv = vg.astype(F32), vv.astype(F32)
                sg = _sigmoid(vg)
                dvg = dav * vv * (sg * (1.0 + vg * (1.0 - sg)))
                dvv = dav * (vg * sg)
                dvbuf[pl.ds(r0, rows), gcols] = dvg
                dvbuf[pl.ds(r0, rows), vcols] = dvv
                if r0 < tm:
                    bacc[:, gcols] += _sum8(dvg)
                    bacc[:, vcols] += _sum8(dvv)
        for c0 in range(0, f2, cw):
            cols = pl.ds(c0, cw)
            for r0 in range(0, tm, rb):
                uv = u_ref[pl.ds(r0, rb), cols].astype(F32)
                window = dvbuf[pl.ds(r0, rb + SUBLANES), cols]
                du = None
                for k in range(3):
                    dvk = _shifted_rows(window, 2 - k, rb)
                    term = dvk * wb[k, :, cols]
                    du = term if du is None else du + term
                    wacc[k, :, cols] += _sum8(uv * dvk)
                du_ref[pl.ds(r0, rb), cols] = du.astype(BF16)

        @pl.when(i == n - 1)
        def _():
            db_ref[...] = jnp.sum(bacc[...], axis=0, keepdims=True)
            dw_ref[...] = jnp.sum(wacc[...], axis=1)

    return pl.pallas_call(
        body, grid=(n,),
        in_specs=[_row_spec(tm, f2), _row_spec(tm, f2), _next_spec(tm, hb, f2, t),
                  _row_spec(tm, f), _next_spec(tm, hb, f, t), _vec_spec(3, f2)],
        out_specs=[_row_spec(tm, f2), _vec_spec(3, f2), _vec_spec(1, f2)],
        out_shape=[jax.ShapeDtypeStruct((t, f2), BF16), jax.ShapeDtypeStruct((3, f2), F32),
                   jax.ShapeDtypeStruct((1, f2), F32)],
        scratch_shapes=[pltpu.VMEM((tm + hb, f2), F32), pltpu.VMEM((3, rb, f2), F32),
                        pltpu.VMEM((3, SUBLANES, f2), F32), pltpu.VMEM((SUBLANES, f2), F32)],
        compiler_params=_cparams(_ARB), name=name)(u, v, v, da, da, w)


def _glu(u, b1, d):
    return (u[:, :d] + b1[:, :d]) * _sigmoid(u[:, d:] + b1[:, d:])


ROWS_TAPS = 32
ROWS_NORM = 16


def _fill_glu_buf(buf, u_ref, up_ref, b1_ref, i, tm, d):
    cw = _pick(d, (COLS_BLK, LANES))
    for c0 in range(0, d, cw):
        b1 = jnp.concatenate([b1_ref[:, pl.ds(c0, cw)], b1_ref[:, pl.ds(d + c0, cw)]], axis=1)
        up = jnp.concatenate([up_ref[:, pl.ds(c0, cw)], up_ref[:, pl.ds(d + c0, cw)]], axis=1)
        buf[pl.ds(0, HALO_A), pl.ds(c0, cw)] = jnp.where(i > 0, _glu(up, b1, cw), 0.0)
        for r0 in range(0, tm, ROWS_TAPS):
            rows = pl.ds(r0, ROWS_TAPS)
            uv = jnp.concatenate([u_ref[rows, pl.ds(c0, cw)], u_ref[rows, pl.ds(d + c0, cw)]], axis=1)
            buf[pl.ds(HALO_A + r0, ROWS_TAPS), pl.ds(c0, cw)] = _glu(uv, b1, cw)


def _taps31(buf, r0, cols, offs, use):
    nv = ROWS_TAPS // SUBLANES
    nrows = ROWS_TAPS + SUBLANES * (-(-max(offs) // SUBLANES))
    window = buf[pl.ds(r0, nrows), cols]
    shifted = {b: _shifted_rows(window, b, nrows - SUBLANES) if b else window
               for b in sorted({o % SUBLANES for o in offs})}
    for k, o in enumerate(offs):
        b, a = o % SUBLANES, o // SUBLANES
        use(k, [shifted[b][SUBLANES * (a + v):SUBLANES * (a + v + 1)] for v in range(nv)])


def _conv_taps_blk(buf, wb, r0, cols, offs):
    nv = ROWS_TAPS // SUBLANES
    acc = [None] * nv

    def use(k, rows):
        wk = wb[k, :, cols]
        for v in range(nv):
            term = rows[v] * wk
            acc[v] = term if acc[v] is None else acc[v] + term

    _taps31(buf, r0, cols, offs, use)
    return jnp.concatenate(acc, axis=0)


def _layernorm_parts(x):
    mu = jnp.mean(x, axis=-1, keepdims=True)
    xc = x - mu
    rstd = lax.rsqrt(jnp.mean(xc * xc, axis=-1, keepdims=True) + LN_EPS)
    return xc * rstd, rstd


_FWD_OFFS = tuple(HALO_A - (CONF_CONV_WIDTH - 1) + k for k in range(CONF_CONV_WIDTH))
_BWD_OFFS = tuple(CONF_CONV_WIDTH - 1 - k for k in range(CONF_CONV_WIDTH))


def _a_fwd(u1, b1, dww, dwb, lng, lnb, *, name):
    t, d2 = u1.shape
    d = d2 // 2
    tm = _pick(t, (TM_CONV,))

    def body(u_ref, up_ref, b1_ref, w_ref, wbias_ref, g_ref, bb_ref, o_ref, u3_ref, buf, wb):
        i = pl.program_id(0)
        _fill_glu_buf(buf, u_ref, up_ref, b1_ref, i, tm, d)
        _bcast_rows(wb, w_ref, 0, CONF_CONV_WIDTH)
        for c0 in range(0, d, LANES):
            cols = pl.ds(c0, LANES)
            for r0 in range(0, tm, ROWS_TAPS):
                u3_ref[pl.ds(r0, ROWS_TAPS), cols] = (_conv_taps_blk(buf, wb, r0, cols, _FWD_OFFS)
                                                      + wbias_ref[:, cols])
        for r0 in range(0, tm, ROWS_NORM):
            rows = pl.ds(r0, ROWS_NORM)
            xhat, _ = _layernorm_parts(u3_ref[rows, :])
            u4 = xhat * g_ref[...] + bb_ref[...]
            o_ref[rows, :] = (u4 * _sigmoid(u4)).astype(BF16)

    return pl.pallas_call(
        body, grid=(t // tm,),
        in_specs=[_row_spec(tm, d2), _prev_spec(tm, HALO_A, d2), _vec_spec(1, d2),
                  _vec_spec(CONF_CONV_WIDTH, d)] + [_vec_spec(1, d)] * 3,
        out_specs=[_row_spec(tm, d)] * 2,
        out_shape=[jax.ShapeDtypeStruct((t, d), BF16), jax.ShapeDtypeStruct((t, d), F32)],
        scratch_shapes=[pltpu.VMEM((tm + HALO_A, d), F32), pltpu.VMEM((CONF_CONV_WIDTH, SUBLANES, d), F32)],
        compiler_params=_cparams(_PAR), name=name)(u1, u1, b1, dww, dwb, lng, lnb)


def _a_bwd_norm(u3, du5, lng, lnb, *, name):
    t, d = u3.shape
    tm = _pick(t, (TM_ROW, 256, 128))
    n = t // tm

    def body(u3_ref, du5_ref, g_ref, bb_ref, du3_ref, dg_ref, db_ref, dwb_ref, ag, ab, aw):
        i = pl.program_id(0)

        @pl.when(i == 0)
        def _():
            for a in (ag, ab, aw):
                a[...] = jnp.zeros_like(a)

        g = g_ref[...]
        for r0 in range(0, tm, ROWS_NORM):
            rows = pl.ds(r0, ROWS_NORM)
            xhat, rstd = _layernorm_parts(u3_ref[rows, :])
            u4 = xhat * g + bb_ref[...]
            sg = _sigmoid(u4)
            du4 = du5_ref[rows, :] * (sg * (1.0 + u4 * (1.0 - sg)))
            dxh = du4 * g
            du3 = rstd * (dxh - jnp.mean(dxh, axis=-1, keepdims=True)
                          - xhat * jnp.mean(dxh * xhat, axis=-1, keepdims=True))
            du3_ref[rows, :] = du3
            ag[...] += _sum8(du4 * xhat)
            ab[...] += _sum8(du4)
            aw[...] += _sum8(du3)

        @pl.when(i == n - 1)
        def _():
            for a, o in ((ag, dg_ref), (ab, db_ref), (aw, dwb_ref)):
                o[...] = jnp.sum(a[...], axis=0, keepdims=True)

    return pl.pallas_call(
        body, grid=(n,),
        in_specs=[_row_spec(tm, d)] * 2 + [_vec_spec(1, d)] * 2,
        out_specs=[_row_spec(tm, d)] + [_vec_spec(1, d)] * 3,
        out_shape=[jax.ShapeDtypeStruct((t, d), F32)] + [jax.ShapeDtypeStruct((1, d), F32)] * 3,
        scratch_shapes=[pltpu.VMEM((SUBLANES, d), F32)] * 3,
        compiler_params=_cparams(_ARB), name=name)(u3, du5, lng, lnb)


def _a_bwd_conv(u1, du3, b1, dww, *, name):
    t, d2 = u1.shape
    d = d2 // 2
    tm = _pick(t, (TM_CONV,))
    n = t // tm
    kw = CONF_CONV_WIDTH
    nv = ROWS_TAPS // SUBLANES

    def body(u_ref, up_ref, g3_ref, g3n_ref, b1_ref, w_ref, du1_ref, dw_ref, db1_ref,
             buf, gbuf, wb, wacc, bacc):
        i = pl.program_id(0)

        @pl.when(i == 0)
        def _():
            wacc[...] = jnp.zeros_like(wacc)
            bacc[...] = jnp.zeros_like(bacc)

        _fill_glu_buf(buf, u_ref, up_ref, b1_ref, i, tm, d)
        _bcast_rows(wb, w_ref, 0, kw)
        gbuf[pl.ds(0, tm), :] = g3_ref[...]
        gbuf[pl.ds(tm, HALO_A), :] = jnp.where(i < n - 1, g3n_ref[...], 0.0)
        for c0 in range(0, d, LANES):
            cols, gcols = pl.ds(c0, LANES), pl.ds(d + c0, LANES)
            for r0 in range(0, tm, ROWS_TAPS):
                rows = pl.ds(r0, ROWS_TAPS)
                u2 = [buf[pl.ds(HALO_A + r0 + SUBLANES * v, SUBLANES), cols] for v in range(nv)]
                acc = [None] * nv

                def use(k, gs):
                    wk = wb[k, :, cols]
                    part = None
                    for v in range(nv):
                        term = gs[v] * wk
                        acc[v] = term if acc[v] is None else acc[v] + term
                        prod = u2[v] * gs[v]
                        part = prod if part is None else part + prod
                    wacc[k, :, cols] += part

                _taps31(gbuf, r0, cols, _BWD_OFFS, use)
                du2 = jnp.concatenate(acc, axis=0)
                av = u_ref[rows, cols] + b1_ref[:, cols]
                sg = _sigmoid(u_ref[rows, gcols] + b1_ref[:, gcols])
                da = du2 * sg
                dg = du2 * av * (sg * (1.0 - sg))
                du1_ref[rows, cols] = da.astype(BF16)
                du1_ref[rows, gcols] = dg.astype(BF16)
                bacc[:, cols] += _sum8(da)
                bacc[:, gcols] += _sum8(dg)

        @pl.when(i == n - 1)
        def _():
            dw_ref[...] = jnp.sum(wacc[...], axis=1)
            db1_ref[...] = jnp.sum(bacc[...], axis=0, keepdims=True)

    return pl.pallas_call(
        body, grid=(n,),
        in_specs=[_row_spec(tm, d2), _prev_spec(tm, HALO_A, d2), _row_spec(tm, d),
                  _next_spec(tm, HALO_A, d, t), _vec_spec(1, d2), _vec_spec(kw, d)],
        out_specs=[_row_spec(tm, d2), _vec_spec(kw, d), _vec_spec(1, d2)],
        out_shape=[jax.ShapeDtypeStruct((t, d2), BF16), jax.ShapeDtypeStruct((kw, d), F32),
                   jax.ShapeDtypeStruct((1, d2), F32)],
        scratch_shapes=[pltpu.VMEM((tm + HALO_A, d), F32), pltpu.VMEM((tm + HALO_A, d), F32),
                        pltpu.VMEM((kw, SUBLANES, d), F32),
                        pltpu.VMEM((kw, SUBLANES, d), F32), pltpu.VMEM((SUBLANES, d2), F32)],
        compiler_params=_cparams(_ARB), name=name)(u1, u1, du3, du3, b1, dww)


def _pool_counts(i, tm, w):
    pos = (i * tm + lax.broadcasted_iota(jnp.int32, (tm, 1), 0) + 1).astype(F32)
    return jnp.minimum(pos, float(w))


def _b_pool_fwd(h, *, name):
    t, d = h.shape
    gd = d // len(POOL_WINDOWS)
    tm = _pick(t, (TM_CONV,))
    hb = HALO_POOL

    def body(h_ref, hp_ref, o_ref, buf):
        i = pl.program_id(0)
        buf[pl.ds(0, hb), :] = jnp.where(i > 0, hp_ref[...], 0.0)
        buf[pl.ds(hb, tm), :] = h_ref[...]
        for g, w in enumerate(POOL_WINDOWS):
            cols = pl.ds(g * gd, gd)
            cur = buf[pl.ds(hb, tm), cols]
            s = cur
            for j in range(1, w):
                s = s + buf[pl.ds(hb - j, tm), cols]
            o_ref[:, cols] = (s / _pool_counts(i, tm, w) - cur).astype(BF16)

    return pl.pallas_call(
        body, grid=(t // tm,),
        in_specs=[_row_spec(tm, d), _prev_spec(tm, hb, d)],
        out_specs=_row_spec(tm, d), out_shape=jax.ShapeDtypeStruct((t, d), BF16),
        scratch_shapes=[pltpu.VMEM((tm + hb, d), F32)],
        compiler_params=_cparams(_PAR), name=name)(h, h)


def _b_pool_bwd(dp, *, name):
    t, d = dp.shape
    gd = d // len(POOL_WINDOWS)
    tm = _pick(t, (TM_CONV,))
    hb = HALO_POOL
    n = t // tm

    def body(dp_ref, dpn_ref, o_ref, buf):
        i = pl.program_id(0)
        for g, w in enumerate(POOL_WINDOWS):
            cols = pl.ds(g * gd, gd)
            buf[pl.ds(0, tm), cols] = dp_ref[:, cols] / _pool_counts(i, tm, w)
            buf[pl.ds(tm, hb), cols] = jnp.where(i < n - 1, dpn_ref[:, cols] * (1.0 / w), 0.0)
            s = buf[pl.ds(0, tm), cols]
            for j in range(1, w):
                s = s + buf[pl.ds(j, tm), cols]
            o_ref[:, cols] = s - dp_ref[:, cols]

    return pl.pallas_call(
        body, grid=(n,),
        in_specs=[_row_spec(tm, d), _next_spec(tm, hb, d, t)],
        out_specs=_row_spec(tm, d), out_shape=jax.ShapeDtypeStruct((t, d), F32),
        scratch_shapes=[pltpu.VMEM((tm + hb, d), F32)],
        compiler_params=_cparams(_PAR), name=name)(dp, dp)


def _b_affine_fwd(mixed, gb, scale, *, name):
    t, d = mixed.shape
    tm = _pick(t, (TM_ROW, 256, 128))

    def body(m_ref, gb_ref, s_ref, o_ref):
        o_ref[...] = ((m_ref[...] + gb_ref[...]) * s_ref[...]).astype(BF16)

    return pl.pallas_call(
        body, grid=(t // tm,), in_specs=[_row_spec(tm, d)] + [_vec_spec(1, d)] * 2,
        out_specs=_row_spec(tm, d), out_shape=jax.ShapeDtypeStruct((t, d), BF16),
        compiler_params=_cparams(_PAR), name=name)(mixed, gb, scale)


def _b_affine_bwd(dy, mixed, gb, scale, *, name):
    t, d = mixed.shape
    tm = _pick(t, (TM_ROW, 256, 128))
    n = t // tm

    def body(dy_ref, m_ref, gb_ref, s_ref, dm_ref, ds_ref, dgb_ref, a1, a2):
        i = pl.program_id(0)

        @pl.when(i == 0)
        def _():
            a1[...] = jnp.zeros_like(a1)
            a2[...] = jnp.zeros_like(a2)

        dy_v = dy_ref[...]
        dm_ref[...] = (dy_v * s_ref[...]).astype(BF16)
        a1[...] += _sum8(dy_v * (m_ref[...] + gb_ref[...]))
        a2[...] += _sum8(dy_v)

        @pl.when(i == n - 1)
        def _():
            ds_ref[...] = jnp.sum(a1[...], axis=0, keepdims=True)
            dgb_ref[...] = jnp.sum(a2[...], axis=0, keepdims=True) * s_ref[...]

    return pl.pallas_call(
        body, grid=(n,), in_specs=[_row_spec(tm, d)] * 2 + [_vec_spec(1, d)] * 2,
        out_specs=[_row_spec(tm, d)] + [_vec_spec(1, d)] * 2,
        out_shape=[jax.ShapeDtypeStruct((t, d), BF16)] + [jax.ShapeDtypeStruct((1, d), F32)] * 2,
        scratch_shapes=[pltpu.VMEM((SUBLANES, d), F32)] * 2,
        compiler_params=_cparams(_ARB), name=name)(dy, mixed, gb, scale)


def _c_gate_fwd(bcx, wc, *, name):
    t, d3 = bcx.shape
    d = d3 // 3
    tm = _pick(t, (TM_CONV,))
    hb = HALO_3

    rb = ROWS_BLK
    cw = _pick(d, (COLS_BLK, LANES))

    def body(x_ref, xp_ref, w_ref, z_ref, buf, wb):
        i = pl.program_id(0)
        _bcast_rows(wb, w_ref, 0, 3)
        wb[3] = jnp.zeros(wb.shape[1:], F32)
        for c0 in range(0, d, cw):
            cols, ccols, vcols = pl.ds(c0, cw), pl.ds(d + c0, cw), pl.ds(2 * d + c0, cw)
            buf[pl.ds(0, hb), cols] = jnp.where(i > 0, xp_ref[:, ccols] * xp_ref[:, vcols], 0.0)
            for r0 in range(0, tm, rb):
                rows = pl.ds(r0, rb)
                buf[pl.ds(hb + r0, rb), cols] = x_ref[rows, ccols] * x_ref[rows, vcols]
            for r0 in range(0, tm, rb):
                rows = pl.ds(r0, rb)
                q, _ = _conv3_blk(buf, wb, hb - 2 + r0, rb, cols)
                z_ref[rows, cols] = (x_ref[rows, cols] * q).astype(BF16)

    return pl.pallas_call(
        body, grid=(t // tm,),
        in_specs=[_row_spec(tm, d3), _prev_spec(tm, hb, d3), _vec_spec(3, d)],
        out_specs=_row_spec(tm, d), out_shape=jax.ShapeDtypeStruct((t, d), BF16),
        scratch_shapes=[pltpu.VMEM((tm + hb, d), F32), pltpu.VMEM((4, rb, d), F32)],
        compiler_params=_cparams(_PAR), name=name)(bcx, bcx, wc)


def _c_gate_bwd(bcx, dz, wc, *, name):
    t, d3 = bcx.shape
    d = d3 // 3
    tm = _pick(t, (TM_CONV,))
    hb = HALO_3
    n = t // tm

    rb = ROWS_BLK
    cw = _pick(d, (COLS_BLK, LANES))

    def body(x_ref, xp_ref, xn_ref, dz_ref, dzn_ref, w_ref, o_ref, dw_ref, pbuf, qbuf, wb, wacc):
        i = pl.program_id(0)

        @pl.when(i == 0)
        def _():
            wacc[...] = jnp.zeros_like(wacc)

        _bcast_rows(wb, w_ref, 0, 3)
        wb[3] = jnp.zeros(wb.shape[1:], F32)
        for c0 in range(0, d, cw):
            cols, ccols, vcols = pl.ds(c0, cw), pl.ds(d + c0, cw), pl.ds(2 * d + c0, cw)
            pbuf[pl.ds(0, hb), cols] = jnp.where(i > 0, xp_ref[:, ccols] * xp_ref[:, vcols], 0.0)
            qbuf[pl.ds(tm, hb), cols] = jnp.where(i < n - 1, dzn_ref[:, cols] * xn_ref[:, cols], 0.0)
            for r0 in range(0, tm, rb):
                rows = pl.ds(r0, rb)
                pbuf[pl.ds(hb + r0, rb), cols] = x_ref[rows, ccols] * x_ref[rows, vcols]
            for r0 in range(0, tm, rb):
                rows = pl.ds(r0, rb)
                q, _ = _conv3_blk(pbuf, wb, hb - 2 + r0, rb, cols)
                dz_v = dz_ref[rows, cols]
                qbuf[rows, cols] = dz_v * x_ref[rows, cols]
                o_ref[rows, cols] = (dz_v * q).astype(BF16)
            for r0 in range(0, tm, rb):
                rows = pl.ds(r0, rb)
                pv = pbuf[pl.ds(hb + r0, rb), cols]
                window = qbuf[pl.ds(r0, rb + SUBLANES), cols]
                dp = None
                for k in range(3):
                    dqk = _shifted_rows(window, 2 - k, rb)
                    term = dqk * wb[k, :, cols]
                    dp = term if dp is None else dp + term
                    wacc[k, :, cols] += _sum8(pv * dqk)
                o_ref[rows, ccols] = (dp * x_ref[rows, vcols]).astype(BF16)
                o_ref[rows, vcols] = (dp * x_ref[rows, ccols]).astype(BF16)

        @pl.when(i == n - 1)
        def _():
            dw_ref[...] = jnp.sum(wacc[...], axis=1)

    return pl.pallas_call(
        body, grid=(n,),
        in_specs=[_row_spec(tm, d3), _prev_spec(tm, hb, d3), _next_spec(tm, hb, d3, t),
                  _row_spec(tm, d), _next_spec(tm, hb, d, t), _vec_spec(3, d)],
        out_specs=[_row_spec(tm, d3), _vec_spec(3, d)],
        out_shape=[jax.ShapeDtypeStruct((t, d3), BF16), jax.ShapeDtypeStruct((3, d), F32)],
        scratch_shapes=[pltpu.VMEM((tm + hb, d), F32), pltpu.VMEM((tm + hb, d), F32),
                        pltpu.VMEM((4, rb, d), F32), pltpu.VMEM((3, SUBLANES, d), F32)],
        compiler_params=_cparams(_ARB), name=name)(bcx, bcx, bcx, dz, dz, wc)


def _row(v):
    return v.reshape(1, -1)


def _kind_of(j):
    return "f" if j % 2 else "abc"[(j // 2) % N_MIXERS]


BIG = ("a_pw1_w", "a_pw2_w", "b_group_w", "c_in_w", "c_out_w", "f_up_w", "f_down_w")
COL_SHARDED = ("a_pw1_w", "c_in_w", "f_up_w")


def _local_step(x, target, mod, p):
    nsub = 2 * DEPTH
    norm_names = (("norm_pre_mix", "norm_post_mix"), ("norm_pre_ffn", "norm_post_ffn"))
    gpre = [_row(p[norm_names[s][0]][i]) for i in range(DEPTH) for s in (0, 1)]
    gpost = [_row(p[norm_names[s][1]][i]) for i in range(DEPTH) for s in (0, 1)]
    sh = [_row(mod[i, 3 * s + 0]) for i in range(DEPTH) for s in (0, 1)]
    sc = [_row(mod[i, 3 * s + 1]) for i in range(DEPTH) for s in (0, 1)]
    gt = [_row(mod[i, 3 * s + 2]) for i in range(DEPTH) for s in (0, 1)]

    def h_dtype(j):
        return F32 if _kind_of(j) == "b" else BF16

    xs, hs, ys, saved = [x], [], [], []

    hs.append(_fwd_first(x, gpre[0], sc[0], sh[0], h_dtype=h_dtype(0), name="fwd_first"))
    for j in range(nsub):
        i, kind = j // 2, _kind_of(j)
        slot = i // N_MIXERS
        h = hs[j]
        tag = f"{kind}{j}"
        if kind == "f":
            u = _mm(h, p["f_up_w"], mode="nn", layer=i, out_dtype=BF16, name=f"ffn_up_{tag}")
            a, vpre = _ffn_gate_fwd(u, p["f_dw_w"][i], _row(p["f_dw_b"][i]), name=f"ffn_gate_{tag}")
            y = _mm(a, p["f_down_w"][i], mode="nn", out_dtype=BF16, name=f"ffn_down_{tag}")
            saved.append((u, a, vpre))
        elif kind == "a":
            u1 = _mm(h, p["a_pw1_w"], mode="nn", layer=slot, name=f"a_pw1_{tag}")
            u5, u3 = _a_fwd(u1, _row(p["a_pw1_b"][slot]), p["a_dw_w"][slot], _row(p["a_dw_b"][slot]),
                            _row(p["a_ln_g"][slot]), _row(p["a_ln_b"][slot]), name=f"a_conv_{tag}")
            y = _mm(u5, p["a_pw2_w"][slot], mode="nn", bias=_row(p["a_pw2_b"][slot]), out_dtype=BF16,
                    name=f"a_pw2_{tag}")
            saved.append((u1, u5, u3))
        elif kind == "b":
            pooled = _b_pool_fwd(h, name=f"b_pool_{tag}")
            mixed = _mm_group(pooled, p["b_group_w"][slot], mode="nn", name=f"b_mix_{tag}")
            y = _b_affine_fwd(mixed, _row(p["b_group_b"][slot]), _row(p["b_scale"][slot]), name=f"b_aff_{tag}")
            saved.append((pooled, mixed))
        else:
            bcx = _mm(h, p["c_in_w"], mode="nn", layer=slot, name=f"c_in_{tag}")
            z = _c_gate_fwd(bcx, p["c_conv_w"][slot], name=f"c_gate_{tag}")
            y = _mm(z, p["c_out_w"][slot], mode="nn", out_dtype=BF16, name=f"c_out_{tag}")
            saved.append((bcx, z))
        ys.append(y)
        if j + 1 < nsub:
            x_new, h_next = _fwd_mid(xs[j], y, gpost[j], gt[j], gpre[j + 1], sc[j + 1], sh[j + 1],
                                     h_dtype=h_dtype(j + 1), name=f"fwd_mid_{j}")
            xs.append(x_new)
            hs.append(h_next)

    n_of = {"a": len([i for i in range(DEPTH) if i % N_MIXERS == 0]),
            "b": len([i for i in range(DEPTH) if i % N_MIXERS == 1]),
            "c": len([i for i in range(DEPTH) if i % N_MIXERS == 2]), "f": DEPTH, "n": DEPTH}
    g = {k: [None] * n_of[k[0]] for k in p}
    dmod = [[None] * 6 for _ in range(DEPTH)]

    last = nsub - 1
    dx, dy, dgpost, dgt, sdy, loss = _last_fwd_bwd(xs[last], ys[last], target, gpost[last], gt[last],
                                                   name="loss_head")
    for j in range(last, -1, -1):
        i, kind = j // 2, _kind_of(j)
        slot = i // N_MIXERS
        sub = j % 2
        tag = f"{kind}{j}"
        g[norm_names[sub][1]][i] = dgpost
        dmod[i][3 * sub + 2] = dgt
        h = hs[j]
        if kind == "f":
            u, a, vpre = saved[j]
            da = _mm(dy, p["f_down_w"][i], mode="nt", out_dtype=BF16, name=f"ffn_dda_{tag}")
            g["f_down_w"][i] = _mm(a, dy, mode="tn", tk=TK_TOKENS, name=f"ffn_dwdown_{tag}")
            du, dw, db = _ffn_gate_bwd(u, vpre, da, p["f_dw_w"][i], name=f"ffn_gate_bwd_{tag}")
            g["f_dw_w"][i], g["f_dw_b"][i] = dw, db
            dh = _mm(du, p["f_up_w"], mode="nt", layer=i, name=f"ffn_ddh_{tag}")
            g["f_up_w"][i] = _mm(h, du, mode="tn", tk=TK_TOKENS, layer=i, name=f"ffn_dwup_{tag}")
        elif kind == "a":
            u1, u5, u3 = saved[j]
            g["a_pw2_b"][slot] = sdy
            du5 = _mm(dy, p["a_pw2_w"][slot], mode="nt", name=f"a_ddu5_{tag}")
            g["a_pw2_w"][slot] = _mm(u5, dy, mode="tn", tk=TK_TOKENS, name=f"a_dw2_{tag}")
            b1 = _row(p["a_pw1_b"][slot])
            du3, dlg, dlb, ddwb = _a_bwd_norm(u3, du5, _row(p["a_ln_g"][slot]), _row(p["a_ln_b"][slot]),
                                              name=f"a_bwd_norm_{tag}")
            g["a_ln_g"][slot], g["a_ln_b"][slot], g["a_dw_b"][slot] = dlg, dlb, ddwb
            du1, ddww, db1 = _a_bwd_conv(u1, du3, b1, p["a_dw_w"][slot], name=f"a_bwd_conv_{tag}")
            g["a_dw_w"][slot], g["a_pw1_b"][slot] = ddww, db1
            dh = _mm(du1, p["a_pw1_w"], mode="nt", layer=slot, name=f"a_ddh_{tag}")
            g["a_pw1_w"][slot] = _mm(h, du1, mode="tn", tk=TK_TOKENS, layer=slot, name=f"a_dw1_{tag}")
        elif kind == "b":
            pooled, mixed = saved[j]
            dmixed, dscale, dgb = _b_affine_bwd(dy, mixed, _row(p["b_group_b"][slot]), _row(p["b_scale"][slot]),
                                                name=f"b_aff_bwd_{tag}")
            g["b_scale"][slot], g["b_group_b"][slot] = dscale, dgb
            dpooled = _mm_group(dmixed, p["b_group_w"][slot], mode="nt", name=f"b_dpool_{tag}")
            g["b_group_w"][slot] = _mm_group(pooled, dmixed, mode="tn", tm=TK_TOKENS, name=f"b_dw_{tag}")
            dh = _b_pool_bwd(dpooled, name=f"b_pool_bwd_{tag}")
        else:
            bcx, z = saved[j]
            dz = _mm(dy, p["c_out_w"][slot], mode="nt", name=f"c_ddz_{tag}")
            g["c_out_w"][slot] = _mm(z, dy, mode="tn", tk=TK_TOKENS, name=f"c_dwout_{tag}")
            dbcx, dwc = _c_gate_bwd(bcx, dz, p["c_conv_w"][slot], name=f"c_gate_bwd_{tag}")
            g["c_conv_w"][slot] = dwc
            dh = _mm(dbcx, p["c_in_w"], mode="nt", layer=slot, name=f"c_ddh_{tag}")
            g["c_in_w"][slot] = _mm(h, dbcx, mode="tn", tk=TK_TOKENS, layer=slot, name=f"c_dwin_{tag}")
        if j > 0:
            pj = j - 1
            dy_dtype = F32 if _kind_of(pj) == "b" else BF16
            dx, dy, dsh, dsc, dgpre, dgpost, dgt, sdy = _bwd_mid(
                dx, dh, xs[j], gpre[j], sc[j], ys[pj], gpost[pj], gt[pj], dy_dtype=dy_dtype, name=f"bwd_mid_{j}")
        else:
            dx, dsh, dsc, dgpre = _bwd_first(dx, dh, xs[0], gpre[0], sc[0], name="bwd_first")
        dmod[i][3 * sub + 0] = dsh
        dmod[i][3 * sub + 1] = dsc
        g[norm_names[sub][0]][i] = dgpre

    small = {k: jnp.stack(v).reshape(p[k].shape) for k, v in g.items() if k not in BIG}
    big = {k: v for k, v in g.items() if k in BIG}
    dmod_arr = jnp.stack([jnp.concatenate(r, axis=0) for r in dmod])
    return loss, dx, dmod_arr, small, big


_MESH = pl.DeviceIdType.MESH
_ANY = pl.BlockSpec(memory_space=pl.ANY)
_VMEM = pl.BlockSpec(memory_space=pltpu.VMEM)


def _place():
    return lax.axis_index("x"), lax.axis_index("y"), lax.axis_index("c")


def _other_chips(x, y):
    return [(1 - x, y), (x, 1 - y), (1 - x, 1 - y)]


def _remote(src, dst, send_sem, recv_sem, to):
    return pltpu.make_async_remote_copy(src_ref=src, dst_ref=dst, send_sem=send_sem, recv_sem=recv_sem,
                                        device_id=to, device_id_type=_MESH)


def _all_gather8(blk, *, name):
    r, cdim = blk.shape

    def body(x_ref, out_ref, send_sems, recv_sems, local_sem):
        x, y, c = _place()
        me, sibling = (x, y, c), (x, y, 1 - c)
        chips = _other_chips(x, y)

        def slot(px, py, pc):
            return out_ref.at[4 * px + 2 * py + pc]

        def copy(k, block, to, src=None):
            return _remote(slot(*block) if src is None else src, slot(*block),
                           send_sems.at[k], recv_sems.at[k], to)

        mine = pltpu.make_async_copy(x_ref, slot(*me), local_sem)
        mine.start()
        first = [copy(0, me, sibling, src=x_ref)]
        first += [copy(1 + j, me, (*chip, c), src=x_ref) for j, chip in enumerate(chips)]
        for cp in first:
            cp.start()
        passed = [copy(4 + j, (*chip, c), sibling) for j, chip in enumerate(chips)]
        for j, chip in enumerate(chips):
            copy(1 + j, (*chip, c), me).wait_recv()
            passed[j].start()
        copy(0, sibling, me).wait_recv()
        for j, chip in enumerate(chips):
            copy(4 + j, (*chip, 1 - c), me).wait_recv()
        for cp in first + passed:
            cp.wait_send()
        mine.wait()

    return pl.pallas_call(
        body, out_shape=jax.ShapeDtypeStruct((NDEV, r, cdim), blk.dtype),
        in_specs=[_VMEM], out_specs=_VMEM,
        scratch_shapes=[pltpu.SemaphoreType.DMA((7,)), pltpu.SemaphoreType.DMA((7,)), pltpu.SemaphoreType.DMA],
        compiler_params=pltpu.CompilerParams(vmem_limit_bytes=VMEM_LIMIT), name=name)(blk)


def _gather_dst(kind):
    if kind == "col":
        return lambda s, h: (s, h)
    if kind == "row":
        return lambda s, h: (h, slice(None), s)
    return lambda s, h: (slice(None), s, h)


def _cast_into_gathered(src, kind, out_shape, shard, *, name):
    _, a, rh, cdim = src.shape
    tr = _pick(rh, (512, 256, 128, 64, 32, 16))
    if kind == "col":
        out_idx = lambda h, ai, r, s: (s[0], h, ai, r, 0)
    elif kind == "row":
        out_idx = lambda h, ai, r, s: (h, ai, s[0], r, 0)
    else:
        out_idx = lambda h, ai, r, s: (ai, s[0], h, r, 0)

    def body(s_ref, x_ref, o_ref):
        o_ref[...] = x_ref[...].astype(BF16)

    grid_spec = pltpu.PrefetchScalarGridSpec(
        num_scalar_prefetch=1, grid=(2, a, rh // tr),
        in_specs=[pl.BlockSpec((None, None, tr, cdim), lambda h, ai, r, s: (h, ai, r, 0))],
        out_specs=pl.BlockSpec((None, None, None, tr, cdim), out_idx))
    return pl.pallas_call(
        body, grid_spec=grid_spec, out_shape=jax.ShapeDtypeStruct(out_shape, BF16),
        compiler_params=_cparams(_PAR, _PAR, _PAR), name=name)(shard, src)


def _gather_weights(bufs, kinds, *, name):
    nt = len(bufs)

    def body(*refs):
        out_refs = refs[nt:2 * nt]
        send_sems, recv_sems = refs[2 * nt:]
        x, y, c = _place()
        sibling = (x, y, 1 - c)
        chips = _other_chips(x, y)
        s_me = 2 * x + y

        def at(k, s, h):
            return out_refs[k].at[_gather_dst(kinds[k])(s, h)]

        sends, passed = [], []
        for k in range(nt):
            for j, chip in enumerate(chips):
                cp = _remote(at(k, s_me, c), at(k, s_me, c), send_sems.at[k, j], recv_sems.at[k, j], (*chip, c))
                cp.start()
                sends.append(cp)
        for k in range(nt):
            for j, (px, py) in enumerate(chips):
                got = at(k, 2 * px + py, c)
                _remote(got, got, send_sems.at[k, j], recv_sems.at[k, j], (px, py, c)).wait_recv()
                cp = _remote(got, got, send_sems.at[k, 3 + j], recv_sems.at[k, 3 + j], sibling)
                cp.start()
                passed.append(cp)
        for k in range(nt):
            for j, (px, py) in enumerate(chips):
                got = at(k, 2 * px + py, 1 - c)
                _remote(got, got, send_sems.at[k, 3 + j], recv_sems.at[k, 3 + j], sibling).wait_recv()
        for cp in sends + passed:
            cp.wait_send()

    return pl.pallas_call(
        body, out_shape=[jax.ShapeDtypeStruct(b.shape, BF16) for b in bufs],
        in_specs=[_ANY] * nt, out_specs=[_ANY] * nt, input_output_aliases={k: k for k in range(nt)},
        scratch_shapes=[pltpu.SemaphoreType.DMA((nt, 6)), pltpu.SemaphoreType.DMA((nt, 6))],
        name=name)(*bufs)


def _pair_exchange(gs, layers_of, *, name):
    n, nt = len(gs), len(layers_of)

    def body(*refs):
        g_refs, out_refs = refs[:n], refs[n:n + nt]
        send_sems, recv_sems = refs[n + nt:]
        x, y, c = _place()
        sibling = (x, y, 1 - c)
        copies = []
        for t, ks in enumerate(layers_of):
            for l, k in enumerate(ks):
                cp = _remote(g_refs[k].at[:, 1 - c], out_refs[t].at[l], send_sems.at[k], recv_sems.at[k], sibling)
                cp.start()
                copies.append(cp)
        for cp in copies:
            cp.wait()

    out_shape = [jax.ShapeDtypeStruct((len(ks), NSHARD) + gs[ks[0]].shape[2:], F32) for ks in layers_of]
    return pl.pallas_call(
        body, out_shape=out_shape, in_specs=[_ANY] * n, out_specs=[_ANY] * nt,
        scratch_shapes=[pltpu.SemaphoreType.DMA((n,)), pltpu.SemaphoreType.DMA((n,))],
        name=name)(*gs)


def _pair_sum(g, r1, s_acc, layer, half, *, name):
    _, _, rh, cdim = g.shape
    tr = _pick(rh, (256, 128, 176, 64, 32, 16))

    def body(half_ref, g_ref, r_ref, s_in_ref, o_ref):
        o_ref[...] = (g_ref[...] + r_ref[...]).astype(BF16)

    grid_spec = pltpu.PrefetchScalarGridSpec(
        num_scalar_prefetch=1, grid=(NSHARD, rh // tr),
        in_specs=[pl.BlockSpec((None, None, tr, cdim), lambda s, r, hf: (s, hf[0], r, 0)),
                  pl.BlockSpec((None, None, tr, cdim), lambda s, r, hf: (layer, s, r, 0)),
                  _ANY],
        out_specs=pl.BlockSpec((None, None, tr, cdim), lambda s, r, hf: (layer, s, r, 0)))
    return pl.pallas_call(
        body, grid_spec=grid_spec, out_shape=jax.ShapeDtypeStruct(s_acc.shape, BF16),
        input_output_aliases={3: 0},
        compiler_params=_cparams(_PAR, _PAR), name=name)(half, g, r1, s_acc)


def _chip_exchange(ss, *, name):
    nt = len(ss)

    def body(*refs):
        s_refs, out_refs = refs[:nt], refs[nt:2 * nt]
        send_sems, recv_sems = refs[2 * nt:]
        x, y, c = _place()
        copies = []
        for t in range(nt):
            for j, (px, py) in enumerate(_other_chips(x, y)):
                cp = _remote(s_refs[t].at[:, 2 * px + py], out_refs[t].at[j],
                             send_sems.at[t, j], recv_sems.at[t, j], (px, py, c))
                cp.start()
                copies.append(cp)
        for cp in copies:
            cp.wait()

    out_shape = [jax.ShapeDtypeStruct((3, s.shape[0]) + s.shape[2:], BF16) for s in ss]
    return pl.pallas_call(
        body, out_shape=out_shape, in_specs=[_ANY] * nt, out_specs=[_ANY] * nt,
        scratch_shapes=[pltpu.SemaphoreType.DMA((nt, 3)), pltpu.SemaphoreType.DMA((nt, 3))],
        name=name)(*ss)


def _chip_sum(s_t, r3_t, place, *, name):
    nl, _, rh, cdim = s_t.shape
    tr = _pick(rh, (256, 128, 176, 64, 32, 16))

    def body(pz, s_ref, r_ref, o_ref):
        acc = s_ref[...].astype(F32) + r_ref[0].astype(F32)
        o_ref[...] = (acc + r_ref[1].astype(F32)) + r_ref[2].astype(F32)

    grid_spec = pltpu.PrefetchScalarGridSpec(
        num_scalar_prefetch=1, grid=(nl, rh // tr),
        in_specs=[pl.BlockSpec((None, None, tr, cdim), lambda l, r, pz: (l, pz[0], r, 0)),
                  pl.BlockSpec((3, None, tr, cdim), lambda l, r, pz: (0, l, r, 0))],
        out_specs=pl.BlockSpec((None, None, tr, cdim), lambda l, r, pz: (l, pz[1], r, 0)))
    return pl.pallas_call(
        body, grid_spec=grid_spec, out_shape=jax.ShapeDtypeStruct((nl, 2, rh, cdim), F32),
        compiler_params=_cparams(_PAR, _PAR), name=name)(place, s_t, r3_t)


def _join_halves(reds, *, name):
    nt = len(reds)

    def body(*refs):
        out_refs = refs[nt:2 * nt]
        send_sems, recv_sems = refs[2 * nt:]
        x, y, c = _place()
        sibling = (x, y, 1 - c)
        copies = []
        for t in range(nt):
            cp = _remote(out_refs[t].at[:, c], out_refs[t].at[:, c], send_sems.at[t], recv_sems.at[t], sibling)
            cp.start()
            copies.append(cp)
        for t, cp in enumerate(copies):
            cp.wait_send()
            got = out_refs[t].at[:, 1 - c]
            _remote(got, got, send_sems.at[t], recv_sems.at[t], sibling).wait_recv()

    return pl.pallas_call(
        body, out_shape=[jax.ShapeDtypeStruct(r.shape, F32) for r in reds],
        in_specs=[_ANY] * nt, out_specs=[_ANY] * nt, input_output_aliases={t: t for t in range(nt)},
        scratch_shapes=[pltpu.SemaphoreType.DMA((nt,)), pltpu.SemaphoreType.DMA((nt,))],
        name=name)(*reds)


def _sum_devices(g, *, name):
    _, r, cdim = g.shape
    tr = _pick(r, (512, 256, 128, 64, 32, 16, 8))

    def body(g_ref, o_ref):
        acc = g_ref[0]
        for e in range(1, NDEV):
            acc = acc + g_ref[e]
        o_ref[...] = acc

    return pl.pallas_call(
        body, grid=(r // tr,), in_specs=[pl.BlockSpec((NDEV, tr, cdim), lambda i: (0, i, 0))],
        out_specs=pl.BlockSpec((tr, cdim), lambda i: (i, 0)),
        out_shape=jax.ShapeDtypeStruct((r, cdim), F32),
        compiler_params=_cparams(_PAR), name=name)(g)


def _mod_fwd(c_all, mod_w, mod_b_cols, *, name):
    nl, d, n = mod_w.shape
    ne = c_all.shape[0]
    tn = _pick(n, (768, 512, 384, 256, 128))

    def body(c_ref, w_ref, b_ref, o_ref):
        cv = c_ref[...]
        act = (cv * _sigmoid(cv)).astype(BF16)
        o_ref[...] = jnp.dot(act, w_ref[...].astype(BF16), preferred_element_type=F32) + b_ref[...]

    return pl.pallas_call(
        body, grid=(nl, n // tn),
        in_specs=[pl.BlockSpec((ne, d), lambda i, j: (0, 0)),
                  pl.BlockSpec((None, d, tn), lambda i, j: (i, 0, j)),
                  pl.BlockSpec((None, 1, tn), lambda i, j: (i, 0, j))],
        out_specs=pl.BlockSpec((None, ne, tn), lambda i, j: (i, 0, j)),
        out_shape=jax.ShapeDtypeStruct((nl, ne, n), F32),
        compiler_params=_cparams(_PAR, _PAR), name=name)(c_all, mod_w, mod_b_cols)


def _adam_math(w, g, m, v):
    m2 = ADAM_B1 * m + (1.0 - ADAM_B1) * g
    v2 = ADAM_B2 * v + (1.0 - ADAM_B2) * (g * g)
    m_hat = m2 / (1.0 - ADAM_B1 ** ADAM_STEP)
    v_hat = v2 / (1.0 - ADAM_B2 ** ADAM_STEP)
    delta = -ADAM_LR * (m_hat / (jnp.sqrt(v_hat) + ADAM_EPS) + ADAM_WD * w)
    return delta, m2, v2


def _adamw(w, g, m, v, *, name):
    rows, cdim = w.shape
    tr = _pick(rows, tuple(t for t in (512, 256, 128, 64, 32, 16, 8) if t * cdim <= 256 * 1024))

    def body(w_ref, g_ref, m_ref, v_ref, d_ref, mo_ref, vo_ref):
        d_ref[...], mo_ref[...], vo_ref[...] = _adam_math(w_ref[...], g_ref[...], m_ref[...], v_ref[...])

    spec = pl.BlockSpec((tr, cdim), lambda i: (i, 0))
    return pl.pallas_call(
        body, grid=(rows // tr,), in_specs=[spec] * 4, out_specs=[spec] * 3,
        out_shape=[jax.ShapeDtypeStruct((rows, cdim), F32)] * 3,
        compiler_params=_cparams(_PAR), name=name)(w, g, m, v)


def _mod_w_update(c_t, dmod, w, m, v, *, name):
    nl, d, n = w.shape
    ne = c_t.shape[1]
    tr = _pick(d, (128, 64, 32, 16, 8))

    def body(c_ref, dm_ref, w_ref, m_ref, v_ref, g_ref, d_ref, mo_ref, vo_ref):
        cv = c_ref[...]
        act = cv * _sigmoid(cv)
        dm = dm_ref[...]
        g = act[:, 0:1] * dm[0:1, :]
        for e in range(1, ne):
            g = g + act[:, e:e + 1] * dm[e:e + 1, :]
        g_ref[...] = g
        d_ref[...], mo_ref[...], vo_ref[...] = _adam_math(w_ref[...], g, m_ref[...], v_ref[...])

    big = pl.BlockSpec((None, tr, n), lambda i, r: (i, r, 0))
    return pl.pallas_call(
        body, grid=(nl, d // tr),
        in_specs=[pl.BlockSpec((tr, ne), lambda i, r: (r, 0)),
                  pl.BlockSpec((None, ne, n), lambda i, r: (i, 0, 0)), big, big, big],
        out_specs=[big] * 4, out_shape=[jax.ShapeDtypeStruct((nl, d, n), F32)] * 4,
        compiler_params=_cparams(_PAR, _PAR), name=name)(c_t, dmod, w, m, v)


PACK_ROWS = 256


def _pack(arrs):
    flat = jnp.concatenate([a.reshape(-1) for a in arrs])
    tile = PACK_ROWS * LANES
    pad = (-flat.shape[0]) % tile
    return jnp.pad(flat, (0, pad)).reshape(-1, LANES)


def _unpack(packed, shapes, lead=()):
    flat = packed.reshape(lead + (-1,))
    out, off = [], 0
    for shp in shapes:
        size = 1
        for s in shp:
            size *= s
        out.append(flat[..., off:off + size].reshape(lead + tuple(shp)))
        off += size
    return out


SMALL_SHARD_AXIS = {"a_pw1_b": 1, "a_dw_w": 2, "a_dw_b": 1, "a_ln_g": 1, "a_ln_b": 1, "a_pw2_b": 1,
                    "c_conv_w": 2, "f_dw_w": 2}
SMALL_REPLICATED = ("norm_pre_mix", "norm_post_mix", "norm_pre_ffn", "norm_post_ffn",
                    "b_group_b", "b_scale", "f_dw_b")
WEIGHT_ORDER = ("mod_w", "mod_b", "norm_pre_mix", "norm_post_mix", "norm_pre_ffn", "norm_post_ffn",
                "a_pw1_w", "a_pw1_b", "a_dw_w", "a_dw_b", "a_ln_g", "a_ln_b", "a_pw2_w", "a_pw2_b",
                "b_group_w", "b_group_b", "b_scale", "c_in_w", "c_conv_w", "c_out_w",
                "f_up_w", "f_dw_w", "f_dw_b", "f_down_w")


def _as_layers_rows_cols(name, w):
    if name == "b_group_w":
        return w.reshape(w.shape[1], w.shape[2], w.shape[3])
    return w


def _step(x, c, loss_target, w, m, v):
    xi, yi, ci = _place()
    shard = 2 * xi + yi
    example = 4 * xi + 2 * yi + ci
    d = x.shape[-1]

    small_names = tuple(SMALL_SHARD_AXIS)
    gathered0 = _all_gather8(_pack([c] + [w[k] for k in small_names]), name="gather_small")
    parts = _unpack(gathered0, [c.shape] + [w[k].shape for k in small_names], lead=(NDEV,))
    c_all = parts[0].reshape(NDEV, d)
    p = {}
    for k, part in zip(small_names, parts[1:]):
        p[k] = jnp.concatenate([part[2 * s] for s in range(NSHARD)], axis=SMALL_SHARD_AXIS[k])
    for k in SMALL_REPLICATED:
        p[k] = w[k]

    ncol = w["mod_w"].shape[2]
    mod_b_cols = lax.dynamic_slice_in_dim(w["mod_b"], shard * ncol, ncol, axis=1).reshape(DEPTH, 1, ncol)
    mod_part = _mod_fwd(c_all, w["mod_w"], mod_b_cols, name="mod_fwd")
    gathered1 = _all_gather8(mod_part.reshape(DEPTH * NDEV, ncol), name="gather_mod")
    mod_all = gathered1.reshape(NSHARD, 2, DEPTH, NDEV, ncol)[:, 0]
    mod_mine = lax.dynamic_index_in_dim(mod_all, example, axis=2, keepdims=False)
    mod = jnp.transpose(mod_mine, (1, 0, 2)).reshape(DEPTH, 6, d)

    shard_arr = shard.reshape(1).astype(jnp.int32)
    bufs, kinds = [], []
    for k in BIG:
        wk = _as_layers_rows_cols(k, w[k])
        nl, r, cdim = wk.shape
        if nl >= 2:
            a, rh = nl // 2, r
        else:
            a, rh = 1, r // 2
        if k in COL_SHARDED:
            kind, out_shape = "col", (NSHARD, 2, a, rh, cdim)
        elif nl >= 2:
            kind, out_shape = "row", (2, a, NSHARD, rh, cdim)
        else:
            kind, out_shape = "row1", (1, NSHARD, 2, rh, cdim)
        kinds.append(kind)
        bufs.append(_cast_into_gathered(wk.reshape(2, a, rh, cdim), kind, out_shape, shard_arr, name=f"cast_{k}"))
    full = _gather_weights(bufs, kinds, name="gather_weights")
    for k, f in zip(BIG, full):
        nl, r, cdim = _as_layers_rows_cols(k, w[k]).shape
        if k in COL_SHARDED:
            p[k] = f.reshape(NSHARD, nl, r, cdim)
        elif k == "b_group_w":
            p[k] = f.reshape(1, nl, NSHARD * r, cdim)
        else:
            p[k] = f.reshape(nl, NSHARD * r, cdim)

    loss, grad_x, dmod, small, big = _local_step(x[0], loss_target[0], mod, p)

    gs, layers_of = [], []
    for k in BIG:
        ks = []
        for g in big[k]:
            if k == "b_group_w":
                ng, rr, cc = g.shape
                g = jnp.transpose(g.reshape(ng, NSHARD, rr // NSHARD, cc), (1, 0, 2, 3)).reshape(NSHARD, -1, cc)
            elif k not in COL_SHARDED:
                g = g.reshape(NSHARD, g.shape[0] // NSHARD, g.shape[1])
            ks.append(len(gs))
            gs.append(g.reshape(NSHARD, 2, g.shape[1] // 2, g.shape[2]))
        layers_of.append(ks)
    half = ci.reshape(1).astype(jnp.int32)
    place = jnp.stack([shard, ci]).astype(jnp.int32)
    r1 = _pair_exchange(gs, layers_of, name="grad_pair_exchange")
    ss = []
    for k, ks, r1_t in zip(BIG, layers_of, r1):
        s_t = lax.empty(r1_t.shape, BF16)
        for l, i in enumerate(ks):
            s_t = _pair_sum(gs[i], r1_t, s_t, l, half, name=f"grad_pair_sum_{k}_{l}")
        ss.append(s_t)
    r3 = _chip_exchange(ss, name="grad_chip_exchange")
    reds = [_chip_sum(s_t, r3_t, place, name=f"grad_chip_sum_{k}") for k, s_t, r3_t in zip(BIG, ss, r3)]
    joined = _join_halves(reds, name="grad_join_halves")
    grads = {k: j.reshape(w[k].shape) for k, j in zip(BIG, joined)}

    rep_names = SMALL_REPLICATED
    small_list = [small[k] for k in rep_names] + [small[k] for k in small_names] + [dmod]
    gathered2 = _all_gather8(_pack(small_list), name="gather_small_grads")
    summed = _sum_devices(gathered2, name="sum_small_grads")
    shapes = [s.shape for s in small_list]
    sums = _unpack(summed, shapes)
    for k, s in zip(rep_names, sums[:len(rep_names)]):
        grads[k] = s
    for k, s in zip(small_names, sums[len(rep_names):-1]):
        ax = SMALL_SHARD_AXIS[k]
        grads[k] = lax.dynamic_slice_in_dim(s, shard * w[k].shape[ax], w[k].shape[ax], axis=ax)
    grads["mod_b"] = sums[-1].reshape(w["mod_b"].shape)
    dmod_all = _unpack(gathered2, shapes, lead=(NDEV,))[-1].reshape(NDEV, DEPTH, NSHARD, ncol)
    dmod_cols = jnp.transpose(lax.dynamic_index_in_dim(dmod_all, shard, axis=2, keepdims=False), (1, 0, 2))

    delta, new_m, new_v = {}, {}, {}
    grads["mod_w"], delta["mod_w"], new_m["mod_w"], new_v["mod_w"] = _mod_w_update(
        c_all.T, dmod_cols, w["mod_w"], m["mod_w"], v["mod_w"], name="mod_w_update")
    for k in BIG:
        cdim = w[k].shape[-1]
        outs = _adamw(*[t.reshape(-1, cdim) for t in (w[k], grads[k], m[k], v[k])], name=f"adamw_{k}")
        delta[k], new_m[k], new_v[k] = [o.reshape(w[k].shape) for o in outs]
    rest = ("mod_b",) + rep_names + small_names
    packs = [_pack([t[k] for k in rest]) for t in (w, grads, m, v)]
    outs = _adamw(*packs, name="adamw_small")
    rest_shapes = [w[k].shape for k in rest]
    for dst, o in zip((delta, new_m, new_v), outs):
        for k, t in zip(rest, _unpack(o, rest_shapes)):
            dst[k] = t

    loss_all = lax.psum(loss[0, 0], ("x", "y", "c"))
    return (loss_all, grad_x[None], *[grads[k] for k in WEIGHT_ORDER], *[delta[k] for k in WEIGHT_ORDER],
            *[new_m[k] for k in WEIGHT_ORDER], *[new_v[k] for k in WEIGHT_ORDER])


def kernel(x, c, mod_w, mod_b, norm_pre_mix, norm_post_mix, norm_pre_ffn, norm_post_ffn, a_pw1_w, a_pw1_b, a_dw_w, a_dw_b, a_ln_g, a_ln_b, a_pw2_w, a_pw2_b, b_group_w, b_group_b, b_scale, c_in_w, c_conv_w, c_out_w, f_up_w, f_dw_w, f_dw_b, f_down_w, loss_target, m_mod_w, m_mod_b, m_norm_pre_mix, m_norm_post_mix, m_norm_pre_ffn, m_norm_post_ffn, m_a_pw1_w, m_a_pw1_b, m_a_dw_w, m_a_dw_b, m_a_ln_g, m_a_ln_b, m_a_pw2_w, m_a_pw2_b, m_b_group_w, m_b_group_b, m_b_scale, m_c_in_w, m_c_conv_w, m_c_out_w, m_f_up_w, m_f_dw_w, m_f_dw_b, m_f_down_w, v_mod_w, v_mod_b, v_norm_pre_mix, v_norm_post_mix, v_norm_pre_ffn, v_norm_post_ffn, v_a_pw1_w, v_a_pw1_b, v_a_dw_w, v_a_dw_b, v_a_ln_g, v_a_ln_b, v_a_pw2_w, v_a_pw2_b, v_b_group_w, v_b_group_b, v_b_scale, v_c_in_w, v_c_conv_w, v_c_out_w, v_f_up_w, v_f_dw_w, v_f_dw_b, v_f_down_w):
    given = dict(locals())
    w = {k: given[k] for k in WEIGHT_ORDER}
    m = {k: given["m_" + k] for k in WEIGHT_ORDER}
    v = {k: given["v_" + k] for k in WEIGHT_ORDER}
    return _step(x, c, loss_target, w, m, v)
```

```python
import functools

import jax
import jax.numpy as jnp
from jax import lax
from jax.experimental import pallas as pl
from jax.experimental.pallas import tpu as pltpu

F32 = jnp.float32
BF16 = jnp.bfloat16

DEPTH = 4
N_MIXERS = 3
CONF_CONV_WIDTH = 31
POOL_WINDOWS = (2, 4, 8, 16)
RMS_EPS = 1e-6
LN_EPS = 1e-5
ADAM_LR = 0.001
ADAM_B1 = 0.9
ADAM_B2 = 0.999
ADAM_EPS = 1e-08
ADAM_WD = 0.01
ADAM_STEP = 10

TM_ROW = 512
TM_CONV = 128
TK_TOKENS = 2048
TM_MM = 1024
TN_MM = 1024
SUBLANES = 8
LANES = 128
NSHARD = 4
NDEV = 8
HALO_A = 32
HALO_POOL = 16
HALO_3 = 8
HALO_BF16 = 16
VMEM_LIMIT = 56 * 1024 * 1024

_PAR = "parallel"
_ARB = "arbitrary"


def _cparams(*sem):
    return pltpu.CompilerParams(dimension_semantics=sem, vmem_limit_bytes=VMEM_LIMIT)


def _pick(n, prefs):
    for p in prefs:
        if p <= n and n % p == 0:
            return p
    return n


def _row_spec(tm, width):
    return pl.BlockSpec((tm, width), lambda i: (i, 0))


def _vec_spec(rows, width):
    return pl.BlockSpec((rows, width), lambda i: (0, 0))


def _prev_spec(tm, hb, width):
    return pl.BlockSpec((hb, width), lambda i: (jnp.maximum(i * (tm // hb) - 1, 0), 0))


def _next_spec(tm, hb, width, total):
    last = total // hb - 1
    return pl.BlockSpec((hb, width), lambda i: (jnp.minimum((i + 1) * (tm // hb), last), 0))


def _sum8(v):
    r, c = v.shape
    return jnp.sum(v.reshape(r // SUBLANES, SUBLANES, c), axis=0)


def _rms(x):
    r = lax.rsqrt(jnp.mean(x * x, axis=-1, keepdims=True) + RMS_EPS)
    return x * r, r


def _rms_bwd(dy, xn, r):
    return r * (dy - xn * jnp.mean(dy * xn, axis=-1, keepdims=True))


def _sigmoid(x):
    return 1.0 / (1.0 + jnp.exp(-x))


_DIMS = {"nn": ((1,), (0,)), "nt": ((1,), (1,)), "tn": ((0,), (0,))}


def _mm(a, b, *, mode, name, out_dtype=F32, bias=None, tm=TM_MM, tn=TN_MM, tk=None, layer=None):
    sharded = layer is not None
    if mode == "nn":
        m, k = a.shape
        n = NSHARD * b.shape[3] if sharded else b.shape[1]
    elif mode == "nt":
        m, k = a.shape
        n = b.shape[2] if sharded else b.shape[0]
    else:
        (k, m), (_, n) = a.shape, b.shape
    ns = n // NSHARD
    ks = k // NSHARD
    tm = _pick(m, (tm, 1408, 512, 256, 128))
    if sharded and mode != "nt":
        tn = _pick(ns, (1408, 768, 512, 256, 128))
    else:
        tn = _pick(n, (tn, 1408, 512, 256, 128))
    if sharded and mode == "nt":
        tk = _pick(ks, (1408, 768, 512, 256, 128))
    else:
        tk = _pick(k, (tk or k, 2816, 1024, 512, 256, 128))
    nk = k // tk
    per_n = ns // tn if sharded and mode != "nt" else 1
    per_k = ks // tk if sharded and mode == "nt" else 1
    dims = (_DIMS[mode], ((), ()))

    def split(idx, per):
        return (idx, 0) if per == 1 else (idx // per, idx % per)

    def body(*refs):
        a_ref, b_ref = refs[0], refs[1]
        bias_ref = refs[2] if bias is not None else None
        o_ref = refs[3] if bias is not None else refs[2]
        part = lax.dot_general(a_ref[...].astype(BF16), b_ref[...].astype(BF16), dims,
                               preferred_element_type=F32)

        def finish(r):
            if bias_ref is not None:
                r = r + bias_ref[...]
            o_ref[...] = r.astype(out_dtype)

        if nk == 1:
            finish(part)
        else:
            acc_ref = refs[-1]
            kk = pl.program_id(2)

            @pl.when(kk == 0)
            def _():
                acc_ref[...] = part

            @pl.when(kk > 0)
            def _():
                acc_ref[...] += part

            @pl.when(kk == nk - 1)
            def _():
                finish(acc_ref[...])

    out_spec = pl.BlockSpec((tm, tn), lambda i, j, kk: (i, j))
    out_shape = jax.ShapeDtypeStruct((m, n), out_dtype)
    if mode == "nn":
        a_spec = pl.BlockSpec((tm, tk), lambda i, j, kk: (i, kk))
        if sharded:
            b_spec = pl.BlockSpec((None, None, tk, tn),
                                  lambda i, j, kk: (split(j, per_n)[0], layer, kk, split(j, per_n)[1]))
        else:
            b_spec = pl.BlockSpec((tk, tn), lambda i, j, kk: (kk, j))
    elif mode == "nt":
        a_spec = pl.BlockSpec((tm, tk), lambda i, j, kk: (i, kk))
        if sharded:
            b_spec = pl.BlockSpec((None, None, tn, tk),
                                  lambda i, j, kk: (split(kk, per_k)[0], layer, j, split(kk, per_k)[1]))
        else:
            b_spec = pl.BlockSpec((tn, tk), lambda i, j, kk: (j, kk))
    else:
        a_spec = pl.BlockSpec((tk, tm), lambda i, j, kk: (kk, i))
        b_spec = pl.BlockSpec((tk, tn), lambda i, j, kk: (kk, j))
        if sharded:
            out_spec = pl.BlockSpec((None, tm, tn), lambda i, j, kk: (split(j, per_n)[0], i, split(j, per_n)[1]))
            out_shape = jax.ShapeDtypeStruct((NSHARD, m, ns), out_dtype)
    in_specs = [a_spec, b_spec]
    args = [a, b]
    if bias is not None:
        in_specs.append(pl.BlockSpec((1, tn), lambda i, j, kk: (0, j)))
        args.append(bias)
    return pl.pallas_call(
        body, grid=(m // tm, n // tn, nk), in_specs=in_specs, out_specs=out_spec, out_shape=out_shape,
        scratch_shapes=[pltpu.VMEM((tm, tn), F32)] if nk > 1 else [],
        compiler_params=_cparams(_PAR, _PAR, _ARB), name=name)(*args)


def _mm_group(a, b, *, mode, name, out_dtype=F32, tm=2048):
    t = a.shape[0]
    tm = _pick(t, (tm, 1024, 512, 256, 128))
    nt_ = t // tm
    g = len(POOL_WINDOWS)
    gd = a.shape[1] // g
    dims = (_DIMS[mode], ((), ()))

    if mode == "tn":
        def body(a_ref, b_ref, o_ref, acc_ref):
            kk = pl.program_id(1)
            part = lax.dot_general(a_ref[...].astype(BF16), b_ref[...].astype(BF16), dims,
                                   preferred_element_type=F32)

            @pl.when(kk == 0)
            def _():
                acc_ref[...] = part

            @pl.when(kk > 0)
            def _():
                acc_ref[...] += part

            @pl.when(kk == nt_ - 1)
            def _():
                o_ref[...] = acc_ref[...]

        return pl.pallas_call(
            body, grid=(g, nt_),
            in_specs=[pl.BlockSpec((tm, gd), lambda gi, kk: (kk, gi)),
                      pl.BlockSpec((tm, gd), lambda gi, kk: (kk, gi))],
            out_specs=pl.BlockSpec((None, gd, gd), lambda gi, kk: (gi, 0, 0)),
            out_shape=jax.ShapeDtypeStruct((g, gd, gd), F32),
            scratch_shapes=[pltpu.VMEM((gd, gd), F32)],
            compiler_params=_cparams(_PAR, _ARB), name=name)(a, b)

    def body(a_ref, b_ref, o_ref):
        o_ref[...] = lax.dot_general(a_ref[...].astype(BF16), b_ref[...].astype(BF16), dims,
                                     preferred_element_type=F32).astype(out_dtype)

    return pl.pallas_call(
        body, grid=(nt_, g),
        in_specs=[pl.BlockSpec((tm, gd), lambda i, gi: (i, gi)),
                  pl.BlockSpec((None, gd, gd), lambda i, gi: (gi, 0, 0))],
        out_specs=pl.BlockSpec((tm, gd), lambda i, gi: (i, gi)),
        out_shape=jax.ShapeDtypeStruct((t, g * gd), out_dtype),
        compiler_params=_cparams(_PAR, _PAR), name=name)(a, b)


def _pre(x, gpre, sc, sh):
    xn, r = _rms(x)
    return (xn * gpre) * (1.0 + sc) + sh, xn, r


def _fwd_first(x, gpre, sc, sh, *, h_dtype, name):
    t, d = x.shape
    tm = _pick(t, (TM_ROW, 256, 128))

    def body(x_ref, gpre_ref, sc_ref, sh_ref, h_ref):
        h, _, _ = _pre(x_ref[...], gpre_ref[...], sc_ref[...], sh_ref[...])
        h_ref[...] = h.astype(h_dtype)

    return pl.pallas_call(
        body, grid=(t // tm,),
        in_specs=[_row_spec(tm, d)] + [_vec_spec(1, d)] * 3,
        out_specs=_row_spec(tm, d), out_shape=jax.ShapeDtypeStruct((t, d), h_dtype),
        compiler_params=_cparams(_PAR), name=name)(x, gpre, sc, sh)


def _fwd_mid(x, y, gpost, gt, gpre, sc, sh, *, h_dtype, name):
    t, d = x.shape
    tm = _pick(t, (TM_ROW, 256, 128))

    def body(x_ref, y_ref, gpost_ref, gt_ref, gpre_ref, sc_ref, sh_ref, xn_ref, h_ref):
        yn, _ = _rms(y_ref[...].astype(F32))
        x_new = x_ref[...] + gt_ref[...] * (yn * gpost_ref[...])
        xn_ref[...] = x_new
        h, _, _ = _pre(x_new, gpre_ref[...], sc_ref[...], sh_ref[...])
        h_ref[...] = h.astype(h_dtype)

    return pl.pallas_call(
        body, grid=(t // tm,),
        in_specs=[_row_spec(tm, d)] * 2 + [_vec_spec(1, d)] * 5,
        out_specs=[_row_spec(tm, d)] * 2,
        out_shape=[jax.ShapeDtypeStruct((t, d), F32), jax.ShapeDtypeStruct((t, d), h_dtype)],
        compiler_params=_cparams(_PAR), name=name)(x, y, gpost, gt, gpre, sc, sh)


def _post_bwd(dx, y, gpost, gt):
    yn, r2 = _rms(y)
    dyn = dx * (gt * gpost)
    dy = _rms_bwd(dyn, yn, r2)
    return dy, dx * yn


def _last_fwd_bwd(x, y, target, gpost, gt, *, name):
    t, d = x.shape
    tm = _pick(t, (TM_ROW, 256, 128))
    n = t // tm

    def body(x_ref, y_ref, tg_ref, gpost_ref, gt_ref, dx_ref, dy_ref, dgpost_ref, dgt_ref, sdy_ref,
             loss_ref, qa, sa, la):
        i = pl.program_id(0)

        @pl.when(i == 0)
        def _():
            qa[...] = jnp.zeros_like(qa)
            sa[...] = jnp.zeros_like(sa)
            la[...] = jnp.zeros_like(la)

        yv = y_ref[...].astype(F32)
        yn, r2 = _rms(yv)
        gt_v, gpost_v = gt_ref[...], gpost_ref[...]
        err = x_ref[...] + gt_v * (yn * gpost_v) - tg_ref[...]
        la[...] += _sum8(err * err)
        dx = err * (1.0 / d)
        dx_ref[...] = dx
        dy = _rms_bwd(dx * (gt_v * gpost_v), yn, r2)
        dy_ref[...] = dy.astype(dy_ref.dtype)
        qa[...] += _sum8(dx * yn)
        sa[...] += _sum8(dy)

        @pl.when(i == n - 1)
        def _():
            q = jnp.sum(qa[...], axis=0, keepdims=True)
            dgpost_ref[...] = gt_v * q
            dgt_ref[...] = gpost_v * q
            sdy_ref[...] = jnp.sum(sa[...], axis=0, keepdims=True)
            tot = jnp.sum(jnp.sum(la[...], axis=0, keepdims=True), axis=1, keepdims=True)
            loss_ref[...] = tot * (0.5 / d)

    return pl.pallas_call(
        body, grid=(n,),
        in_specs=[_row_spec(tm, d)] * 3 + [_vec_spec(1, d)] * 2,
        out_specs=[_row_spec(tm, d)] * 2 + [_vec_spec(1, d)] * 3 + [_vec_spec(1, 1)],
        out_shape=[jax.ShapeDtypeStruct((t, d), F32), jax.ShapeDtypeStruct((t, d), BF16)]
        + [jax.ShapeDtypeStruct((1, d), F32)] * 3 + [jax.ShapeDtypeStruct((1, 1), F32)],
        scratch_shapes=[pltpu.VMEM((SUBLANES, d), F32)] * 3,
        compiler_params=_cparams(_ARB), name=name)(x, y, target, gpost, gt)


def _bwd_mid(dx_new, dh, x_in, gpre, sc, y_prev, gpost_p, gt_p, *, dy_dtype, name):
    t, d = x_in.shape
    tm = _pick(t, (TM_ROW, 256, 128))
    n = t // tm

    def body(dxn_ref, dh_ref, x_ref, gpre_ref, sc_ref, y_ref, gpost_ref, gt_ref,
             dx_ref, dy_ref, dsh_ref, dsc_ref, dgpre_ref, dgpost_ref, dgt_ref, sdy_ref, a1, a2, aq, asd):
        i = pl.program_id(0)

        @pl.when(i == 0)
        def _():
            for a in (a1, a2, aq, asd):
                a[...] = jnp.zeros_like(a)

        dh_v = dh_ref[...]
        xn, r = _rms(x_ref[...])
        dx = dxn_ref[...] + _rms_bwd(dh_v * ((1.0 + sc_ref[...]) * gpre_ref[...]), xn, r)
        dx_ref[...] = dx
        a1[...] += _sum8(dh_v)
        a2[...] += _sum8(dh_v * xn)
        dy, dxyn = _post_bwd(dx, y_ref[...].astype(F32), gpost_ref[...], gt_ref[...])
        dy_ref[...] = dy.astype(dy_dtype)
        aq[...] += _sum8(dxyn)
        asd[...] += _sum8(dy)

        @pl.when(i == n - 1)
        def _():
            s2 = jnp.sum(a2[...], axis=0, keepdims=True)
            q = jnp.sum(aq[...], axis=0, keepdims=True)
            dsh_ref[...] = jnp.sum(a1[...], axis=0, keepdims=True)
            dsc_ref[...] = gpre_ref[...] * s2
            dgpre_ref[...] = (1.0 + sc_ref[...]) * s2
            dgpost_ref[...] = gt_ref[...] * q
            dgt_ref[...] = gpost_ref[...] * q
            sdy_ref[...] = jnp.sum(asd[...], axis=0, keepdims=True)

    return pl.pallas_call(
        body, grid=(n,),
        in_specs=[_row_spec(tm, d)] * 3 + [_vec_spec(1, d)] * 2 + [_row_spec(tm, d)] + [_vec_spec(1, d)] * 2,
        out_specs=[_row_spec(tm, d)] * 2 + [_vec_spec(1, d)] * 6,
        out_shape=[jax.ShapeDtypeStruct((t, d), F32), jax.ShapeDtypeStruct((t, d), dy_dtype)]
        + [jax.ShapeDtypeStruct((1, d), F32)] * 6,
        scratch_shapes=[pltpu.VMEM((SUBLANES, d), F32)] * 4,
        compiler_params=_cparams(_ARB), name=name)(dx_new, dh, x_in, gpre, sc, y_prev, gpost_p, gt_p)


def _bwd_first(dx_new, dh, x_in, gpre, sc, *, name):
    t, d = x_in.shape
    tm = _pick(t, (TM_ROW, 256, 128))
    n = t // tm

    def body(dxn_ref, dh_ref, x_ref, gpre_ref, sc_ref, dx_ref, dsh_ref, dsc_ref, dgpre_ref, a1, a2):
        i = pl.program_id(0)

        @pl.when(i == 0)
        def _():
            a1[...] = jnp.zeros_like(a1)
            a2[...] = jnp.zeros_like(a2)

        dh_v = dh_ref[...]
        xn, r = _rms(x_ref[...])
        dx_ref[...] = dxn_ref[...] + _rms_bwd(dh_v * ((1.0 + sc_ref[...]) * gpre_ref[...]), xn, r)
        a1[...] += _sum8(dh_v)
        a2[...] += _sum8(dh_v * xn)

        @pl.when(i == n - 1)
        def _():
            s2 = jnp.sum(a2[...], axis=0, keepdims=True)
            dsh_ref[...] = jnp.sum(a1[...], axis=0, keepdims=True)
            dsc_ref[...] = gpre_ref[...] * s2
            dgpre_ref[...] = (1.0 + sc_ref[...]) * s2

    return pl.pallas_call(
        body, grid=(n,),
        in_specs=[_row_spec(tm, d)] * 3 + [_vec_spec(1, d)] * 2,
        out_specs=[_row_spec(tm, d)] + [_vec_spec(1, d)] * 3,
        out_shape=[jax.ShapeDtypeStruct((t, d), F32)] + [jax.ShapeDtypeStruct((1, d), F32)] * 3,
        scratch_shapes=[pltpu.VMEM((SUBLANES, d), F32)] * 2,
        compiler_params=_cparams(_ARB), name=name)(dx_new, dh, x_in, gpre, sc)


def _conv3_rows(buf, w_ref, rows, first):
    out = buf[pl.ds(first, rows), :] * w_ref[pl.ds(0, 1), :]
    for k in (1, 2):
        out = out + buf[pl.ds(first + k, rows), :] * w_ref[pl.ds(k, 1), :]
    return out


ROWS_BLK = 16
COLS_BLK = 256


def _bcast_rows(dst, src_ref, first, nrows):
    for k in range(nrows):
        dst[first + k] = jnp.broadcast_to(src_ref[pl.ds(k, 1), :], dst.shape[1:])


def _shifted_rows(x, off, rows):
    if off % SUBLANES == 0:
        return x[off:off + rows]
    return pltpu.roll(x, x.shape[0] - off, axis=0)[:rows]


def _conv3_blk(buf, wb, first, rows, cols):
    base = first - first % SUBLANES
    window = buf[pl.ds(base, rows + SUBLANES), cols]
    xs = [_shifted_rows(window, first - base + k, rows) for k in range(3)]
    out = xs[0] * wb[0, pl.ds(0, rows), cols]
    out = out + xs[1] * wb[1, pl.ds(0, rows), cols]
    out = out + xs[2] * wb[2, pl.ds(0, rows), cols]
    return out + wb[3, pl.ds(0, rows), cols], xs


def _ffn_gate_fwd(u, w, b, *, name):
    t, f2 = u.shape
    f = f2 // 2
    tm = _pick(t, (TM_CONV,))
    hb = HALO_BF16
    rb = ROWS_BLK
    cw = _pick(f, (COLS_BLK, LANES))

    def body(u_ref, up_ref, w_ref, b_ref, a_ref, v_ref, buf, wb):
        i = pl.program_id(0)
        buf[pl.ds(hb, tm), :] = u_ref[...].astype(F32)
        buf[pl.ds(0, hb), :] = jnp.where(i > 0, up_ref[...].astype(F32), 0.0)
        _bcast_rows(wb, w_ref, 0, 3)
        _bcast_rows(wb, b_ref, 3, 1)
        for c0 in range(0, f, cw):
            gcols, vcols = pl.ds(c0, cw), pl.ds(f + c0, cw)
            for r0 in range(0, tm, rb):
                rows = pl.ds(r0, rb)
                vg, _ = _conv3_blk(buf, wb, hb - 2 + r0, rb, gcols)
                vv, _ = _conv3_blk(buf, wb, hb - 2 + r0, rb, vcols)
                v_ref[rows, gcols] = vg.astype(BF16)
                v_ref[rows, vcols] = vv.astype(BF16)
                a_ref[rows, gcols] = (vg * _sigmoid(vg) * vv).astype(BF16)

    return pl.pallas_call(
        body, grid=(t // tm,),
        in_specs=[_row_spec(tm, f2), _prev_spec(tm, hb, f2), _vec_spec(3, f2), _vec_spec(1, f2)],
        out_specs=[_row_spec(tm, f), _row_spec(tm, f2)],
        out_shape=[jax.ShapeDtypeStruct((t, f), BF16), jax.ShapeDtypeStruct((t, f2), BF16)],
        scratch_shapes=[pltpu.VMEM((tm + hb, f2), F32), pltpu.VMEM((4, rb, f2), F32)],
        compiler_params=_cparams(_PAR), name=name)(u, u, w, b)


def _ffn_gate_bwd(u, v, da, w, *, name):
    t, f2 = u.shape
    f = f2 // 2
    tm = _pick(t, (TM_CONV,))
    hb = HALO_BF16
    n = t // tm
    rb = ROWS_BLK
    cw = _pick(f, (COLS_BLK, LANES))
    blocks = [(r0, rb) for r0 in range(0, tm, rb)] + [(tm, hb)]

    def body(u_ref, v_ref, vn_ref, da_ref, dan_ref, w_ref, du_ref, dw_ref, db_ref, dvbuf, wb, wacc, bacc):
        i = pl.program_id(0)

        @pl.when(i == 0)
        def _():
            wacc[...] = jnp.zeros_like(wacc)
            bacc[...] = jnp.zeros_like(bacc)

        _bcast_rows(wb, w_ref, 0, 3)
        for c0 in range(0, f, cw):
            gcols, vcols = pl.ds(c0, cw), pl.ds(f + c0, cw)
            for r0, rows in blocks:
                if r0 < tm:
                    vg, vv = v_ref[pl.ds(r0, rows), gcols], v_ref[pl.ds(r0, rows), vcols]
                    dav = da_ref[pl.ds(r0, rows), gcols].astype(F32)
                else:
                    vg, vv = vn_ref[:, gcols], vn_ref[:, vcols]
                    dav = jnp.where(i < n - 1, dan_ref[:, gcols].astype(F32), 0.0)
                vg, vv = vg.astype(F32), vv.astype(F32)
                sg = _sigmoid(vg)
                dvg = dav * vv * (sg * (1.0 + vg * (1.0 - sg)))
                dvv = dav * (vg * sg)
                dvbuf[pl.ds(r0, rows), gcols] = dvg
                dvbuf[pl.ds(r0, rows), vcols] = dvv
                if r0 < tm:
                    bacc[:, gcols] += _sum8(dvg)
                    bacc[:, vcols] += _sum8(dvv)
        for c0 in range(0, f2, cw):
            cols = pl.ds(c0, cw)
            for r0 in range(0, tm, rb):
                uv = u_ref[pl.ds(r0, rb), cols].astype(F32)
                window = dvbuf[pl.ds(r0, rb + SUBLANES), cols]
                du = None
                for k in range(3):
                    dvk = _shifted_rows(window, 2 - k, rb)
                    term = dvk * wb[k, :, cols]
                    du = term if du is None else du + term
                    wacc[k, :, cols] += _sum8(uv * dvk)
                du_ref[pl.ds(r0, rb), cols] = du.astype(BF16)

        @pl.when(i == n - 1)
        def _():
            db_ref[...] = jnp.sum(bacc[...], axis=0, keepdims=True)
            dw_ref[...] = jnp.sum(wacc[...], axis=1)

    return pl.pallas_call(
        body, grid=(n,),
        in_specs=[_row_spec(tm, f2), _row_spec(tm, f2), _next_spec(tm, hb, f2, t),
                  _row_spec(tm, f), _next_spec(tm, hb, f, t), _vec_spec(3, f2)],
        out_specs=[_row_spec(tm, f2), _vec_spec(3, f2), _vec_spec(1, f2)],
        out_shape=[jax.ShapeDtypeStruct((t, f2), BF16), jax.ShapeDtypeStruct((3, f2), F32),
                   jax.ShapeDtypeStruct((1, f2), F32)],
        scratch_shapes=[pltpu.VMEM((tm + hb, f2), F32), pltpu.VMEM((3, rb, f2), F32),
                        pltpu.VMEM((3, SUBLANES, f2), F32), pltpu.VMEM((SUBLANES, f2), F32)],
        compiler_params=_cparams(_ARB), name=name)(u, v, v, da, da, w)


def _glu(u, b1, d):
    return (u[:, :d] + b1[:, :d]) * _sigmoid(u[:, d:] + b1[:, d:])


ROWS_TAPS = 32
ROWS_NORM = 16


def _fill_glu_buf(buf, u_ref, up_ref, b1_ref, i, tm, d):
    cw = _pick(d, (COLS_BLK, LANES))
    for c0 in range(0, d, cw):
        b1 = jnp.concatenate([b1_ref[:, pl.ds(c0, cw)], b1_ref[:, pl.ds(d + c0, cw)]], axis=1)
        up = jnp.concatenate([up_ref[:, pl.ds(c0, cw)], up_ref[:, pl.ds(d + c0, cw)]], axis=1)
        buf[pl.ds(0, HALO_A), pl.ds(c0, cw)] = jnp.where(i > 0, _glu(up, b1, cw), 0.0)
        for r0 in range(0, tm, ROWS_TAPS):
            rows = pl.ds(r0, ROWS_TAPS)
            uv = jnp.concatenate([u_ref[rows, pl.ds(c0, cw)], u_ref[rows, pl.ds(d + c0, cw)]], axis=1)
            buf[pl.ds(HALO_A + r0, ROWS_TAPS), pl.ds(c0, cw)] = _glu(uv, b1, cw)


def _taps31(buf, r0, cols, offs, use):
    nv = ROWS_TAPS // SUBLANES
    nrows = ROWS_TAPS + SUBLANES * (-(-max(offs) // SUBLANES))
    window = buf[pl.ds(r0, nrows), cols]
    shifted = {b: _shifted_rows(window, b, nrows - SUBLANES) if b else window
               for b in sorted({o % SUBLANES for o in offs})}
    for k, o in enumerate(offs):
        b, a = o % SUBLANES, o // SUBLANES
        use(k, [shifted[b][SUBLANES * (a + v):SUBLANES * (a + v + 1)] for v in range(nv)])


def _conv_taps_blk(buf, wb, r0, cols, offs):
    nv = ROWS_TAPS // SUBLANES
    acc = [None] * nv

    def use(k, rows):
        wk = wb[k, :, cols]
        for v in range(nv):
            term = rows[v] * wk
            acc[v] = term if acc[v] is None else acc[v] + term

    _taps31(buf, r0, cols, offs, use)
    return jnp.concatenate(acc, axis=0)


def _layernorm_parts(x):
    mu = jnp.mean(x, axis=-1, keepdims=True)
    xc = x - mu
    rstd = lax.rsqrt(jnp.mean(xc * xc, axis=-1, keepdims=True) + LN_EPS)
    return xc * rstd, rstd


_FWD_OFFS = tuple(HALO_A - (CONF_CONV_WIDTH - 1) + k for k in range(CONF_CONV_WIDTH))
_BWD_OFFS = tuple(CONF_CONV_WIDTH - 1 - k for k in range(CONF_CONV_WIDTH))


def _a_fwd(u1, b1, dww, dwb, lng, lnb, *, name):
    t, d2 = u1.shape
    d = d2 // 2
    tm = _pick(t, (TM_CONV,))

    def body(u_ref, up_ref, b1_ref, w_ref, wbias_ref, g_ref, bb_ref, o_ref, u3_ref, buf, wb):
        i = pl.program_id(0)
        _fill_glu_buf(buf, u_ref, up_ref, b1_ref, i, tm, d)
        _bcast_rows(wb, w_ref, 0, CONF_CONV_WIDTH)
        for c0 in range(0, d, LANES):
            cols = pl.ds(c0, LANES)
            for r0 in range(0, tm, ROWS_TAPS):
                u3_ref[pl.ds(r0, ROWS_TAPS), cols] = (_conv_taps_blk(buf, wb, r0, cols, _FWD_OFFS)
                                                      + wbias_ref[:, cols])
        for r0 in range(0, tm, ROWS_NORM):
            rows = pl.ds(r0, ROWS_NORM)
            xhat, _ = _layernorm_parts(u3_ref[rows, :])
            u4 = xhat * g_ref[...] + bb_ref[...]
            o_ref[rows, :] = (u4 * _sigmoid(u4)).astype(BF16)

    return pl.pallas_call(
        body, grid=(t // tm,),
        in_specs=[_row_spec(tm, d2), _prev_spec(tm, HALO_A, d2), _vec_spec(1, d2),
                  _vec_spec(CONF_CONV_WIDTH, d)] + [_vec_spec(1, d)] * 3,
        out_specs=[_row_spec(tm, d)] * 2,
        out_shape=[jax.ShapeDtypeStruct((t, d), BF16), jax.ShapeDtypeStruct((t, d), F32)],
        scratch_shapes=[pltpu.VMEM((tm + HALO_A, d), F32), pltpu.VMEM((CONF_CONV_WIDTH, SUBLANES, d), F32)],
        compiler_params=_cparams(_PAR), name=name)(u1, u1, b1, dww, dwb, lng, lnb)


def _a_bwd_norm(u3, du5, lng, lnb, *, name):
    t, d = u3.shape
    tm = _pick(t, (TM_ROW, 256, 128))
    n = t // tm

    def body(u3_ref, du5_ref, g_ref, bb_ref, du3_ref, dg_ref, db_ref, dwb_ref, ag, ab, aw):
        i = pl.program_id(0)

        @pl.when(i == 0)
        def _():
            for a in (ag, ab, aw):
                a[...] = jnp.zeros_like(a)

        g = g_ref[...]
        for r0 in range(0, tm, ROWS_NORM):
            rows = pl.ds(r0, ROWS_NORM)
            xhat, rstd = _layernorm_parts(u3_ref[rows, :])
            u4 = xhat * g + bb_ref[...]
            sg = _sigmoid(u4)
            du4 = du5_ref[rows, :] * (sg * (1.0 + u4 * (1.0 - sg)))
            dxh = du4 * g
            du3 = rstd * (dxh - jnp.mean(dxh, axis=-1, keepdims=True)
                          - xhat * jnp.mean(dxh * xhat, axis=-1, keepdims=True))
            du3_ref[rows, :] = du3
            ag[...] += _sum8(du4 * xhat)
            ab[...] += _sum8(du4)
            aw[...] += _sum8(du3)

        @pl.when(i == n - 1)
        def _():
            for a, o in ((ag, dg_ref), (ab, db_ref), (aw, dwb_ref)):
                o[...] = jnp.sum(a[...], axis=0, keepdims=True)

    return pl.pallas_call(
        body, grid=(n,),
        in_specs=[_row_spec(tm, d)] * 2 + [_vec_spec(1, d)] * 2,
        out_specs=[_row_spec(tm, d)] + [_vec_spec(1, d)] * 3,
        out_shape=[jax.ShapeDtypeStruct((t, d), F32)] + [jax.ShapeDtypeStruct((1, d), F32)] * 3,
        scratch_shapes=[pltpu.VMEM((SUBLANES, d), F32)] * 3,
        compiler_params=_cparams(_ARB), name=name)(u3, du5, lng, lnb)


def _a_bwd_conv(u1, du3, b1, dww, *, name):
    t, d2 = u1.shape
    d = d2 // 2
    tm = _pick(t, (TM_CONV,))
    n = t // tm
    kw = CONF_CONV_WIDTH
    nv = ROWS_TAPS // SUBLANES

    def body(u_ref, up_ref, g3_ref, g3n_ref, b1_ref, w_ref, du1_ref, dw_ref, db1_ref,
             buf, gbuf, wb, wacc, bacc):
        i = pl.program_id(0)

        @pl.when(i == 0)
        def _():
            wacc[...] = jnp.zeros_like(wacc)
            bacc[...] = jnp.zeros_like(bacc)

        _fill_glu_buf(buf, u_ref, up_ref, b1_ref, i, tm, d)
        _bcast_rows(wb, w_ref, 0, kw)
        gbuf[pl.ds(0, tm), :] = g3_ref[...]
        gbuf[pl.ds(tm, HALO_A), :] = jnp.where(i < n - 1, g3n_ref[...], 0.0)
        for c0 in range(0, d, LANES):
            cols, gcols = pl.ds(c0, LANES), pl.ds(d + c0, LANES)
            for r0 in range(0, tm, ROWS_TAPS):
                rows = pl.ds(r0, ROWS_TAPS)
                u2 = [buf[pl.ds(HALO_A + r0 + SUBLANES * v, SUBLANES), cols] for v in range(nv)]
                acc = [None] * nv

                def use(k, gs):
                    wk = wb[k, :, cols]
                    part = None
                    for v in range(nv):
                        term = gs[v] * wk
                        acc[v] = term if acc[v] is None else acc[v] + term
                        prod = u2[v] * gs[v]
                        part = prod if part is None else part + prod
                    wacc[k, :, cols] += part

                _taps31(gbuf, r0, cols, _BWD_OFFS, use)
                du2 = jnp.concatenate(acc, axis=0)
                av = u_ref[rows, cols] + b1_ref[:, cols]
                sg = _sigmoid(u_ref[rows, gcols] + b1_ref[:, gcols])
                da = du2 * sg
                dg = du2 * av * (sg * (1.0 - sg))
                du1_ref[rows, cols] = da.astype(BF16)
                du1_ref[rows, gcols] = dg.astype(BF16)
                bacc[:, cols] += _sum8(da)
                bacc[:, gcols] += _sum8(dg)

        @pl.when(i == n - 1)
        def _():
            dw_ref[...] = jnp.sum(wacc[...], axis=1)
            db1_ref[...] = jnp.sum(bacc[...], axis=0, keepdims=True)

    return pl.pallas_call(
        body, grid=(n,),
        in_specs=[_row_spec(tm, d2), _prev_spec(tm, HALO_A, d2), _row_spec(tm, d),
                  _next_spec(tm, HALO_A, d, t), _vec_spec(1, d2), _vec_spec(kw, d)],
        out_specs=[_row_spec(tm, d2), _vec_spec(kw, d), _vec_spec(1, d2)],
        out_shape=[jax.ShapeDtypeStruct((t, d2), BF16), jax.ShapeDtypeStruct((kw, d), F32),
                   jax.ShapeDtypeStruct((1, d2), F32)],
        scratch_shapes=[pltpu.VMEM((tm + HALO_A, d), F32), pltpu.VMEM((tm + HALO_A, d), F32),
                        pltpu.VMEM((kw, SUBLANES, d), F32),
                        pltpu.VMEM((kw, SUBLANES, d), F32), pltpu.VMEM((SUBLANES, d2), F32)],
        compiler_params=_cparams(_ARB), name=name)(u1, u1, du3, du3, b1, dww)


def _pool_counts(i, tm, w):
    pos = (i * tm + lax.broadcasted_iota(jnp.int32, (tm, 1), 0) + 1).astype(F32)
    return jnp.minimum(pos, float(w))


def _b_pool_fwd(h, *, name):
    t, d = h.shape
    gd = d // len(POOL_WINDOWS)
    tm = _pick(t, (TM_CONV,))
    hb = HALO_POOL

    def body(h_ref, hp_ref, o_ref, buf):
        i = pl.program_id(0)
        buf[pl.ds(0, hb), :] = jnp.where(i > 0, hp_ref[...], 0.0)
        buf[pl.ds(hb, tm), :] = h_ref[...]
        for g, w in enumerate(POOL_WINDOWS):
            cols = pl.ds(g * gd, gd)
            cur = buf[pl.ds(hb, tm), cols]
            s = cur
            for j in range(1, w):
                s = s + buf[pl.ds(hb - j, tm), cols]
            o_ref[:, cols] = (s / _pool_counts(i, tm, w) - cur).astype(BF16)

    return pl.pallas_call(
        body, grid=(t // tm,),
        in_specs=[_row_spec(tm, d), _prev_spec(tm, hb, d)],
        out_specs=_row_spec(tm, d), out_shape=jax.ShapeDtypeStruct((t, d), BF16),
        scratch_shapes=[pltpu.VMEM((tm + hb, d), F32)],
        compiler_params=_cparams(_PAR), name=name)(h, h)


def _b_pool_bwd(dp, *, name):
    t, d = dp.shape
    gd = d // len(POOL_WINDOWS)
    tm = _pick(t, (TM_CONV,))
    hb = HALO_POOL
    n = t // tm

    def body(dp_ref, dpn_ref, o_ref, buf):
        i = pl.program_id(0)
        for g, w in enumerate(POOL_WINDOWS):
            cols = pl.ds(g * gd, gd)
            buf[pl.ds(0, tm), cols] = dp_ref[:, cols] / _pool_counts(i, tm, w)
            buf[pl.ds(tm, hb), cols] = jnp.where(i < n - 1, dpn_ref[:, cols] * (1.0 / w), 0.0)
            s = buf[pl.ds(0, tm), cols]
            for j in range(1, w):
                s = s + buf[pl.ds(j, tm), cols]
            o_ref[:, cols] = s - dp_ref[:, cols]

    return pl.pallas_call(
        body, grid=(n,),
        in_specs=[_row_spec(tm, d), _next_spec(tm, hb, d, t)],
        out_specs=_row_spec(tm, d), out_shape=jax.ShapeDtypeStruct((t, d), F32),
        scratch_shapes=[pltpu.VMEM((tm + hb, d), F32)],
        compiler_params=_cparams(_PAR), name=name)(dp, dp)


def _b_affine_fwd(mixed, gb, scale, *, name):
    t, d = mixed.shape
    tm = _pick(t, (TM_ROW, 256, 128))

    def body(m_ref, gb_ref, s_ref, o_ref):
        o_ref[...] = ((m_ref[...] + gb_ref[...]) * s_ref[...]).astype(BF16)

    return pl.pallas_call(
        body, grid=(t // tm,), in_specs=[_row_spec(tm, d)] + [_vec_spec(1, d)] * 2,
        out_specs=_row_spec(tm, d), out_shape=jax.ShapeDtypeStruct((t, d), BF16),
        compiler_params=_cparams(_PAR), name=name)(mixed, gb, scale)


def _b_affine_bwd(dy, mixed, gb, scale, *, name):
    t, d = mixed.shape
    tm = _pick(t, (TM_ROW, 256, 128))
    n = t // tm

    def body(dy_ref, m_ref, gb_ref, s_ref, dm_ref, ds_ref, dgb_ref, a1, a2):
        i = pl.program_id(0)

        @pl.when(i == 0)
        def _():
            a1[...] = jnp.zeros_like(a1)
            a2[...] = jnp.zeros_like(a2)

        dy_v = dy_ref[...]
        dm_ref[...] = (dy_v * s_ref[...]).astype(BF16)
        a1[...] += _sum8(dy_v * (m_ref[...] + gb_ref[...]))
        a2[...] += _sum8(dy_v)

        @pl.when(i == n - 1)
        def _():
            ds_ref[...] = jnp.sum(a1[...], axis=0, keepdims=True)
            dgb_ref[...] = jnp.sum(a2[...], axis=0, keepdims=True) * s_ref[...]

    return pl.pallas_call(
        body, grid=(n,), in_specs=[_row_spec(tm, d)] * 2 + [_vec_spec(1, d)] * 2,
        out_specs=[_row_spec(tm, d)] + [_vec_spec(1, d)] * 2,
        out_shape=[jax.ShapeDtypeStruct((t, d), BF16)] + [jax.ShapeDtypeStruct((1, d), F32)] * 2,
        scratch_shapes=[pltpu.VMEM((SUBLANES, d), F32)] * 2,
        compiler_params=_cparams(_ARB), name=name)(dy, mixed, gb, scale)


def _c_gate_fwd(bcx, wc, *, name):
    t, d3 = bcx.shape
    d = d3 // 3
    tm = _pick(t, (TM_CONV,))
    hb = HALO_3

    rb = ROWS_BLK
    cw = _pick(d, (COLS_BLK, LANES))

    def body(x_ref, xp_ref, w_ref, z_ref, buf, wb):
        i = pl.program_id(0)
        _bcast_rows(wb, w_ref, 0, 3)
        wb[3] = jnp.zeros(wb.shape[1:], F32)
        for c0 in range(0, d, cw):
            cols, ccols, vcols = pl.ds(c0, cw), pl.ds(d + c0, cw), pl.ds(2 * d + c0, cw)
            buf[pl.ds(0, hb), cols] = jnp.where(i > 0, xp_ref[:, ccols] * xp_ref[:, vcols], 0.0)
            for r0 in range(0, tm, rb):
                rows = pl.ds(r0, rb)
                buf[pl.ds(hb + r0, rb), cols] = x_ref[rows, ccols] * x_ref[rows, vcols]
            for r0 in range(0, tm, rb):
                rows = pl.ds(r0, rb)
                q, _ = _conv3_blk(buf, wb, hb - 2 + r0, rb, cols)
                z_ref[rows, cols] = (x_ref[rows, cols] * q).astype(BF16)

    return pl.pallas_call(
        body, grid=(t // tm,),
        in_specs=[_row_spec(tm, d3), _prev_spec(tm, hb, d3), _vec_spec(3, d)],
        out_specs=_row_spec(tm, d), out_shape=jax.ShapeDtypeStruct((t, d), BF16),
        scratch_shapes=[pltpu.VMEM((tm + hb, d), F32), pltpu.VMEM((4, rb, d), F32)],
        compiler_params=_cparams(_PAR), name=name)(bcx, bcx, wc)


def _c_gate_bwd(bcx, dz, wc, *, name):
    t, d3 = bcx.shape
    d = d3 // 3
    tm = _pick(t, (TM_CONV,))
    hb = HALO_3
    n = t // tm

    rb = ROWS_BLK
    cw = _pick(d, (COLS_BLK, LANES))

    def body(x_ref, xp_ref, xn_ref, dz_ref, dzn_ref, w_ref, o_ref, dw_ref, pbuf, qbuf, wb, wacc):
        i = pl.program_id(0)

        @pl.when(i == 0)
        def _():
            wacc[...] = jnp.zeros_like(wacc)

        _bcast_rows(wb, w_ref, 0, 3)
        wb[3] = jnp.zeros(wb.shape[1:], F32)
        for c0 in range(0, d, cw):
            cols, ccols, vcols = pl.ds(c0, cw), pl.ds(d + c0, cw), pl.ds(2 * d + c0, cw)
            pbuf[pl.ds(0, hb), cols] = jnp.where(i > 0, xp_ref[:, ccols] * xp_ref[:, vcols], 0.0)
            qbuf[pl.ds(tm, hb), cols] = jnp.where(i < n - 1, dzn_ref[:, cols] * xn_ref[:, cols], 0.0)
            for r0 in range(0, tm, rb):
                rows = pl.ds(r0, rb)
                pbuf[pl.ds(hb + r0, rb), cols] = x_ref[rows, ccols] * x_ref[rows, vcols]
            for r0 in range(0, tm, rb):
                rows = pl.ds(r0, rb)
                q, _ = _conv3_blk(pbuf, wb, hb - 2 + r0, rb, cols)
                dz_v = dz_ref[rows, cols]
                qbuf[rows, cols] = dz_v * x_ref[rows, cols]
                o_ref[rows, cols] = (dz_v * q).astype(BF16)
            for r0 in range(0, tm, rb):
                rows = pl.ds(r0, rb)
                pv = pbuf[pl.ds(hb + r0, rb), cols]
                window = qbuf[pl.ds(r0, rb + SUBLANES), cols]
                dp = None
                for k in range(3):
                    dqk = _shifted_rows(window, 2 - k, rb)
                    term = dqk * wb[k, :, cols]
                    dp = term if dp is None else dp + term
                    wacc[k, :, cols] += _sum8(pv * dqk)
                o_ref[rows, ccols] = (dp * x_ref[rows, vcols]).astype(BF16)
                o_ref[rows, vcols] = (dp * x_ref[rows, ccols]).astype(BF16)

        @pl.when(i == n - 1)
        def _():
            dw_ref[...] = jnp.sum(wacc[...], axis=1)

    return pl.pallas_call(
        body, grid=(n,),
        in_specs=[_row_spec(tm, d3), _prev_spec(tm, hb, d3), _next_spec(tm, hb, d3, t),
                  _row_spec(tm, d), _next_spec(tm, hb, d, t), _vec_spec(3, d)],
        out_specs=[_row_spec(tm, d3), _vec_spec(3, d)],
        out_shape=[jax.ShapeDtypeStruct((t, d3), BF16), jax.ShapeDtypeStruct((3, d), F32)],
        scratch_shapes=[pltpu.VMEM((tm + hb, d), F32), pltpu.VMEM((tm + hb, d), F32),
                        pltpu.VMEM((4, rb, d), F32), pltpu.VMEM((3, SUBLANES, d), F32)],
        compiler_params=_cparams(_ARB), name=name)(bcx, bcx, bcx, dz, dz, wc)


def _row(v):
    return v.reshape(1, -1)


def _kind_of(j):
    return "f" if j % 2 else "abc"[(j // 2) % N_MIXERS]


BIG = ("a_pw1_w", "a_pw2_w", "b_group_w", "c_in_w", "c_out_w", "f_up_w", "f_down_w")
COL_SHARDED = ("a_pw1_w", "c_in_w", "f_up_w")


def _local_step(x, target, mod, p):
    nsub = 2 * DEPTH
    norm_names = (("norm_pre_mix", "norm_post_mix"), ("norm_pre_ffn", "norm_post_ffn"))
    gpre = [_row(p[norm_names[s][0]][i]) for i in range(DEPTH) for s in (0, 1)]
    gpost = [_row(p[norm_names[s][1]][i]) for i in range(DEPTH) for s in (0, 1)]
    sh = [_row(mod[i, 3 * s + 0]) for i in range(DEPTH) for s in (0, 1)]
    sc = [_row(mod[i, 3 * s + 1]) for i in range(DEPTH) for s in (0, 1)]
    gt = [_row(mod[i, 3 * s + 2]) for i in range(DEPTH) for s in (0, 1)]

    def h_dtype(j):
        return F32 if _kind_of(j) == "b" else BF16

    xs, hs, ys, saved = [x], [], [], []

    hs.append(_fwd_first(x, gpre[0], sc[0], sh[0], h_dtype=h_dtype(0), name="fwd_first"))
    for j in range(nsub):
        i, kind = j // 2, _kind_of(j)
        slot = i // N_MIXERS
        h = hs[j]
        tag = f"{kind}{j}"
        if kind == "f":
            u = _mm(h, p["f_up_w"], mode="nn", layer=i, out_dtype=BF16, name=f"ffn_up_{tag}")
            a, vpre = _ffn_gate_fwd(u, p["f_dw_w"][i], _row(p["f_dw_b"][i]), name=f"ffn_gate_{tag}")
            y = _mm(a, p["f_down_w"][i], mode="nn", out_dtype=BF16, name=f"ffn_down_{tag}")
            saved.append((u, a, vpre))
        elif kind == "a":
            u1 = _mm(h, p["a_pw1_w"], mode="nn", layer=slot, name=f"a_pw1_{tag}")
            u5, u3 = _a_fwd(u1, _row(p["a_pw1_b"][slot]), p["a_dw_w"][slot], _row(p["a_dw_b"][slot]),
                            _row(p["a_ln_g"][slot]), _row(p["a_ln_b"][slot]), name=f"a_conv_{tag}")
            y = _mm(u5, p["a_pw2_w"][slot], mode="nn", bias=_row(p["a_pw2_b"][slot]), out_dtype=BF16,
                    name=f"a_pw2_{tag}")
            saved.append((u1, u5, u3))
        elif kind == "b":
            pooled = _b_pool_fwd(h, name=f"b_pool_{tag}")
            mixed = _mm_group(pooled, p["b_group_w"][slot], mode="nn", name=f"b_mix_{tag}")
            y = _b_affine_fwd(mixed, _row(p["b_group_b"][slot]), _row(p["b_scale"][slot]), name=f"b_aff_{tag}")
            saved.append((pooled, mixed))
        else:
            bcx = _mm(h, p["c_in_w"], mode="nn", layer=slot, name=f"c_in_{tag}")
            z = _c_gate_fwd(bcx, p["c_conv_w"][slot], name=f"c_gate_{tag}")
            y = _mm(z, p["c_out_w"][slot], mode="nn", out_dtype=BF16, name=f"c_out_{tag}")
            saved.append((bcx, z))
        ys.append(y)
        if j + 1 < nsub:
            x_new, h_next = _fwd_mid(xs[j], y, gpost[j], gt[j], gpre[j + 1], sc[j + 1], sh[j + 1],
                                     h_dtype=h_dtype(j + 1), name=f"fwd_mid_{j}")
            xs.append(x_new)
            hs.append(h_next)

    n_of = {"a": len([i for i in range(DEPTH) if i % N_MIXERS == 0]),
            "b": len([i for i in range(DEPTH) if i % N_MIXERS == 1]),
            "c": len([i for i in range(DEPTH) if i % N_MIXERS == 2]), "f": DEPTH, "n": DEPTH}
    g = {k: [None] * n_of[k[0]] for k in p}
    dmod = [[None] * 6 for _ in range(DEPTH)]

    last = nsub - 1
    dx, dy, dgpost, dgt, sdy, loss = _last_fwd_bwd(xs[last], ys[last], target, gpost[last], gt[last],
                                                   name="loss_head")
    for j in range(last, -1, -1):
        i, kind = j // 2, _kind_of(j)
        slot = i // N_MIXERS
        sub = j % 2
        tag = f"{kind}{j}"
        g[norm_names[sub][1]][i] = dgpost
        dmod[i][3 * sub + 2] = dgt
        h = hs[j]
        if kind == "f":
            u, a, vpre = saved[j]
            da = _mm(dy, p["f_down_w"][i], mode="nt", out_dtype=BF16, name=f"ffn_dda_{tag}")
            g["f_down_w"][i] = _mm(a, dy, mode="tn", tk=TK_TOKENS, name=f"ffn_dwdown_{tag}")
            du, dw, db = _ffn_gate_bwd(u, vpre, da, p["f_dw_w"][i], name=f"ffn_gate_bwd_{tag}")
            g["f_dw_w"][i], g["f_dw_b"][i] = dw, db
            dh = _mm(du, p["f_up_w"], mode="nt", layer=i, name=f"ffn_ddh_{tag}")
            g["f_up_w"][i] = _mm(h, du, mode="tn", tk=TK_TOKENS, layer=i, name=f"ffn_dwup_{tag}")
        elif kind == "a":
            u1, u5, u3 = saved[j]
            g["a_pw2_b"][slot] = sdy
            du5 = _mm(dy, p["a_pw2_w"][slot], mode="nt", name=f"a_ddu5_{tag}")
            g["a_pw2_w"][slot] = _mm(u5, dy, mode="tn", tk=TK_TOKENS, name=f"a_dw2_{tag}")
            b1 = _row(p["a_pw1_b"][slot])
            du3, dlg, dlb, ddwb = _a_bwd_norm(u3, du5, _row(p["a_ln_g"][slot]), _row(p["a_ln_b"][slot]),
                                              name=f"a_bwd_norm_{tag}")
            g["a_ln_g"][slot], g["a_ln_b"][slot], g["a_dw_b"][slot] = dlg, dlb, ddwb
            du1, ddww, db1 = _a_bwd_conv(u1, du3, b1, p["a_dw_w"][slot], name=f"a_bwd_conv_{tag}")
            g["a_dw_w"][slot], g["a_pw1_b"][slot] = ddww, db1
            dh = _mm(du1, p["a_pw1_w"], mode="nt", layer=slot, name=f"a_ddh_{tag}")
            g["a_pw1_w"][slot] = _mm(h, du1, mode="tn", tk=TK_TOKENS, layer=slot, name=f"a_dw1_{tag}")
        elif kind == "b":
            pooled, mixed = saved[j]
            dmixed, dscale, dgb = _b_affine_bwd(dy, mixed, _row(p["b_group_b"][slot]), _row(p["b_scale"][slot]),
                                                name=f"b_aff_bwd_{tag}")
            g["b_scale"][slot], g["b_group_b"][slot] = dscale, dgb
            dpooled = _mm_group(dmixed, p["b_group_w"][slot], mode="nt", name=f"b_dpool_{tag}")
            g["b_group_w"][slot] = _mm_group(pooled, dmixed, mode="tn", tm=TK_TOKENS, name=f"b_dw_{tag}")
            dh = _b_pool_bwd(dpooled, name=f"b_pool_bwd_{tag}")
        else:
            bcx, z = saved[j]
            dz = _mm(dy, p["c_out_w"][slot], mode="nt", name=f"c_ddz_{tag}")
            g["c_out_w"][slot] = _mm(z, dy, mode="tn", tk=TK_TOKENS, name=f"c_dwout_{tag}")
            dbcx, dwc = _c_gate_bwd(bcx, dz, p["c_conv_w"][slot], name=f"c_gate_bwd_{tag}")
            g["c_conv_w"][slot] = dwc
            dh = _mm(dbcx, p["c_in_w"], mode="nt", layer=slot, name=f"c_ddh_{tag}")
            g["c_in_w"][slot] = _mm(h, dbcx, mode="tn", tk=TK_TOKENS, layer=slot, name=f"c_dwin_{tag}")
        if j > 0:
            pj = j - 1
            dy_dtype = F32 if _kind_of(pj) == "b" else BF16
            dx, dy, dsh, dsc, dgpre, dgpost, dgt, sdy = _bwd_mid(
                dx, dh, xs[j], gpre[j], sc[j], ys[pj], gpost[pj], gt[pj], dy_dtype=dy_dtype, name=f"bwd_mid_{j}")
        else:
            dx, dsh, dsc, dgpre = _bwd_first(dx, dh, xs[0], gpre[0], sc[0], name="bwd_first")
        dmod[i][3 * sub + 0] = dsh
        dmod[i][3 * sub + 1] = dsc
        g[norm_names[sub][0]][i] = dgpre

    small = {k: jnp.stack(v).reshape(p[k].shape) for k, v in g.items() if k not in BIG}
    big = {k: v for k, v in g.items() if k in BIG}
    dmod_arr = jnp.stack([jnp.concatenate(r, axis=0) for r in dmod])
    return loss, dx, dmod_arr, small, big


_MESH = pl.DeviceIdType.MESH
_ANY = pl.BlockSpec(memory_space=pl.ANY)
_VMEM = pl.BlockSpec(memory_space=pltpu.VMEM)


def _place():
    return lax.axis_index("x"), lax.axis_index("y"), lax.axis_index("c")


def _other_chips(x, y):
    return [(1 - x, y), (x, 1 - y), (1 - x, 1 - y)]


def _remote(src, dst, send_sem, recv_sem, to):
    return pltpu.make_async_remote_copy(src_ref=src, dst_ref=dst, send_sem=send_sem, recv_sem=recv_sem,
                                        device_id=to, device_id_type=_MESH)


def _all_gather8(blk, *, name):
    r, cdim = blk.shape

    def body(x_ref, out_ref, send_sems, recv_sems, local_sem):
        x, y, c = _place()
        me, sibling = (x, y, c), (x, y, 1 - c)
        chips = _other_chips(x, y)

        def slot(px, py, pc):
            return out_ref.at[4 * px + 2 * py + pc]

        def copy(k, block, to, src=None):
            return _remote(slot(*block) if src is None else src, slot(*block),
                           send_sems.at[k], recv_sems.at[k], to)

        mine = pltpu.make_async_copy(x_ref, slot(*me), local_sem)
        mine.start()
        first = [copy(0, me, sibling, src=x_ref)]
        first += [copy(1 + j, me, (*chip, c), src=x_ref) for j, chip in enumerate(chips)]
        for cp in first:
            cp.start()
        passed = [copy(4 + j, (*chip, c), sibling) for j, chip in enumerate(chips)]
        for j, chip in enumerate(chips):
            copy(1 + j, (*chip, c), me).wait_recv()
            passed[j].start()
        copy(0, sibling, me).wait_recv()
        for j, chip in enumerate(chips):
            copy(4 + j, (*chip, 1 - c), me).wait_recv()
        for cp in first + passed:
            cp.wait_send()
        mine.wait()

    return pl.pallas_call(
        body, out_shape=jax.ShapeDtypeStruct((NDEV, r, cdim), blk.dtype),
        in_specs=[_VMEM], out_specs=_VMEM,
        scratch_shapes=[pltpu.SemaphoreType.DMA((7,)), pltpu.SemaphoreType.DMA((7,)), pltpu.SemaphoreType.DMA],
        compiler_params=pltpu.CompilerParams(vmem_limit_bytes=VMEM_LIMIT), name=name)(blk)


def _gather_dst(kind):
    if kind == "col":
        return lambda s, h: (s, h)
    if kind == "row":
        return lambda s, h: (h, slice(None), s)
    return lambda s, h: (slice(None), s, h)


def _cast_into_gathered(src, kind, out_shape, shard, *, name):
    _, a, rh, cdim = src.shape
    tr = _pick(rh, (512, 256, 128, 64, 32, 16))
    if kind == "col":
        out_idx = lambda h, ai, r, s: (s[0], h, ai, r, 0)
    elif kind == "row":
        out_idx = lambda h, ai, r, s: (h, ai, s[0], r, 0)
    else:
        out_idx = lambda h, ai, r, s: (ai, s[0], h, r, 0)

    def body(s_ref, x_ref, o_ref):
        o_ref[...] = x_ref[...].astype(BF16)

    grid_spec = pltpu.PrefetchScalarGridSpec(
        num_scalar_prefetch=1, grid=(2, a, rh // tr),
        in_specs=[pl.BlockSpec((None, None, tr, cdim), lambda h, ai, r, s: (h, ai, r, 0))],
        out_specs=pl.BlockSpec((None, None, None, tr, cdim), out_idx))
    return pl.pallas_call(
        body, grid_spec=grid_spec, out_shape=jax.ShapeDtypeStruct(out_shape, BF16),
        compiler_params=_cparams(_PAR, _PAR, _PAR), name=name)(shard, src)


def _gather_weights(bufs, kinds, *, name):
    nt = len(bufs)

    def body(*refs):
        out_refs = refs[nt:2 * nt]
        send_sems, recv_sems = refs[2 * nt:]
        x, y, c = _place()
        sibling = (x, y, 1 - c)
        nbr_x, nbr_y = (1 - x, y, c), (x, 1 - y, c)
        s_me, s_x, s_y, s_d = 2 * x + y, 2 * (1 - x) + y, 2 * x + (1 - y), 2 * (1 - x) + (1 - y)

        def at(k, s, h):
            return out_refs[k].at[_gather_dst(kinds[k])(s, h)]

        def part(ref, k, q):
            rows = bufs[k].shape[3] // 2
            return ref.at[:, pl.ds(q * rows, rows), :]

        def copy(ref, k, col, to):
            return _remote(ref, ref, send_sems.at[k, col], recv_sems.at[k, col], to)

        started = []

        def start(cp):
            cp.start()
            started.append(cp)

        for k in range(nt):
            start(copy(at(k, s_me, c), k, 0, nbr_x))
            start(copy(at(k, s_me, c), k, 1, nbr_y))
        for k in range(nt):
            got_y, got_x = at(k, s_y, c), at(k, s_x, c)
            copy(got_y, k, 1, nbr_y).wait_recv()
            start(copy(part(got_y, k, 0), k, 2, nbr_x))
            start(copy(got_y, k, 5, sibling))
            copy(got_x, k, 0, nbr_x).wait_recv()
            start(copy(part(got_x, k, 1), k, 3, nbr_y))
            start(copy(got_x, k, 4, sibling))
        for k in range(nt):
            got_d = at(k, s_d, c)
            for q in (0, 1):
                copy(part(got_d, k, q), k, 2 + q, sibling).wait_recv()
                start(copy(part(got_d, k, q), k, 6 + q, sibling))
        for k in range(nt):
            copy(at(k, s_x, 1 - c), k, 4, sibling).wait_recv()
            copy(at(k, s_y, 1 - c), k, 5, sibling).wait_recv()
            for q in (0, 1):
                copy(part(at(k, s_d, 1 - c), k, q), k, 6 + q, sibling).wait_recv()
        for cp in started:
            cp.wait_send()

    return pl.pallas_call(
        body, out_shape=[jax.ShapeDtypeStruct(b.shape, BF16) for b in bufs],
        in_specs=[_ANY] * nt, out_specs=[_ANY] * nt, input_output_aliases={k: k for k in range(nt)},
        scratch_shapes=[pltpu.SemaphoreType.DMA((nt, 8)), pltpu.SemaphoreType.DMA((nt, 8))],
        name=name)(*bufs)


def _pair_exchange(gs, layers_of, *, name):
    n, nt = len(gs), len(layers_of)

    def body(*refs):
        g_refs, out_refs = refs[:n], refs[n:n + nt]
        send_sems, recv_sems = refs[n + nt:]
        x, y, c = _place()
        sibling = (x, y, 1 - c)
        copies = []
        for t, ks in enumerate(layers_of):
            for l, k in enumerate(ks):
                cp = _remote(g_refs[k].at[:, 1 - c], out_refs[t].at[l], send_sems.at[k], recv_sems.at[k], sibling)
                cp.start()
                copies.append(cp)
        for cp in copies:
            cp.wait()

    out_shape = [jax.ShapeDtypeStruct((len(ks), NSHARD) + gs[ks[0]].shape[2:], F32) for ks in layers_of]
    return pl.pallas_call(
        body, out_shape=out_shape, in_specs=[_ANY] * n, out_specs=[_ANY] * nt,
        scratch_shapes=[pltpu.SemaphoreType.DMA((n,)), pltpu.SemaphoreType.DMA((n,))],
        name=name)(*gs)


def _pair_sum(g, r1, s_acc, layer, half, *, name):
    _, _, rh, cdim = g.shape
    tr = _pick(rh, (256, 128, 176, 64, 32, 16))

    def body(half_ref, g_ref, r_ref, s_in_ref, o_ref):
        o_ref[...] = (g_ref[...] + r_ref[...]).astype(BF16)

    grid_spec = pltpu.PrefetchScalarGridSpec(
        num_scalar_prefetch=1, grid=(NSHARD, rh // tr),
        in_specs=[pl.BlockSpec((None, None, tr, cdim), lambda s, r, hf: (s, hf[0], r, 0)),
                  pl.BlockSpec((None, None, tr, cdim), lambda s, r, hf: (layer, s, r, 0)),
                  _ANY],
        out_specs=pl.BlockSpec((None, None, tr, cdim), lambda s, r, hf: (layer, s, r, 0)))
    return pl.pallas_call(
        body, grid_spec=grid_spec, out_shape=jax.ShapeDtypeStruct(s_acc.shape, BF16),
        input_output_aliases={3: 0},
        compiler_params=_cparams(_PAR, _PAR), name=name)(half, g, r1, s_acc)


def _chip_exchange(ss, *, name):
    nt = len(ss)

    def body(*refs):
        s_refs, out_refs, stage_refs = refs[:nt], refs[nt:2 * nt], refs[2 * nt:3 * nt]
        send_sems, recv_sems = refs[3 * nt:]
        x, y, c = _place()
        nbr_x, nbr_y = (1 - x, y, c), (x, 1 - y, c)
        s_x, s_y, s_d = 2 * (1 - x) + y, 2 * x + (1 - y), 2 * (1 - x) + (1 - y)

        def part(ref, t, q):
            rows = ss[t].shape[2] // 2
            return ref.at[:, pl.ds(q * rows, rows), :]

        def copy(src, dst, t, col, to):
            return _remote(src, dst, send_sems.at[t, col], recv_sems.at[t, col], to)

        started = []

        def start(cp):
            cp.start()
            started.append(cp)

        for t in range(nt):
            start(copy(s_refs[t].at[:, s_x], out_refs[t].at[0], t, 0, nbr_x))
            start(copy(s_refs[t].at[:, s_y], out_refs[t].at[1], t, 1, nbr_y))
            start(copy(part(s_refs[t].at[:, s_d], t, 0), stage_refs[t].at[0], t, 2, nbr_x))
            start(copy(part(s_refs[t].at[:, s_d], t, 1), stage_refs[t].at[1], t, 3, nbr_y))
        for t in range(nt):
            st0, st1 = stage_refs[t].at[0], stage_refs[t].at[1]
            copy(st0, st0, t, 2, nbr_x).wait_recv()
            start(copy(st0, part(out_refs[t].at[2], t, 0), t, 4, nbr_y))
            copy(st1, st1, t, 3, nbr_y).wait_recv()
            start(copy(st1, part(out_refs[t].at[2], t, 1), t, 5, nbr_x))
        for t in range(nt):
            copy(out_refs[t].at[0], out_refs[t].at[0], t, 0, nbr_x).wait_recv()
            copy(out_refs[t].at[1], out_refs[t].at[1], t, 1, nbr_y).wait_recv()
            for q in (0, 1):
                got = part(out_refs[t].at[2], t, q)
                copy(got, got, t, 4 + q, nbr_x).wait_recv()
        for cp in started:
            cp.wait_send()

    out_shape = [jax.ShapeDtypeStruct((3, s.shape[0]) + s.shape[2:], BF16) for s in ss]
    out_shape += [jax.ShapeDtypeStruct((2, s.shape[0], s.shape[2] // 2, s.shape[3]), BF16) for s in ss]
    return pl.pallas_call(
        body, out_shape=out_shape, in_specs=[_ANY] * nt, out_specs=[_ANY] * (2 * nt),
        scratch_shapes=[pltpu.SemaphoreType.DMA((nt, 6)), pltpu.SemaphoreType.DMA((nt, 6))],
        name=name)(*ss)[:nt]


def _chip_sum(s_t, r3_t, place, *, name):
    nl, _, rh, cdim = s_t.shape
    tr = _pick(rh, (256, 128, 176, 64, 32, 16))

    def body(pz, s_ref, r_ref, o_ref):
        acc = s_ref[...].astype(F32) + r_ref[0].astype(F32)
        o_ref[...] = (acc + r_ref[1].astype(F32)) + r_ref[2].astype(F32)

    grid_spec = pltpu.PrefetchScalarGridSpec(
        num_scalar_prefetch=1, grid=(nl, rh // tr),
        in_specs=[pl.BlockSpec((None, None, tr, cdim), lambda l, r, pz: (l, pz[0], r, 0)),
                  pl.BlockSpec((3, None, tr, cdim), lambda l, r, pz: (0, l, r, 0))],
        out_specs=pl.BlockSpec((None, None, tr, cdim), lambda l, r, pz: (l, pz[1], r, 0)))
    return pl.pallas_call(
        body, grid_spec=grid_spec, out_shape=jax.ShapeDtypeStruct((nl, 2, rh, cdim), F32),
        compiler_params=_cparams(_PAR, _PAR), name=name)(place, s_t, r3_t)


def _join_halves(reds, *, name):
    nt = len(reds)

    def body(*refs):
        out_refs = refs[nt:2 * nt]
        send_sems, recv_sems = refs[2 * nt:]
        x, y, c = _place()
        sibling = (x, y, 1 - c)
        copies = []
        for t in range(nt):
            cp = _remote(out_refs[t].at[:, c], out_refs[t].at[:, c], send_sems.at[t], recv_sems.at[t], sibling)
            cp.start()
            copies.append(cp)
        for t, cp in enumerate(copies):
            cp.wait_send()
            got = out_refs[t].at[:, 1 - c]
            _remote(got, got, send_sems.at[t], recv_sems.at[t], sibling).wait_recv()

    return pl.pallas_call(
        body, out_shape=[jax.ShapeDtypeStruct(r.shape, F32) for r in reds],
        in_specs=[_ANY] * nt, out_specs=[_ANY] * nt, input_output_aliases={t: t for t in range(nt)},
        scratch_shapes=[pltpu.SemaphoreType.DMA((nt,)), pltpu.SemaphoreType.DMA((nt,))],
        name=name)(*reds)


def _sum_devices(g, *, name):
    _, r, cdim = g.shape
    tr = _pick(r, (512, 256, 128, 64, 32, 16, 8))

    def body(g_ref, o_ref):
        acc = g_ref[0]
        for e in range(1, NDEV):
            acc = acc + g_ref[e]
        o_ref[...] = acc

    return pl.pallas_call(
        body, grid=(r // tr,), in_specs=[pl.BlockSpec((NDEV, tr, cdim), lambda i: (0, i, 0))],
        out_specs=pl.BlockSpec((tr, cdim), lambda i: (i, 0)),
        out_shape=jax.ShapeDtypeStruct((r, cdim), F32),
        compiler_params=_cparams(_PAR), name=name)(g)


def _mod_fwd(c_all, mod_w, mod_b_cols, *, name):
    nl, d, n = mod_w.shape
    ne = c_all.shape[0]
    tn = _pick(n, (768, 512, 384, 256, 128))

    def body(c_ref, w_ref, b_ref, o_ref):
        cv = c_ref[...]
        act = (cv * _sigmoid(cv)).astype(BF16)
        o_ref[...] = jnp.dot(act, w_ref[...].astype(BF16), preferred_element_type=F32) + b_ref[...]

    return pl.pallas_call(
        body, grid=(nl, n // tn),
        in_specs=[pl.BlockSpec((ne, d), lambda i, j: (0, 0)),
                  pl.BlockSpec((None, d, tn), lambda i, j: (i, 0, j)),
                  pl.BlockSpec((None, 1, tn), lambda i, j: (i, 0, j))],
        out_specs=pl.BlockSpec((None, ne, tn), lambda i, j: (i, 0, j)),
        out_shape=jax.ShapeDtypeStruct((nl, ne, n), F32),
        compiler_params=_cparams(_PAR, _PAR), name=name)(c_all, mod_w, mod_b_cols)


def _adam_math(w, g, m, v):
    m2 = ADAM_B1 * m + (1.0 - ADAM_B1) * g
    v2 = ADAM_B2 * v + (1.0 - ADAM_B2) * (g * g)
    m_hat = m2 / (1.0 - ADAM_B1 ** ADAM_STEP)
    v_hat = v2 / (1.0 - ADAM_B2 ** ADAM_STEP)
    delta = -ADAM_LR * (m_hat / (jnp.sqrt(v_hat) + ADAM_EPS) + ADAM_WD * w)
    return delta, m2, v2


def _adamw(w, g, m, v, *, name):
    rows, cdim = w.shape
    tr = _pick(rows, tuple(t for t in (512, 256, 128, 64, 32, 16, 8) if t * cdim <= 256 * 1024))

    def body(w_ref, g_ref, m_ref, v_ref, d_ref, mo_ref, vo_ref):
        d_ref[...], mo_ref[...], vo_ref[...] = _adam_math(w_ref[...], g_ref[...], m_ref[...], v_ref[...])

    spec = pl.BlockSpec((tr, cdim), lambda i: (i, 0))
    return pl.pallas_call(
        body, grid=(rows // tr,), in_specs=[spec] * 4, out_specs=[spec] * 3,
        out_shape=[jax.ShapeDtypeStruct((rows, cdim), F32)] * 3,
        compiler_params=_cparams(_PAR), name=name)(w, g, m, v)


def _mod_w_update(c_t, dmod, w, m, v, *, name):
    nl, d, n = w.shape
    ne = c_t.shape[1]
    tr = _pick(d, (128, 64, 32, 16, 8))

    def body(c_ref, dm_ref, w_ref, m_ref, v_ref, g_ref, d_ref, mo_ref, vo_ref):
        cv = c_ref[...]
        act = cv * _sigmoid(cv)
        dm = dm_ref[...]
        g = act[:, 0:1] * dm[0:1, :]
        for e in range(1, ne):
            g = g + act[:, e:e + 1] * dm[e:e + 1, :]
        g_ref[...] = g
        d_ref[...], mo_ref[...], vo_ref[...] = _adam_math(w_ref[...], g, m_ref[...], v_ref[...])

    big = pl.BlockSpec((None, tr, n), lambda i, r: (i, r, 0))
    return pl.pallas_call(
        body, grid=(nl, d // tr),
        in_specs=[pl.BlockSpec((tr, ne), lambda i, r: (r, 0)),
                  pl.BlockSpec((None, ne, n), lambda i, r: (i, 0, 0)), big, big, big],
        out_specs=[big] * 4, out_shape=[jax.ShapeDtypeStruct((nl, d, n), F32)] * 4,
        compiler_params=_cparams(_PAR, _PAR), name=name)(c_t, dmod, w, m, v)


PACK_ROWS = 256


def _pack(arrs):
    flat = jnp.concatenate([a.reshape(-1) for a in arrs])
    tile = PACK_ROWS * LANES
    pad = (-flat.shape[0]) % tile
    return jnp.pad(flat, (0, pad)).reshape(-1, LANES)


def _unpack(packed, shapes, lead=()):
    flat = packed.reshape(lead + (-1,))
    out, off = [], 0
    for shp in shapes:
        size = 1
        for s in shp:
            size *= s
        out.append(flat[..., off:off + size].reshape(lead + tuple(shp)))
        off += size
    return out


SMALL_SHARD_AXIS = {"a_pw1_b": 1, "a_dw_w": 2, "a_dw_b": 1, "a_ln_g": 1, "a_ln_b": 1, "a_pw2_b": 1,
                    "c_conv_w": 2, "f_dw_w": 2}
SMALL_REPLICATED = ("norm_pre_mix", "norm_post_mix", "norm_pre_ffn", "norm_post_ffn",
                    "b_group_b", "b_scale", "f_dw_b")
WEIGHT_ORDER = ("mod_w", "mod_b", "norm_pre_mix", "norm_post_mix", "norm_pre_ffn", "norm_post_ffn",
                "a_pw1_w", "a_pw1_b", "a_dw_w", "a_dw_b", "a_ln_g", "a_ln_b", "a_pw2_w", "a_pw2_b",
                "b_group_w", "b_group_b", "b_scale", "c_in_w", "c_conv_w", "c_out_w",
                "f_up_w", "f_dw_w", "f_dw_b", "f_down_w")


def _as_layers_rows_cols(name, w):
    if name == "b_group_w":
        return w.reshape(w.shape[1], w.shape[2], w.shape[3])
    return w


def _step(x, c, loss_target, w, m, v):
    xi, yi, ci = _place()
    shard = 2 * xi + yi
    example = 4 * xi + 2 * yi + ci
    d = x.shape[-1]

    small_names = tuple(SMALL_SHARD_AXIS)
    gathered0 = _all_gather8(_pack([c] + [w[k] for k in small_names]), name="gather_small")
    parts = _unpack(gathered0, [c.shape] + [w[k].shape for k in small_names], lead=(NDEV,))
    c_all = parts[0].reshape(NDEV, d)
    p = {}
    for k, part in zip(small_names, parts[1:]):
        p[k] = jnp.concatenate([part[2 * s] for s in range(NSHARD)], axis=SMALL_SHARD_AXIS[k])
    for k in SMALL_REPLICATED:
        p[k] = w[k]

    ncol = w["mod_w"].shape[2]
    mod_b_cols = lax.dynamic_slice_in_dim(w["mod_b"], shard * ncol, ncol, axis=1).reshape(DEPTH, 1, ncol)
    mod_part = _mod_fwd(c_all, w["mod_w"], mod_b_cols, name="mod_fwd")
    gathered1 = _all_gather8(mod_part.reshape(DEPTH * NDEV, ncol), name="gather_mod")
    mod_all = gathered1.reshape(NSHARD, 2, DEPTH, NDEV, ncol)[:, 0]
    mod_mine = lax.dynamic_index_in_dim(mod_all, example, axis=2, keepdims=False)
    mod = jnp.transpose(mod_mine, (1, 0, 2)).reshape(DEPTH, 6, d)

    shard_arr = shard.reshape(1).astype(jnp.int32)
    bufs, kinds = [], []
    for k in BIG:
        wk = _as_layers_rows_cols(k, w[k])
        nl, r, cdim = wk.shape
        if nl >= 2:
            a, rh = nl // 2, r
        else:
            a, rh = 1, r // 2
        if k in COL_SHARDED:
            kind, out_shape = "col", (NSHARD, 2, a, rh, cdim)
        elif nl >= 2:
            kind, out_shape = "row", (2, a, NSHARD, rh, cdim)
        else:
            kind, out_shape = "row1", (1, NSHARD, 2, rh, cdim)
        kinds.append(kind)
        bufs.append(_cast_into_gathered(wk.reshape(2, a, rh, cdim), kind, out_shape, shard_arr, name=f"cast_{k}"))
    full = _gather_weights(bufs, kinds, name="gather_weights")
    for k, f in zip(BIG, full):
        nl, r, cdim = _as_layers_rows_cols(k, w[k]).shape
        if k in COL_SHARDED:
            p[k] = f.reshape(NSHARD, nl, r, cdim)
        elif k == "b_group_w":
            p[k] = f.reshape(1, nl, NSHARD * r, cdim)
        else:
            p[k] = f.reshape(nl, NSHARD * r, cdim)

    loss, grad_x, dmod, small, big = _local_step(x[0], loss_target[0], mod, p)

    gs, layers_of = [], []
    for k in BIG:
        ks = []
        for g in big[k]:
            if k == "b_group_w":
                ng, rr, cc = g.shape
                g = jnp.transpose(g.reshape(ng, NSHARD, rr // NSHARD, cc), (1, 0, 2, 3)).reshape(NSHARD, -1, cc)
            elif k not in COL_SHARDED:
                g = g.reshape(NSHARD, g.shape[0] // NSHARD, g.shape[1])
            ks.append(len(gs))
            gs.append(g.reshape(NSHARD, 2, g.shape[1] // 2, g.shape[2]))
        layers_of.append(ks)
    half = ci.reshape(1).astype(jnp.int32)
    place = jnp.stack([shard, ci]).astype(jnp.int32)
    r1 = _pair_exchange(gs, layers_of, name="grad_pair_exchange")
    ss = []
    for k, ks, r1_t in zip(BIG, layers_of, r1):
        s_t = lax.empty(r1_t.shape, BF16)
        for l, i in enumerate(ks):
            s_t = _pair_sum(gs[i], r1_t, s_t, l, half, name=f"grad_pair_sum_{k}_{l}")
        ss.append(s_t)
    r3 = _chip_exchange(ss, name="grad_chip_exchange")
    reds = [_chip_sum(s_t, r3_t, place, name=f"grad_chip_sum_{k}") for k, s_t, r3_t in zip(BIG, ss, r3)]
    joined = _join_halves(reds, name="grad_join_halves")
    grads = {k: j.reshape(w[k].shape) for k, j in zip(BIG, joined)}

    rep_names = SMALL_REPLICATED
    small_list = [small[k] for k in rep_names] + [small[k] for k in small_names] + [dmod]
    gathered2 = _all_gather8(_pack(small_list), name="gather_small_grads")
    summed = _sum_devices(gathered2, name="sum_small_grads")
    shapes = [s.shape for s in small_list]
    sums = _unpack(summed, shapes)
    for k, s in zip(rep_names, sums[:len(rep_names)]):
        grads[k] = s
    for k, s in zip(small_names, sums[len(rep_names):-1]):
        ax = SMALL_SHARD_AXIS[k]
        grads[k] = lax.dynamic_slice_in_dim(s, shard * w[k].shape[ax], w[k].shape[ax], axis=ax)
    grads["mod_b"] = sums[-1].reshape(w["mod_b"].shape)
    dmod_all = _unpack(gathered2, shapes, lead=(NDEV,))[-1].reshape(NDEV, DEPTH, NSHARD, ncol)
    dmod_cols = jnp.transpose(lax.dynamic_index_in_dim(dmod_all, shard, axis=2, keepdims=False), (1, 0, 2))

    delta, new_m, new_v = {}, {}, {}
    grads["mod_w"], delta["mod_w"], new_m["mod_w"], new_v["mod_w"] = _mod_w_update(
        c_all.T, dmod_cols, w["mod_w"], m["mod_w"], v["mod_w"], name="mod_w_update")
    for k in BIG:
        cdim = w[k].shape[-1]
        outs = _adamw(*[t.reshape(-1, cdim) for t in (w[k], grads[k], m[k], v[k])], name=f"adamw_{k}")
        delta[k], new_m[k], new_v[k] = [o.reshape(w[k].shape) for o in outs]
    rest = ("mod_b",) + rep_names + small_names
    packs = [_pack([t[k] for k in rest]) for t in (w, grads, m, v)]
    outs = _adamw(*packs, name="adamw_small")
    rest_shapes = [w[k].shape for k in rest]
    for dst, o in zip((delta, new_m, new_v), outs):
        for k, t in zip(rest, _unpack(o, rest_shapes)):
            dst[k] = t

    loss_all = lax.psum(loss[0, 0], ("x", "y", "c"))
    return (loss_all, grad_x[None], *[grads[k] for k in WEIGHT_ORDER], *[delta[k] for k in WEIGHT_ORDER],
            *[new_m[k] for k in WEIGHT_ORDER], *[new_v[k] for k in WEIGHT_ORDER])


def kernel(x, c, mod_w, mod_b, norm_pre_mix, norm_post_mix, norm_pre_ffn, norm_post_ffn, a_pw1_w, a_pw1_b, a_dw_w, a_dw_b, a_ln_g, a_ln_b, a_pw2_w, a_pw2_b, b_group_w, b_group_b, b_scale, c_in_w, c_conv_w, c_out_w, f_up_w, f_dw_w, f_dw_b, f_down_w, loss_target, m_mod_w, m_mod_b, m_norm_pre_mix, m_norm_post_mix, m_norm_pre_ffn, m_norm_post_ffn, m_a_pw1_w, m_a_pw1_b, m_a_dw_w, m_a_dw_b, m_a_ln_g, m_a_ln_b, m_a_pw2_w, m_a_pw2_b, m_b_group_w, m_b_group_b, m_b_scale, m_c_in_w, m_c_conv_w, m_c_out_w, m_f_up_w, m_f_dw_w, m_f_dw_b, m_f_down_w, v_mod_w, v_mod_b, v_norm_pre_mix, v_norm_post_mix, v_norm_pre_ffn, v_norm_post_ffn, v_a_pw1_w, v_a_pw1_b, v_a_dw_w, v_a_dw_b, v_a_ln_g, v_a_ln_b, v_a_pw2_w, v_a_pw2_b, v_b_group_w, v_b_group_b, v_b_scale, v_c_in_w, v_c_conv_w, v_c_out_w, v_f_up_w, v_f_dw_w, v_f_dw_b, v_f_down_w):
    given = dict(locals())
    w = {k: given[k] for k in WEIGHT_ORDER}
    m = {k: given["m_" + k] for k in WEIGHT_ORDER}
    v = {k: given["v_" + k] for k in WEIGHT_ORDER}
    return _step(x, c, loss_target, w, m, v)
```

```python
import functools

import jax
import jax.numpy as jnp
from jax import lax
from jax.experimental import pallas as pl
from jax.experimental.pallas import tpu as pltpu

F32 = jnp.float32
BF16 = jnp.bfloat16

DEPTH = 4
N_MIXERS = 3
CONF_CONV_WIDTH = 31
POOL_WINDOWS = (2, 4, 8, 16)
RMS_EPS = 1e-6
LN_EPS = 1e-5
ADAM_LR = 0.001
ADAM_B1 = 0.9
ADAM_B2 = 0.999
ADAM_EPS = 1e-08
ADAM_WD = 0.01
ADAM_STEP = 10

TM_ROW = 512
TM_CONV = 128
TK_TOKENS = 2048
TM_MM = 1024
TN_MM = 1024
SUBLANES = 8
LANES = 128
NSHARD = 4
NDEV = 8
HALO_A = 32
HALO_POOL = 16
HALO_3 = 8
HALO_BF16 = 16
VMEM_LIMIT = 56 * 1024 * 1024

_PAR = "parallel"
_ARB = "arbitrary"


def _cparams(*sem):
    return pltpu.CompilerParams(dimension_semantics=sem, vmem_limit_bytes=VMEM_LIMIT)


def _pick(n, prefs):
    for p in prefs:
        if p <= n and n % p == 0:
            return p
    return n


def _row_spec(tm, width):
    return pl.BlockSpec((tm, width), lambda i: (i, 0))


def _vec_spec(rows, width):
    return pl.BlockSpec((rows, width), lambda i: (0, 0))


def _prev_spec(tm, hb, width):
    return pl.BlockSpec((hb, width), lambda i: (jnp.maximum(i * (tm // hb) - 1, 0), 0))


def _next_spec(tm, hb, width, total):
    last = total // hb - 1
    return pl.BlockSpec((hb, width), lambda i: (jnp.minimum((i + 1) * (tm // hb), last), 0))


def _sum8(v):
    r, c = v.shape
    return jnp.sum(v.reshape(r // SUBLANES, SUBLANES, c), axis=0)


def _rms(x):
    r = lax.rsqrt(jnp.mean(x * x, axis=-1, keepdims=True) + RMS_EPS)
    return x * r, r


def _rms_bwd(dy, xn, r):
    return r * (dy - xn * jnp.mean(dy * xn, axis=-1, keepdims=True))


def _sigmoid(x):
    return 1.0 / (1.0 + jnp.exp(-x))


_DIMS = {"nn": ((1,), (0,)), "nt": ((1,), (1,)), "tn": ((0,), (0,))}


def _mm(a, b, *, mode, name, out_dtype=F32, bias=None, tm=TM_MM, tn=TN_MM, tk=None, layer=None):
    sharded = layer is not None
    if mode == "nn":
        m, k = a.shape
        n = NSHARD * b.shape[3] if sharded else b.shape[1]
    elif mode == "nt":
        m, k = a.shape
        n = b.shape[2] if sharded else b.shape[0]
    else:
        (k, m), (_, n) = a.shape, b.shape
    ns = n // NSHARD
    ks = k // NSHARD
    tm = _pick(m, (tm, 1408, 512, 256, 128))
    if sharded and mode != "nt":
        tn = _pick(ns, (1408, 768, 512, 256, 128))
    else:
        tn = _pick(n, (tn, 1408, 512, 256, 128))
    if sharded and mode == "nt":
        tk = _pick(ks, (1408, 768, 512, 256, 128))
    else:
        tk = _pick(k, (tk or k, 2816, 1024, 512, 256, 128))
    nk = k // tk
    per_n = ns // tn if sharded and mode != "nt" else 1
    per_k = ks // tk if sharded and mode == "nt" else 1
    dims = (_DIMS[mode], ((), ()))

    def split(idx, per):
        return (idx, 0) if per == 1 else (idx // per, idx % per)

    def body(*refs):
        a_ref, b_ref = refs[0], refs[1]
        bias_ref = refs[2] if bias is not None else None
        o_ref = refs[3] if bias is not None else refs[2]
        part = lax.dot_general(a_ref[...].astype(BF16), b_ref[...].astype(BF16), dims,
                               preferred_element_type=F32)

        def finish(r):
            if bias_ref is not None:
                r = r + bias_ref[...]
            o_ref[...] = r.astype(out_dtype)

        if nk == 1:
            finish(part)
        else:
            acc_ref = refs[-1]
            kk = pl.program_id(2)

            @pl.when(kk == 0)
            def _():
                acc_ref[...] = part

            @pl.when(kk > 0)
            def _():
                acc_ref[...] += part

            @pl.when(kk == nk - 1)
            def _():
                finish(acc_ref[...])

    out_spec = pl.BlockSpec((tm, tn), lambda i, j, kk: (i, j))
    out_shape = jax.ShapeDtypeStruct((m, n), out_dtype)
    if mode == "nn":
        a_spec = pl.BlockSpec((tm, tk), lambda i, j, kk: (i, kk))
        if sharded:
            b_spec = pl.BlockSpec((None, None, tk, tn),
                                  lambda i, j, kk: (split(j, per_n)[0], layer, kk, split(j, per_n)[1]))
        else:
            b_spec = pl.BlockSpec((tk, tn), lambda i, j, kk: (kk, j))
    elif mode == "nt":
        a_spec = pl.BlockSpec((tm, tk), lambda i, j, kk: (i, kk))
        if sharded:
            b_spec = pl.BlockSpec((None, None, tn, tk),
                                  lambda i, j, kk: (split(kk, per_k)[0], layer, j, split(kk, per_k)[1]))
        else:
            b_spec = pl.BlockSpec((tn, tk), lambda i, j, kk: (j, kk))
    else:
        a_spec = pl.BlockSpec((tk, tm), lambda i, j, kk: (kk, i))
        b_spec = pl.BlockSpec((tk, tn), lambda i, j, kk: (kk, j))
        if sharded:
            out_spec = pl.BlockSpec((None, tm, tn), lambda i, j, kk: (split(j, per_n)[0], i, split(j, per_n)[1]))
            out_shape = jax.ShapeDtypeStruct((NSHARD, m, ns), out_dtype)
    in_specs = [a_spec, b_spec]
    args = [a, b]
    if bias is not None:
        in_specs.append(pl.BlockSpec((1, tn), lambda i, j, kk: (0, j)))
        args.append(bias)
    return pl.pallas_call(
        body, grid=(m // tm, n // tn, nk), in_specs=in_specs, out_specs=out_spec, out_shape=out_shape,
        scratch_shapes=[pltpu.VMEM((tm, tn), F32)] if nk > 1 else [],
        compiler_params=_cparams(_PAR, _PAR, _ARB), name=name)(*args)


def _mm_group(a, b, *, mode, name, out_dtype=F32, tm=2048):
    t = a.shape[0]
    tm = _pick(t, (tm, 1024, 512, 256, 128))
    nt_ = t // tm
    g = len(POOL_WINDOWS)
    gd = a.shape[1] // g
    dims = (_DIMS[mode], ((), ()))

    if mode == "tn":
        def body(a_ref, b_ref, o_ref, acc_ref):
            kk = pl.program_id(1)
            part = lax.dot_general(a_ref[...].astype(BF16), b_ref[...].astype(BF16), dims,
                                   preferred_element_type=F32)

            @pl.when(kk == 0)
            def _():
                acc_ref[...] = part

            @pl.when(kk > 0)
            def _():
                acc_ref[...] += part

            @pl.when(kk == nt_ - 1)
            def _():
                o_ref[...] = acc_ref[...].astype(out_dtype)

        return pl.pallas_call(
            body, grid=(g, nt_),
            in_specs=[pl.BlockSpec((tm, gd), lambda gi, kk: (kk, gi)),
                      pl.BlockSpec((tm, gd), lambda gi, kk: (kk, gi))],
            out_specs=pl.BlockSpec((None, gd, gd), lambda gi, kk: (gi, 0, 0)),
            out_shape=jax.ShapeDtypeStruct((g, gd, gd), out_dtype),
            scratch_shapes=[pltpu.VMEM((gd, gd), F32)],
            compiler_params=_cparams(_PAR, _ARB), name=name)(a, b)

    def body(a_ref, b_ref, o_ref):
        o_ref[...] = lax.dot_general(a_ref[...].astype(BF16), b_ref[...].astype(BF16), dims,
                                     preferred_element_type=F32).astype(out_dtype)

    return pl.pallas_call(
        body, grid=(nt_, g),
        in_specs=[pl.BlockSpec((tm, gd), lambda i, gi: (i, gi)),
                  pl.BlockSpec((None, gd, gd), lambda i, gi: (gi, 0, 0))],
        out_specs=pl.BlockSpec((tm, gd), lambda i, gi: (i, gi)),
        out_shape=jax.ShapeDtypeStruct((t, g * gd), out_dtype),
        compiler_params=_cparams(_PAR, _PAR), name=name)(a, b)


def _pre(x, gpre, sc, sh):
    xn, r = _rms(x)
    return (xn * gpre) * (1.0 + sc) + sh, xn, r


def _fwd_first(x, gpre, sc, sh, *, h_dtype, name):
    t, d = x.shape
    tm = _pick(t, (TM_ROW, 256, 128))

    def body(x_ref, gpre_ref, sc_ref, sh_ref, h_ref):
        h, _, _ = _pre(x_ref[...], gpre_ref[...], sc_ref[...], sh_ref[...])
        h_ref[...] = h.astype(h_dtype)

    return pl.pallas_call(
        body, grid=(t // tm,),
        in_specs=[_row_spec(tm, d)] + [_vec_spec(1, d)] * 3,
        out_specs=_row_spec(tm, d), out_shape=jax.ShapeDtypeStruct((t, d), h_dtype),
        compiler_params=_cparams(_PAR), name=name)(x, gpre, sc, sh)


def _fwd_mid(x, y, gpost, gt, gpre, sc, sh, *, h_dtype, name):
    t, d = x.shape
    tm = _pick(t, (TM_ROW, 256, 128))

    def body(x_ref, y_ref, gpost_ref, gt_ref, gpre_ref, sc_ref, sh_ref, xn_ref, h_ref):
        yn, _ = _rms(y_ref[...].astype(F32))
        x_new = x_ref[...] + gt_ref[...] * (yn * gpost_ref[...])
        xn_ref[...] = x_new
        h, _, _ = _pre(x_new, gpre_ref[...], sc_ref[...], sh_ref[...])
        h_ref[...] = h.astype(h_dtype)

    return pl.pallas_call(
        body, grid=(t // tm,),
        in_specs=[_row_spec(tm, d)] * 2 + [_vec_spec(1, d)] * 5,
        out_specs=[_row_spec(tm, d)] * 2,
        out_shape=[jax.ShapeDtypeStruct((t, d), F32), jax.ShapeDtypeStruct((t, d), h_dtype)],
        compiler_params=_cparams(_PAR), name=name)(x, y, gpost, gt, gpre, sc, sh)


def _post_bwd(dx, y, gpost, gt):
    yn, r2 = _rms(y)
    dyn = dx * (gt * gpost)
    dy = _rms_bwd(dyn, yn, r2)
    return dy, dx * yn


def _last_fwd_bwd(x, y, target, gpost, gt, *, name):
    t, d = x.shape
    tm = _pick(t, (TM_ROW, 256, 128))
    n = t // tm

    def body(x_ref, y_ref, tg_ref, gpost_ref, gt_ref, dx_ref, dy_ref, dgpost_ref, dgt_ref, sdy_ref,
             loss_ref, qa, sa, la):
        i = pl.program_id(0)

        @pl.when(i == 0)
        def _():
            qa[...] = jnp.zeros_like(qa)
            sa[...] = jnp.zeros_like(sa)
            la[...] = jnp.zeros_like(la)

        yv = y_ref[...].astype(F32)
        yn, r2 = _rms(yv)
        gt_v, gpost_v = gt_ref[...], gpost_ref[...]
        err = x_ref[...] + gt_v * (yn * gpost_v) - tg_ref[...]
        la[...] += _sum8(err * err)
        dx = err * (1.0 / d)
        dx_ref[...] = dx
        dy = _rms_bwd(dx * (gt_v * gpost_v), yn, r2)
        dy_ref[...] = dy.astype(dy_ref.dtype)
        qa[...] += _sum8(dx * yn)
        sa[...] += _sum8(dy)

        @pl.when(i == n - 1)
        def _():
            q = jnp.sum(qa[...], axis=0, keepdims=True)
            dgpost_ref[...] = gt_v * q
            dgt_ref[...] = gpost_v * q
            sdy_ref[...] = jnp.sum(sa[...], axis=0, keepdims=True)
            tot = jnp.sum(jnp.sum(la[...], axis=0, keepdims=True), axis=1, keepdims=True)
            loss_ref[...] = tot * (0.5 / d)

    return pl.pallas_call(
        body, grid=(n,),
        in_specs=[_row_spec(tm, d)] * 3 + [_vec_spec(1, d)] * 2,
        out_specs=[_row_spec(tm, d)] * 2 + [_vec_spec(1, d)] * 3 + [_vec_spec(1, 1)],
        out_shape=[jax.ShapeDtypeStruct((t, d), F32), jax.ShapeDtypeStruct((t, d), BF16)]
        + [jax.ShapeDtypeStruct((1, d), F32)] * 3 + [jax.ShapeDtypeStruct((1, 1), F32)],
        scratch_shapes=[pltpu.VMEM((SUBLANES, d), F32)] * 3,
        compiler_params=_cparams(_ARB), name=name)(x, y, target, gpost, gt)


def _bwd_mid(dx_new, dh, x_in, gpre, sc, y_prev, gpost_p, gt_p, *, dy_dtype, name):
    t, d = x_in.shape
    tm = _pick(t, (TM_ROW, 256, 128))
    n = t // tm

    def body(dxn_ref, dh_ref, x_ref, gpre_ref, sc_ref, y_ref, gpost_ref, gt_ref,
             dx_ref, dy_ref, dsh_ref, dsc_ref, dgpre_ref, dgpost_ref, dgt_ref, sdy_ref, a1, a2, aq, asd):
        i = pl.program_id(0)

        @pl.when(i == 0)
        def _():
            for a in (a1, a2, aq, asd):
                a[...] = jnp.zeros_like(a)

        dh_v = dh_ref[...]
        xn, r = _rms(x_ref[...])
        dx = dxn_ref[...] + _rms_bwd(dh_v * ((1.0 + sc_ref[...]) * gpre_ref[...]), xn, r)
        dx_ref[...] = dx
        a1[...] += _sum8(dh_v)
        a2[...] += _sum8(dh_v * xn)
        dy, dxyn = _post_bwd(dx, y_ref[...].astype(F32), gpost_ref[...], gt_ref[...])
        dy_ref[...] = dy.astype(dy_dtype)
        aq[...] += _sum8(dxyn)
        asd[...] += _sum8(dy)

        @pl.when(i == n - 1)
        def _():
            s2 = jnp.sum(a2[...], axis=0, keepdims=True)
            q = jnp.sum(aq[...], axis=0, keepdims=True)
            dsh_ref[...] = jnp.sum(a1[...], axis=0, keepdims=True)
            dsc_ref[...] = gpre_ref[...] * s2
            dgpre_ref[...] = (1.0 + sc_ref[...]) * s2
            dgpost_ref[...] = gt_ref[...] * q
            dgt_ref[...] = gpost_ref[...] * q
            sdy_ref[...] = jnp.sum(asd[...], axis=0, keepdims=True)

    return pl.pallas_call(
        body, grid=(n,),
        in_specs=[_row_spec(tm, d)] * 3 + [_vec_spec(1, d)] * 2 + [_row_spec(tm, d)] + [_vec_spec(1, d)] * 2,
        out_specs=[_row_spec(tm, d)] * 2 + [_vec_spec(1, d)] * 6,
        out_shape=[jax.ShapeDtypeStruct((t, d), F32), jax.ShapeDtypeStruct((t, d), dy_dtype)]
        + [jax.ShapeDtypeStruct((1, d), F32)] * 6,
        scratch_shapes=[pltpu.VMEM((SUBLANES, d), F32)] * 4,
        compiler_params=_cparams(_ARB), name=name)(dx_new, dh, x_in, gpre, sc, y_prev, gpost_p, gt_p)


def _bwd_first(dx_new, dh, x_in, gpre, sc, *, name):
    t, d = x_in.shape
    tm = _pick(t, (TM_ROW, 256, 128))
    n = t // tm

    def body(dxn_ref, dh_ref, x_ref, gpre_ref, sc_ref, dx_ref, dsh_ref, dsc_ref, dgpre_ref, a1, a2):
        i = pl.program_id(0)

        @pl.when(i == 0)
        def _():
            a1[...] = jnp.zeros_like(a1)
            a2[...] = jnp.zeros_like(a2)

        dh_v = dh_ref[...]
        xn, r = _rms(x_ref[...])
        dx_ref[...] = dxn_ref[...] + _rms_bwd(dh_v * ((1.0 + sc_ref[...]) * gpre_ref[...]), xn, r)
        a1[...] += _sum8(dh_v)
        a2[...] += _sum8(dh_v * xn)

        @pl.when(i == n - 1)
        def _():
            s2 = jnp.sum(a2[...], axis=0, keepdims=True)
            dsh_ref[...] = jnp.sum(a1[...], axis=0, keepdims=True)
            dsc_ref[...] = gpre_ref[...] * s2
            dgpre_ref[...] = (1.0 + sc_ref[...]) * s2

    return pl.pallas_call(
        body, grid=(n,),
        in_specs=[_row_spec(tm, d)] * 3 + [_vec_spec(1, d)] * 2,
        out_specs=[_row_spec(tm, d)] + [_vec_spec(1, d)] * 3,
        out_shape=[jax.ShapeDtypeStruct((t, d), F32)] + [jax.ShapeDtypeStruct((1, d), F32)] * 3,
        scratch_shapes=[pltpu.VMEM((SUBLANES, d), F32)] * 2,
        compiler_params=_cparams(_ARB), name=name)(dx_new, dh, x_in, gpre, sc)


def _conv3_rows(buf, w_ref, rows, first):
    out = buf[pl.ds(first, rows), :] * w_ref[pl.ds(0, 1), :]
    for k in (1, 2):
        out = out + buf[pl.ds(first + k, rows), :] * w_ref[pl.ds(k, 1), :]
    return out


ROWS_BLK = 16
COLS_BLK = 256


def _bcast_rows(dst, src_ref, first, nrows):
    for k in range(nrows):
        dst[first + k] = jnp.broadcast_to(src_ref[pl.ds(k, 1), :], dst.shape[1:])


def _shifted_rows(x, off, rows):
    if off % SUBLANES == 0:
        return x[off:off + rows]
    return pltpu.roll(x, x.shape[0] - off, axis=0)[:rows]


def _conv3_blk(buf, wb, first, rows, cols):
    base = first - first % SUBLANES
    window = buf[pl.ds(base, rows + SUBLANES), cols]
    xs = [_shifted_rows(window, first - base + k, rows) for k in range(3)]
    out = xs[0] * wb[0, pl.ds(0, rows), cols]
    out = out + xs[1] * wb[1, pl.ds(0, rows), cols]
    out = out + xs[2] * wb[2, pl.ds(0, rows), cols]
    return out + wb[3, pl.ds(0, rows), cols], xs


def _ffn_gate_fwd(u, w, b, *, name):
    t, f2 = u.shape
    f = f2 // 2
    tm = _pick(t, (TM_CONV,))
    hb = HALO_BF16
    rb = ROWS_BLK
    cw = _pick(f, (COLS_BLK, LANES))

    def body(u_ref, up_ref, w_ref, b_ref, a_ref, v_ref, buf, wb):
        i = pl.program_id(0)
        buf[pl.ds(hb, tm), :] = u_ref[...].astype(F32)
        buf[pl.ds(0, hb), :] = jnp.where(i > 0, up_ref[...].astype(F32), 0.0)
        _bcast_rows(wb, w_ref, 0, 3)
        _bcast_rows(wb, b_ref, 3, 1)
        for c0 in range(0, f, cw):
            gcols, vcols = pl.ds(c0, cw), pl.ds(f + c0, cw)
            for r0 in range(0, tm, rb):
                rows = pl.ds(r0, rb)
                vg, _ = _conv3_blk(buf, wb, hb - 2 + r0, rb, gcols)
                vv, _ = _conv3_blk(buf, wb, hb - 2 + r0, rb, vcols)
                v_ref[rows, gcols] = vg.astype(BF16)
                v_ref[rows, vcols] = vv.astype(BF16)
                a_ref[rows, gcols] = (vg * _sigmoid(vg) * vv).astype(BF16)

    return pl.pallas_call(
        body, grid=(t // tm,),
        in_specs=[_row_spec(tm, f2), _prev_spec(tm, hb, f2), _vec_spec(3, f2), _vec_spec(1, f2)],
        out_specs=[_row_spec(tm, f), _row_spec(tm, f2)],
        out_shape=[jax.ShapeDtypeStruct((t, f), BF16), jax.ShapeDtypeStruct((t, f2), BF16)],
        scratch_shapes=[pltpu.VMEM((tm + hb, f2), F32), pltpu.VMEM((4, rb, f2), F32)],
        compiler_params=_cparams(_PAR), name=name)(u, u, w, b)


def _ffn_gate_bwd(u, v, da, w, *, name):
    t, f2 = u.shape
    f = f2 // 2
    tm = _pick(t, (TM_CONV,))
    hb = HALO_BF16
    n = t // tm
    rb = ROWS_BLK
    cw = _pick(f, (COLS_BLK, LANES))
    blocks = [(r0, rb) for r0 in range(0, tm, rb)] + [(tm, hb)]

    def body(u_ref, v_ref, vn_ref, da_ref, dan_ref, w_ref, du_ref, dw_ref, db_ref, dvbuf, wb, wacc, bacc):
        i = pl.program_id(0)

        @pl.when(i == 0)
        def _():
            wacc[...] = jnp.zeros_like(wacc)
            bacc[...] = jnp.zeros_like(bacc)

        _bcast_rows(wb, w_ref, 0, 3)
        for c0 in range(0, f, cw):
            gcols, vcols = pl.ds(c0, cw), pl.ds(f + c0, cw)
            for r0, rows in blocks:
                if r0 < tm:
                    vg, vv = v_ref[pl.ds(r0, rows), gcols], v_ref[pl.ds(r0, rows), vcols]
                    dav = da_ref[pl.ds(r0, rows), gcols].astype(F32)
                else:
                    vg, vv = vn_ref[:, gcols], vn_ref[:, vcols]
                    dav = jnp.where(i < n - 1, dan_ref[:, gcols].astype(F32), 0.0)
                vg, vv = vg.astype(F32), vv.astype(F32)
                sg = _sigmoid(vg)
                dvg = dav * vv * (sg * (1.0 + vg * (1.0 - sg)))
                dvv = dav * (vg * sg)
                dvbuf[pl.ds(r0, rows), gcols] = dvg
                dvbuf[pl.ds(r0, rows), vcols] = dvv
                if r0 < tm:
                    bacc[:, gcols] += _sum8(dvg)
                    bacc[:, vcols] += _sum8(dvv)
        for c0 in range(0, f2, cw):
            cols = pl.ds(c0, cw)
            for r0 in range(0, tm, rb):
                uv = u_ref[pl.ds(r0, rb), cols].astype(F32)
                window = dvbuf[pl.ds(r0, rb + SUBLANES), cols]
                du = None
                for k in range(3):
                    dvk = _shifted_rows(window, 2 - k, rb)
                    term = dvk * wb[k, :, cols]
                    du = term if du is None else du + term
                    wacc[k, :, cols] += _sum8(uv * dvk)
                du_ref[pl.ds(r0, rb), cols] = du.astype(BF16)

        @pl.when(i == n - 1)
        def _():
            db_ref[...] = jnp.sum(bacc[...], axis=0, keepdims=True)
            dw_ref[...] = jnp.sum(wacc[...], axis=1)

    return pl.pallas_call(
        body, grid=(n,),
        in_specs=[_row_spec(tm, f2), _row_spec(tm, f2), _next_spec(tm, hb, f2, t),
                  _row_spec(tm, f), _next_spec(tm, hb, f, t), _vec_spec(3, f2)],
        out_specs=[_row_spec(tm, f2), _vec_spec(3, f2), _vec_spec(1, f2)],
        out_shape=[jax.ShapeDtypeStruct((t, f2), BF16), jax.ShapeDtypeStruct((3, f2), F32),
                   jax.ShapeDtypeStruct((1, f2), F32)],
        scratch_shapes=[pltpu.VMEM((tm + hb, f2), F32), pltpu.VMEM((3, rb, f2), F32),
                        pltpu.VMEM((3, SUBLANES, f2), F32), pltpu.VMEM((SUBLANES, f2), F32)],
        compiler_params=_cparams(_ARB), name=name)(u, v, v, da, da, w)


def _glu(u, b1, d):
    return (u[:, :d] + b1[:, :d]) * _sigmoid(u[:, d:] + b1[:, d:])


ROWS_TAPS = 32
ROWS_NORM = 16


def _fill_glu_buf(buf, u_ref, up_ref, b1_ref, i, tm, d):
    cw = _pick(d, (COLS_BLK, LANES))
    for c0 in range(0, d, cw):
        b1 = jnp.concatenate([b1_ref[:, pl.ds(c0, cw)], b1_ref[:, pl.ds(d + c0, cw)]], axis=1)
        up = jnp.concatenate([up_ref[:, pl.ds(c0, cw)], up_ref[:, pl.ds(d + c0, cw)]], axis=1)
        buf[pl.ds(0, HALO_A), pl.ds(c0, cw)] = jnp.where(i > 0, _glu(up, b1, cw), 0.0)
        for r0 in range(0, tm, ROWS_TAPS):
            rows = pl.ds(r0, ROWS_TAPS)
            uv = jnp.concatenate([u_ref[rows, pl.ds(c0, cw)], u_ref[rows, pl.ds(d + c0, cw)]], axis=1)
            buf[pl.ds(HALO_A + r0, ROWS_TAPS), pl.ds(c0, cw)] = _glu(uv, b1, cw)


def _taps31(buf, r0, cols, offs, use):
    nv = ROWS_TAPS // SUBLANES
    nrows = ROWS_TAPS + SUBLANES * (-(-max(offs) // SUBLANES))
    window = buf[pl.ds(r0, nrows), cols]
    shifted = {b: _shifted_rows(window, b, nrows - SUBLANES) if b else window
               for b in sorted({o % SUBLANES for o in offs})}
    for k, o in enumerate(offs):
        b, a = o % SUBLANES, o // SUBLANES
        use(k, [shifted[b][SUBLANES * (a + v):SUBLANES * (a + v + 1)] for v in range(nv)])


def _conv_taps_blk(buf, wb, r0, cols, offs):
    nv = ROWS_TAPS // SUBLANES
    acc = [None] * nv

    def use(k, rows):
        wk = wb[k, :, cols]
        for v in range(nv):
            term = rows[v] * wk
            acc[v] = term if acc[v] is None else acc[v] + term

    _taps31(buf, r0, cols, offs, use)
    return jnp.concatenate(acc, axis=0)


def _layernorm_parts(x):
    mu = jnp.mean(x, axis=-1, keepdims=True)
    xc = x - mu
    rstd = lax.rsqrt(jnp.mean(xc * xc, axis=-1, keepdims=True) + LN_EPS)
    return xc * rstd, rstd


_FWD_OFFS = tuple(HALO_A - (CONF_CONV_WIDTH - 1) + k for k in range(CONF_CONV_WIDTH))
_BWD_OFFS = tuple(CONF_CONV_WIDTH - 1 - k for k in range(CONF_CONV_WIDTH))


def _a_fwd(u1, b1, dww, dwb, lng, lnb, *, name):
    t, d2 = u1.shape
    d = d2 // 2
    tm = _pick(t, (TM_CONV,))

    def body(u_ref, up_ref, b1_ref, w_ref, wbias_ref, g_ref, bb_ref, o_ref, u3_ref, buf, wb):
        i = pl.program_id(0)
        _fill_glu_buf(buf, u_ref, up_ref, b1_ref, i, tm, d)
        _bcast_rows(wb, w_ref, 0, CONF_CONV_WIDTH)
        for c0 in range(0, d, LANES):
            cols = pl.ds(c0, LANES)
            for r0 in range(0, tm, ROWS_TAPS):
                u3_ref[pl.ds(r0, ROWS_TAPS), cols] = (_conv_taps_blk(buf, wb, r0, cols, _FWD_OFFS)
                                                      + wbias_ref[:, cols])
        for r0 in range(0, tm, ROWS_NORM):
            rows = pl.ds(r0, ROWS_NORM)
            xhat, _ = _layernorm_parts(u3_ref[rows, :])
            u4 = xhat * g_ref[...] + bb_ref[...]
            o_ref[rows, :] = (u4 * _sigmoid(u4)).astype(BF16)

    return pl.pallas_call(
        body, grid=(t // tm,),
        in_specs=[_row_spec(tm, d2), _prev_spec(tm, HALO_A, d2), _vec_spec(1, d2),
                  _vec_spec(CONF_CONV_WIDTH, d)] + [_vec_spec(1, d)] * 3,
        out_specs=[_row_spec(tm, d)] * 2,
        out_shape=[jax.ShapeDtypeStruct((t, d), BF16), jax.ShapeDtypeStruct((t, d), F32)],
        scratch_shapes=[pltpu.VMEM((tm + HALO_A, d), F32), pltpu.VMEM((CONF_CONV_WIDTH, SUBLANES, d), F32)],
        compiler_params=_cparams(_PAR), name=name)(u1, u1, b1, dww, dwb, lng, lnb)


def _a_bwd_norm(u3, du5, lng, lnb, *, name):
    t, d = u3.shape
    tm = _pick(t, (TM_ROW, 256, 128))
    n = t // tm

    def body(u3_ref, du5_ref, g_ref, bb_ref, du3_ref, dg_ref, db_ref, dwb_ref, ag, ab, aw):
        i = pl.program_id(0)

        @pl.when(i == 0)
        def _():
            for a in (ag, ab, aw):
                a[...] = jnp.zeros_like(a)

        g = g_ref[...]
        for r0 in range(0, tm, ROWS_NORM):
            rows = pl.ds(r0, ROWS_NORM)
            xhat, rstd = _layernorm_parts(u3_ref[rows, :])
            u4 = xhat * g + bb_ref[...]
            sg = _sigmoid(u4)
            du4 = du5_ref[rows, :] * (sg * (1.0 + u4 * (1.0 - sg)))
            dxh = du4 * g
            du3 = rstd * (dxh - jnp.mean(dxh, axis=-1, keepdims=True)
                          - xhat * jnp.mean(dxh * xhat, axis=-1, keepdims=True))
            du3_ref[rows, :] = du3
            ag[...] += _sum8(du4 * xhat)
            ab[...] += _sum8(du4)
            aw[...] += _sum8(du3)

        @pl.when(i == n - 1)
        def _():
            for a, o in ((ag, dg_ref), (ab, db_ref), (aw, dwb_ref)):
                o[...] = jnp.sum(a[...], axis=0, keepdims=True)

    return pl.pallas_call(
        body, grid=(n,),
        in_specs=[_row_spec(tm, d)] * 2 + [_vec_spec(1, d)] * 2,
        out_specs=[_row_spec(tm, d)] + [_vec_spec(1, d)] * 3,
        out_shape=[jax.ShapeDtypeStruct((t, d), F32)] + [jax.ShapeDtypeStruct((1, d), F32)] * 3,
        scratch_shapes=[pltpu.VMEM((SUBLANES, d), F32)] * 3,
        compiler_params=_cparams(_ARB), name=name)(u3, du5, lng, lnb)


def _a_bwd_conv(u1, du3, b1, dww, *, name):
    t, d2 = u1.shape
    d = d2 // 2
    tm = _pick(t, (TM_CONV,))
    n = t // tm
    kw = CONF_CONV_WIDTH
    nv = ROWS_TAPS // SUBLANES

    def body(u_ref, up_ref, g3_ref, g3n_ref, b1_ref, w_ref, du1_ref, dw_ref, db1_ref,
             buf, gbuf, wb, wacc, bacc):
        i = pl.program_id(0)

        @pl.when(i == 0)
        def _():
            wacc[...] = jnp.zeros_like(wacc)
            bacc[...] = jnp.zeros_like(bacc)

        _fill_glu_buf(buf, u_ref, up_ref, b1_ref, i, tm, d)
        _bcast_rows(wb, w_ref, 0, kw)
        gbuf[pl.ds(0, tm), :] = g3_ref[...]
        gbuf[pl.ds(tm, HALO_A), :] = jnp.where(i < n - 1, g3n_ref[...], 0.0)
        for c0 in range(0, d, LANES):
            cols, gcols = pl.ds(c0, LANES), pl.ds(d + c0, LANES)
            for r0 in range(0, tm, ROWS_TAPS):
                rows = pl.ds(r0, ROWS_TAPS)
                u2 = [buf[pl.ds(HALO_A + r0 + SUBLANES * v, SUBLANES), cols] for v in range(nv)]
                acc = [None] * nv

                def use(k, gs):
                    wk = wb[k, :, cols]
                    part = None
                    for v in range(nv):
                        term = gs[v] * wk
                        acc[v] = term if acc[v] is None else acc[v] + term
                        prod = u2[v] * gs[v]
                        part = prod if part is None else part + prod
                    wacc[k, :, cols] += part

                _taps31(gbuf, r0, cols, _BWD_OFFS, use)
                du2 = jnp.concatenate(acc, axis=0)
                av = u_ref[rows, cols] + b1_ref[:, cols]
                sg = _sigmoid(u_ref[rows, gcols] + b1_ref[:, gcols])
                da = du2 * sg
                dg = du2 * av * (sg * (1.0 - sg))
                du1_ref[rows, cols] = da.astype(BF16)
                du1_ref[rows, gcols] = dg.astype(BF16)
                bacc[:, cols] += _sum8(da)
                bacc[:, gcols] += _sum8(dg)

        @pl.when(i == n - 1)
        def _():
            dw_ref[...] = jnp.sum(wacc[...], axis=1)
            db1_ref[...] = jnp.sum(bacc[...], axis=0, keepdims=True)

    return pl.pallas_call(
        body, grid=(n,),
        in_specs=[_row_spec(tm, d2), _prev_spec(tm, HALO_A, d2), _row_spec(tm, d),
                  _next_spec(tm, HALO_A, d, t), _vec_spec(1, d2), _vec_spec(kw, d)],
        out_specs=[_row_spec(tm, d2), _vec_spec(kw, d), _vec_spec(1, d2)],
        out_shape=[jax.ShapeDtypeStruct((t, d2), BF16), jax.ShapeDtypeStruct((kw, d), F32),
                   jax.ShapeDtypeStruct((1, d2), F32)],
        scratch_shapes=[pltpu.VMEM((tm + HALO_A, d), F32), pltpu.VMEM((tm + HALO_A, d), F32),
                        pltpu.VMEM((kw, SUBLANES, d), F32),
                        pltpu.VMEM((kw, SUBLANES, d), F32), pltpu.VMEM((SUBLANES, d2), F32)],
        compiler_params=_cparams(_ARB), name=name)(u1, u1, du3, du3, b1, dww)


def _pool_counts(i, tm, w):
    pos = (i * tm + lax.broadcasted_iota(jnp.int32, (tm, 1), 0) + 1).astype(F32)
    return jnp.minimum(pos, float(w))


def _b_pool_fwd(h, *, name):
    t, d = h.shape
    gd = d // len(POOL_WINDOWS)
    tm = _pick(t, (TM_CONV,))
    hb = HALO_POOL

    def body(h_ref, hp_ref, o_ref, buf):
        i = pl.program_id(0)
        buf[pl.ds(0, hb), :] = jnp.where(i > 0, hp_ref[...], 0.0)
        buf[pl.ds(hb, tm), :] = h_ref[...]
        for g, w in enumerate(POOL_WINDOWS):
            cols = pl.ds(g * gd, gd)
            cur = buf[pl.ds(hb, tm), cols]
            s = cur
            for j in range(1, w):
                s = s + buf[pl.ds(hb - j, tm), cols]
            o_ref[:, cols] = (s / _pool_counts(i, tm, w) - cur).astype(BF16)

    return pl.pallas_call(
        body, grid=(t // tm,),
        in_specs=[_row_spec(tm, d), _prev_spec(tm, hb, d)],
        out_specs=_row_spec(tm, d), out_shape=jax.ShapeDtypeStruct((t, d), BF16),
        scratch_shapes=[pltpu.VMEM((tm + hb, d), F32)],
        compiler_params=_cparams(_PAR), name=name)(h, h)


def _b_pool_bwd(dp, *, name):
    t, d = dp.shape
    gd = d // len(POOL_WINDOWS)
    tm = _pick(t, (TM_CONV,))
    hb = HALO_POOL
    n = t // tm

    def body(dp_ref, dpn_ref, o_ref, buf):
        i = pl.program_id(0)
        for g, w in enumerate(POOL_WINDOWS):
            cols = pl.ds(g * gd, gd)
            buf[pl.ds(0, tm), cols] = dp_ref[:, cols] / _pool_counts(i, tm, w)
            buf[pl.ds(tm, hb), cols] = jnp.where(i < n - 1, dpn_ref[:, cols] * (1.0 / w), 0.0)
            s = buf[pl.ds(0, tm), cols]
            for j in range(1, w):
                s = s + buf[pl.ds(j, tm), cols]
            o_ref[:, cols] = s - dp_ref[:, cols]

    return pl.pallas_call(
        body, grid=(n,),
        in_specs=[_row_spec(tm, d), _next_spec(tm, hb, d, t)],
        out_specs=_row_spec(tm, d), out_shape=jax.ShapeDtypeStruct((t, d), F32),
        scratch_shapes=[pltpu.VMEM((tm + hb, d), F32)],
        compiler_params=_cparams(_PAR), name=name)(dp, dp)


def _b_affine_fwd(mixed, gb, scale, *, name):
    t, d = mixed.shape
    tm = _pick(t, (TM_ROW, 256, 128))

    def body(m_ref, gb_ref, s_ref, o_ref):
        o_ref[...] = ((m_ref[...] + gb_ref[...]) * s_ref[...]).astype(BF16)

    return pl.pallas_call(
        body, grid=(t // tm,), in_specs=[_row_spec(tm, d)] + [_vec_spec(1, d)] * 2,
        out_specs=_row_spec(tm, d), out_shape=jax.ShapeDtypeStruct((t, d), BF16),
        compiler_params=_cparams(_PAR), name=name)(mixed, gb, scale)


def _b_affine_bwd(dy, mixed, gb, scale, *, name):
    t, d = mixed.shape
    tm = _pick(t, (TM_ROW, 256, 128))
    n = t // tm

    def body(dy_ref, m_ref, gb_ref, s_ref, dm_ref, ds_ref, dgb_ref, a1, a2):
        i = pl.program_id(0)

        @pl.when(i == 0)
        def _():
            a1[...] = jnp.zeros_like(a1)
            a2[...] = jnp.zeros_like(a2)

        dy_v = dy_ref[...]
        dm_ref[...] = (dy_v * s_ref[...]).astype(BF16)
        a1[...] += _sum8(dy_v * (m_ref[...] + gb_ref[...]))
        a2[...] += _sum8(dy_v)

        @pl.when(i == n - 1)
        def _():
            ds_ref[...] = jnp.sum(a1[...], axis=0, keepdims=True)
            dgb_ref[...] = jnp.sum(a2[...], axis=0, keepdims=True) * s_ref[...]

    return pl.pallas_call(
        body, grid=(n,), in_specs=[_row_spec(tm, d)] * 2 + [_vec_spec(1, d)] * 2,
        out_specs=[_row_spec(tm, d)] + [_vec_spec(1, d)] * 2,
        out_shape=[jax.ShapeDtypeStruct((t, d), BF16)] + [jax.ShapeDtypeStruct((1, d), F32)] * 2,
        scratch_shapes=[pltpu.VMEM((SUBLANES, d), F32)] * 2,
        compiler_params=_cparams(_ARB), name=name)(dy, mixed, gb, scale)


def _c_gate_fwd(bcx, wc, *, name):
    t, d3 = bcx.shape
    d = d3 // 3
    tm = _pick(t, (TM_CONV,))
    hb = HALO_3

    rb = ROWS_BLK
    cw = _pick(d, (COLS_BLK, LANES))

    def body(x_ref, xp_ref, w_ref, z_ref, buf, wb):
        i = pl.program_id(0)
        _bcast_rows(wb, w_ref, 0, 3)
        wb[3] = jnp.zeros(wb.shape[1:], F32)
        for c0 in range(0, d, cw):
            cols, ccols, vcols = pl.ds(c0, cw), pl.ds(d + c0, cw), pl.ds(2 * d + c0, cw)
            buf[pl.ds(0, hb), cols] = jnp.where(i > 0, xp_ref[:, ccols] * xp_ref[:, vcols], 0.0)
            for r0 in range(0, tm, rb):
                rows = pl.ds(r0, rb)
                buf[pl.ds(hb + r0, rb), cols] = x_ref[rows, ccols] * x_ref[rows, vcols]
            for r0 in range(0, tm, rb):
                rows = pl.ds(r0, rb)
                q, _ = _conv3_blk(buf, wb, hb - 2 + r0, rb, cols)
                z_ref[rows, cols] = (x_ref[rows, cols] * q).astype(BF16)

    return pl.pallas_call(
        body, grid=(t // tm,),
        in_specs=[_row_spec(tm, d3), _prev_spec(tm, hb, d3), _vec_spec(3, d)],
        out_specs=_row_spec(tm, d), out_shape=jax.ShapeDtypeStruct((t, d), BF16),
        scratch_shapes=[pltpu.VMEM((tm + hb, d), F32), pltpu.VMEM((4, rb, d), F32)],
        compiler_params=_cparams(_PAR), name=name)(bcx, bcx, wc)


def _c_gate_bwd(bcx, dz, wc, *, name):
    t, d3 = bcx.shape
    d = d3 // 3
    tm = _pick(t, (TM_CONV,))
    hb = HALO_3
    n = t // tm

    rb = ROWS_BLK
    cw = _pick(d, (COLS_BLK, LANES))

    def body(x_ref, xp_ref, xn_ref, dz_ref, dzn_ref, w_ref, o_ref, dw_ref, pbuf, qbuf, wb, wacc):
        i = pl.program_id(0)

        @pl.when(i == 0)
        def _():
            wacc[...] = jnp.zeros_like(wacc)

        _bcast_rows(wb, w_ref, 0, 3)
        wb[3] = jnp.zeros(wb.shape[1:], F32)
        for c0 in range(0, d, cw):
            cols, ccols, vcols = pl.ds(c0, cw), pl.ds(d + c0, cw), pl.ds(2 * d + c0, cw)
            pbuf[pl.ds(0, hb), cols] = jnp.where(i > 0, xp_ref[:, ccols] * xp_ref[:, vcols], 0.0)
            qbuf[pl.ds(tm, hb), cols] = jnp.where(i < n - 1, dzn_ref[:, cols] * xn_ref[:, cols], 0.0)
            for r0 in range(0, tm, rb):
                rows = pl.ds(r0, rb)
                pbuf[pl.ds(hb + r0, rb), cols] = x_ref[rows, ccols] * x_ref[rows, vcols]
            for r0 in range(0, tm, rb):
                rows = pl.ds(r0, rb)
                q, _ = _conv3_blk(pbuf, wb, hb - 2 + r0, rb, cols)
                dz_v = dz_ref[rows, cols]
                qbuf[rows, cols] = dz_v * x_ref[rows, cols]
                o_ref[rows, cols] = (dz_v * q).astype(BF16)
            for r0 in range(0, tm, rb):
                rows = pl.ds(r0, rb)
                pv = pbuf[pl.ds(hb + r0, rb), cols]
                window = qbuf[pl.ds(r0, rb + SUBLANES), cols]
                dp = None
                for k in range(3):
                    dqk = _shifted_rows(window, 2 - k, rb)
                    term = dqk * wb[k, :, cols]
                    dp = term if dp is None else dp + term
                    wacc[k, :, cols] += _sum8(pv * dqk)
                o_ref[rows, ccols] = (dp * x_ref[rows, vcols]).astype(BF16)
                o_ref[rows, vcols] = (dp * x_ref[rows, ccols]).astype(BF16)

        @pl.when(i == n - 1)
        def _():
            dw_ref[...] = jnp.sum(wacc[...], axis=1)

    return pl.pallas_call(
        body, grid=(n,),
        in_specs=[_row_spec(tm, d3), _prev_spec(tm, hb, d3), _next_spec(tm, hb, d3, t),
                  _row_spec(tm, d), _next_spec(tm, hb, d, t), _vec_spec(3, d)],
        out_specs=[_row_spec(tm, d3), _vec_spec(3, d)],
        out_shape=[jax.ShapeDtypeStruct((t, d3), BF16), jax.ShapeDtypeStruct((3, d), F32)],
        scratch_shapes=[pltpu.VMEM((tm + hb, d), F32), pltpu.VMEM((tm + hb, d), F32),
                        pltpu.VMEM((4, rb, d), F32), pltpu.VMEM((3, SUBLANES, d), F32)],
        compiler_params=_cparams(_ARB), name=name)(bcx, bcx, bcx, dz, dz, wc)


def _row(v):
    return v.reshape(1, -1)


def _kind_of(j):
    return "f" if j % 2 else "abc"[(j // 2) % N_MIXERS]


BIG = ("a_pw1_w", "a_pw2_w", "b_group_w", "c_in_w", "c_out_w", "f_up_w", "f_down_w")
COL_SHARDED = ("a_pw1_w", "c_in_w", "f_up_w")


def _local_step(x, target, mod, p):
    nsub = 2 * DEPTH
    norm_names = (("norm_pre_mix", "norm_post_mix"), ("norm_pre_ffn", "norm_post_ffn"))
    gpre = [_row(p[norm_names[s][0]][i]) for i in range(DEPTH) for s in (0, 1)]
    gpost = [_row(p[norm_names[s][1]][i]) for i in range(DEPTH) for s in (0, 1)]
    sh = [_row(mod[i, 3 * s + 0]) for i in range(DEPTH) for s in (0, 1)]
    sc = [_row(mod[i, 3 * s + 1]) for i in range(DEPTH) for s in (0, 1)]
    gt = [_row(mod[i, 3 * s + 2]) for i in range(DEPTH) for s in (0, 1)]

    def h_dtype(j):
        return F32 if _kind_of(j) == "b" else BF16

    xs, hs, ys, saved = [x], [], [], []

    hs.append(_fwd_first(x, gpre[0], sc[0], sh[0], h_dtype=h_dtype(0), name="fwd_first"))
    for j in range(nsub):
        i, kind = j // 2, _kind_of(j)
        slot = i // N_MIXERS
        h = hs[j]
        tag = f"{kind}{j}"
        if kind == "f":
            u = _mm(h, p["f_up_w"], mode="nn", layer=i, out_dtype=BF16, name=f"ffn_up_{tag}")
            a, vpre = _ffn_gate_fwd(u, p["f_dw_w"][i], _row(p["f_dw_b"][i]), name=f"ffn_gate_{tag}")
            y = _mm(a, p["f_down_w"][i], mode="nn", out_dtype=BF16, name=f"ffn_down_{tag}")
            saved.append((u, a, vpre))
        elif kind == "a":
            u1 = _mm(h, p["a_pw1_w"], mode="nn", layer=slot, name=f"a_pw1_{tag}")
            u5, u3 = _a_fwd(u1, _row(p["a_pw1_b"][slot]), p["a_dw_w"][slot], _row(p["a_dw_b"][slot]),
                            _row(p["a_ln_g"][slot]), _row(p["a_ln_b"][slot]), name=f"a_conv_{tag}")
            y = _mm(u5, p["a_pw2_w"][slot], mode="nn", bias=_row(p["a_pw2_b"][slot]), out_dtype=BF16,
                    name=f"a_pw2_{tag}")
            saved.append((u1, u5, u3))
        elif kind == "b":
            pooled = _b_pool_fwd(h, name=f"b_pool_{tag}")
            mixed = _mm_group(pooled, p["b_group_w"][slot], mode="nn", name=f"b_mix_{tag}")
            y = _b_affine_fwd(mixed, _row(p["b_group_b"][slot]), _row(p["b_scale"][slot]), name=f"b_aff_{tag}")
            saved.append((pooled, mixed))
        else:
            bcx = _mm(h, p["c_in_w"], mode="nn", layer=slot, name=f"c_in_{tag}")
            z = _c_gate_fwd(bcx, p["c_conv_w"][slot], name=f"c_gate_{tag}")
            y = _mm(z, p["c_out_w"][slot], mode="nn", out_dtype=BF16, name=f"c_out_{tag}")
            saved.append((bcx, z))
        ys.append(y)
        if j + 1 < nsub:
            x_new, h_next = _fwd_mid(xs[j], y, gpost[j], gt[j], gpre[j + 1], sc[j + 1], sh[j + 1],
                                     h_dtype=h_dtype(j + 1), name=f"fwd_mid_{j}")
            xs.append(x_new)
            hs.append(h_next)

    n_of = {"a": len([i for i in range(DEPTH) if i % N_MIXERS == 0]),
            "b": len([i for i in range(DEPTH) if i % N_MIXERS == 1]),
            "c": len([i for i in range(DEPTH) if i % N_MIXERS == 2]), "f": DEPTH, "n": DEPTH}
    g = {k: [None] * n_of[k[0]] for k in p}
    dmod = [[None] * 6 for _ in range(DEPTH)]

    last = nsub - 1
    dx, dy, dgpost, dgt, sdy, loss = _last_fwd_bwd(xs[last], ys[last], target, gpost[last], gt[last],
                                                   name="loss_head")
    for j in range(last, -1, -1):
        i, kind = j // 2, _kind_of(j)
        slot = i // N_MIXERS
        sub = j % 2
        tag = f"{kind}{j}"
        g[norm_names[sub][1]][i] = dgpost
        dmod[i][3 * sub + 2] = dgt
        h = hs[j]
        if kind == "f":
            u, a, vpre = saved[j]
            da = _mm(dy, p["f_down_w"][i], mode="nt", out_dtype=BF16, name=f"ffn_dda_{tag}")
            g["f_down_w"][i] = _mm(a, dy, mode="tn", tk=TK_TOKENS, out_dtype=BF16, name=f"ffn_dwdown_{tag}")
            du, dw, db = _ffn_gate_bwd(u, vpre, da, p["f_dw_w"][i], name=f"ffn_gate_bwd_{tag}")
            g["f_dw_w"][i], g["f_dw_b"][i] = dw, db
            dh = _mm(du, p["f_up_w"], mode="nt", layer=i, name=f"ffn_ddh_{tag}")
            g["f_up_w"][i] = _mm(h, du, mode="tn", tk=TK_TOKENS, layer=i, out_dtype=BF16,
                                 name=f"ffn_dwup_{tag}")
        elif kind == "a":
            u1, u5, u3 = saved[j]
            g["a_pw2_b"][slot] = sdy
            du5 = _mm(dy, p["a_pw2_w"][slot], mode="nt", name=f"a_ddu5_{tag}")
            g["a_pw2_w"][slot] = _mm(u5, dy, mode="tn", tk=TK_TOKENS, out_dtype=BF16, name=f"a_dw2_{tag}")
            b1 = _row(p["a_pw1_b"][slot])
            du3, dlg, dlb, ddwb = _a_bwd_norm(u3, du5, _row(p["a_ln_g"][slot]), _row(p["a_ln_b"][slot]),
                                              name=f"a_bwd_norm_{tag}")
            g["a_ln_g"][slot], g["a_ln_b"][slot], g["a_dw_b"][slot] = dlg, dlb, ddwb
            du1, ddww, db1 = _a_bwd_conv(u1, du3, b1, p["a_dw_w"][slot], name=f"a_bwd_conv_{tag}")
            g["a_dw_w"][slot], g["a_pw1_b"][slot] = ddww, db1
            dh = _mm(du1, p["a_pw1_w"], mode="nt", layer=slot, name=f"a_ddh_{tag}")
            g["a_pw1_w"][slot] = _mm(h, du1, mode="tn", tk=TK_TOKENS, layer=slot, out_dtype=BF16,
                                     name=f"a_dw1_{tag}")
        elif kind == "b":
            pooled, mixed = saved[j]
            dmixed, dscale, dgb = _b_affine_bwd(dy, mixed, _row(p["b_group_b"][slot]), _row(p["b_scale"][slot]),
                                                name=f"b_aff_bwd_{tag}")
            g["b_scale"][slot], g["b_group_b"][slot] = dscale, dgb
            dpooled = _mm_group(dmixed, p["b_group_w"][slot], mode="nt", name=f"b_dpool_{tag}")
            g["b_group_w"][slot] = _mm_group(pooled, dmixed, mode="tn", tm=TK_TOKENS, out_dtype=BF16,
                                             name=f"b_dw_{tag}")
            dh = _b_pool_bwd(dpooled, name=f"b_pool_bwd_{tag}")
        else:
            bcx, z = saved[j]
            dz = _mm(dy, p["c_out_w"][slot], mode="nt", name=f"c_ddz_{tag}")
            g["c_out_w"][slot] = _mm(z, dy, mode="tn", tk=TK_TOKENS, out_dtype=BF16, name=f"c_dwout_{tag}")
            dbcx, dwc = _c_gate_bwd(bcx, dz, p["c_conv_w"][slot], name=f"c_gate_bwd_{tag}")
            g["c_conv_w"][slot] = dwc
            dh = _mm(dbcx, p["c_in_w"], mode="nt", layer=slot, name=f"c_ddh_{tag}")
            g["c_in_w"][slot] = _mm(h, dbcx, mode="tn", tk=TK_TOKENS, layer=slot, out_dtype=BF16,
                                    name=f"c_dwin_{tag}")
        if j > 0:
            pj = j - 1
            dy_dtype = F32 if _kind_of(pj) == "b" else BF16
            dx, dy, dsh, dsc, dgpre, dgpost, dgt, sdy = _bwd_mid(
                dx, dh, xs[j], gpre[j], sc[j], ys[pj], gpost[pj], gt[pj], dy_dtype=dy_dtype, name=f"bwd_mid_{j}")
        else:
            dx, dsh, dsc, dgpre = _bwd_first(dx, dh, xs[0], gpre[0], sc[0], name="bwd_first")
        dmod[i][3 * sub + 0] = dsh
        dmod[i][3 * sub + 1] = dsc
        g[norm_names[sub][0]][i] = dgpre

    small = {k: jnp.stack(v).reshape(p[k].shape) for k, v in g.items() if k not in BIG}
    big = {k: v for k, v in g.items() if k in BIG}
    dmod_arr = jnp.stack([jnp.concatenate(r, axis=0) for r in dmod])
    return loss, dx, dmod_arr, small, big


_MESH = pl.DeviceIdType.MESH
_ANY = pl.BlockSpec(memory_space=pl.ANY)
_VMEM = pl.BlockSpec(memory_space=pltpu.VMEM)


def _place():
    return lax.axis_index("x"), lax.axis_index("y"), lax.axis_index("c")


def _other_chips(x, y):
    return [(1 - x, y), (x, 1 - y), (1 - x, 1 - y)]


def _remote(src, dst, send_sem, recv_sem, to):
    return pltpu.make_async_remote_copy(src_ref=src, dst_ref=dst, send_sem=send_sem, recv_sem=recv_sem,
                                        device_id=to, device_id_type=_MESH)


def _all_gather8(blk, *, name):
    r, cdim = blk.shape

    def body(x_ref, out_ref, send_sems, recv_sems, local_sem):
        x, y, c = _place()
        me, sibling = (x, y, c), (x, y, 1 - c)
        chips = _other_chips(x, y)

        def slot(px, py, pc):
            return out_ref.at[4 * px + 2 * py + pc]

        def copy(k, block, to, src=None):
            return _remote(slot(*block) if src is None else src, slot(*block),
                           send_sems.at[k], recv_sems.at[k], to)

        mine = pltpu.make_async_copy(x_ref, slot(*me), local_sem)
        mine.start()
        first = [copy(0, me, sibling, src=x_ref)]
        first += [copy(1 + j, me, (*chip, c), src=x_ref) for j, chip in enumerate(chips)]
        for cp in first:
            cp.start()
        passed = [copy(4 + j, (*chip, c), sibling) for j, chip in enumerate(chips)]
        for j, chip in enumerate(chips):
            copy(1 + j, (*chip, c), me).wait_recv()
            passed[j].start()
        copy(0, sibling, me).wait_recv()
        for j, chip in enumerate(chips):
            copy(4 + j, (*chip, 1 - c), me).wait_recv()
        for cp in first + passed:
            cp.wait_send()
        mine.wait()

    return pl.pallas_call(
        body, out_shape=jax.ShapeDtypeStruct((NDEV, r, cdim), blk.dtype),
        in_specs=[_VMEM], out_specs=_VMEM,
        scratch_shapes=[pltpu.SemaphoreType.DMA((7,)), pltpu.SemaphoreType.DMA((7,)), pltpu.SemaphoreType.DMA],
        compiler_params=pltpu.CompilerParams(vmem_limit_bytes=VMEM_LIMIT), name=name)(blk)


def _gather_dst(kind):
    if kind == "col":
        return lambda s, h: (s, h)
    if kind == "row":
        return lambda s, h: (h, slice(None), s)
    return lambda s, h: (slice(None), s, h)


def _cast_into_gathered(src, kind, out_shape, shard, *, name):
    _, a, rh, cdim = src.shape
    tr = _pick(rh, (512, 256, 128, 64, 32, 16))
    if kind == "col":
        out_idx = lambda h, ai, r, s: (s[0], h, ai, r, 0)
    elif kind == "row":
        out_idx = lambda h, ai, r, s: (h, ai, s[0], r, 0)
    else:
        out_idx = lambda h, ai, r, s: (ai, s[0], h, r, 0)

    def body(s_ref, x_ref, o_ref):
        o_ref[...] = x_ref[...].astype(BF16)

    grid_spec = pltpu.PrefetchScalarGridSpec(
        num_scalar_prefetch=1, grid=(2, a, rh // tr),
        in_specs=[pl.BlockSpec((None, None, tr, cdim), lambda h, ai, r, s: (h, ai, r, 0))],
        out_specs=pl.BlockSpec((None, None, None, tr, cdim), out_idx))
    return pl.pallas_call(
        body, grid_spec=grid_spec, out_shape=jax.ShapeDtypeStruct(out_shape, BF16),
        compiler_params=_cparams(_PAR, _PAR, _PAR), name=name)(shard, src)


def _gather_weights(bufs, kinds, *, name):
    nt = len(bufs)

    def body(*refs):
        out_refs = refs[nt:2 * nt]
        send_sems, recv_sems = refs[2 * nt:]
        x, y, c = _place()
        sibling = (x, y, 1 - c)
        nbr_x, nbr_y = (1 - x, y, c), (x, 1 - y, c)
        s_me, s_x, s_y, s_d = 2 * x + y, 2 * (1 - x) + y, 2 * x + (1 - y), 2 * (1 - x) + (1 - y)

        def at(k, s, h):
            return out_refs[k].at[_gather_dst(kinds[k])(s, h)]

        def part(ref, k, q):
            rows = bufs[k].shape[3] // 2
            return ref.at[:, pl.ds(q * rows, rows), :]

        def copy(ref, k, col, to):
            return _remote(ref, ref, send_sems.at[k, col], recv_sems.at[k, col], to)

        started = []

        def start(cp):
            cp.start()
            started.append(cp)

        for k in range(nt):
            start(copy(at(k, s_me, c), k, 0, nbr_x))
            start(copy(at(k, s_me, c), k, 1, nbr_y))
        for k in range(nt):
            got_y, got_x = at(k, s_y, c), at(k, s_x, c)
            copy(got_y, k, 1, nbr_y).wait_recv()
            start(copy(part(got_y, k, 0), k, 2, nbr_x))
            start(copy(got_y, k, 5, sibling))
            copy(got_x, k, 0, nbr_x).wait_recv()
            start(copy(part(got_x, k, 1), k, 3, nbr_y))
            start(copy(got_x, k, 4, sibling))
        for k in range(nt):
            got_d = at(k, s_d, c)
            for q in (0, 1):
                copy(part(got_d, k, q), k, 2 + q, sibling).wait_recv()
                start(copy(part(got_d, k, q), k, 6 + q, sibling))
        for k in range(nt):
            copy(at(k, s_x, 1 - c), k, 4, sibling).wait_recv()
            copy(at(k, s_y, 1 - c), k, 5, sibling).wait_recv()
            for q in (0, 1):
                copy(part(at(k, s_d, 1 - c), k, q), k, 6 + q, sibling).wait_recv()
        for cp in started:
            cp.wait_send()

    return pl.pallas_call(
        body, out_shape=[jax.ShapeDtypeStruct(b.shape, BF16) for b in bufs],
        in_specs=[_ANY] * nt, out_specs=[_ANY] * nt, input_output_aliases={k: k for k in range(nt)},
        scratch_shapes=[pltpu.SemaphoreType.DMA((nt, 8)), pltpu.SemaphoreType.DMA((nt, 8))],
        name=name)(*bufs)


def _pair_exchange(gs, layers_of, *, name):
    n, nt = len(gs), len(layers_of)

    def body(*refs):
        g_refs, out_refs = refs[:n], refs[n:n + nt]
        send_sems, recv_sems = refs[n + nt:]
        x, y, c = _place()
        sibling = (x, y, 1 - c)
        copies = []
        for t, ks in enumerate(layers_of):
            for l, k in enumerate(ks):
                cp = _remote(g_refs[k].at[:, 1 - c], out_refs[t].at[l], send_sems.at[k], recv_sems.at[k], sibling)
                cp.start()
                copies.append(cp)
        for cp in copies:
            cp.wait()

    out_shape = [jax.ShapeDtypeStruct((len(ks), NSHARD) + gs[ks[0]].shape[2:], gs[ks[0]].dtype)
                 for ks in layers_of]
    return pl.pallas_call(
        body, out_shape=out_shape, in_specs=[_ANY] * n, out_specs=[_ANY] * nt,
        scratch_shapes=[pltpu.SemaphoreType.DMA((n,)), pltpu.SemaphoreType.DMA((n,))],
        name=name)(*gs)


def _pair_sum(g, r1, s_acc, layer, half, *, name):
    _, _, rh, cdim = g.shape
    tr = _pick(rh, (256, 128, 176, 64, 32, 16))

    def body(half_ref, g_ref, r_ref, s_in_ref, o_ref):
        o_ref[...] = (g_ref[...].astype(F32) + r_ref[...].astype(F32)).astype(BF16)

    grid_spec = pltpu.PrefetchScalarGridSpec(
        num_scalar_prefetch=1, grid=(NSHARD, rh // tr),
        in_specs=[pl.BlockSpec((None, None, tr, cdim), lambda s, r, hf: (s, hf[0], r, 0)),
                  pl.BlockSpec((None, None, tr, cdim), lambda s, r, hf: (layer, s, r, 0)),
                  _ANY],
        out_specs=pl.BlockSpec((None, None, tr, cdim), lambda s, r, hf: (layer, s, r, 0)))
    return pl.pallas_call(
        body, grid_spec=grid_spec, out_shape=jax.ShapeDtypeStruct(s_acc.shape, BF16),
        input_output_aliases={3: 0},
        compiler_params=_cparams(_PAR, _PAR), name=name)(half, g, r1, s_acc)


def _chip_exchange(ss, *, name):
    nt = len(ss)

    def body(*refs):
        s_refs, out_refs, stage_refs = refs[:nt], refs[nt:2 * nt], refs[2 * nt:3 * nt]
        send_sems, recv_sems = refs[3 * nt:]
        x, y, c = _place()
        nbr_x, nbr_y = (1 - x, y, c), (x, 1 - y, c)
        s_x, s_y, s_d = 2 * (1 - x) + y, 2 * x + (1 - y), 2 * (1 - x) + (1 - y)

        def part(ref, t, q):
            rows = ss[t].shape[2] // 2
            return ref.at[:, pl.ds(q * rows, rows), :]

        def copy(src, dst, t, col, to):
            return _remote(src, dst, send_sems.at[t, col], recv_sems.at[t, col], to)

        started = []

        def start(cp):
            cp.start()
            started.append(cp)

        for t in range(nt):
            start(copy(s_refs[t].at[:, s_x], out_refs[t].at[0], t, 0, nbr_x))
            start(copy(s_refs[t].at[:, s_y], out_refs[t].at[1], t, 1, nbr_y))
            start(copy(part(s_refs[t].at[:, s_d], t, 0), stage_refs[t].at[0], t, 2, nbr_x))
            start(copy(part(s_refs[t].at[:, s_d], t, 1), stage_refs[t].at[1], t, 3, nbr_y))
        for t in range(nt):
            st0, st1 = stage_refs[t].at[0], stage_refs[t].at[1]
            copy(st0, st0, t, 2, nbr_x).wait_recv()
            start(copy(st0, part(out_refs[t].at[2], t, 0), t, 4, nbr_y))
            copy(st1, st1, t, 3, nbr_y).wait_recv()
            start(copy(st1, part(out_refs[t].at[2], t, 1), t, 5, nbr_x))
        for t in range(nt):
            copy(out_refs[t].at[0], out_refs[t].at[0], t, 0, nbr_x).wait_recv()
            copy(out_refs[t].at[1], out_refs[t].at[1], t, 1, nbr_y).wait_recv()
            for q in (0, 1):
                got = part(out_refs[t].at[2], t, q)
                copy(got, got, t, 4 + q, nbr_x).wait_recv()
        for cp in started:
            cp.wait_send()

    out_shape = [jax.ShapeDtypeStruct((3, s.shape[0]) + s.shape[2:], BF16) for s in ss]
    out_shape += [jax.ShapeDtypeStruct((2, s.shape[0], s.shape[2] // 2, s.shape[3]), BF16) for s in ss]
    return pl.pallas_call(
        body, out_shape=out_shape, in_specs=[_ANY] * nt, out_specs=[_ANY] * (2 * nt),
        scratch_shapes=[pltpu.SemaphoreType.DMA((nt, 6)), pltpu.SemaphoreType.DMA((nt, 6))],
        name=name)(*ss)[:nt]


def _chip_sum(s_t, r3_t, place, *, name):
    nl, _, rh, cdim = s_t.shape
    tr = _pick(rh, (256, 128, 176, 64, 32, 16))

    def body(pz, s_ref, r_ref, o_ref):
        acc = s_ref[...].astype(F32) + r_ref[0].astype(F32)
        o_ref[...] = (acc + r_ref[1].astype(F32)) + r_ref[2].astype(F32)

    grid_spec = pltpu.PrefetchScalarGridSpec(
        num_scalar_prefetch=1, grid=(nl, rh // tr),
        in_specs=[pl.BlockSpec((None, None, tr, cdim), lambda l, r, pz: (l, pz[0], r, 0)),
                  pl.BlockSpec((3, None, tr, cdim), lambda l, r, pz: (0, l, r, 0))],
        out_specs=pl.BlockSpec((None, None, tr, cdim), lambda l, r, pz: (l, pz[1], r, 0)))
    return pl.pallas_call(
        body, grid_spec=grid_spec, out_shape=jax.ShapeDtypeStruct((nl, 2, rh, cdim), F32),
        compiler_params=_cparams(_PAR, _PAR), name=name)(place, s_t, r3_t)


def _join_halves(reds, *, name):
    nt = len(reds)

    def body(*refs):
        out_refs = refs[nt:2 * nt]
        send_sems, recv_sems = refs[2 * nt:]
        x, y, c = _place()
        sibling = (x, y, 1 - c)
        copies = []
        for t in range(nt):
            cp = _remote(out_refs[t].at[:, c], out_refs[t].at[:, c], send_sems.at[t], recv_sems.at[t], sibling)
            cp.start()
            copies.append(cp)
        for t, cp in enumerate(copies):
            cp.wait_send()
            got = out_refs[t].at[:, 1 - c]
            _remote(got, got, send_sems.at[t], recv_sems.at[t], sibling).wait_recv()

    return pl.pallas_call(
        body, out_shape=[jax.ShapeDtypeStruct(r.shape, F32) for r in reds],
        in_specs=[_ANY] * nt, out_specs=[_ANY] * nt, input_output_aliases={t: t for t in range(nt)},
        scratch_shapes=[pltpu.SemaphoreType.DMA((nt,)), pltpu.SemaphoreType.DMA((nt,))],
        name=name)(*reds)


def _sum_devices(g, *, name):
    _, r, cdim = g.shape
    tr = _pick(r, (512, 256, 128, 64, 32, 16, 8))

    def body(g_ref, o_ref):
        acc = g_ref[0]
        for e in range(1, NDEV):
            acc = acc + g_ref[e]
        o_ref[...] = acc

    return pl.pallas_call(
        body, grid=(r // tr,), in_specs=[pl.BlockSpec((NDEV, tr, cdim), lambda i: (0, i, 0))],
        out_specs=pl.BlockSpec((tr, cdim), lambda i: (i, 0)),
        out_shape=jax.ShapeDtypeStruct((r, cdim), F32),
        compiler_params=_cparams(_PAR), name=name)(g)


def _mod_fwd(c_all, mod_w, mod_b_cols, *, name):
    nl, d, n = mod_w.shape
    ne = c_all.shape[0]
    tn = _pick(n, (768, 512, 384, 256, 128))

    def body(c_ref, w_ref, b_ref, o_ref):
        cv = c_ref[...]
        act = (cv * _sigmoid(cv)).astype(BF16)
        o_ref[...] = jnp.dot(act, w_ref[...].astype(BF16), preferred_element_type=F32) + b_ref[...]

    return pl.pallas_call(
        body, grid=(nl, n // tn),
        in_specs=[pl.BlockSpec((ne, d), lambda i, j: (0, 0)),
                  pl.BlockSpec((None, d, tn), lambda i, j: (i, 0, j)),
                  pl.BlockSpec((None, 1, tn), lambda i, j: (i, 0, j))],
        out_specs=pl.BlockSpec((None, ne, tn), lambda i, j: (i, 0, j)),
        out_shape=jax.ShapeDtypeStruct((nl, ne, n), F32),
        compiler_params=_cparams(_PAR, _PAR), name=name)(c_all, mod_w, mod_b_cols)


def _adam_math(w, g, m, v):
    m2 = ADAM_B1 * m + (1.0 - ADAM_B1) * g
    v2 = ADAM_B2 * v + (1.0 - ADAM_B2) * (g * g)
    m_hat = m2 / (1.0 - ADAM_B1 ** ADAM_STEP)
    v_hat = v2 / (1.0 - ADAM_B2 ** ADAM_STEP)
    delta = -ADAM_LR * (m_hat / (jnp.sqrt(v_hat) + ADAM_EPS) + ADAM_WD * w)
    return delta, m2, v2


def _adamw(w, g, m, v, *, name):
    rows, cdim = w.shape
    tr = _pick(rows, tuple(t for t in (512, 256, 128, 64, 32, 16, 8) if t * cdim <= 256 * 1024))

    def body(w_ref, g_ref, m_ref, v_ref, d_ref, mo_ref, vo_ref):
        d_ref[...], mo_ref[...], vo_ref[...] = _adam_math(w_ref[...], g_ref[...], m_ref[...], v_ref[...])

    spec = pl.BlockSpec((tr, cdim), lambda i: (i, 0))
    return pl.pallas_call(
        body, grid=(rows // tr,), in_specs=[spec] * 4, out_specs=[spec] * 3,
        out_shape=[jax.ShapeDtypeStruct((rows, cdim), F32)] * 3,
        compiler_params=_cparams(_PAR), name=name)(w, g, m, v)


def _mod_w_update(c_t, dmod, w, m, v, *, name):
    nl, d, n = w.shape
    ne = c_t.shape[1]
    tr = _pick(d, (128, 64, 32, 16, 8))

    def body(c_ref, dm_ref, w_ref, m_ref, v_ref, g_ref, d_ref, mo_ref, vo_ref):
        cv = c_ref[...]
        act = cv * _sigmoid(cv)
        dm = dm_ref[...]
        g = act[:, 0:1] * dm[0:1, :]
        for e in range(1, ne):
            g = g + act[:, e:e + 1] * dm[e:e + 1, :]
        g_ref[...] = g
        d_ref[...], mo_ref[...], vo_ref[...] = _adam_math(w_ref[...], g, m_ref[...], v_ref[...])

    big = pl.BlockSpec((None, tr, n), lambda i, r: (i, r, 0))
    return pl.pallas_call(
        body, grid=(nl, d // tr),
        in_specs=[pl.BlockSpec((tr, ne), lambda i, r: (r, 0)),
                  pl.BlockSpec((None, ne, n), lambda i, r: (i, 0, 0)), big, big, big],
        out_specs=[big] * 4, out_shape=[jax.ShapeDtypeStruct((nl, d, n), F32)] * 4,
        compiler_params=_cparams(_PAR, _PAR), name=name)(c_t, dmod, w, m, v)


PACK_ROWS = 256


def _pack(arrs):
    flat = jnp.concatenate([a.reshape(-1) for a in arrs])
    tile = PACK_ROWS * LANES
    pad = (-flat.shape[0]) % tile
    return jnp.pad(flat, (0, pad)).reshape(-1, LANES)


def _unpack(packed, shapes, lead=()):
    flat = packed.reshape(lead + (-1,))
    out, off = [], 0
    for shp in shapes:
        size = 1
        for s in shp:
            size *= s
        out.append(flat[..., off:off + size].reshape(lead + tuple(shp)))
        off += size
    return out


SMALL_SHARD_AXIS = {"a_pw1_b": 1, "a_dw_w": 2, "a_dw_b": 1, "a_ln_g": 1, "a_ln_b": 1, "a_pw2_b": 1,
                    "c_conv_w": 2, "f_dw_w": 2}
SMALL_REPLICATED = ("norm_pre_mix", "norm_post_mix", "norm_pre_ffn", "norm_post_ffn",
                    "b_group_b", "b_scale", "f_dw_b")
WEIGHT_ORDER = ("mod_w", "mod_b", "norm_pre_mix", "norm_post_mix", "norm_pre_ffn", "norm_post_ffn",
                "a_pw1_w", "a_pw1_b", "a_dw_w", "a_dw_b", "a_ln_g", "a_ln_b", "a_pw2_w", "a_pw2_b",
                "b_group_w", "b_group_b", "b_scale", "c_in_w", "c_conv_w", "c_out_w",
                "f_up_w", "f_dw_w", "f_dw_b", "f_down_w")


def _as_layers_rows_cols(name, w):
    if name == "b_group_w":
        return w.reshape(w.shape[1], w.shape[2], w.shape[3])
    return w


def _step(x, c, loss_target, w, m, v):
    xi, yi, ci = _place()
    shard = 2 * xi + yi
    example = 4 * xi + 2 * yi + ci
    d = x.shape[-1]

    small_names = tuple(SMALL_SHARD_AXIS)
    gathered0 = _all_gather8(_pack([c] + [w[k] for k in small_names]), name="gather_small")
    parts = _unpack(gathered0, [c.shape] + [w[k].shape for k in small_names], lead=(NDEV,))
    c_all = parts[0].reshape(NDEV, d)
    p = {}
    for k, part in zip(small_names, parts[1:]):
        p[k] = jnp.concatenate([part[2 * s] for s in range(NSHARD)], axis=SMALL_SHARD_AXIS[k])
    for k in SMALL_REPLICATED:
        p[k] = w[k]

    ncol = w["mod_w"].shape[2]
    mod_b_cols = lax.dynamic_slice_in_dim(w["mod_b"], shard * ncol, ncol, axis=1).reshape(DEPTH, 1, ncol)
    mod_part = _mod_fwd(c_all, w["mod_w"], mod_b_cols, name="mod_fwd")
    gathered1 = _all_gather8(mod_part.reshape(DEPTH * NDEV, ncol), name="gather_mod")
    mod_all = gathered1.reshape(NSHARD, 2, DEPTH, NDEV, ncol)[:, 0]
    mod_mine = lax.dynamic_index_in_dim(mod_all, example, axis=2, keepdims=False)
    mod = jnp.transpose(mod_mine, (1, 0, 2)).reshape(DEPTH, 6, d)

    shard_arr = shard.reshape(1).astype(jnp.int32)
    bufs, kinds = [], []
    for k in BIG:
        wk = _as_layers_rows_cols(k, w[k])
        nl, r, cdim = wk.shape
        if nl >= 2:
            a, rh = nl // 2, r
        else:
            a, rh = 1, r // 2
        if k in COL_SHARDED:
            kind, out_shape = "col", (NSHARD, 2, a, rh, cdim)
        elif nl >= 2:
            kind, out_shape = "row", (2, a, NSHARD, rh, cdim)
        else:
            kind, out_shape = "row1", (1, NSHARD, 2, rh, cdim)
        kinds.append(kind)
        bufs.append(_cast_into_gathered(wk.reshape(2, a, rh, cdim), kind, out_shape, shard_arr, name=f"cast_{k}"))
    full = _gather_weights(bufs, kinds, name="gather_weights")
    for k, f in zip(BIG, full):
        nl, r, cdim = _as_layers_rows_cols(k, w[k]).shape
        if k in COL_SHARDED:
            p[k] = f.reshape(NSHARD, nl, r, cdim)
        elif k == "b_group_w":
            p[k] = f.reshape(1, nl, NSHARD * r, cdim)
        else:
            p[k] = f.reshape(nl, NSHARD * r, cdim)

    loss, grad_x, dmod, small, big = _local_step(x[0], loss_target[0], mod, p)

    gs, layers_of = [], []
    for k in BIG:
        ks = []
        for g in big[k]:
            if k == "b_group_w":
                ng, rr, cc = g.shape
                g = jnp.transpose(g.reshape(ng, NSHARD, rr // NSHARD, cc), (1, 0, 2, 3)).reshape(NSHARD, -1, cc)
            elif k not in COL_SHARDED:
                g = g.reshape(NSHARD, g.shape[0] // NSHARD, g.shape[1])
            ks.append(len(gs))
            gs.append(g.reshape(NSHARD, 2, g.shape[1] // 2, g.shape[2]))
        layers_of.append(ks)
    half = ci.reshape(1).astype(jnp.int32)
    place = jnp.stack([shard, ci]).astype(jnp.int32)
    r1 = _pair_exchange(gs, layers_of, name="grad_pair_exchange")
    ss = []
    for k, ks, r1_t in zip(BIG, layers_of, r1):
        s_t = lax.empty(r1_t.shape, BF16)
        for l, i in enumerate(ks):
            s_t = _pair_sum(gs[i], r1_t, s_t, l, half, name=f"grad_pair_sum_{k}_{l}")
        ss.append(s_t)
    r3 = _chip_exchange(ss, name="grad_chip_exchange")
    reds = [_chip_sum(s_t, r3_t, place, name=f"grad_chip_sum_{k}") for k, s_t, r3_t in zip(BIG, ss, r3)]
    joined = _join_halves(reds, name="grad_join_halves")
    grads = {k: j.reshape(w[k].shape) for k, j in zip(BIG, joined)}

    rep_names = SMALL_REPLICATED
    small_list = [small[k] for k in rep_names] + [small[k] for k in small_names] + [dmod]
    gathered2 = _all_gather8(_pack(small_list), name="gather_small_grads")
    summed = _sum_devices(gathered2, name="sum_small_grads")
    shapes = [s.shape for s in small_list]
    sums = _unpack(summed, shapes)
    for k, s in zip(rep_names, sums[:len(rep_names)]):
        grads[k] = s
    for k, s in zip(small_names, sums[len(rep_names):-1]):
        ax = SMALL_SHARD_AXIS[k]
        grads[k] = lax.dynamic_slice_in_dim(s, shard * w[k].shape[ax], w[k].shape[ax], axis=ax)
    grads["mod_b"] = sums[-1].reshape(w["mod_b"].shape)
    dmod_all = _unpack(gathered2, shapes, lead=(NDEV,))[-1].reshape(NDEV, DEPTH, NSHARD, ncol)
    dmod_cols = jnp.transpose(lax.dynamic_index_in_dim(dmod_all, shard, axis=2, keepdims=False), (1, 0, 2))

    delta, new_m, new_v = {}, {}, {}
    grads["mod_w"], delta["mod_w"], new_m["mod_w"], new_v["mod_w"] = _mod_w_update(
        c_all.T, dmod_cols, w["mod_w"], m["mod_w"], v["mod_w"], name="mod_w_update")
    for k in BIG:
        cdim = w[k].shape[-1]
        outs = _adamw(*[t.reshape(-1, cdim) for t in (w[k], grads[k], m[k], v[k])], name=f"adamw_{k}")
        delta[k], new_m[k], new_v[k] = [o.reshape(w[k].shape) for o in outs]
    rest = ("mod_b",) + rep_names + small_names
    packs = [_pack([t[k] for k in rest]) for t in (w, grads, m, v)]
    outs = _adamw(*packs, name="adamw_small")
    rest_shapes = [w[k].shape for k in rest]
    for dst, o in zip((delta, new_m, new_v), outs):
        for k, t in zip(rest, _unpack(o, rest_shapes)):
            dst[k] = t

    loss_all = lax.psum(loss[0, 0], ("x", "y", "c"))
    return (loss_all, grad_x[None], *[grads[k] for k in WEIGHT_ORDER], *[delta[k] for k in WEIGHT_ORDER],
            *[new_m[k] for k in WEIGHT_ORDER], *[new_v[k] for k in WEIGHT_ORDER])


def kernel(x, c, mod_w, mod_b, norm_pre_mix, norm_post_mix, norm_pre_ffn, norm_post_ffn, a_pw1_w, a_pw1_b, a_dw_w, a_dw_b, a_ln_g, a_ln_b, a_pw2_w, a_pw2_b, b_group_w, b_group_b, b_scale, c_in_w, c_conv_w, c_out_w, f_up_w, f_dw_w, f_dw_b, f_down_w, loss_target, m_mod_w, m_mod_b, m_norm_pre_mix, m_norm_post_mix, m_norm_pre_ffn, m_norm_post_ffn, m_a_pw1_w, m_a_pw1_b, m_a_dw_w, m_a_dw_b, m_a_ln_g, m_a_ln_b, m_a_pw2_w, m_a_pw2_b, m_b_group_w, m_b_group_b, m_b_scale, m_c_in_w, m_c_conv_w, m_c_out_w, m_f_up_w, m_f_dw_w, m_f_dw_b, m_f_down_w, v_mod_w, v_mod_b, v_norm_pre_mix, v_norm_post_mix, v_norm_pre_ffn, v_norm_post_ffn, v_a_pw1_w, v_a_pw1_b, v_a_dw_w, v_a_dw_b, v_a_ln_g, v_a_ln_b, v_a_pw2_w, v_a_pw2_b, v_b_group_w, v_b_group_b, v_b_scale, v_c_in_w, v_c_conv_w, v_c_out_w, v_f_up_w, v_f_dw_w, v_f_dw_b, v_f_down_w):
    given = dict(locals())
    w = {k: given[k] for k in WEIGHT_ORDER}
    m = {k: given["m_" + k] for k in WEIGHT_ORDER}
    v = {k: given["v_" + k] for k in WEIGHT_ORDER}
    return _step(x, c, loss_target, w, m, v)
```

```python
import jax
import jax.numpy as jnp
from jax import lax
from jax.experimental import pallas as pl
from jax.experimental.pallas import tpu as pltpu

F32 = jnp.float32
BF16 = jnp.bfloat16

DEPTH = 4
N_MIXERS = 3
CONF_CONV_WIDTH = 31
POOL_WINDOWS = (2, 4, 8, 16)
RMS_EPS = 1e-6
LN_EPS = 1e-5
ADAM_LR = 0.001
ADAM_B1 = 0.9
ADAM_B2 = 0.999
ADAM_EPS = 1e-08
ADAM_WD = 0.01
ADAM_STEP = 10

TM_ROW = 512
TM_CONV = 128
TK_TOKENS = 2048
TM_MM = 1024
TN_MM = 1024
SUBLANES = 8
LANES = 128
NSHARD = 4
NDEV = 8
HALO_A = 32
HALO_POOL = 16
HALO_3 = 8
HALO_BF16 = 16
VMEM_LIMIT = 56 * 1024 * 1024

_PAR = "parallel"
_ARB = "arbitrary"


def _cparams(*sem):
    return pltpu.CompilerParams(dimension_semantics=sem, vmem_limit_bytes=VMEM_LIMIT)


def _pick(n, prefs):
    for p in prefs:
        if p <= n and n % p == 0:
            return p
    return n


def _row_spec(tm, width):
    return pl.BlockSpec((tm, width), lambda i: (i, 0))


def _vec_spec(rows, width):
    return pl.BlockSpec((rows, width), lambda i: (0, 0))


def _prev_spec(tm, hb, width):
    return pl.BlockSpec((hb, width), lambda i: (jnp.maximum(i * (tm // hb) - 1, 0), 0))


def _next_spec(tm, hb, width, total):
    last = total // hb - 1
    return pl.BlockSpec((hb, width), lambda i: (jnp.minimum((i + 1) * (tm // hb), last), 0))


def _sum8(v):
    r, c = v.shape
    return jnp.sum(v.reshape(r // SUBLANES, SUBLANES, c), axis=0)


def _rms(x):
    r = lax.rsqrt(jnp.mean(x * x, axis=-1, keepdims=True) + RMS_EPS)
    return x * r, r


def _rms_bwd(dy, xn, r):
    return r * (dy - xn * jnp.mean(dy * xn, axis=-1, keepdims=True))


def _sigmoid(x):
    return 1.0 / (1.0 + jnp.exp(-x))


_DIMS = {"nn": ((1,), (0,)), "nt": ((1,), (1,)), "tn": ((0,), (0,))}


def _mm(a, b, *, mode, name, out_dtype=F32, bias=None, tm=TM_MM, tn=TN_MM, tk=None, layer=None):
    sharded = layer is not None
    if mode == "nn":
        m, k = a.shape
        n = NSHARD * b.shape[3] if sharded else b.shape[1]
    elif mode == "nt":
        m, k = a.shape
        n = b.shape[2] if sharded else b.shape[0]
    else:
        (k, m), (_, n) = a.shape, b.shape
    ns = n // NSHARD
    ks = k // NSHARD
    tm = _pick(m, (tm, 1408, 512, 256, 128))
    if sharded and mode != "nt":
        tn = _pick(ns, (1408, 768, 512, 256, 128))
    else:
        tn = _pick(n, (tn, 1408, 512, 256, 128))
    if sharded and mode == "nt":
        tk = _pick(ks, (1408, 768, 512, 256, 128))
    else:
        tk = _pick(k, (tk or k, 2816, 1024, 512, 256, 128))
    nk = k // tk
    per_n = ns // tn if sharded and mode != "nt" else 1
    per_k = ks // tk if sharded and mode == "nt" else 1
    dims = (_DIMS[mode], ((), ()))

    def split(idx, per):
        return (idx, 0) if per == 1 else (idx // per, idx % per)

    def body(*refs):
        a_ref, b_ref = refs[0], refs[1]
        bias_ref = refs[2] if bias is not None else None
        o_ref = refs[3] if bias is not None else refs[2]
        part = lax.dot_general(a_ref[...].astype(BF16), b_ref[...].astype(BF16), dims,
                               preferred_element_type=F32)

        def finish(r):
            if bias_ref is not None:
                r = r + bias_ref[...]
            o_ref[...] = r.astype(out_dtype)

        if nk == 1:
            finish(part)
        else:
            acc_ref = refs[-1]
            kk = pl.program_id(2)

            @pl.when(kk == 0)
            def _():
                acc_ref[...] = part

            @pl.when(kk > 0)
            def _():
                acc_ref[...] += part

            @pl.when(kk == nk - 1)
            def _():
                finish(acc_ref[...])

    out_spec = pl.BlockSpec((tm, tn), lambda i, j, kk: (i, j))
    out_shape = jax.ShapeDtypeStruct((m, n), out_dtype)
    if mode == "nn":
        a_spec = pl.BlockSpec((tm, tk), lambda i, j, kk: (i, kk))
        if sharded:
            b_spec = pl.BlockSpec((None, None, tk, tn),
                                  lambda i, j, kk: (split(j, per_n)[0], layer, kk, split(j, per_n)[1]))
        else:
            b_spec = pl.BlockSpec((tk, tn), lambda i, j, kk: (kk, j))
    elif mode == "nt":
        a_spec = pl.BlockSpec((tm, tk), lambda i, j, kk: (i, kk))
        if sharded:
            b_spec = pl.BlockSpec((None, None, tn, tk),
                                  lambda i, j, kk: (split(kk, per_k)[0], layer, j, split(kk, per_k)[1]))
        else:
            b_spec = pl.BlockSpec((tn, tk), lambda i, j, kk: (j, kk))
    else:
        a_spec = pl.BlockSpec((tk, tm), lambda i, j, kk: (kk, i))
        b_spec = pl.BlockSpec((tk, tn), lambda i, j, kk: (kk, j))
        if sharded:
            out_spec = pl.BlockSpec((None, tm, tn), lambda i, j, kk: (split(j, per_n)[0], i, split(j, per_n)[1]))
            out_shape = jax.ShapeDtypeStruct((NSHARD, m, ns), out_dtype)
    in_specs = [a_spec, b_spec]
    args = [a, b]
    if bias is not None:
        in_specs.append(pl.BlockSpec((1, tn), lambda i, j, kk: (0, j)))
        args.append(bias)
    return pl.pallas_call(
        body, grid=(m // tm, n // tn, nk), in_specs=in_specs, out_specs=out_spec, out_shape=out_shape,
        scratch_shapes=[pltpu.VMEM((tm, tn), F32)] if nk > 1 else [],
        compiler_params=_cparams(_PAR, _PAR, _ARB), name=name)(*args)


def _mm_group(a, b, *, mode, name, out_dtype=F32, tm=2048):
    t = a.shape[0]
    tm = _pick(t, (tm, 1024, 512, 256, 128))
    nt_ = t // tm
    g = len(POOL_WINDOWS)
    gd = a.shape[1] // g
    dims = (_DIMS[mode], ((), ()))

    if mode == "tn":
        def body(a_ref, b_ref, o_ref, acc_ref):
            kk = pl.program_id(1)
            part = lax.dot_general(a_ref[...].astype(BF16), b_ref[...].astype(BF16), dims,
                                   preferred_element_type=F32)

            @pl.when(kk == 0)
            def _():
                acc_ref[...] = part

            @pl.when(kk > 0)
            def _():
                acc_ref[...] += part

            @pl.when(kk == nt_ - 1)
            def _():
                o_ref[...] = acc_ref[...].astype(out_dtype)

        return pl.pallas_call(
            body, grid=(g, nt_),
            in_specs=[pl.BlockSpec((tm, gd), lambda gi, kk: (kk, gi)),
                      pl.BlockSpec((tm, gd), lambda gi, kk: (kk, gi))],
            out_specs=pl.BlockSpec((None, gd, gd), lambda gi, kk: (gi, 0, 0)),
            out_shape=jax.ShapeDtypeStruct((g, gd, gd), out_dtype),
            scratch_shapes=[pltpu.VMEM((gd, gd), F32)],
            compiler_params=_cparams(_PAR, _ARB), name=name)(a, b)

    def body(a_ref, b_ref, o_ref):
        o_ref[...] = lax.dot_general(a_ref[...].astype(BF16), b_ref[...].astype(BF16), dims,
                                     preferred_element_type=F32).astype(out_dtype)

    return pl.pallas_call(
        body, grid=(nt_, g),
        in_specs=[pl.BlockSpec((tm, gd), lambda i, gi: (i, gi)),
                  pl.BlockSpec((None, gd, gd), lambda i, gi: (gi, 0, 0))],
        out_specs=pl.BlockSpec((tm, gd), lambda i, gi: (i, gi)),
        out_shape=jax.ShapeDtypeStruct((t, g * gd), out_dtype),
        compiler_params=_cparams(_PAR, _PAR), name=name)(a, b)


def _pre(x, gpre, sc, sh):
    xn, r = _rms(x)
    return (xn * gpre) * (1.0 + sc) + sh, xn, r


def _fwd_first(x, gpre, sc, sh, *, h_dtype, name):
    t, d = x.shape
    tm = _pick(t, (TM_ROW, 256, 128))

    def body(x_ref, gpre_ref, sc_ref, sh_ref, h_ref):
        h, _, _ = _pre(x_ref[...], gpre_ref[...], sc_ref[...], sh_ref[...])
        h_ref[...] = h.astype(h_dtype)

    return pl.pallas_call(
        body, grid=(t // tm,),
        in_specs=[_row_spec(tm, d)] + [_vec_spec(1, d)] * 3,
        out_specs=_row_spec(tm, d), out_shape=jax.ShapeDtypeStruct((t, d), h_dtype),
        compiler_params=_cparams(_PAR), name=name)(x, gpre, sc, sh)


def _fwd_mid(x, y, gpost, gt, gpre, sc, sh, *, h_dtype, name):
    t, d = x.shape
    tm = _pick(t, (TM_ROW, 256, 128))

    def body(x_ref, y_ref, gpost_ref, gt_ref, gpre_ref, sc_ref, sh_ref, xn_ref, h_ref):
        yn, _ = _rms(y_ref[...].astype(F32))
        x_new = x_ref[...] + gt_ref[...] * (yn * gpost_ref[...])
        xn_ref[...] = x_new
        h, _, _ = _pre(x_new, gpre_ref[...], sc_ref[...], sh_ref[...])
        h_ref[...] = h.astype(h_dtype)

    return pl.pallas_call(
        body, grid=(t // tm,),
        in_specs=[_row_spec(tm, d)] * 2 + [_vec_spec(1, d)] * 5,
        out_specs=[_row_spec(tm, d)] * 2,
        out_shape=[jax.ShapeDtypeStruct((t, d), F32), jax.ShapeDtypeStruct((t, d), h_dtype)],
        compiler_params=_cparams(_PAR), name=name)(x, y, gpost, gt, gpre, sc, sh)


def _post_bwd(dx, y, gpost, gt):
    yn, r2 = _rms(y)
    dyn = dx * (gt * gpost)
    dy = _rms_bwd(dyn, yn, r2)
    return dy, dx * yn


def _last_fwd_bwd(x, y, target, gpost, gt, *, name):
    t, d = x.shape
    tm = _pick(t, (TM_ROW, 256, 128))
    n = t // tm

    def body(x_ref, y_ref, tg_ref, gpost_ref, gt_ref, dx_ref, dy_ref, dgpost_ref, dgt_ref, sdy_ref,
             loss_ref, qa, sa, la):
        i = pl.program_id(0)

        @pl.when(i == 0)
        def _():
            qa[...] = jnp.zeros_like(qa)
            sa[...] = jnp.zeros_like(sa)
            la[...] = jnp.zeros_like(la)

        yv = y_ref[...].astype(F32)
        yn, r2 = _rms(yv)
        gt_v, gpost_v = gt_ref[...], gpost_ref[...]
        err = x_ref[...] + gt_v * (yn * gpost_v) - tg_ref[...]
        la[...] += _sum8(err * err)
        dx = err * (1.0 / d)
        dx_ref[...] = dx
        dy = _rms_bwd(dx * (gt_v * gpost_v), yn, r2)
        dy_ref[...] = dy.astype(dy_ref.dtype)
        qa[...] += _sum8(dx * yn)
        sa[...] += _sum8(dy)

        @pl.when(i == n - 1)
        def _():
            q = jnp.sum(qa[...], axis=0, keepdims=True)
            dgpost_ref[...] = gt_v * q
            dgt_ref[...] = gpost_v * q
            sdy_ref[...] = jnp.sum(sa[...], axis=0, keepdims=True)
            tot = jnp.sum(jnp.sum(la[...], axis=0, keepdims=True), axis=1, keepdims=True)
            loss_ref[...] = tot * (0.5 / d)

    return pl.pallas_call(
        body, grid=(n,),
        in_specs=[_row_spec(tm, d)] * 3 + [_vec_spec(1, d)] * 2,
        out_specs=[_row_spec(tm, d)] * 2 + [_vec_spec(1, d)] * 3 + [_vec_spec(1, 1)],
        out_shape=[jax.ShapeDtypeStruct((t, d), F32), jax.ShapeDtypeStruct((t, d), BF16)]
        + [jax.ShapeDtypeStruct((1, d), F32)] * 3 + [jax.ShapeDtypeStruct((1, 1), F32)],
        scratch_shapes=[pltpu.VMEM((SUBLANES, d), F32)] * 3,
        compiler_params=_cparams(_ARB), name=name)(x, y, target, gpost, gt)


def _bwd_mid(dx_new, dh, x_in, gpre, sc, y_prev, gpost_p, gt_p, *, dy_dtype, name):
    t, d = x_in.shape
    tm = _pick(t, (TM_ROW, 256, 128))
    n = t // tm

    def body(dxn_ref, dh_ref, x_ref, gpre_ref, sc_ref, y_ref, gpost_ref, gt_ref,
             dx_ref, dy_ref, dsh_ref, dsc_ref, dgpre_ref, dgpost_ref, dgt_ref, sdy_ref, a1, a2, aq, asd):
        i = pl.program_id(0)

        @pl.when(i == 0)
        def _():
            for a in (a1, a2, aq, asd):
                a[...] = jnp.zeros_like(a)

        dh_v = dh_ref[...]
        xn, r = _rms(x_ref[...])
        dx = dxn_ref[...] + _rms_bwd(dh_v * ((1.0 + sc_ref[...]) * gpre_ref[...]), xn, r)
        dx_ref[...] = dx
        a1[...] += _sum8(dh_v)
        a2[...] += _sum8(dh_v * xn)
        dy, dxyn = _post_bwd(dx, y_ref[...].astype(F32), gpost_ref[...], gt_ref[...])
        dy_ref[...] = dy.astype(dy_dtype)
        aq[...] += _sum8(dxyn)
        asd[...] += _sum8(dy)

        @pl.when(i == n - 1)
        def _():
            s2 = jnp.sum(a2[...], axis=0, keepdims=True)
            q = jnp.sum(aq[...], axis=0, keepdims=True)
            dsh_ref[...] = jnp.sum(a1[...], axis=0, keepdims=True)
            dsc_ref[...] = gpre_ref[...] * s2
            dgpre_ref[...] = (1.0 + sc_ref[...]) * s2
            dgpost_ref[...] = gt_ref[...] * q
            dgt_ref[...] = gpost_ref[...] * q
            sdy_ref[...] = jnp.sum(asd[...], axis=0, keepdims=True)

    return pl.pallas_call(
        body, grid=(n,),
        in_specs=[_row_spec(tm, d)] * 3 + [_vec_spec(1, d)] * 2 + [_row_spec(tm, d)] + [_vec_spec(1, d)] * 2,
        out_specs=[_row_spec(tm, d)] * 2 + [_vec_spec(1, d)] * 6,
        out_shape=[jax.ShapeDtypeStruct((t, d), F32), jax.ShapeDtypeStruct((t, d), dy_dtype)]
        + [jax.ShapeDtypeStruct((1, d), F32)] * 6,
        scratch_shapes=[pltpu.VMEM((SUBLANES, d), F32)] * 4,
        compiler_params=_cparams(_ARB), name=name)(dx_new, dh, x_in, gpre, sc, y_prev, gpost_p, gt_p)


def _bwd_first(dx_new, dh, x_in, gpre, sc, *, name):
    t, d = x_in.shape
    tm = _pick(t, (TM_ROW, 256, 128))
    n = t // tm

    def body(dxn_ref, dh_ref, x_ref, gpre_ref, sc_ref, dx_ref, dsh_ref, dsc_ref, dgpre_ref, a1, a2):
        i = pl.program_id(0)

        @pl.when(i == 0)
        def _():
            a1[...] = jnp.zeros_like(a1)
            a2[...] = jnp.zeros_like(a2)

        dh_v = dh_ref[...]
        xn, r = _rms(x_ref[...])
        dx_ref[...] = dxn_ref[...] + _rms_bwd(dh_v * ((1.0 + sc_ref[...]) * gpre_ref[...]), xn, r)
        a1[...] += _sum8(dh_v)
        a2[...] += _sum8(dh_v * xn)

        @pl.when(i == n - 1)
        def _():
            s2 = jnp.sum(a2[...], axis=0, keepdims=True)
            dsh_ref[...] = jnp.sum(a1[...], axis=0, keepdims=True)
            dsc_ref[...] = gpre_ref[...] * s2
            dgpre_ref[...] = (1.0 + sc_ref[...]) * s2

    return pl.pallas_call(
        body, grid=(n,),
        in_specs=[_row_spec(tm, d)] * 3 + [_vec_spec(1, d)] * 2,
        out_specs=[_row_spec(tm, d)] + [_vec_spec(1, d)] * 3,
        out_shape=[jax.ShapeDtypeStruct((t, d), F32)] + [jax.ShapeDtypeStruct((1, d), F32)] * 3,
        scratch_shapes=[pltpu.VMEM((SUBLANES, d), F32)] * 2,
        compiler_params=_cparams(_ARB), name=name)(dx_new, dh, x_in, gpre, sc)


ROWS_BLK = 16
COLS_BLK = 256


def _bcast_rows(dst, src_ref, first, nrows):
    for k in range(nrows):
        dst[first + k] = jnp.broadcast_to(src_ref[pl.ds(k, 1), :], dst.shape[1:])


def _shifted_rows(x, off, rows):
    if off % SUBLANES == 0:
        return x[off:off + rows]
    return pltpu.roll(x, x.shape[0] - off, axis=0)[:rows]


def _conv3_blk(buf, wb, first, rows, cols):
    base = first - first % SUBLANES
    window = buf[pl.ds(base, rows + SUBLANES), cols]
    xs = [_shifted_rows(window, first - base + k, rows) for k in range(3)]
    out = xs[0] * wb[0, pl.ds(0, rows), cols]
    out = out + xs[1] * wb[1, pl.ds(0, rows), cols]
    out = out + xs[2] * wb[2, pl.ds(0, rows), cols]
    return out + wb[3, pl.ds(0, rows), cols], xs


def _ffn_gate_fwd(u, w, b, *, name):
    t, f2 = u.shape
    f = f2 // 2
    tm = _pick(t, (TM_CONV,))
    hb = HALO_BF16
    rb = ROWS_BLK
    cw = _pick(f, (COLS_BLK, LANES))

    def body(u_ref, up_ref, w_ref, b_ref, a_ref, v_ref, buf, wb):
        i = pl.program_id(0)
        buf[pl.ds(hb, tm), :] = u_ref[...].astype(F32)
        buf[pl.ds(0, hb), :] = jnp.where(i > 0, up_ref[...].astype(F32), 0.0)
        _bcast_rows(wb, w_ref, 0, 3)
        _bcast_rows(wb, b_ref, 3, 1)
        for c0 in range(0, f, cw):
            gcols, vcols = pl.ds(c0, cw), pl.ds(f + c0, cw)
            for r0 in range(0, tm, rb):
                rows = pl.ds(r0, rb)
                vg, _ = _conv3_blk(buf, wb, hb - 2 + r0, rb, gcols)
                vv, _ = _conv3_blk(buf, wb, hb - 2 + r0, rb, vcols)
                v_ref[rows, gcols] = vg.astype(BF16)
                v_ref[rows, vcols] = vv.astype(BF16)
                a_ref[rows, gcols] = (vg * _sigmoid(vg) * vv).astype(BF16)

    return pl.pallas_call(
        body, grid=(t // tm,),
        in_specs=[_row_spec(tm, f2), _prev_spec(tm, hb, f2), _vec_spec(3, f2), _vec_spec(1, f2)],
        out_specs=[_row_spec(tm, f), _row_spec(tm, f2)],
        out_shape=[jax.ShapeDtypeStruct((t, f), BF16), jax.ShapeDtypeStruct((t, f2), BF16)],
        scratch_shapes=[pltpu.VMEM((tm + hb, f2), F32), pltpu.VMEM((4, rb, f2), F32)],
        compiler_params=_cparams(_PAR), name=name)(u, u, w, b)


def _ffn_gate_bwd(u, v, da, w, *, name):
    t, f2 = u.shape
    f = f2 // 2
    tm = _pick(t, (TM_CONV,))
    hb = HALO_BF16
    n = t // tm
    rb = ROWS_BLK
    cw = _pick(f, (COLS_BLK, LANES))
    blocks = [(r0, rb) for r0 in range(0, tm, rb)] + [(tm, hb)]

    def body(u_ref, v_ref, vn_ref, da_ref, dan_ref, w_ref, du_ref, dw_ref, db_ref, dvbuf, wb, wacc, bacc):
        i = pl.program_id(0)

        @pl.when(i == 0)
        def _():
            wacc[...] = jnp.zeros_like(wacc)
            bacc[...] = jnp.zeros_like(bacc)

        _bcast_rows(wb, w_ref, 0, 3)
        for c0 in range(0, f, cw):
            gcols, vcols = pl.ds(c0, cw), pl.ds(f + c0, cw)
            for r0, rows in blocks:
                if r0 < tm:
                    vg, vv = v_ref[pl.ds(r0, rows), gcols], v_ref[pl.ds(r0, rows), vcols]
                    dav = da_ref[pl.ds(r0, rows), gcols].astype(F32)
                else:
                    vg, vv = vn_ref[:, gcols], vn_ref[:, vcols]
                    dav = jnp.where(i < n - 1, dan_ref[:, gcols].astype(F32), 0.0)
                vg, vv = vg.astype(F32), vv.astype(F32)
                sg = _sigmoid(vg)
                dvg = dav * vv * (sg * (1.0 + vg * (1.0 - sg)))
                dvv = dav * (vg * sg)
                dvbuf[pl.ds(r0, rows), gcols] = dvg
                dvbuf[pl.ds(r0, rows), vcols] = dvv
                if r0 < tm:
                    bacc[:, gcols] += _sum8(dvg)
                    bacc[:, vcols] += _sum8(dvv)
        for c0 in range(0, f2, cw):
            cols = pl.ds(c0, cw)
            for r0 in range(0, tm, rb):
                uv = u_ref[pl.ds(r0, rb), cols].astype(F32)
                window = dvbuf[pl.ds(r0, rb + SUBLANES), cols]
                du = None
                for k in range(3):
                    dvk = _shifted_rows(window, 2 - k, rb)
                    term = dvk * wb[k, :, cols]
                    du = term if du is None else du + term
                    wacc[k, :, cols] += _sum8(uv * dvk)
                du_ref[pl.ds(r0, rb), cols] = du.astype(BF16)

        @pl.when(i == n - 1)
        def _():
            db_ref[...] = jnp.sum(bacc[...], axis=0, keepdims=True)
            dw_ref[...] = jnp.sum(wacc[...], axis=1)

    return pl.pallas_call(
        body, grid=(n,),
        in_specs=[_row_spec(tm, f2), _row_spec(tm, f2), _next_spec(tm, hb, f2, t),
                  _row_spec(tm, f), _next_spec(tm, hb, f, t), _vec_spec(3, f2)],
        out_specs=[_row_spec(tm, f2), _vec_spec(3, f2), _vec_spec(1, f2)],
        out_shape=[jax.ShapeDtypeStruct((t, f2), BF16), jax.ShapeDtypeStruct((3, f2), F32),
                   jax.ShapeDtypeStruct((1, f2), F32)],
        scratch_shapes=[pltpu.VMEM((tm + hb, f2), F32), pltpu.VMEM((3, rb, f2), F32),
                        pltpu.VMEM((3, SUBLANES, f2), F32), pltpu.VMEM((SUBLANES, f2), F32)],
        compiler_params=_cparams(_ARB), name=name)(u, v, v, da, da, w)


def _glu(u, b1, d):
    return (u[:, :d] + b1[:, :d]) * _sigmoid(u[:, d:] + b1[:, d:])


ROWS_TAPS = 32
ROWS_NORM = 16


def _fill_glu_buf(buf, u_ref, up_ref, b1_ref, i, tm, d):
    cw = _pick(d, (COLS_BLK, LANES))
    for c0 in range(0, d, cw):
        b1 = jnp.concatenate([b1_ref[:, pl.ds(c0, cw)], b1_ref[:, pl.ds(d + c0, cw)]], axis=1)
        up = jnp.concatenate([up_ref[:, pl.ds(c0, cw)], up_ref[:, pl.ds(d + c0, cw)]], axis=1)
        buf[pl.ds(0, HALO_A), pl.ds(c0, cw)] = jnp.where(i > 0, _glu(up, b1, cw), 0.0)
        for r0 in range(0, tm, ROWS_TAPS):
            rows = pl.ds(r0, ROWS_TAPS)
            uv = jnp.concatenate([u_ref[rows, pl.ds(c0, cw)], u_ref[rows, pl.ds(d + c0, cw)]], axis=1)
            buf[pl.ds(HALO_A + r0, ROWS_TAPS), pl.ds(c0, cw)] = _glu(uv, b1, cw)


def _taps31(buf, r0, cols, offs, use):
    nv = ROWS_TAPS // SUBLANES
    nrows = ROWS_TAPS + SUBLANES * (-(-max(offs) // SUBLANES))
    window = buf[pl.ds(r0, nrows), cols]
    shifted = {b: _shifted_rows(window, b, nrows - SUBLANES) if b else window
               for b in sorted({o % SUBLANES for o in offs})}
    for k, o in enumerate(offs):
        b, a = o % SUBLANES, o // SUBLANES
        use(k, [shifted[b][SUBLANES * (a + v):SUBLANES * (a + v + 1)] for v in range(nv)])


def _conv_taps_blk(buf, wb, r0, cols, offs):
    nv = ROWS_TAPS // SUBLANES
    acc = [None] * nv

    def use(k, rows):
        wk = wb[k, :, cols]
        for v in range(nv):
            term = rows[v] * wk
            acc[v] = term if acc[v] is None else acc[v] + term

    _taps31(buf, r0, cols, offs, use)
    return jnp.concatenate(acc, axis=0)


def _layernorm_parts(x):
    mu = jnp.mean(x, axis=-1, keepdims=True)
    xc = x - mu
    rstd = lax.rsqrt(jnp.mean(xc * xc, axis=-1, keepdims=True) + LN_EPS)
    return xc * rstd, rstd


_FWD_OFFS = tuple(HALO_A - (CONF_CONV_WIDTH - 1) + k for k in range(CONF_CONV_WIDTH))
_BWD_OFFS = tuple(CONF_CONV_WIDTH - 1 - k for k in range(CONF_CONV_WIDTH))


def _a_fwd(u1, b1, dww, dwb, lng, lnb, *, name):
    t, d2 = u1.shape
    d = d2 // 2
    tm = _pick(t, (TM_CONV,))

    def body(u_ref, up_ref, b1_ref, w_ref, wbias_ref, g_ref, bb_ref, o_ref, u3_ref, buf, wb):
        i = pl.program_id(0)
        _fill_glu_buf(buf, u_ref, up_ref, b1_ref, i, tm, d)
        _bcast_rows(wb, w_ref, 0, CONF_CONV_WIDTH)
        for c0 in range(0, d, LANES):
            cols = pl.ds(c0, LANES)
            for r0 in range(0, tm, ROWS_TAPS):
                u3_ref[pl.ds(r0, ROWS_TAPS), cols] = (_conv_taps_blk(buf, wb, r0, cols, _FWD_OFFS)
                                                      + wbias_ref[:, cols])
        for r0 in range(0, tm, ROWS_NORM):
            rows = pl.ds(r0, ROWS_NORM)
            xhat, _ = _layernorm_parts(u3_ref[rows, :])
            u4 = xhat * g_ref[...] + bb_ref[...]
            o_ref[rows, :] = (u4 * _sigmoid(u4)).astype(BF16)

    return pl.pallas_call(
        body, grid=(t // tm,),
        in_specs=[_row_spec(tm, d2), _prev_spec(tm, HALO_A, d2), _vec_spec(1, d2),
                  _vec_spec(CONF_CONV_WIDTH, d)] + [_vec_spec(1, d)] * 3,
        out_specs=[_row_spec(tm, d)] * 2,
        out_shape=[jax.ShapeDtypeStruct((t, d), BF16), jax.ShapeDtypeStruct((t, d), F32)],
        scratch_shapes=[pltpu.VMEM((tm + HALO_A, d), F32), pltpu.VMEM((CONF_CONV_WIDTH, SUBLANES, d), F32)],
        compiler_params=_cparams(_PAR), name=name)(u1, u1, b1, dww, dwb, lng, lnb)


def _a_bwd_norm(u3, du5, lng, lnb, *, name):
    t, d = u3.shape
    tm = _pick(t, (TM_ROW, 256, 128))
    n = t // tm

    def body(u3_ref, du5_ref, g_ref, bb_ref, du3_ref, dg_ref, db_ref, dwb_ref, ag, ab, aw):
        i = pl.program_id(0)

        @pl.when(i == 0)
        def _():
            for a in (ag, ab, aw):
                a[...] = jnp.zeros_like(a)

        g = g_ref[...]
        for r0 in range(0, tm, ROWS_NORM):
            rows = pl.ds(r0, ROWS_NORM)
            xhat, rstd = _layernorm_parts(u3_ref[rows, :])
            u4 = xhat * g + bb_ref[...]
            sg = _sigmoid(u4)
            du4 = du5_ref[rows, :] * (sg * (1.0 + u4 * (1.0 - sg)))
            dxh = du4 * g
            du3 = rstd * (dxh - jnp.mean(dxh, axis=-1, keepdims=True)
                          - xhat * jnp.mean(dxh * xhat, axis=-1, keepdims=True))
            du3_ref[rows, :] = du3
            ag[...] += _sum8(du4 * xhat)
            ab[...] += _sum8(du4)
            aw[...] += _sum8(du3)

        @pl.when(i == n - 1)
        def _():
            for a, o in ((ag, dg_ref), (ab, db_ref), (aw, dwb_ref)):
                o[...] = jnp.sum(a[...], axis=0, keepdims=True)

    return pl.pallas_call(
        body, grid=(n,),
        in_specs=[_row_spec(tm, d)] * 2 + [_vec_spec(1, d)] * 2,
        out_specs=[_row_spec(tm, d)] + [_vec_spec(1, d)] * 3,
        out_shape=[jax.ShapeDtypeStruct((t, d), F32)] + [jax.ShapeDtypeStruct((1, d), F32)] * 3,
        scratch_shapes=[pltpu.VMEM((SUBLANES, d), F32)] * 3,
        compiler_params=_cparams(_ARB), name=name)(u3, du5, lng, lnb)


def _a_bwd_conv(u1, du3, b1, dww, *, name):
    t, d2 = u1.shape
    d = d2 // 2
    tm = _pick(t, (TM_CONV,))
    n = t // tm
    kw = CONF_CONV_WIDTH
    nv = ROWS_TAPS // SUBLANES

    def body(u_ref, up_ref, g3_ref, g3n_ref, b1_ref, w_ref, du1_ref, dw_ref, db1_ref,
             buf, gbuf, wb, wacc, bacc):
        i = pl.program_id(0)

        @pl.when(i == 0)
        def _():
            wacc[...] = jnp.zeros_like(wacc)
            bacc[...] = jnp.zeros_like(bacc)

        _fill_glu_buf(buf, u_ref, up_ref, b1_ref, i, tm, d)
        _bcast_rows(wb, w_ref, 0, kw)
        gbuf[pl.ds(0, tm), :] = g3_ref[...]
        gbuf[pl.ds(tm, HALO_A), :] = jnp.where(i < n - 1, g3n_ref[...], 0.0)
        for c0 in range(0, d, LANES):
            cols, gcols = pl.ds(c0, LANES), pl.ds(d + c0, LANES)
            for r0 in range(0, tm, ROWS_TAPS):
                rows = pl.ds(r0, ROWS_TAPS)
                u2 = [buf[pl.ds(HALO_A + r0 + SUBLANES * v, SUBLANES), cols] for v in range(nv)]
                acc = [None] * nv

                def use(k, gs):
                    wk = wb[k, :, cols]
                    part = None
                    for v in range(nv):
                        term = gs[v] * wk
                        acc[v] = term if acc[v] is None else acc[v] + term
                        prod = u2[v] * gs[v]
                        part = prod if part is None else part + prod
                    wacc[k, :, cols] += part

                _taps31(gbuf, r0, cols, _BWD_OFFS, use)
                du2 = jnp.concatenate(acc, axis=0)
                av = u_ref[rows, cols] + b1_ref[:, cols]
                sg = _sigmoid(u_ref[rows, gcols] + b1_ref[:, gcols])
                da = du2 * sg
                dg = du2 * av * (sg * (1.0 - sg))
                du1_ref[rows, cols] = da.astype(BF16)
                du1_ref[rows, gcols] = dg.astype(BF16)
                bacc[:, cols] += _sum8(da)
                bacc[:, gcols] += _sum8(dg)

        @pl.when(i == n - 1)
        def _():
            dw_ref[...] = jnp.sum(wacc[...], axis=1)
            db1_ref[...] = jnp.sum(bacc[...], axis=0, keepdims=True)

    return pl.pallas_call(
        body, grid=(n,),
        in_specs=[_row_spec(tm, d2), _prev_spec(tm, HALO_A, d2), _row_spec(tm, d),
                  _next_spec(tm, HALO_A, d, t), _vec_spec(1, d2), _vec_spec(kw, d)],
        out_specs=[_row_spec(tm, d2), _vec_spec(kw, d), _vec_spec(1, d2)],
        out_shape=[jax.ShapeDtypeStruct((t, d2), BF16), jax.ShapeDtypeStruct((kw, d), F32),
                   jax.ShapeDtypeStruct((1, d2), F32)],
        scratch_shapes=[pltpu.VMEM((tm + HALO_A, d), F32), pltpu.VMEM((tm + HALO_A, d), F32),
                        pltpu.VMEM((kw, SUBLANES, d), F32),
                        pltpu.VMEM((kw, SUBLANES, d), F32), pltpu.VMEM((SUBLANES, d2), F32)],
        compiler_params=_cparams(_ARB), name=name)(u1, u1, du3, du3, b1, dww)


def _pool_counts(i, tm, w):
    pos = (i * tm + lax.broadcasted_iota(jnp.int32, (tm, 1), 0) + 1).astype(F32)
    return jnp.minimum(pos, float(w))


def _b_pool_fwd(h, *, name):
    t, d = h.shape
    gd = d // len(POOL_WINDOWS)
    tm = _pick(t, (TM_CONV,))
    hb = HALO_POOL

    def body(h_ref, hp_ref, o_ref, buf):
        i = pl.program_id(0)
        buf[pl.ds(0, hb), :] = jnp.where(i > 0, hp_ref[...], 0.0)
        buf[pl.ds(hb, tm), :] = h_ref[...]
        for g, w in enumerate(POOL_WINDOWS):
            cols = pl.ds(g * gd, gd)
            window = buf[:, cols]
            cur = window[hb:]
            s = cur
            for j in range(1, w):
                s = s + _shifted_rows(window, hb - j, tm)
            o_ref[:, cols] = (s / _pool_counts(i, tm, w) - cur).astype(BF16)

    return pl.pallas_call(
        body, grid=(t // tm,),
        in_specs=[_row_spec(tm, d), _prev_spec(tm, hb, d)],
        out_specs=_row_spec(tm, d), out_shape=jax.ShapeDtypeStruct((t, d), BF16),
        scratch_shapes=[pltpu.VMEM((tm + hb, d), F32)],
        compiler_params=_cparams(_PAR), name=name)(h, h)


def _b_pool_bwd(dp, *, name):
    t, d = dp.shape
    gd = d // len(POOL_WINDOWS)
    tm = _pick(t, (TM_CONV,))
    hb = HALO_POOL
    n = t // tm

    def body(dp_ref, dpn_ref, o_ref, buf):
        i = pl.program_id(0)
        for g, w in enumerate(POOL_WINDOWS):
            cols = pl.ds(g * gd, gd)
            buf[pl.ds(0, tm), cols] = dp_ref[:, cols] / _pool_counts(i, tm, w)
            buf[pl.ds(tm, hb), cols] = jnp.where(i < n - 1, dpn_ref[:, cols] * (1.0 / w), 0.0)
            window = buf[:, cols]
            s = window[:tm]
            for j in range(1, w):
                s = s + _shifted_rows(window, j, tm)
            o_ref[:, cols] = s - dp_ref[:, cols]

    return pl.pallas_call(
        body, grid=(n,),
        in_specs=[_row_spec(tm, d), _next_spec(tm, hb, d, t)],
        out_specs=_row_spec(tm, d), out_shape=jax.ShapeDtypeStruct((t, d), F32),
        scratch_shapes=[pltpu.VMEM((tm + hb, d), F32)],
        compiler_params=_cparams(_PAR), name=name)(dp, dp)


def _b_affine_fwd(mixed, gb, scale, *, name):
    t, d = mixed.shape
    tm = _pick(t, (TM_ROW, 256, 128))

    def body(m_ref, gb_ref, s_ref, o_ref):
        o_ref[...] = ((m_ref[...] + gb_ref[...]) * s_ref[...]).astype(BF16)

    return pl.pallas_call(
        body, grid=(t // tm,), in_specs=[_row_spec(tm, d)] + [_vec_spec(1, d)] * 2,
        out_specs=_row_spec(tm, d), out_shape=jax.ShapeDtypeStruct((t, d), BF16),
        compiler_params=_cparams(_PAR), name=name)(mixed, gb, scale)


def _b_affine_bwd(dy, mixed, gb, scale, *, name):
    t, d = mixed.shape
    tm = _pick(t, (TM_ROW, 256, 128))
    n = t // tm

    def body(dy_ref, m_ref, gb_ref, s_ref, dm_ref, ds_ref, dgb_ref, a1, a2):
        i = pl.program_id(0)

        @pl.when(i == 0)
        def _():
            a1[...] = jnp.zeros_like(a1)
            a2[...] = jnp.zeros_like(a2)

        dy_v = dy_ref[...]
        dm_ref[...] = (dy_v * s_ref[...]).astype(BF16)
        a1[...] += _sum8(dy_v * (m_ref[...] + gb_ref[...]))
        a2[...] += _sum8(dy_v)

        @pl.when(i == n - 1)
        def _():
            ds_ref[...] = jnp.sum(a1[...], axis=0, keepdims=True)
            dgb_ref[...] = jnp.sum(a2[...], axis=0, keepdims=True) * s_ref[...]

    return pl.pallas_call(
        body, grid=(n,), in_specs=[_row_spec(tm, d)] * 2 + [_vec_spec(1, d)] * 2,
        out_specs=[_row_spec(tm, d)] + [_vec_spec(1, d)] * 2,
        out_shape=[jax.ShapeDtypeStruct((t, d), BF16)] + [jax.ShapeDtypeStruct((1, d), F32)] * 2,
        scratch_shapes=[pltpu.VMEM((SUBLANES, d), F32)] * 2,
        compiler_params=_cparams(_ARB), name=name)(dy, mixed, gb, scale)


def _c_gate_fwd(bcx, wc, *, name):
    t, d3 = bcx.shape
    d = d3 // 3
    tm = _pick(t, (TM_CONV,))
    hb = HALO_3

    rb = ROWS_BLK
    cw = _pick(d, (COLS_BLK, LANES))

    def body(x_ref, xp_ref, w_ref, z_ref, buf, wb):
        i = pl.program_id(0)
        _bcast_rows(wb, w_ref, 0, 3)
        wb[3] = jnp.zeros(wb.shape[1:], F32)
        for c0 in range(0, d, cw):
            cols, ccols, vcols = pl.ds(c0, cw), pl.ds(d + c0, cw), pl.ds(2 * d + c0, cw)
            buf[pl.ds(0, hb), cols] = jnp.where(i > 0, xp_ref[:, ccols] * xp_ref[:, vcols], 0.0)
            for r0 in range(0, tm, rb):
                rows = pl.ds(r0, rb)
                buf[pl.ds(hb + r0, rb), cols] = x_ref[rows, ccols] * x_ref[rows, vcols]
            for r0 in range(0, tm, rb):
                rows = pl.ds(r0, rb)
                q, _ = _conv3_blk(buf, wb, hb - 2 + r0, rb, cols)
                z_ref[rows, cols] = (x_ref[rows, cols] * q).astype(BF16)

    return pl.pallas_call(
        body, grid=(t // tm,),
        in_specs=[_row_spec(tm, d3), _prev_spec(tm, hb, d3), _vec_spec(3, d)],
        out_specs=_row_spec(tm, d), out_shape=jax.ShapeDtypeStruct((t, d), BF16),
        scratch_shapes=[pltpu.VMEM((tm + hb, d), F32), pltpu.VMEM((4, rb, d), F32)],
        compiler_params=_cparams(_PAR), name=name)(bcx, bcx, wc)


def _c_gate_bwd(bcx, dz, wc, *, name):
    t, d3 = bcx.shape
    d = d3 // 3
    tm = _pick(t, (TM_CONV,))
    hb = HALO_3
    n = t // tm

    rb = ROWS_BLK
    cw = _pick(d, (COLS_BLK, LANES))

    def body(x_ref, xp_ref, xn_ref, dz_ref, dzn_ref, w_ref, o_ref, dw_ref, pbuf, qbuf, wb, wacc):
        i = pl.program_id(0)

        @pl.when(i == 0)
        def _():
            wacc[...] = jnp.zeros_like(wacc)

        _bcast_rows(wb, w_ref, 0, 3)
        wb[3] = jnp.zeros(wb.shape[1:], F32)
        for c0 in range(0, d, cw):
            cols, ccols, vcols = pl.ds(c0, cw), pl.ds(d + c0, cw), pl.ds(2 * d + c0, cw)
            pbuf[pl.ds(0, hb), cols] = jnp.where(i > 0, xp_ref[:, ccols] * xp_ref[:, vcols], 0.0)
            qbuf[pl.ds(tm, hb), cols] = jnp.where(i < n - 1, dzn_ref[:, cols] * xn_ref[:, cols], 0.0)
            for r0 in range(0, tm, rb):
                rows = pl.ds(r0, rb)
                pbuf[pl.ds(hb + r0, rb), cols] = x_ref[rows, ccols] * x_ref[rows, vcols]
            for r0 in range(0, tm, rb):
                rows = pl.ds(r0, rb)
                q, _ = _conv3_blk(pbuf, wb, hb - 2 + r0, rb, cols)
                dz_v = dz_ref[rows, cols]
                qbuf[rows, cols] = dz_v * x_ref[rows, cols]
                o_ref[rows, cols] = (dz_v * q).astype(BF16)
            for r0 in range(0, tm, rb):
                rows = pl.ds(r0, rb)
                pv = pbuf[pl.ds(hb + r0, rb), cols]
                window = qbuf[pl.ds(r0, rb + SUBLANES), cols]
                dp = None
                for k in range(3):
                    dqk = _shifted_rows(window, 2 - k, rb)
                    term = dqk * wb[k, :, cols]
                    dp = term if dp is None else dp + term
                    wacc[k, :, cols] += _sum8(pv * dqk)
                o_ref[rows, ccols] = (dp * x_ref[rows, vcols]).astype(BF16)
                o_ref[rows, vcols] = (dp * x_ref[rows, ccols]).astype(BF16)

        @pl.when(i == n - 1)
        def _():
            dw_ref[...] = jnp.sum(wacc[...], axis=1)

    return pl.pallas_call(
        body, grid=(n,),
        in_specs=[_row_spec(tm, d3), _prev_spec(tm, hb, d3), _next_spec(tm, hb, d3, t),
                  _row_spec(tm, d), _next_spec(tm, hb, d, t), _vec_spec(3, d)],
        out_specs=[_row_spec(tm, d3), _vec_spec(3, d)],
        out_shape=[jax.ShapeDtypeStruct((t, d3), BF16), jax.ShapeDtypeStruct((3, d), F32)],
        scratch_shapes=[pltpu.VMEM((tm + hb, d), F32), pltpu.VMEM((tm + hb, d), F32),
                        pltpu.VMEM((4, rb, d), F32), pltpu.VMEM((3, SUBLANES, d), F32)],
        compiler_params=_cparams(_ARB), name=name)(bcx, bcx, bcx, dz, dz, wc)


def _row(v):
    return v.reshape(1, -1)


def _kind_of(j):
    return "f" if j % 2 else "abc"[(j // 2) % N_MIXERS]


BIG = ("a_pw1_w", "a_pw2_w", "b_group_w", "c_in_w", "c_out_w", "f_up_w", "f_down_w")
COL_SHARDED = ("a_pw1_w", "c_in_w", "f_up_w")


def _local_step(x, target, mod, p):
    nsub = 2 * DEPTH
    norm_names = (("norm_pre_mix", "norm_post_mix"), ("norm_pre_ffn", "norm_post_ffn"))
    gpre = [_row(p[norm_names[s][0]][i]) for i in range(DEPTH) for s in (0, 1)]
    gpost = [_row(p[norm_names[s][1]][i]) for i in range(DEPTH) for s in (0, 1)]
    sh = [_row(mod[i, 3 * s + 0]) for i in range(DEPTH) for s in (0, 1)]
    sc = [_row(mod[i, 3 * s + 1]) for i in range(DEPTH) for s in (0, 1)]
    gt = [_row(mod[i, 3 * s + 2]) for i in range(DEPTH) for s in (0, 1)]

    def h_dtype(j):
        return F32 if _kind_of(j) == "b" else BF16

    xs, hs, ys, saved = [x], [], [], []

    hs.append(_fwd_first(x, gpre[0], sc[0], sh[0], h_dtype=h_dtype(0), name="fwd_first"))
    for j in range(nsub):
        i, kind = j // 2, _kind_of(j)
        slot = i // N_MIXERS
        h = hs[j]
        tag = f"{kind}{j}"
        if kind == "f":
            u = _mm(h, p["f_up_w"], mode="nn", layer=i, out_dtype=BF16, name=f"ffn_up_{tag}")
            a, vpre = _ffn_gate_fwd(u, p["f_dw_w"][i], _row(p["f_dw_b"][i]), name=f"ffn_gate_{tag}")
            y = _mm(a, p["f_down_w"][i], mode="nn", out_dtype=BF16, name=f"ffn_down_{tag}")
            saved.append((u, a, vpre))
        elif kind == "a":
            u1 = _mm(h, p["a_pw1_w"], mode="nn", layer=slot, name=f"a_pw1_{tag}")
            u5, u3 = _a_fwd(u1, _row(p["a_pw1_b"][slot]), p["a_dw_w"][slot], _row(p["a_dw_b"][slot]),
                            _row(p["a_ln_g"][slot]), _row(p["a_ln_b"][slot]), name=f"a_conv_{tag}")
            y = _mm(u5, p["a_pw2_w"][slot], mode="nn", bias=_row(p["a_pw2_b"][slot]), out_dtype=BF16,
                    name=f"a_pw2_{tag}")
            saved.append((u1, u5, u3))
        elif kind == "b":
            pooled = _b_pool_fwd(h, name=f"b_pool_{tag}")
            mixed = _mm_group(pooled, p["b_group_w"][slot], mode="nn", name=f"b_mix_{tag}")
            y = _b_affine_fwd(mixed, _row(p["b_group_b"][slot]), _row(p["b_scale"][slot]), name=f"b_aff_{tag}")
            saved.append((pooled, mixed))
        else:
            bcx = _mm(h, p["c_in_w"], mode="nn", layer=slot, name=f"c_in_{tag}")
            z = _c_gate_fwd(bcx, p["c_conv_w"][slot], name=f"c_gate_{tag}")
            y = _mm(z, p["c_out_w"][slot], mode="nn", out_dtype=BF16, name=f"c_out_{tag}")
            saved.append((bcx, z))
        ys.append(y)
        if j + 1 < nsub:
            x_new, h_next = _fwd_mid(xs[j], y, gpost[j], gt[j], gpre[j + 1], sc[j + 1], sh[j + 1],
                                     h_dtype=h_dtype(j + 1), name=f"fwd_mid_{j}")
            xs.append(x_new)
            hs.append(h_next)

    n_of = {"a": len([i for i in range(DEPTH) if i % N_MIXERS == 0]),
            "b": len([i for i in range(DEPTH) if i % N_MIXERS == 1]),
            "c": len([i for i in range(DEPTH) if i % N_MIXERS == 2]), "f": DEPTH, "n": DEPTH}
    g = {k: [None] * n_of[k[0]] for k in p}
    dmod = [[None] * 6 for _ in range(DEPTH)]

    last = nsub - 1
    dx, dy, dgpost, dgt, sdy, loss = _last_fwd_bwd(xs[last], ys[last], target, gpost[last], gt[last],
                                                   name="loss_head")
    for j in range(last, -1, -1):
        i, kind = j // 2, _kind_of(j)
        slot = i // N_MIXERS
        sub = j % 2
        tag = f"{kind}{j}"
        g[norm_names[sub][1]][i] = dgpost
        dmod[i][3 * sub + 2] = dgt
        h = hs[j]
        if kind == "f":
            u, a, vpre = saved[j]
            da = _mm(dy, p["f_down_w"][i], mode="nt", out_dtype=BF16, name=f"ffn_dda_{tag}")
            g["f_down_w"][i] = _mm(a, dy, mode="tn", tk=TK_TOKENS, out_dtype=BF16, name=f"ffn_dwdown_{tag}")
            du, dw, db = _ffn_gate_bwd(u, vpre, da, p["f_dw_w"][i], name=f"ffn_gate_bwd_{tag}")
            g["f_dw_w"][i], g["f_dw_b"][i] = dw, db
            dh = _mm(du, p["f_up_w"], mode="nt", layer=i, name=f"ffn_ddh_{tag}")
            g["f_up_w"][i] = _mm(h, du, mode="tn", tk=TK_TOKENS, layer=i, out_dtype=BF16,
                                 name=f"ffn_dwup_{tag}")
        elif kind == "a":
            u1, u5, u3 = saved[j]
            g["a_pw2_b"][slot] = sdy
            du5 = _mm(dy, p["a_pw2_w"][slot], mode="nt", name=f"a_ddu5_{tag}")
            g["a_pw2_w"][slot] = _mm(u5, dy, mode="tn", tk=TK_TOKENS, out_dtype=BF16, name=f"a_dw2_{tag}")
            b1 = _row(p["a_pw1_b"][slot])
            du3, dlg, dlb, ddwb = _a_bwd_norm(u3, du5, _row(p["a_ln_g"][slot]), _row(p["a_ln_b"][slot]),
                                              name=f"a_bwd_norm_{tag}")
            g["a_ln_g"][slot], g["a_ln_b"][slot], g["a_dw_b"][slot] = dlg, dlb, ddwb
            du1, ddww, db1 = _a_bwd_conv(u1, du3, b1, p["a_dw_w"][slot], name=f"a_bwd_conv_{tag}")
            g["a_dw_w"][slot], g["a_pw1_b"][slot] = ddww, db1
            dh = _mm(du1, p["a_pw1_w"], mode="nt", layer=slot, name=f"a_ddh_{tag}")
            g["a_pw1_w"][slot] = _mm(h, du1, mode="tn", tk=TK_TOKENS, layer=slot, out_dtype=BF16,
                                     name=f"a_dw1_{tag}")
        elif kind == "b":
            pooled, mixed = saved[j]
            dmixed, dscale, dgb = _b_affine_bwd(dy, mixed, _row(p["b_group_b"][slot]), _row(p["b_scale"][slot]),
                                                name=f"b_aff_bwd_{tag}")
            g["b_scale"][slot], g["b_group_b"][slot] = dscale, dgb
            dpooled = _mm_group(dmixed, p["b_group_w"][slot], mode="nt", name=f"b_dpool_{tag}")
            g["b_group_w"][slot] = _mm_group(pooled, dmixed, mode="tn", tm=TK_TOKENS, out_dtype=BF16,
                                             name=f"b_dw_{tag}")
            dh = _b_pool_bwd(dpooled, name=f"b_pool_bwd_{tag}")
        else:
            bcx, z = saved[j]
            dz = _mm(dy, p["c_out_w"][slot], mode="nt", name=f"c_ddz_{tag}")
            g["c_out_w"][slot] = _mm(z, dy, mode="tn", tk=TK_TOKENS, out_dtype=BF16, name=f"c_dwout_{tag}")
            dbcx, dwc = _c_gate_bwd(bcx, dz, p["c_conv_w"][slot], name=f"c_gate_bwd_{tag}")
            g["c_conv_w"][slot] = dwc
            dh = _mm(dbcx, p["c_in_w"], mode="nt", layer=slot, name=f"c_ddh_{tag}")
            g["c_in_w"][slot] = _mm(h, dbcx, mode="tn", tk=TK_TOKENS, layer=slot, out_dtype=BF16,
                                    name=f"c_dwin_{tag}")
        if j > 0:
            pj = j - 1
            dy_dtype = F32 if _kind_of(pj) == "b" else BF16
            dx, dy, dsh, dsc, dgpre, dgpost, dgt, sdy = _bwd_mid(
                dx, dh, xs[j], gpre[j], sc[j], ys[pj], gpost[pj], gt[pj], dy_dtype=dy_dtype, name=f"bwd_mid_{j}")
        else:
            dx, dsh, dsc, dgpre = _bwd_first(dx, dh, xs[0], gpre[0], sc[0], name="bwd_first")
        dmod[i][3 * sub + 0] = dsh
        dmod[i][3 * sub + 1] = dsc
        g[norm_names[sub][0]][i] = dgpre

    small = {k: jnp.stack(v).reshape(p[k].shape) for k, v in g.items() if k not in BIG}
    big = {k: v for k, v in g.items() if k in BIG}
    dmod_arr = jnp.stack([jnp.concatenate(r, axis=0) for r in dmod])
    return loss, dx, dmod_arr, small, big


_MESH = pl.DeviceIdType.MESH
_ANY = pl.BlockSpec(memory_space=pl.ANY)
_VMEM = pl.BlockSpec(memory_space=pltpu.VMEM)


def _place():
    return lax.axis_index("x"), lax.axis_index("y"), lax.axis_index("c")


def _other_chips(x, y):
    return [(1 - x, y), (x, 1 - y), (1 - x, 1 - y)]


def _remote(src, dst, send_sem, recv_sem, to):
    return pltpu.make_async_remote_copy(src_ref=src, dst_ref=dst, send_sem=send_sem, recv_sem=recv_sem,
                                        device_id=to, device_id_type=_MESH)


def _all_gather8(blk, *, name):
    r, cdim = blk.shape

    def body(x_ref, out_ref, send_sems, recv_sems, local_sem):
        x, y, c = _place()
        me, sibling = (x, y, c), (x, y, 1 - c)
        chips = _other_chips(x, y)

        def slot(px, py, pc):
            return out_ref.at[4 * px + 2 * py + pc]

        def copy(k, block, to, src=None):
            return _remote(slot(*block) if src is None else src, slot(*block),
                           send_sems.at[k], recv_sems.at[k], to)

        mine = pltpu.make_async_copy(x_ref, slot(*me), local_sem)
        mine.start()
        first = [copy(0, me, sibling, src=x_ref)]
        first += [copy(1 + j, me, (*chip, c), src=x_ref) for j, chip in enumerate(chips)]
        for cp in first:
            cp.start()
        passed = [copy(4 + j, (*chip, c), sibling) for j, chip in enumerate(chips)]
        for j, chip in enumerate(chips):
            copy(1 + j, (*chip, c), me).wait_recv()
            passed[j].start()
        copy(0, sibling, me).wait_recv()
        for j, chip in enumerate(chips):
            copy(4 + j, (*chip, 1 - c), me).wait_recv()
        for cp in first + passed:
            cp.wait_send()
        mine.wait()

    return pl.pallas_call(
        body, out_shape=jax.ShapeDtypeStruct((NDEV, r, cdim), blk.dtype),
        in_specs=[_VMEM], out_specs=_VMEM,
        scratch_shapes=[pltpu.SemaphoreType.DMA((7,)), pltpu.SemaphoreType.DMA((7,)), pltpu.SemaphoreType.DMA],
        compiler_params=pltpu.CompilerParams(vmem_limit_bytes=VMEM_LIMIT), name=name)(blk)


def _gather_dst(kind):
    if kind == "col":
        return lambda s, h: (s, h)
    if kind == "row":
        return lambda s, h: (h, slice(None), s)
    return lambda s, h: (slice(None), s, h)


def _cast_into_gathered(src, kind, out_shape, shard, *, name):
    _, a, rh, cdim = src.shape
    tr = _pick(rh, (512, 256, 128, 64, 32, 16))
    if kind == "col":
        out_idx = lambda h, ai, r, s: (s[0], h, ai, r, 0)
    elif kind == "row":
        out_idx = lambda h, ai, r, s: (h, ai, s[0], r, 0)
    else:
        out_idx = lambda h, ai, r, s: (ai, s[0], h, r, 0)

    def body(s_ref, x_ref, o_ref):
        o_ref[...] = x_ref[...].astype(BF16)

    grid_spec = pltpu.PrefetchScalarGridSpec(
        num_scalar_prefetch=1, grid=(2, a, rh // tr),
        in_specs=[pl.BlockSpec((None, None, tr, cdim), lambda h, ai, r, s: (h, ai, r, 0))],
        out_specs=pl.BlockSpec((None, None, None, tr, cdim), out_idx))
    return pl.pallas_call(
        body, grid_spec=grid_spec, out_shape=jax.ShapeDtypeStruct(out_shape, BF16),
        compiler_params=_cparams(_PAR, _PAR, _PAR), name=name)(shard, src)


def _gather_weights(bufs, kinds, *, name):
    nt = len(bufs)

    def body(*refs):
        out_refs = refs[nt:2 * nt]
        send_sems, recv_sems = refs[2 * nt:]
        x, y, c = _place()
        sibling = (x, y, 1 - c)
        nbr_x, nbr_y = (1 - x, y, c), (x, 1 - y, c)
        s_me, s_x, s_y, s_d = 2 * x + y, 2 * (1 - x) + y, 2 * x + (1 - y), 2 * (1 - x) + (1 - y)

        def at(k, s, h):
            return out_refs[k].at[_gather_dst(kinds[k])(s, h)]

        def part(ref, k, q):
            rows = bufs[k].shape[3] // 2
            return ref.at[:, pl.ds(q * rows, rows), :]

        def copy(ref, k, col, to):
            return _remote(ref, ref, send_sems.at[k, col], recv_sems.at[k, col], to)

        started = []

        def start(cp):
            cp.start()
            started.append(cp)

        for k in range(nt):
            start(copy(at(k, s_me, c), k, 0, nbr_x))
            start(copy(at(k, s_me, c), k, 1, nbr_y))
        for k in range(nt):
            got_y, got_x = at(k, s_y, c), at(k, s_x, c)
            copy(got_y, k, 1, nbr_y).wait_recv()
            start(copy(part(got_y, k, 0), k, 2, nbr_x))
            start(copy(got_y, k, 5, sibling))
            copy(got_x, k, 0, nbr_x).wait_recv()
            start(copy(part(got_x, k, 1), k, 3, nbr_y))
            start(copy(got_x, k, 4, sibling))
        for k in range(nt):
            got_d = at(k, s_d, c)
            for q in (0, 1):
                copy(part(got_d, k, q), k, 2 + q, sibling).wait_recv()
                start(copy(part(got_d, k, q), k, 6 + q, sibling))
        for k in range(nt):
            copy(at(k, s_x, 1 - c), k, 4, sibling).wait_recv()
            copy(at(k, s_y, 1 - c), k, 5, sibling).wait_recv()
            for q in (0, 1):
                copy(part(at(k, s_d, 1 - c), k, q), k, 6 + q, sibling).wait_recv()
        for cp in started:
            cp.wait_send()

    return pl.pallas_call(
        body, out_shape=[jax.ShapeDtypeStruct(b.shape, BF16) for b in bufs],
        in_specs=[_ANY] * nt, out_specs=[_ANY] * nt, input_output_aliases={k: k for k in range(nt)},
        scratch_shapes=[pltpu.SemaphoreType.DMA((nt, 8)), pltpu.SemaphoreType.DMA((nt, 8))],
        name=name)(*bufs)


def _pair_exchange(gs, layers_of, *, name):
    n, nt = len(gs), len(layers_of)

    def body(*refs):
        g_refs, out_refs = refs[:n], refs[n:n + nt]
        send_sems, recv_sems = refs[n + nt:]
        x, y, c = _place()
        sibling = (x, y, 1 - c)
        copies = []
        for t, ks in enumerate(layers_of):
            for l, k in enumerate(ks):
                cp = _remote(g_refs[k].at[:, 1 - c], out_refs[t].at[l], send_sems.at[k], recv_sems.at[k], sibling)
                cp.start()
                copies.append(cp)
        for cp in copies:
            cp.wait()

    out_shape = [jax.ShapeDtypeStruct((len(ks), NSHARD) + gs[ks[0]].shape[2:], gs[ks[0]].dtype)
                 for ks in layers_of]
    return pl.pallas_call(
        body, out_shape=out_shape, in_specs=[_ANY] * n, out_specs=[_ANY] * nt,
        scratch_shapes=[pltpu.SemaphoreType.DMA((n,)), pltpu.SemaphoreType.DMA((n,))],
        name=name)(*gs)


def _pair_sum(g, r1, s_acc, layer, half, *, name):
    _, _, rh, cdim = g.shape
    tr = _pick(rh, (256, 128, 176, 64, 32, 16))

    def body(half_ref, g_ref, r_ref, s_in_ref, o_ref):
        o_ref[...] = (g_ref[...].astype(F32) + r_ref[...].astype(F32)).astype(BF16)

    grid_spec = pltpu.PrefetchScalarGridSpec(
        num_scalar_prefetch=1, grid=(NSHARD, rh // tr),
        in_specs=[pl.BlockSpec((None, None, tr, cdim), lambda s, r, hf: (s, hf[0], r, 0)),
                  pl.BlockSpec((None, None, tr, cdim), lambda s, r, hf: (layer, s, r, 0)),
                  _ANY],
        out_specs=pl.BlockSpec((None, None, tr, cdim), lambda s, r, hf: (layer, s, r, 0)))
    return pl.pallas_call(
        body, grid_spec=grid_spec, out_shape=jax.ShapeDtypeStruct(s_acc.shape, BF16),
        input_output_aliases={3: 0},
        compiler_params=_cparams(_PAR, _PAR), name=name)(half, g, r1, s_acc)


def _chip_exchange(ss, *, name):
    nt = len(ss)

    def body(*refs):
        s_refs, out_refs, stage_refs = refs[:nt], refs[nt:2 * nt], refs[2 * nt:3 * nt]
        send_sems, recv_sems = refs[3 * nt:]
        x, y, c = _place()
        nbr_x, nbr_y = (1 - x, y, c), (x, 1 - y, c)
        s_x, s_y, s_d = 2 * (1 - x) + y, 2 * x + (1 - y), 2 * (1 - x) + (1 - y)

        def part(ref, t, q):
            rows = ss[t].shape[2] // 2
            return ref.at[:, pl.ds(q * rows, rows), :]

        def copy(src, dst, t, col, to):
            return _remote(src, dst, send_sems.at[t, col], recv_sems.at[t, col], to)

        started = []

        def start(cp):
            cp.start()
            started.append(cp)

        for t in range(nt):
            start(copy(s_refs[t].at[:, s_x], out_refs[t].at[0], t, 0, nbr_x))
            start(copy(s_refs[t].at[:, s_y], out_refs[t].at[1], t, 1, nbr_y))
            start(copy(part(s_refs[t].at[:, s_d], t, 0), stage_refs[t].at[0], t, 2, nbr_x))
            start(copy(part(s_refs[t].at[:, s_d], t, 1), stage_refs[t].at[1], t, 3, nbr_y))
        for t in range(nt):
            st0, st1 = stage_refs[t].at[0], stage_refs[t].at[1]
            copy(st0, st0, t, 2, nbr_x).wait_recv()
            start(copy(st0, part(out_refs[t].at[2], t, 0), t, 4, nbr_y))
            copy(st1, st1, t, 3, nbr_y).wait_recv()
            start(copy(st1, part(out_refs[t].at[2], t, 1), t, 5, nbr_x))
        for t in range(nt):
            copy(out_refs[t].at[0], out_refs[t].at[0], t, 0, nbr_x).wait_recv()
            copy(out_refs[t].at[1], out_refs[t].at[1], t, 1, nbr_y).wait_recv()
            for q in (0, 1):
                got = part(out_refs[t].at[2], t, q)
                copy(got, got, t, 4 + q, nbr_x).wait_recv()
        for cp in started:
            cp.wait_send()

    out_shape = [jax.ShapeDtypeStruct((3, s.shape[0]) + s.shape[2:], BF16) for s in ss]
    out_shape += [jax.ShapeDtypeStruct((2, s.shape[0], s.shape[2] // 2, s.shape[3]), BF16) for s in ss]
    return pl.pallas_call(
        body, out_shape=out_shape, in_specs=[_ANY] * nt, out_specs=[_ANY] * (2 * nt),
        scratch_shapes=[pltpu.SemaphoreType.DMA((nt, 6)), pltpu.SemaphoreType.DMA((nt, 6))],
        name=name)(*ss)[:nt]


def _chip_sum(s_t, r3_t, place, *, name):
    nl, _, rh, cdim = s_t.shape
    tr = _pick(rh, (256, 128, 176, 64, 32, 16))

    def body(pz, s_ref, r_ref, o_ref):
        acc = s_ref[...].astype(F32) + r_ref[0].astype(F32)
        o_ref[...] = (acc + r_ref[1].astype(F32)) + r_ref[2].astype(F32)

    grid_spec = pltpu.PrefetchScalarGridSpec(
        num_scalar_prefetch=1, grid=(nl, rh // tr),
        in_specs=[pl.BlockSpec((None, None, tr, cdim), lambda l, r, pz: (l, pz[0], r, 0)),
                  pl.BlockSpec((3, None, tr, cdim), lambda l, r, pz: (0, l, r, 0))],
        out_specs=pl.BlockSpec((None, None, tr, cdim), lambda l, r, pz: (l, pz[1], r, 0)))
    return pl.pallas_call(
        body, grid_spec=grid_spec, out_shape=jax.ShapeDtypeStruct((nl, 2, rh, cdim), F32),
        compiler_params=_cparams(_PAR, _PAR), name=name)(place, s_t, r3_t)


def _join_halves(reds, *, name):
    nt = len(reds)

    def body(*refs):
        out_refs = refs[nt:2 * nt]
        send_sems, recv_sems = refs[2 * nt:]
        x, y, c = _place()
        sibling = (x, y, 1 - c)
        copies = []
        for t in range(nt):
            cp = _remote(out_refs[t].at[:, c], out_refs[t].at[:, c], send_sems.at[t], recv_sems.at[t], sibling)
            cp.start()
            copies.append(cp)
        for t, cp in enumerate(copies):
            cp.wait_send()
            got = out_refs[t].at[:, 1 - c]
            _remote(got, got, send_sems.at[t], recv_sems.at[t], sibling).wait_recv()

    return pl.pallas_call(
        body, out_shape=[jax.ShapeDtypeStruct(r.shape, F32) for r in reds],
        in_specs=[_ANY] * nt, out_specs=[_ANY] * nt, input_output_aliases={t: t for t in range(nt)},
        scratch_shapes=[pltpu.SemaphoreType.DMA((nt,)), pltpu.SemaphoreType.DMA((nt,))],
        name=name)(*reds)


def _sum_devices(g, *, name):
    _, r, cdim = g.shape
    tr = _pick(r, (512, 256, 128, 64, 32, 16, 8))

    def body(g_ref, o_ref):
        acc = g_ref[0]
        for e in range(1, NDEV):
            acc = acc + g_ref[e]
        o_ref[...] = acc

    return pl.pallas_call(
        body, grid=(r // tr,), in_specs=[pl.BlockSpec((NDEV, tr, cdim), lambda i: (0, i, 0))],
        out_specs=pl.BlockSpec((tr, cdim), lambda i: (i, 0)),
        out_shape=jax.ShapeDtypeStruct((r, cdim), F32),
        compiler_params=_cparams(_PAR), name=name)(g)


def _mod_fwd(c_all, mod_w, mod_b_cols, *, name):
    nl, d, n = mod_w.shape
    ne = c_all.shape[0]
    tn = _pick(n, (768, 512, 384, 256, 128))

    def body(c_ref, w_ref, b_ref, o_ref):
        cv = c_ref[...]
        act = (cv * _sigmoid(cv)).astype(BF16)
        o_ref[...] = jnp.dot(act, w_ref[...].astype(BF16), preferred_element_type=F32) + b_ref[...]

    return pl.pallas_call(
        body, grid=(nl, n // tn),
        in_specs=[pl.BlockSpec((ne, d), lambda i, j: (0, 0)),
                  pl.BlockSpec((None, d, tn), lambda i, j: (i, 0, j)),
                  pl.BlockSpec((None, 1, tn), lambda i, j: (i, 0, j))],
        out_specs=pl.BlockSpec((None, ne, tn), lambda i, j: (i, 0, j)),
        out_shape=jax.ShapeDtypeStruct((nl, ne, n), F32),
        compiler_params=_cparams(_PAR, _PAR), name=name)(c_all, mod_w, mod_b_cols)


def _adam_math(w, g, m, v):
    m2 = ADAM_B1 * m + (1.0 - ADAM_B1) * g
    v2 = ADAM_B2 * v + (1.0 - ADAM_B2) * (g * g)
    m_hat = m2 / (1.0 - ADAM_B1 ** ADAM_STEP)
    v_hat = v2 / (1.0 - ADAM_B2 ** ADAM_STEP)
    delta = -ADAM_LR * (m_hat / (jnp.sqrt(v_hat) + ADAM_EPS) + ADAM_WD * w)
    return delta, m2, v2


def _adamw(w, g, m, v, *, name):
    rows, cdim = w.shape
    tr = _pick(rows, tuple(t for t in (512, 256, 128, 64, 32, 16, 8) if t * cdim <= 256 * 1024))

    def body(w_ref, g_ref, m_ref, v_ref, d_ref, mo_ref, vo_ref):
        d_ref[...], mo_ref[...], vo_ref[...] = _adam_math(w_ref[...], g_ref[...], m_ref[...], v_ref[...])

    spec = pl.BlockSpec((tr, cdim), lambda i: (i, 0))
    return pl.pallas_call(
        body, grid=(rows // tr,), in_specs=[spec] * 4, out_specs=[spec] * 3,
        out_shape=[jax.ShapeDtypeStruct((rows, cdim), F32)] * 3,
        compiler_params=_cparams(_PAR), name=name)(w, g, m, v)


def _mod_w_update(c_t, dmod, w, m, v, *, name):
    nl, d, n = w.shape
    ne = c_t.shape[1]
    tr = _pick(d, (128, 64, 32, 16, 8))

    def body(c_ref, dm_ref, w_ref, m_ref, v_ref, g_ref, d_ref, mo_ref, vo_ref):
        cv = c_ref[...]
        act = cv * _sigmoid(cv)
        dm = dm_ref[...]
        g = act[:, 0:1] * dm[0:1, :]
        for e in range(1, ne):
            g = g + act[:, e:e + 1] * dm[e:e + 1, :]
        g_ref[...] = g
        d_ref[...], mo_ref[...], vo_ref[...] = _adam_math(w_ref[...], g, m_ref[...], v_ref[...])

    big = pl.BlockSpec((None, tr, n), lambda i, r: (i, r, 0))
    return pl.pallas_call(
        body, grid=(nl, d // tr),
        in_specs=[pl.BlockSpec((tr, ne), lambda i, r: (r, 0)),
                  pl.BlockSpec((None, ne, n), lambda i, r: (i, 0, 0)), big, big, big],
        out_specs=[big] * 4, out_shape=[jax.ShapeDtypeStruct((nl, d, n), F32)] * 4,
        compiler_params=_cparams(_PAR, _PAR), name=name)(c_t, dmod, w, m, v)


PACK_ROWS = 256


def _pack(arrs):
    flat = jnp.concatenate([a.reshape(-1) for a in arrs])
    tile = PACK_ROWS * LANES
    pad = (-flat.shape[0]) % tile
    return jnp.pad(flat, (0, pad)).reshape(-1, LANES)


def _unpack(packed, shapes, lead=()):
    flat = packed.reshape(lead + (-1,))
    out, off = [], 0
    for shp in shapes:
        size = 1
        for s in shp:
            size *= s
        out.append(flat[..., off:off + size].reshape(lead + tuple(shp)))
        off += size
    return out


SMALL_SHARD_AXIS = {"a_pw1_b": 1, "a_dw_w": 2, "a_dw_b": 1, "a_ln_g": 1, "a_ln_b": 1, "a_pw2_b": 1,
                    "c_conv_w": 2, "f_dw_w": 2}
SMALL_REPLICATED = ("norm_pre_mix", "norm_post_mix", "norm_pre_ffn", "norm_post_ffn",
                    "b_group_b", "b_scale", "f_dw_b")
WEIGHT_ORDER = ("mod_w", "mod_b", "norm_pre_mix", "norm_post_mix", "norm_pre_ffn", "norm_post_ffn",
                "a_pw1_w", "a_pw1_b", "a_dw_w", "a_dw_b", "a_ln_g", "a_ln_b", "a_pw2_w", "a_pw2_b",
                "b_group_w", "b_group_b", "b_scale", "c_in_w", "c_conv_w", "c_out_w",
                "f_up_w", "f_dw_w", "f_dw_b", "f_down_w")


def _as_layers_rows_cols(name, w):
    if name == "b_group_w":
        return w.reshape(w.shape[1], w.shape[2], w.shape[3])
    return w


def _step(x, c, loss_target, w, m, v):
    xi, yi, ci = _place()
    shard = 2 * xi + yi
    example = 4 * xi + 2 * yi + ci
    d = x.shape[-1]

    small_names = tuple(SMALL_SHARD_AXIS)
    gathered0 = _all_gather8(_pack([c] + [w[k] for k in small_names]), name="gather_small")
    parts = _unpack(gathered0, [c.shape] + [w[k].shape for k in small_names], lead=(NDEV,))
    c_all = parts[0].reshape(NDEV, d)
    p = {}
    for k, part in zip(small_names, parts[1:]):
        p[k] = jnp.concatenate([part[2 * s] for s in range(NSHARD)], axis=SMALL_SHARD_AXIS[k])
    for k in SMALL_REPLICATED:
        p[k] = w[k]

    ncol = w["mod_w"].shape[2]
    mod_b_cols = lax.dynamic_slice_in_dim(w["mod_b"], shard * ncol, ncol, axis=1).reshape(DEPTH, 1, ncol)
    mod_part = _mod_fwd(c_all, w["mod_w"], mod_b_cols, name="mod_fwd")
    gathered1 = _all_gather8(mod_part.reshape(DEPTH * NDEV, ncol), name="gather_mod")
    mod_all = gathered1.reshape(NSHARD, 2, DEPTH, NDEV, ncol)[:, 0]
    mod_mine = lax.dynamic_index_in_dim(mod_all, example, axis=2, keepdims=False)
    mod = jnp.transpose(mod_mine, (1, 0, 2)).reshape(DEPTH, 6, d)

    shard_arr = shard.reshape(1).astype(jnp.int32)
    bufs, kinds = [], []
    for k in BIG:
        wk = _as_layers_rows_cols(k, w[k])
        nl, r, cdim = wk.shape
        if nl >= 2:
            a, rh = nl // 2, r
        else:
            a, rh = 1, r // 2
        if k in COL_SHARDED:
            kind, out_shape = "col", (NSHARD, 2, a, rh, cdim)
        elif nl >= 2:
            kind, out_shape = "row", (2, a, NSHARD, rh, cdim)
        else:
            kind, out_shape = "row1", (1, NSHARD, 2, rh, cdim)
        kinds.append(kind)
        bufs.append(_cast_into_gathered(wk.reshape(2, a, rh, cdim), kind, out_shape, shard_arr, name=f"cast_{k}"))
    full = _gather_weights(bufs, kinds, name="gather_weights")
    for k, f in zip(BIG, full):
        nl, r, cdim = _as_layers_rows_cols(k, w[k]).shape
        if k in COL_SHARDED:
            p[k] = f.reshape(NSHARD, nl, r, cdim)
        elif k == "b_group_w":
            p[k] = f.reshape(1, nl, NSHARD * r, cdim)
        else:
            p[k] = f.reshape(nl, NSHARD * r, cdim)

    loss, grad_x, dmod, small, big = _local_step(x[0], loss_target[0], mod, p)

    gs, layers_of = [], []
    for k in BIG:
        ks = []
        for g in big[k]:
            if k == "b_group_w":
                ng, rr, cc = g.shape
                g = jnp.transpose(g.reshape(ng, NSHARD, rr // NSHARD, cc), (1, 0, 2, 3)).reshape(NSHARD, -1, cc)
            elif k not in COL_SHARDED:
                g = g.reshape(NSHARD, g.shape[0] // NSHARD, g.shape[1])
            ks.append(len(gs))
            gs.append(g.reshape(NSHARD, 2, g.shape[1] // 2, g.shape[2]))
        layers_of.append(ks)
    half = ci.reshape(1).astype(jnp.int32)
    place = jnp.stack([shard, ci]).astype(jnp.int32)
    r1 = _pair_exchange(gs, layers_of, name="grad_pair_exchange")
    ss = []
    for k, ks, r1_t in zip(BIG, layers_of, r1):
        s_t = lax.empty(r1_t.shape, BF16)
        for l, i in enumerate(ks):
            s_t = _pair_sum(gs[i], r1_t, s_t, l, half, name=f"grad_pair_sum_{k}_{l}")
        ss.append(s_t)
    r3 = _chip_exchange(ss, name="grad_chip_exchange")
    reds = [_chip_sum(s_t, r3_t, place, name=f"grad_chip_sum_{k}") for k, s_t, r3_t in zip(BIG, ss, r3)]
    joined = _join_halves(reds, name="grad_join_halves")
    grads = {k: j.reshape(w[k].shape) for k, j in zip(BIG, joined)}

    rep_names = SMALL_REPLICATED
    small_list = [small[k] for k in rep_names] + [small[k] for k in small_names] + [dmod]
    gathered2 = _all_gather8(_pack(small_list), name="gather_small_grads")
    summed = _sum_devices(gathered2, name="sum_small_grads")
    shapes = [s.shape for s in small_list]
    sums = _unpack(summed, shapes)
    for k, s in zip(rep_names, sums[:len(rep_names)]):
        grads[k] = s
    for k, s in zip(small_names, sums[len(rep_names):-1]):
        ax = SMALL_SHARD_AXIS[k]
        grads[k] = lax.dynamic_slice_in_dim(s, shard * w[k].shape[ax], w[k].shape[ax], axis=ax)
    grads["mod_b"] = sums[-1].reshape(w["mod_b"].shape)
    dmod_all = _unpack(gathered2, shapes, lead=(NDEV,))[-1].reshape(NDEV, DEPTH, NSHARD, ncol)
    dmod_cols = jnp.transpose(lax.dynamic_index_in_dim(dmod_all, shard, axis=2, keepdims=False), (1, 0, 2))

    delta, new_m, new_v = {}, {}, {}
    grads["mod_w"], delta["mod_w"], new_m["mod_w"], new_v["mod_w"] = _mod_w_update(
        c_all.T, dmod_cols, w["mod_w"], m["mod_w"], v["mod_w"], name="mod_w_update")
    for k in BIG:
        cdim = w[k].shape[-1]
        outs = _adamw(*[t.reshape(-1, cdim) for t in (w[k], grads[k], m[k], v[k])], name=f"adamw_{k}")
        delta[k], new_m[k], new_v[k] = [o.reshape(w[k].shape) for o in outs]
    rest = ("mod_b",) + rep_names + small_names
    packs = [_pack([t[k] for k in rest]) for t in (w, grads, m, v)]
    outs = _adamw(*packs, name="adamw_small")
    rest_shapes = [w[k].shape for k in rest]
    for dst, o in zip((delta, new_m, new_v), outs):
        for k, t in zip(rest, _unpack(o, rest_shapes)):
            dst[k] = t

    loss_all = lax.psum(loss[0, 0], ("x", "y", "c"))
    return (loss_all, grad_x[None], *[grads[k] for k in WEIGHT_ORDER], *[delta[k] for k in WEIGHT_ORDER],
            *[new_m[k] for k in WEIGHT_ORDER], *[new_v[k] for k in WEIGHT_ORDER])


def kernel(x, c, mod_w, mod_b, norm_pre_mix, norm_post_mix, norm_pre_ffn, norm_post_ffn, a_pw1_w, a_pw1_b, a_dw_w, a_dw_b, a_ln_g, a_ln_b, a_pw2_w, a_pw2_b, b_group_w, b_group_b, b_scale, c_in_w, c_conv_w, c_out_w, f_up_w, f_dw_w, f_dw_b, f_down_w, loss_target, m_mod_w, m_mod_b, m_norm_pre_mix, m_norm_post_mix, m_norm_pre_ffn, m_norm_post_ffn, m_a_pw1_w, m_a_pw1_b, m_a_dw_w, m_a_dw_b, m_a_ln_g, m_a_ln_b, m_a_pw2_w, m_a_pw2_b, m_b_group_w, m_b_group_b, m_b_scale, m_c_in_w, m_c_conv_w, m_c_out_w, m_f_up_w, m_f_dw_w, m_f_dw_b, m_f_down_w, v_mod_w, v_mod_b, v_norm_pre_mix, v_norm_post_mix, v_norm_pre_ffn, v_norm_post_ffn, v_a_pw1_w, v_a_pw1_b, v_a_dw_w, v_a_dw_b, v_a_ln_g, v_a_ln_b, v_a_pw2_w, v_a_pw2_b, v_b_group_w, v_b_group_b, v_b_scale, v_c_in_w, v_c_conv_w, v_c_out_w, v_f_up_w, v_f_dw_w, v_f_dw_b, v_f_down_w):
    given = dict(locals())
    w = {k: given[k] for k in WEIGHT_ORDER}
    m = {k: given["m_" + k] for k in WEIGHT_ORDER}
    v = {k: given["v_" + k] for k in WEIGHT_ORDER}
    return _step(x, c, loss_target, w, m, v)
```

```python
import jax
import jax.numpy as jnp
from jax import lax
from jax.experimental import pallas as pl
from jax.experimental.pallas import tpu as pltpu

F32 = jnp.float32
BF16 = jnp.bfloat16

DEPTH = 4
N_MIXERS = 3
CONF_CONV_WIDTH = 31
POOL_WINDOWS = (2, 4, 8, 16)
RMS_EPS = 1e-6
LN_EPS = 1e-5
ADAM_LR = 0.001
ADAM_B1 = 0.9
ADAM_B2 = 0.999
ADAM_EPS = 1e-08
ADAM_WD = 0.01
ADAM_STEP = 10

TM_ROW = 512
TM_CONV = 256
TK_TOKENS = 2048
TM_MM = 1024
TN_MM = 1024
SUBLANES = 8
LANES = 128
NSHARD = 4
NDEV = 8
HALO_A = 32
HALO_POOL = 16
HALO_3 = 8
HALO_BF16 = 16
VMEM_LIMIT = 56 * 1024 * 1024

_PAR = "parallel"
_ARB = "arbitrary"


def _cparams(*sem):
    return pltpu.CompilerParams(dimension_semantics=sem, vmem_limit_bytes=VMEM_LIMIT)


def _pick(n, prefs):
    for p in prefs:
        if p <= n and n % p == 0:
            return p
    return n


def _row_spec(tm, width):
    return pl.BlockSpec((tm, width), lambda i: (i, 0))


def _vec_spec(rows, width):
    return pl.BlockSpec((rows, width), lambda i: (0, 0))


def _prev_spec(tm, hb, width):
    return pl.BlockSpec((hb, width), lambda i: (jnp.maximum(i * (tm // hb) - 1, 0), 0))


def _next_spec(tm, hb, width, total):
    last = total // hb - 1
    return pl.BlockSpec((hb, width), lambda i: (jnp.minimum((i + 1) * (tm // hb), last), 0))


def _sum8(v):
    r, c = v.shape
    return jnp.sum(v.reshape(r // SUBLANES, SUBLANES, c), axis=0)


def _rms(x):
    r = lax.rsqrt(jnp.mean(x * x, axis=-1, keepdims=True) + RMS_EPS)
    return x * r, r


def _rms_bwd(dy, xn, r):
    return r * (dy - xn * jnp.mean(dy * xn, axis=-1, keepdims=True))


def _sigmoid(x):
    return 1.0 / (1.0 + jnp.exp(-x))


_DIMS = {"nn": ((1,), (0,)), "nt": ((1,), (1,)), "tn": ((0,), (0,))}


def _mm(a, b, *, mode, name, out_dtype=F32, bias=None, tm=TM_MM, tn=TN_MM, tk=None, layer=None):
    sharded = layer is not None
    if mode == "nn":
        m, k = a.shape
        n = NSHARD * b.shape[3] if sharded else b.shape[1]
    elif mode == "nt":
        m, k = a.shape
        n = b.shape[2] if sharded else b.shape[0]
    else:
        (k, m), (_, n) = a.shape, b.shape
    ns = n // NSHARD
    ks = k // NSHARD
    tm = _pick(m, (tm, 1408, 512, 256, 128))
    if sharded and mode != "nt":
        tn = _pick(ns, (1408, 768, 512, 256, 128))
    else:
        tn = _pick(n, (tn, 1408, 512, 256, 128))
    if sharded and mode == "nt":
        tk = _pick(ks, (1408, 768, 512, 256, 128))
    else:
        tk = _pick(k, (tk or k, 2816, 1024, 512, 256, 128))
    nk = k // tk
    per_n = ns // tn if sharded and mode != "nt" else 1
    per_k = ks // tk if sharded and mode == "nt" else 1
    dims = (_DIMS[mode], ((), ()))

    def split(idx, per):
        return (idx, 0) if per == 1 else (idx // per, idx % per)

    def body(*refs):
        a_ref, b_ref = refs[0], refs[1]
        bias_ref = refs[2] if bias is not None else None
        o_ref = refs[3] if bias is not None else refs[2]
        part = lax.dot_general(a_ref[...].astype(BF16), b_ref[...].astype(BF16), dims,
                               preferred_element_type=F32)

        def finish(r):
            if bias_ref is not None:
                r = r + bias_ref[...]
            o_ref[...] = r.astype(out_dtype)

        if nk == 1:
            finish(part)
        else:
            acc_ref = refs[-1]
            kk = pl.program_id(2)

            @pl.when(kk == 0)
            def _():
                acc_ref[...] = part

            @pl.when(kk > 0)
            def _():
                acc_ref[...] += part

            @pl.when(kk == nk - 1)
            def _():
                finish(acc_ref[...])

    out_spec = pl.BlockSpec((tm, tn), lambda i, j, kk: (i, j))
    out_shape = jax.ShapeDtypeStruct((m, n), out_dtype)
    if mode == "nn":
        a_spec = pl.BlockSpec((tm, tk), lambda i, j, kk: (i, kk))
        if sharded:
            b_spec = pl.BlockSpec((None, None, tk, tn),
                                  lambda i, j, kk: (split(j, per_n)[0], layer, kk, split(j, per_n)[1]))
        else:
            b_spec = pl.BlockSpec((tk, tn), lambda i, j, kk: (kk, j))
    elif mode == "nt":
        a_spec = pl.BlockSpec((tm, tk), lambda i, j, kk: (i, kk))
        if sharded:
            b_spec = pl.BlockSpec((None, None, tn, tk),
                                  lambda i, j, kk: (split(kk, per_k)[0], layer, j, split(kk, per_k)[1]))
        else:
            b_spec = pl.BlockSpec((tn, tk), lambda i, j, kk: (j, kk))
    else:
        a_spec = pl.BlockSpec((tk, tm), lambda i, j, kk: (kk, i))
        b_spec = pl.BlockSpec((tk, tn), lambda i, j, kk: (kk, j))
        if sharded:
            out_spec = pl.BlockSpec((None, tm, tn), lambda i, j, kk: (split(j, per_n)[0], i, split(j, per_n)[1]))
            out_shape = jax.ShapeDtypeStruct((NSHARD, m, ns), out_dtype)
    in_specs = [a_spec, b_spec]
    args = [a, b]
    if bias is not None:
        in_specs.append(pl.BlockSpec((1, tn), lambda i, j, kk: (0, j)))
        args.append(bias)
    return pl.pallas_call(
        body, grid=(m // tm, n // tn, nk), in_specs=in_specs, out_specs=out_spec, out_shape=out_shape,
        scratch_shapes=[pltpu.VMEM((tm, tn), F32)] if nk > 1 else [],
        compiler_params=_cparams(_PAR, _PAR, _ARB), name=name)(*args)


def _mm_group(a, b, *, mode, name, out_dtype=F32, tm=2048):
    t = a.shape[0]
    tm = _pick(t, (tm, 1024, 512, 256, 128))
    nt_ = t // tm
    g = len(POOL_WINDOWS)
    gd = a.shape[1] // g
    dims = (_DIMS[mode], ((), ()))

    if mode == "tn":
        def body(a_ref, b_ref, o_ref, acc_ref):
            kk = pl.program_id(1)
            part = lax.dot_general(a_ref[...].astype(BF16), b_ref[...].astype(BF16), dims,
                                   preferred_element_type=F32)

            @pl.when(kk == 0)
            def _():
                acc_ref[...] = part

            @pl.when(kk > 0)
            def _():
                acc_ref[...] += part

            @pl.when(kk == nt_ - 1)
            def _():
                o_ref[...] = acc_ref[...].astype(out_dtype)

        return pl.pallas_call(
            body, grid=(g, nt_),
            in_specs=[pl.BlockSpec((tm, gd), lambda gi, kk: (kk, gi)),
                      pl.BlockSpec((tm, gd), lambda gi, kk: (kk, gi))],
            out_specs=pl.BlockSpec((None, gd, gd), lambda gi, kk: (gi, 0, 0)),
            out_shape=jax.ShapeDtypeStruct((g, gd, gd), out_dtype),
            scratch_shapes=[pltpu.VMEM((gd, gd), F32)],
            compiler_params=_cparams(_PAR, _ARB), name=name)(a, b)

    def body(a_ref, b_ref, o_ref):
        o_ref[...] = lax.dot_general(a_ref[...].astype(BF16), b_ref[...].astype(BF16), dims,
                                     preferred_element_type=F32).astype(out_dtype)

    return pl.pallas_call(
        body, grid=(nt_, g),
        in_specs=[pl.BlockSpec((tm, gd), lambda i, gi: (i, gi)),
                  pl.BlockSpec((None, gd, gd), lambda i, gi: (gi, 0, 0))],
        out_specs=pl.BlockSpec((tm, gd), lambda i, gi: (i, gi)),
        out_shape=jax.ShapeDtypeStruct((t, g * gd), out_dtype),
        compiler_params=_cparams(_PAR, _PAR), name=name)(a, b)


def _pre(x, gpre, sc, sh):
    xn, r = _rms(x)
    return (xn * gpre) * (1.0 + sc) + sh, xn, r


def _fwd_first(x, gpre, sc, sh, *, h_dtype, name):
    t, d = x.shape
    tm = _pick(t, (TM_ROW, 256, 128))

    def body(x_ref, gpre_ref, sc_ref, sh_ref, h_ref):
        h, _, _ = _pre(x_ref[...], gpre_ref[...], sc_ref[...], sh_ref[...])
        h_ref[...] = h.astype(h_dtype)

    return pl.pallas_call(
        body, grid=(t // tm,),
        in_specs=[_row_spec(tm, d)] + [_vec_spec(1, d)] * 3,
        out_specs=_row_spec(tm, d), out_shape=jax.ShapeDtypeStruct((t, d), h_dtype),
        compiler_params=_cparams(_PAR), name=name)(x, gpre, sc, sh)


def _fwd_mid(x, y, gpost, gt, gpre, sc, sh, *, h_dtype, name):
    t, d = x.shape
    tm = _pick(t, (TM_ROW, 256, 128))

    def body(x_ref, y_ref, gpost_ref, gt_ref, gpre_ref, sc_ref, sh_ref, xn_ref, h_ref):
        yn, _ = _rms(y_ref[...].astype(F32))
        x_new = x_ref[...] + gt_ref[...] * (yn * gpost_ref[...])
        xn_ref[...] = x_new
        h, _, _ = _pre(x_new, gpre_ref[...], sc_ref[...], sh_ref[...])
        h_ref[...] = h.astype(h_dtype)

    return pl.pallas_call(
        body, grid=(t // tm,),
        in_specs=[_row_spec(tm, d)] * 2 + [_vec_spec(1, d)] * 5,
        out_specs=[_row_spec(tm, d)] * 2,
        out_shape=[jax.ShapeDtypeStruct((t, d), F32), jax.ShapeDtypeStruct((t, d), h_dtype)],
        compiler_params=_cparams(_PAR), name=name)(x, y, gpost, gt, gpre, sc, sh)


def _post_bwd(dx, y, gpost, gt):
    yn, r2 = _rms(y)
    dyn = dx * (gt * gpost)
    dy = _rms_bwd(dyn, yn, r2)
    return dy, dx * yn


def _last_fwd_bwd(x, y, target, gpost, gt, *, name):
    t, d = x.shape
    tm = _pick(t, (TM_ROW, 256, 128))
    n = t // tm

    def body(x_ref, y_ref, tg_ref, gpost_ref, gt_ref, dx_ref, dy_ref, dgpost_ref, dgt_ref, sdy_ref,
             loss_ref, qa, sa, la):
        i = pl.program_id(0)

        @pl.when(i == 0)
        def _():
            qa[...] = jnp.zeros_like(qa)
            sa[...] = jnp.zeros_like(sa)
            la[...] = jnp.zeros_like(la)

        yv = y_ref[...].astype(F32)
        yn, r2 = _rms(yv)
        gt_v, gpost_v = gt_ref[...], gpost_ref[...]
        err = x_ref[...] + gt_v * (yn * gpost_v) - tg_ref[...]
        la[...] += _sum8(err * err)
        dx = err * (1.0 / d)
        dx_ref[...] = dx
        dy = _rms_bwd(dx * (gt_v * gpost_v), yn, r2)
        dy_ref[...] = dy.astype(dy_ref.dtype)
        qa[...] += _sum8(dx * yn)
        sa[...] += _sum8(dy)

        @pl.when(i == n - 1)
        def _():
            q = jnp.sum(qa[...], axis=0, keepdims=True)
            dgpost_ref[...] = gt_v * q
            dgt_ref[...] = gpost_v * q
            sdy_ref[...] = jnp.sum(sa[...], axis=0, keepdims=True)
            tot = jnp.sum(jnp.sum(la[...], axis=0, keepdims=True), axis=1, keepdims=True)
            loss_ref[...] = tot * (0.5 / d)

    return pl.pallas_call(
        body, grid=(n,),
        in_specs=[_row_spec(tm, d)] * 3 + [_vec_spec(1, d)] * 2,
        out_specs=[_row_spec(tm, d)] * 2 + [_vec_spec(1, d)] * 3 + [_vec_spec(1, 1)],
        out_shape=[jax.ShapeDtypeStruct((t, d), F32), jax.ShapeDtypeStruct((t, d), BF16)]
        + [jax.ShapeDtypeStruct((1, d), F32)] * 3 + [jax.ShapeDtypeStruct((1, 1), F32)],
        scratch_shapes=[pltpu.VMEM((SUBLANES, d), F32)] * 3,
        compiler_params=_cparams(_ARB), name=name)(x, y, target, gpost, gt)


def _bwd_mid(dx_new, dh, x_in, gpre, sc, y_prev, gpost_p, gt_p, *, dy_dtype, name):
    t, d = x_in.shape
    tm = _pick(t, (TM_ROW, 256, 128))
    n = t // tm

    def body(dxn_ref, dh_ref, x_ref, gpre_ref, sc_ref, y_ref, gpost_ref, gt_ref,
             dx_ref, dy_ref, dsh_ref, dsc_ref, dgpre_ref, dgpost_ref, dgt_ref, sdy_ref, a1, a2, aq, asd):
        i = pl.program_id(0)

        @pl.when(i == 0)
        def _():
            for a in (a1, a2, aq, asd):
                a[...] = jnp.zeros_like(a)

        dh_v = dh_ref[...]
        xn, r = _rms(x_ref[...])
        dx = dxn_ref[...] + _rms_bwd(dh_v * ((1.0 + sc_ref[...]) * gpre_ref[...]), xn, r)
        dx_ref[...] = dx
        a1[...] += _sum8(dh_v)
        a2[...] += _sum8(dh_v * xn)
        dy, dxyn = _post_bwd(dx, y_ref[...].astype(F32), gpost_ref[...], gt_ref[...])
        dy_ref[...] = dy.astype(dy_dtype)
        aq[...] += _sum8(dxyn)
        asd[...] += _sum8(dy)

        @pl.when(i == n - 1)
        def _():
            s2 = jnp.sum(a2[...], axis=0, keepdims=True)
            q = jnp.sum(aq[...], axis=0, keepdims=True)
            dsh_ref[...] = jnp.sum(a1[...], axis=0, keepdims=True)
            dsc_ref[...] = gpre_ref[...] * s2
            dgpre_ref[...] = (1.0 + sc_ref[...]) * s2
            dgpost_ref[...] = gt_ref[...] * q
            dgt_ref[...] = gpost_ref[...] * q
            sdy_ref[...] = jnp.sum(asd[...], axis=0, keepdims=True)

    return pl.pallas_call(
        body, grid=(n,),
        in_specs=[_row_spec(tm, d)] * 3 + [_vec_spec(1, d)] * 2 + [_row_spec(tm, d)] + [_vec_spec(1, d)] * 2,
        out_specs=[_row_spec(tm, d)] * 2 + [_vec_spec(1, d)] * 6,
        out_shape=[jax.ShapeDtypeStruct((t, d), F32), jax.ShapeDtypeStruct((t, d), dy_dtype)]
        + [jax.ShapeDtypeStruct((1, d), F32)] * 6,
        scratch_shapes=[pltpu.VMEM((SUBLANES, d), F32)] * 4,
        compiler_params=_cparams(_ARB), name=name)(dx_new, dh, x_in, gpre, sc, y_prev, gpost_p, gt_p)


def _bwd_first(dx_new, dh, x_in, gpre, sc, *, name):
    t, d = x_in.shape
    tm = _pick(t, (TM_ROW, 256, 128))
    n = t // tm

    def body(dxn_ref, dh_ref, x_ref, gpre_ref, sc_ref, dx_ref, dsh_ref, dsc_ref, dgpre_ref, a1, a2):
        i = pl.program_id(0)

        @pl.when(i == 0)
        def _():
            a1[...] = jnp.zeros_like(a1)
            a2[...] = jnp.zeros_like(a2)

        dh_v = dh_ref[...]
        xn, r = _rms(x_ref[...])
        dx_ref[...] = dxn_ref[...] + _rms_bwd(dh_v * ((1.0 + sc_ref[...]) * gpre_ref[...]), xn, r)
        a1[...] += _sum8(dh_v)
        a2[...] += _sum8(dh_v * xn)

        @pl.when(i == n - 1)
        def _():
            s2 = jnp.sum(a2[...], axis=0, keepdims=True)
            dsh_ref[...] = jnp.sum(a1[...], axis=0, keepdims=True)
            dsc_ref[...] = gpre_ref[...] * s2
            dgpre_ref[...] = (1.0 + sc_ref[...]) * s2

    return pl.pallas_call(
        body, grid=(n,),
        in_specs=[_row_spec(tm, d)] * 3 + [_vec_spec(1, d)] * 2,
        out_specs=[_row_spec(tm, d)] + [_vec_spec(1, d)] * 3,
        out_shape=[jax.ShapeDtypeStruct((t, d), F32)] + [jax.ShapeDtypeStruct((1, d), F32)] * 3,
        scratch_shapes=[pltpu.VMEM((SUBLANES, d), F32)] * 2,
        compiler_params=_cparams(_ARB), name=name)(dx_new, dh, x_in, gpre, sc)


ROWS_BLK = 16
COLS_BLK = 256


def _bcast_rows(dst, src_ref, first, nrows):
    for k in range(nrows):
        dst[first + k] = jnp.broadcast_to(src_ref[pl.ds(k, 1), :], dst.shape[1:])


def _shifted_rows(x, off, rows):
    if off % SUBLANES == 0:
        return x[off:off + rows]
    return pltpu.roll(x, x.shape[0] - off, axis=0)[:rows]


def _conv3_blk(buf, wb, first, rows, cols):
    base = first - first % SUBLANES
    window = buf[pl.ds(base, rows + SUBLANES), cols]
    xs = [_shifted_rows(window, first - base + k, rows) for k in range(3)]
    out = xs[0] * wb[0, pl.ds(0, rows), cols]
    out = out + xs[1] * wb[1, pl.ds(0, rows), cols]
    out = out + xs[2] * wb[2, pl.ds(0, rows), cols]
    return out + wb[3, pl.ds(0, rows), cols], xs


def _ffn_gate_fwd(u, w, b, *, name):
    t, f2 = u.shape
    f = f2 // 2
    tm = _pick(t, (TM_CONV,))
    hb = HALO_BF16
    rb = ROWS_BLK
    cw = _pick(f, (COLS_BLK, LANES))

    def body(u_ref, up_ref, w_ref, b_ref, a_ref, v_ref, buf, wb):
        i = pl.program_id(0)
        buf[pl.ds(hb, tm), :] = u_ref[...].astype(F32)
        buf[pl.ds(0, hb), :] = jnp.where(i > 0, up_ref[...].astype(F32), 0.0)
        _bcast_rows(wb, w_ref, 0, 3)
        _bcast_rows(wb, b_ref, 3, 1)
        for c0 in range(0, f, cw):
            gcols, vcols = pl.ds(c0, cw), pl.ds(f + c0, cw)
            for r0 in range(0, tm, rb):
                rows = pl.ds(r0, rb)
                vg, _ = _conv3_blk(buf, wb, hb - 2 + r0, rb, gcols)
                vv, _ = _conv3_blk(buf, wb, hb - 2 + r0, rb, vcols)
                v_ref[rows, gcols] = vg.astype(BF16)
                v_ref[rows, vcols] = vv.astype(BF16)
                a_ref[rows, gcols] = (vg * _sigmoid(vg) * vv).astype(BF16)

    return pl.pallas_call(
        body, grid=(t // tm,),
        in_specs=[_row_spec(tm, f2), _prev_spec(tm, hb, f2), _vec_spec(3, f2), _vec_spec(1, f2)],
        out_specs=[_row_spec(tm, f), _row_spec(tm, f2)],
        out_shape=[jax.ShapeDtypeStruct((t, f), BF16), jax.ShapeDtypeStruct((t, f2), BF16)],
        scratch_shapes=[pltpu.VMEM((tm + hb, f2), F32), pltpu.VMEM((4, rb, f2), F32)],
        compiler_params=_cparams(_PAR), name=name)(u, u, w, b)


def _ffn_gate_bwd(u, v, da, w, *, name):
    t, f2 = u.shape
    f = f2 // 2
    tm = _pick(t, (TM_CONV,))
    hb = HALO_BF16
    n = t // tm
    rb = ROWS_BLK
    cw = _pick(f, (COLS_BLK, LANES))
    blocks = [(r0, rb) for r0 in range(0, tm, rb)] + [(tm, hb)]

    def body(u_ref, v_ref, vn_ref, da_ref, dan_ref, w_ref, du_ref, dw_ref, db_ref, dvbuf, wb, wacc, bacc):
        i = pl.program_id(0)

        @pl.when(i == 0)
        def _():
            wacc[...] = jnp.zeros_like(wacc)
            bacc[...] = jnp.zeros_like(bacc)

        _bcast_rows(wb, w_ref, 0, 3)
        for c0 in range(0, f, cw):
            gcols, vcols = pl.ds(c0, cw), pl.ds(f + c0, cw)
            for r0, rows in blocks:
                if r0 < tm:
                    vg, vv = v_ref[pl.ds(r0, rows), gcols], v_ref[pl.ds(r0, rows), vcols]
                    dav = da_ref[pl.ds(r0, rows), gcols].astype(F32)
                else:
                    vg, vv = vn_ref[:, gcols], vn_ref[:, vcols]
                    dav = jnp.where(i < n - 1, dan_ref[:, gcols].astype(F32), 0.0)
                vg, vv = vg.astype(F32), vv.astype(F32)
                sg = _sigmoid(vg)
                dvg = dav * vv * (sg * (1.0 + vg * (1.0 - sg)))
                dvv = dav * (vg * sg)
                dvbuf[pl.ds(r0, rows), gcols] = dvg
                dvbuf[pl.ds(r0, rows), vcols] = dvv
                if r0 < tm:
                    bacc[:, gcols] += _sum8(dvg)
                    bacc[:, vcols] += _sum8(dvv)
        for c0 in range(0, f2, cw):
            cols = pl.ds(c0, cw)
            for r0 in range(0, tm, rb):
                uv = u_ref[pl.ds(r0, rb), cols].astype(F32)
                window = dvbuf[pl.ds(r0, rb + SUBLANES), cols]
                du = None
                for k in range(3):
                    dvk = _shifted_rows(window, 2 - k, rb)
                    term = dvk * wb[k, :, cols]
                    du = term if du is None else du + term
                    wacc[k, :, cols] += _sum8(uv * dvk)
                du_ref[pl.ds(r0, rb), cols] = du.astype(BF16)

        @pl.when(i == n - 1)
        def _():
            db_ref[...] = jnp.sum(bacc[...], axis=0, keepdims=True)
            dw_ref[...] = jnp.sum(wacc[...], axis=1)

    return pl.pallas_call(
        body, grid=(n,),
        in_specs=[_row_spec(tm, f2), _row_spec(tm, f2), _next_spec(tm, hb, f2, t),
                  _row_spec(tm, f), _next_spec(tm, hb, f, t), _vec_spec(3, f2)],
        out_specs=[_row_spec(tm, f2), _vec_spec(3, f2), _vec_spec(1, f2)],
        out_shape=[jax.ShapeDtypeStruct((t, f2), BF16), jax.ShapeDtypeStruct((3, f2), F32),
                   jax.ShapeDtypeStruct((1, f2), F32)],
        scratch_shapes=[pltpu.VMEM((tm + hb, f2), F32), pltpu.VMEM((3, rb, f2), F32),
                        pltpu.VMEM((3, SUBLANES, f2), F32), pltpu.VMEM((SUBLANES, f2), F32)],
        compiler_params=_cparams(_ARB), name=name)(u, v, v, da, da, w)


def _glu(u, b1, d):
    return (u[:, :d] + b1[:, :d]) * _sigmoid(u[:, d:] + b1[:, d:])


ROWS_TAPS = 32
ROWS_NORM = 16


def _fill_glu_buf(buf, u_ref, up_ref, b1_ref, i, tm, d):
    cw = _pick(d, (COLS_BLK, LANES))
    for c0 in range(0, d, cw):
        b1 = jnp.concatenate([b1_ref[:, pl.ds(c0, cw)], b1_ref[:, pl.ds(d + c0, cw)]], axis=1)
        up = jnp.concatenate([up_ref[:, pl.ds(c0, cw)], up_ref[:, pl.ds(d + c0, cw)]], axis=1)
        buf[pl.ds(0, HALO_A), pl.ds(c0, cw)] = jnp.where(i > 0, _glu(up, b1, cw), 0.0)
        for r0 in range(0, tm, ROWS_TAPS):
            rows = pl.ds(r0, ROWS_TAPS)
            uv = jnp.concatenate([u_ref[rows, pl.ds(c0, cw)], u_ref[rows, pl.ds(d + c0, cw)]], axis=1)
            buf[pl.ds(HALO_A + r0, ROWS_TAPS), pl.ds(c0, cw)] = _glu(uv, b1, cw)


def _taps31(buf, r0, cols, offs, use):
    nv = ROWS_TAPS // SUBLANES
    nrows = ROWS_TAPS + SUBLANES * (-(-max(offs) // SUBLANES))
    window = buf[pl.ds(r0, nrows), cols]
    shifted = {b: _shifted_rows(window, b, nrows - SUBLANES) if b else window
               for b in sorted({o % SUBLANES for o in offs})}
    for k, o in enumerate(offs):
        b, a = o % SUBLANES, o // SUBLANES
        use(k, [shifted[b][SUBLANES * (a + v):SUBLANES * (a + v + 1)] for v in range(nv)])


def _conv_taps_blk(buf, wb, r0, cols, offs):
    nv = ROWS_TAPS // SUBLANES
    acc = [None] * nv

    def use(k, rows):
        wk = wb[k, :, cols]
        for v in range(nv):
            term = rows[v] * wk
            acc[v] = term if acc[v] is None else acc[v] + term

    _taps31(buf, r0, cols, offs, use)
    return jnp.concatenate(acc, axis=0)


def _layernorm_parts(x):
    mu = jnp.mean(x, axis=-1, keepdims=True)
    xc = x - mu
    rstd = lax.rsqrt(jnp.mean(xc * xc, axis=-1, keepdims=True) + LN_EPS)
    return xc * rstd, rstd


_FWD_OFFS = tuple(HALO_A - (CONF_CONV_WIDTH - 1) + k for k in range(CONF_CONV_WIDTH))
_BWD_OFFS = tuple(CONF_CONV_WIDTH - 1 - k for k in range(CONF_CONV_WIDTH))


def _a_fwd(u1, b1, dww, dwb, lng, lnb, *, name):
    t, d2 = u1.shape
    d = d2 // 2
    tm = _pick(t, (TM_CONV,))

    def body(u_ref, up_ref, b1_ref, w_ref, wbias_ref, g_ref, bb_ref, o_ref, u3_ref, buf, wb):
        i = pl.program_id(0)
        _fill_glu_buf(buf, u_ref, up_ref, b1_ref, i, tm, d)
        _bcast_rows(wb, w_ref, 0, CONF_CONV_WIDTH)
        for c0 in range(0, d, LANES):
            cols = pl.ds(c0, LANES)
            for r0 in range(0, tm, ROWS_TAPS):
                u3_ref[pl.ds(r0, ROWS_TAPS), cols] = (_conv_taps_blk(buf, wb, r0, cols, _FWD_OFFS)
                                                      + wbias_ref[:, cols])
        for r0 in range(0, tm, ROWS_NORM):
            rows = pl.ds(r0, ROWS_NORM)
            xhat, _ = _layernorm_parts(u3_ref[rows, :])
            u4 = xhat * g_ref[...] + bb_ref[...]
            o_ref[rows, :] = (u4 * _sigmoid(u4)).astype(BF16)

    return pl.pallas_call(
        body, grid=(t // tm,),
        in_specs=[_row_spec(tm, d2), _prev_spec(tm, HALO_A, d2), _vec_spec(1, d2),
                  _vec_spec(CONF_CONV_WIDTH, d)] + [_vec_spec(1, d)] * 3,
        out_specs=[_row_spec(tm, d)] * 2,
        out_shape=[jax.ShapeDtypeStruct((t, d), BF16), jax.ShapeDtypeStruct((t, d), F32)],
        scratch_shapes=[pltpu.VMEM((tm + HALO_A, d), F32), pltpu.VMEM((CONF_CONV_WIDTH, SUBLANES, d), F32)],
        compiler_params=_cparams(_PAR), name=name)(u1, u1, b1, dww, dwb, lng, lnb)


def _a_bwd_norm(u3, du5, lng, lnb, *, name):
    t, d = u3.shape
    tm = _pick(t, (TM_ROW, 256, 128))
    n = t // tm

    def body(u3_ref, du5_ref, g_ref, bb_ref, du3_ref, dg_ref, db_ref, dwb_ref, ag, ab, aw):
        i = pl.program_id(0)

        @pl.when(i == 0)
        def _():
            for a in (ag, ab, aw):
                a[...] = jnp.zeros_like(a)

        g = g_ref[...]
        for r0 in range(0, tm, ROWS_NORM):
            rows = pl.ds(r0, ROWS_NORM)
            xhat, rstd = _layernorm_parts(u3_ref[rows, :])
            u4 = xhat * g + bb_ref[...]
            sg = _sigmoid(u4)
            du4 = du5_ref[rows, :] * (sg * (1.0 + u4 * (1.0 - sg)))
            dxh = du4 * g
            du3 = rstd * (dxh - jnp.mean(dxh, axis=-1, keepdims=True)
                          - xhat * jnp.mean(dxh * xhat, axis=-1, keepdims=True))
            du3_ref[rows, :] = du3
            ag[...] += _sum8(du4 * xhat)
            ab[...] += _sum8(du4)
            aw[...] += _sum8(du3)

        @pl.when(i == n - 1)
        def _():
            for a, o in ((ag, dg_ref), (ab, db_ref), (aw, dwb_ref)):
                o[...] = jnp.sum(a[...], axis=0, keepdims=True)

    return pl.pallas_call(
        body, grid=(n,),
        in_specs=[_row_spec(tm, d)] * 2 + [_vec_spec(1, d)] * 2,
        out_specs=[_row_spec(tm, d)] + [_vec_spec(1, d)] * 3,
        out_shape=[jax.ShapeDtypeStruct((t, d), F32)] + [jax.ShapeDtypeStruct((1, d), F32)] * 3,
        scratch_shapes=[pltpu.VMEM((SUBLANES, d), F32)] * 3,
        compiler_params=_cparams(_ARB), name=name)(u3, du5, lng, lnb)


def _a_bwd_conv(u1, du3, b1, dww, *, name):
    t, d2 = u1.shape
    d = d2 // 2
    tm = _pick(t, (TM_CONV,))
    n = t // tm
    kw = CONF_CONV_WIDTH
    nv = ROWS_TAPS // SUBLANES

    def body(u_ref, up_ref, g3_ref, g3n_ref, b1_ref, w_ref, du1_ref, dw_ref, db1_ref,
             buf, gbuf, wb, wacc, bacc):
        i = pl.program_id(0)

        @pl.when(i == 0)
        def _():
            wacc[...] = jnp.zeros_like(wacc)
            bacc[...] = jnp.zeros_like(bacc)

        _fill_glu_buf(buf, u_ref, up_ref, b1_ref, i, tm, d)
        _bcast_rows(wb, w_ref, 0, kw)
        gbuf[pl.ds(0, tm), :] = g3_ref[...]
        gbuf[pl.ds(tm, HALO_A), :] = jnp.where(i < n - 1, g3n_ref[...], 0.0)
        for c0 in range(0, d, LANES):
            cols, gcols = pl.ds(c0, LANES), pl.ds(d + c0, LANES)
            for r0 in range(0, tm, ROWS_TAPS):
                rows = pl.ds(r0, ROWS_TAPS)
                u2 = [buf[pl.ds(HALO_A + r0 + SUBLANES * v, SUBLANES), cols] for v in range(nv)]
                acc = [None] * nv

                def use(k, gs):
                    wk = wb[k, :, cols]
                    part = None
                    for v in range(nv):
                        term = gs[v] * wk
                        acc[v] = term if acc[v] is None else acc[v] + term
                        prod = u2[v] * gs[v]
                        part = prod if part is None else part + prod
                    wacc[k, :, cols] += part

                _taps31(gbuf, r0, cols, _BWD_OFFS, use)
                du2 = jnp.concatenate(acc, axis=0)
                av = u_ref[rows, cols] + b1_ref[:, cols]
                sg = _sigmoid(u_ref[rows, gcols] + b1_ref[:, gcols])
                da = du2 * sg
                dg = du2 * av * (sg * (1.0 - sg))
                du1_ref[rows, cols] = da.astype(BF16)
                du1_ref[rows, gcols] = dg.astype(BF16)
                bacc[:, cols] += _sum8(da)
                bacc[:, gcols] += _sum8(dg)

        @pl.when(i == n - 1)
        def _():
            dw_ref[...] = jnp.sum(wacc[...], axis=1)
            db1_ref[...] = jnp.sum(bacc[...], axis=0, keepdims=True)

    return pl.pallas_call(
        body, grid=(n,),
        in_specs=[_row_spec(tm, d2), _prev_spec(tm, HALO_A, d2), _row_spec(tm, d),
                  _next_spec(tm, HALO_A, d, t), _vec_spec(1, d2), _vec_spec(kw, d)],
        out_specs=[_row_spec(tm, d2), _vec_spec(kw, d), _vec_spec(1, d2)],
        out_shape=[jax.ShapeDtypeStruct((t, d2), BF16), jax.ShapeDtypeStruct((kw, d), F32),
                   jax.ShapeDtypeStruct((1, d2), F32)],
        scratch_shapes=[pltpu.VMEM((tm + HALO_A, d), F32), pltpu.VMEM((tm + HALO_A, d), F32),
                        pltpu.VMEM((kw, SUBLANES, d), F32),
                        pltpu.VMEM((kw, SUBLANES, d), F32), pltpu.VMEM((SUBLANES, d2), F32)],
        compiler_params=_cparams(_ARB), name=name)(u1, u1, du3, du3, b1, dww)


def _pool_counts(i, tm, w):
    pos = (i * tm + lax.broadcasted_iota(jnp.int32, (tm, 1), 0) + 1).astype(F32)
    return jnp.minimum(pos, float(w))


def _b_pool_fwd(h, *, name):
    t, d = h.shape
    gd = d // len(POOL_WINDOWS)
    tm = _pick(t, (TM_CONV,))
    hb = HALO_POOL

    def body(h_ref, hp_ref, o_ref, buf):
        i = pl.program_id(0)
        buf[pl.ds(0, hb), :] = jnp.where(i > 0, hp_ref[...], 0.0)
        buf[pl.ds(hb, tm), :] = h_ref[...]
        for g, w in enumerate(POOL_WINDOWS):
            cols = pl.ds(g * gd, gd)
            window = buf[:, cols]
            cur = window[hb:]
            s = cur
            for j in range(1, w):
                s = s + _shifted_rows(window, hb - j, tm)
            o_ref[:, cols] = (s / _pool_counts(i, tm, w) - cur).astype(BF16)

    return pl.pallas_call(
        body, grid=(t // tm,),
        in_specs=[_row_spec(tm, d), _prev_spec(tm, hb, d)],
        out_specs=_row_spec(tm, d), out_shape=jax.ShapeDtypeStruct((t, d), BF16),
        scratch_shapes=[pltpu.VMEM((tm + hb, d), F32)],
        compiler_params=_cparams(_PAR), name=name)(h, h)


def _b_pool_bwd(dp, *, name):
    t, d = dp.shape
    gd = d // len(POOL_WINDOWS)
    tm = _pick(t, (TM_CONV,))
    hb = HALO_POOL
    n = t // tm

    def body(dp_ref, dpn_ref, o_ref, buf):
        i = pl.program_id(0)
        for g, w in enumerate(POOL_WINDOWS):
            cols = pl.ds(g * gd, gd)
            buf[pl.ds(0, tm), cols] = dp_ref[:, cols] / _pool_counts(i, tm, w)
            buf[pl.ds(tm, hb), cols] = jnp.where(i < n - 1, dpn_ref[:, cols] * (1.0 / w), 0.0)
            window = buf[:, cols]
            s = window[:tm]
            for j in range(1, w):
                s = s + _shifted_rows(window, j, tm)
            o_ref[:, cols] = s - dp_ref[:, cols]

    return pl.pallas_call(
        body, grid=(n,),
        in_specs=[_row_spec(tm, d), _next_spec(tm, hb, d, t)],
        out_specs=_row_spec(tm, d), out_shape=jax.ShapeDtypeStruct((t, d), F32),
        scratch_shapes=[pltpu.VMEM((tm + hb, d), F32)],
        compiler_params=_cparams(_PAR), name=name)(dp, dp)


def _b_affine_fwd(mixed, gb, scale, *, name):
    t, d = mixed.shape
    tm = _pick(t, (TM_ROW, 256, 128))

    def body(m_ref, gb_ref, s_ref, o_ref):
        o_ref[...] = ((m_ref[...] + gb_ref[...]) * s_ref[...]).astype(BF16)

    return pl.pallas_call(
        body, grid=(t // tm,), in_specs=[_row_spec(tm, d)] + [_vec_spec(1, d)] * 2,
        out_specs=_row_spec(tm, d), out_shape=jax.ShapeDtypeStruct((t, d), BF16),
        compiler_params=_cparams(_PAR), name=name)(mixed, gb, scale)


def _b_affine_bwd(dy, mixed, gb, scale, *, name):
    t, d = mixed.shape
    tm = _pick(t, (TM_ROW, 256, 128))
    n = t // tm

    def body(dy_ref, m_ref, gb_ref, s_ref, dm_ref, ds_ref, dgb_ref, a1, a2):
        i = pl.program_id(0)

        @pl.when(i == 0)
        def _():
            a1[...] = jnp.zeros_like(a1)
            a2[...] = jnp.zeros_like(a2)

        dy_v = dy_ref[...]
        dm_ref[...] = (dy_v * s_ref[...]).astype(BF16)
        a1[...] += _sum8(dy_v * (m_ref[...] + gb_ref[...]))
        a2[...] += _sum8(dy_v)

        @pl.when(i == n - 1)
        def _():
            ds_ref[...] = jnp.sum(a1[...], axis=0, keepdims=True)
            dgb_ref[...] = jnp.sum(a2[...], axis=0, keepdims=True) * s_ref[...]

    return pl.pallas_call(
        body, grid=(n,), in_specs=[_row_spec(tm, d)] * 2 + [_vec_spec(1, d)] * 2,
        out_specs=[_row_spec(tm, d)] + [_vec_spec(1, d)] * 2,
        out_shape=[jax.ShapeDtypeStruct((t, d), BF16)] + [jax.ShapeDtypeStruct((1, d), F32)] * 2,
        scratch_shapes=[pltpu.VMEM((SUBLANES, d), F32)] * 2,
        compiler_params=_cparams(_ARB), name=name)(dy, mixed, gb, scale)


def _c_gate_fwd(bcx, wc, *, name):
    t, d3 = bcx.shape
    d = d3 // 3
    tm = _pick(t, (TM_CONV,))
    hb = HALO_3

    rb = ROWS_BLK
    cw = _pick(d, (COLS_BLK, LANES))

    def body(x_ref, xp_ref, w_ref, z_ref, buf, wb):
        i = pl.program_id(0)
        _bcast_rows(wb, w_ref, 0, 3)
        wb[3] = jnp.zeros(wb.shape[1:], F32)
        for c0 in range(0, d, cw):
            cols, ccols, vcols = pl.ds(c0, cw), pl.ds(d + c0, cw), pl.ds(2 * d + c0, cw)
            buf[pl.ds(0, hb), cols] = jnp.where(i > 0, xp_ref[:, ccols] * xp_ref[:, vcols], 0.0)
            for r0 in range(0, tm, rb):
                rows = pl.ds(r0, rb)
                buf[pl.ds(hb + r0, rb), cols] = x_ref[rows, ccols] * x_ref[rows, vcols]
            for r0 in range(0, tm, rb):
                rows = pl.ds(r0, rb)
                q, _ = _conv3_blk(buf, wb, hb - 2 + r0, rb, cols)
                z_ref[rows, cols] = (x_ref[rows, cols] * q).astype(BF16)

    return pl.pallas_call(
        body, grid=(t // tm,),
        in_specs=[_row_spec(tm, d3), _prev_spec(tm, hb, d3), _vec_spec(3, d)],
        out_specs=_row_spec(tm, d), out_shape=jax.ShapeDtypeStruct((t, d), BF16),
        scratch_shapes=[pltpu.VMEM((tm + hb, d), F32), pltpu.VMEM((4, rb, d), F32)],
        compiler_params=_cparams(_PAR), name=name)(bcx, bcx, wc)


def _c_gate_bwd(bcx, dz, wc, *, name):
    t, d3 = bcx.shape
    d = d3 // 3
    tm = _pick(t, (TM_CONV,))
    hb = HALO_3
    n = t // tm

    rb = ROWS_BLK
    cw = _pick(d, (COLS_BLK, LANES))

    def body(x_ref, xp_ref, xn_ref, dz_ref, dzn_ref, w_ref, o_ref, dw_ref, pbuf, qbuf, wb, wacc):
        i = pl.program_id(0)

        @pl.when(i == 0)
        def _():
            wacc[...] = jnp.zeros_like(wacc)

        _bcast_rows(wb, w_ref, 0, 3)
        wb[3] = jnp.zeros(wb.shape[1:], F32)
        for c0 in range(0, d, cw):
            cols, ccols, vcols = pl.ds(c0, cw), pl.ds(d + c0, cw), pl.ds(2 * d + c0, cw)
            pbuf[pl.ds(0, hb), cols] = jnp.where(i > 0, xp_ref[:, ccols] * xp_ref[:, vcols], 0.0)
            qbuf[pl.ds(tm, hb), cols] = jnp.where(i < n - 1, dzn_ref[:, cols] * xn_ref[:, cols], 0.0)
            for r0 in range(0, tm, rb):
                rows = pl.ds(r0, rb)
                pbuf[pl.ds(hb + r0, rb), cols] = x_ref[rows, ccols] * x_ref[rows, vcols]
            for r0 in range(0, tm, rb):
                rows = pl.ds(r0, rb)
                q, _ = _conv3_blk(pbuf, wb, hb - 2 + r0, rb, cols)
                dz_v = dz_ref[rows, cols]
                qbuf[rows, cols] = dz_v * x_ref[rows, cols]
                o_ref[rows, cols] = (dz_v * q).astype(BF16)
            for r0 in range(0, tm, rb):
                rows = pl.ds(r0, rb)
                pv = pbuf[pl.ds(hb + r0, rb), cols]
                window = qbuf[pl.ds(r0, rb + SUBLANES), cols]
                dp = None
                for k in range(3):
                    dqk = _shifted_rows(window, 2 - k, rb)
                    term = dqk * wb[k, :, cols]
                    dp = term if dp is None else dp + term
                    wacc[k, :, cols] += _sum8(pv * dqk)
                o_ref[rows, ccols] = (dp * x_ref[rows, vcols]).astype(BF16)
                o_ref[rows, vcols] = (dp * x_ref[rows, ccols]).astype(BF16)

        @pl.when(i == n - 1)
        def _():
            dw_ref[...] = jnp.sum(wacc[...], axis=1)

    return pl.pallas_call(
        body, grid=(n,),
        in_specs=[_row_spec(tm, d3), _prev_spec(tm, hb, d3), _next_spec(tm, hb, d3, t),
                  _row_spec(tm, d), _next_spec(tm, hb, d, t), _vec_spec(3, d)],
        out_specs=[_row_spec(tm, d3), _vec_spec(3, d)],
        out_shape=[jax.ShapeDtypeStruct((t, d3), BF16), jax.ShapeDtypeStruct((3, d), F32)],
        scratch_shapes=[pltpu.VMEM((tm + hb, d), F32), pltpu.VMEM((tm + hb, d), F32),
                        pltpu.VMEM((4, rb, d), F32), pltpu.VMEM((3, SUBLANES, d), F32)],
        compiler_params=_cparams(_ARB), name=name)(bcx, bcx, bcx, dz, dz, wc)


def _row(v):
    return v.reshape(1, -1)


def _kind_of(j):
    return "f" if j % 2 else "abc"[(j // 2) % N_MIXERS]


BIG = ("a_pw1_w", "a_pw2_w", "b_group_w", "c_in_w", "c_out_w", "f_up_w", "f_down_w")
COL_SHARDED = ("a_pw1_w", "c_in_w", "f_up_w")


def _local_step(x, target, mod, p):
    nsub = 2 * DEPTH
    norm_names = (("norm_pre_mix", "norm_post_mix"), ("norm_pre_ffn", "norm_post_ffn"))
    gpre = [_row(p[norm_names[s][0]][i]) for i in range(DEPTH) for s in (0, 1)]
    gpost = [_row(p[norm_names[s][1]][i]) for i in range(DEPTH) for s in (0, 1)]
    sh = [_row(mod[i, 3 * s + 0]) for i in range(DEPTH) for s in (0, 1)]
    sc = [_row(mod[i, 3 * s + 1]) for i in range(DEPTH) for s in (0, 1)]
    gt = [_row(mod[i, 3 * s + 2]) for i in range(DEPTH) for s in (0, 1)]

    def h_dtype(j):
        return F32 if _kind_of(j) == "b" else BF16

    xs, hs, ys, saved = [x], [], [], []

    hs.append(_fwd_first(x, gpre[0], sc[0], sh[0], h_dtype=h_dtype(0), name="fwd_first"))
    for j in range(nsub):
        i, kind = j // 2, _kind_of(j)
        slot = i // N_MIXERS
        h = hs[j]
        tag = f"{kind}{j}"
        if kind == "f":
            u = _mm(h, p["f_up_w"], mode="nn", layer=i, out_dtype=BF16, name=f"ffn_up_{tag}")
            a, vpre = _ffn_gate_fwd(u, p["f_dw_w"][i], _row(p["f_dw_b"][i]), name=f"ffn_gate_{tag}")
            y = _mm(a, p["f_down_w"][i], mode="nn", out_dtype=BF16, name=f"ffn_down_{tag}")
            saved.append((u, a, vpre))
        elif kind == "a":
            u1 = _mm(h, p["a_pw1_w"], mode="nn", layer=slot, name=f"a_pw1_{tag}")
            u5, u3 = _a_fwd(u1, _row(p["a_pw1_b"][slot]), p["a_dw_w"][slot], _row(p["a_dw_b"][slot]),
                            _row(p["a_ln_g"][slot]), _row(p["a_ln_b"][slot]), name=f"a_conv_{tag}")
            y = _mm(u5, p["a_pw2_w"][slot], mode="nn", bias=_row(p["a_pw2_b"][slot]), out_dtype=BF16,
                    name=f"a_pw2_{tag}")
            saved.append((u1, u5, u3))
        elif kind == "b":
            pooled = _b_pool_fwd(h, name=f"b_pool_{tag}")
            mixed = _mm_group(pooled, p["b_group_w"][slot], mode="nn", name=f"b_mix_{tag}")
            y = _b_affine_fwd(mixed, _row(p["b_group_b"][slot]), _row(p["b_scale"][slot]), name=f"b_aff_{tag}")
            saved.append((pooled, mixed))
        else:
            bcx = _mm(h, p["c_in_w"], mode="nn", layer=slot, name=f"c_in_{tag}")
            z = _c_gate_fwd(bcx, p["c_conv_w"][slot], name=f"c_gate_{tag}")
            y = _mm(z, p["c_out_w"][slot], mode="nn", out_dtype=BF16, name=f"c_out_{tag}")
            saved.append((bcx, z))
        ys.append(y)
        if j + 1 < nsub:
            x_new, h_next = _fwd_mid(xs[j], y, gpost[j], gt[j], gpre[j + 1], sc[j + 1], sh[j + 1],
                                     h_dtype=h_dtype(j + 1), name=f"fwd_mid_{j}")
            xs.append(x_new)
            hs.append(h_next)

    n_of = {"a": len([i for i in range(DEPTH) if i % N_MIXERS == 0]),
            "b": len([i for i in range(DEPTH) if i % N_MIXERS == 1]),
            "c": len([i for i in range(DEPTH) if i % N_MIXERS == 2]), "f": DEPTH, "n": DEPTH}
    g = {k: [None] * n_of[k[0]] for k in p}
    dmod = [[None] * 6 for _ in range(DEPTH)]

    last = nsub - 1
    dx, dy, dgpost, dgt, sdy, loss = _last_fwd_bwd(xs[last], ys[last], target, gpost[last], gt[last],
                                                   name="loss_head")
    for j in range(last, -1, -1):
        i, kind = j // 2, _kind_of(j)
        slot = i // N_MIXERS
        sub = j % 2
        tag = f"{kind}{j}"
        g[norm_names[sub][1]][i] = dgpost
        dmod[i][3 * sub + 2] = dgt
        h = hs[j]
        if kind == "f":
            u, a, vpre = saved[j]
            da = _mm(dy, p["f_down_w"][i], mode="nt", out_dtype=BF16, name=f"ffn_dda_{tag}")
            g["f_down_w"][i] = _mm(a, dy, mode="tn", tk=TK_TOKENS, out_dtype=BF16, name=f"ffn_dwdown_{tag}")
            du, dw, db = _ffn_gate_bwd(u, vpre, da, p["f_dw_w"][i], name=f"ffn_gate_bwd_{tag}")
            g["f_dw_w"][i], g["f_dw_b"][i] = dw, db
            dh = _mm(du, p["f_up_w"], mode="nt", layer=i, name=f"ffn_ddh_{tag}")
            g["f_up_w"][i] = _mm(h, du, mode="tn", tk=TK_TOKENS, layer=i, out_dtype=BF16,
                                 name=f"ffn_dwup_{tag}")
        elif kind == "a":
            u1, u5, u3 = saved[j]
            g["a_pw2_b"][slot] = sdy
            du5 = _mm(dy, p["a_pw2_w"][slot], mode="nt", name=f"a_ddu5_{tag}")
            g["a_pw2_w"][slot] = _mm(u5, dy, mode="tn", tk=TK_TOKENS, out_dtype=BF16, name=f"a_dw2_{tag}")
            b1 = _row(p["a_pw1_b"][slot])
            du3, dlg, dlb, ddwb = _a_bwd_norm(u3, du5, _row(p["a_ln_g"][slot]), _row(p["a_ln_b"][slot]),
                                              name=f"a_bwd_norm_{tag}")
            g["a_ln_g"][slot], g["a_ln_b"][slot], g["a_dw_b"][slot] = dlg, dlb, ddwb
            du1, ddww, db1 = _a_bwd_conv(u1, du3, b1, p["a_dw_w"][slot], name=f"a_bwd_conv_{tag}")
            g["a_dw_w"][slot], g["a_pw1_b"][slot] = ddww, db1
            dh = _mm(du1, p["a_pw1_w"], mode="nt", layer=slot, name=f"a_ddh_{tag}")
            g["a_pw1_w"][slot] = _mm(h, du1, mode="tn", tk=TK_TOKENS, layer=slot, out_dtype=BF16,
                                     name=f"a_dw1_{tag}")
        elif kind == "b":
            pooled, mixed = saved[j]
            dmixed, dscale, dgb = _b_affine_bwd(dy, mixed, _row(p["b_group_b"][slot]), _row(p["b_scale"][slot]),
                                                name=f"b_aff_bwd_{tag}")
            g["b_scale"][slot], g["b_group_b"][slot] = dscale, dgb
            dpooled = _mm_group(dmixed, p["b_group_w"][slot], mode="nt", name=f"b_dpool_{tag}")
            g["b_group_w"][slot] = _mm_group(pooled, dmixed, mode="tn", tm=TK_TOKENS, out_dtype=BF16,
                                             name=f"b_dw_{tag}")
            dh = _b_pool_bwd(dpooled, name=f"b_pool_bwd_{tag}")
        else:
            bcx, z = saved[j]
            dz = _mm(dy, p["c_out_w"][slot], mode="nt", name=f"c_ddz_{tag}")
            g["c_out_w"][slot] = _mm(z, dy, mode="tn", tk=TK_TOKENS, out_dtype=BF16, name=f"c_dwout_{tag}")
            dbcx, dwc = _c_gate_bwd(bcx, dz, p["c_conv_w"][slot], name=f"c_gate_bwd_{tag}")
            g["c_conv_w"][slot] = dwc
            dh = _mm(dbcx, p["c_in_w"], mode="nt", layer=slot, name=f"c_ddh_{tag}")
            g["c_in_w"][slot] = _mm(h, dbcx, mode="tn", tk=TK_TOKENS, layer=slot, out_dtype=BF16,
                                    name=f"c_dwin_{tag}")
        if j > 0:
            pj = j - 1
            dy_dtype = F32 if _kind_of(pj) == "b" else BF16
            dx, dy, dsh, dsc, dgpre, dgpost, dgt, sdy = _bwd_mid(
                dx, dh, xs[j], gpre[j], sc[j], ys[pj], gpost[pj], gt[pj], dy_dtype=dy_dtype, name=f"bwd_mid_{j}")
        else:
            dx, dsh, dsc, dgpre = _bwd_first(dx, dh, xs[0], gpre[0], sc[0], name="bwd_first")
        dmod[i][3 * sub + 0] = dsh
        dmod[i][3 * sub + 1] = dsc
        g[norm_names[sub][0]][i] = dgpre

    small = {k: jnp.stack(v).reshape(p[k].shape) for k, v in g.items() if k not in BIG}
    big = {k: v for k, v in g.items() if k in BIG}
    dmod_arr = jnp.stack([jnp.concatenate(r, axis=0) for r in dmod])
    return loss, dx, dmod_arr, small, big


_MESH = pl.DeviceIdType.MESH
_ANY = pl.BlockSpec(memory_space=pl.ANY)
_VMEM = pl.BlockSpec(memory_space=pltpu.VMEM)


def _place():
    return lax.axis_index("x"), lax.axis_index("y"), lax.axis_index("c")


def _other_chips(x, y):
    return [(1 - x, y), (x, 1 - y), (1 - x, 1 - y)]


def _remote(src, dst, send_sem, recv_sem, to):
    return pltpu.make_async_remote_copy(src_ref=src, dst_ref=dst, send_sem=send_sem, recv_sem=recv_sem,
                                        device_id=to, device_id_type=_MESH)


def _all_gather8(blk, *, name):
    r, cdim = blk.shape

    def body(x_ref, out_ref, send_sems, recv_sems, local_sem):
        x, y, c = _place()
        me, sibling = (x, y, c), (x, y, 1 - c)
        chips = _other_chips(x, y)

        def slot(px, py, pc):
            return out_ref.at[4 * px + 2 * py + pc]

        def copy(k, block, to, src=None):
            return _remote(slot(*block) if src is None else src, slot(*block),
                           send_sems.at[k], recv_sems.at[k], to)

        mine = pltpu.make_async_copy(x_ref, slot(*me), local_sem)
        mine.start()
        first = [copy(0, me, sibling, src=x_ref)]
        first += [copy(1 + j, me, (*chip, c), src=x_ref) for j, chip in enumerate(chips)]
        for cp in first:
            cp.start()
        passed = [copy(4 + j, (*chip, c), sibling) for j, chip in enumerate(chips)]
        for j, chip in enumerate(chips):
            copy(1 + j, (*chip, c), me).wait_recv()
            passed[j].start()
        copy(0, sibling, me).wait_recv()
        for j, chip in enumerate(chips):
            copy(4 + j, (*chip, 1 - c), me).wait_recv()
        for cp in first + passed:
            cp.wait_send()
        mine.wait()

    return pl.pallas_call(
        body, out_shape=jax.ShapeDtypeStruct((NDEV, r, cdim), blk.dtype),
        in_specs=[_VMEM], out_specs=_VMEM,
        scratch_shapes=[pltpu.SemaphoreType.DMA((7,)), pltpu.SemaphoreType.DMA((7,)), pltpu.SemaphoreType.DMA],
        compiler_params=pltpu.CompilerParams(vmem_limit_bytes=VMEM_LIMIT), name=name)(blk)


def _gather_dst(kind):
    if kind == "col":
        return lambda s, h: (s, h)
    if kind == "row":
        return lambda s, h: (h, slice(None), s)
    return lambda s, h: (slice(None), s, h)


def _cast_into_gathered(src, kind, out_shape, shard, *, name):
    _, a, rh, cdim = src.shape
    tr = _pick(rh, (512, 256, 128, 64, 32, 16))
    if kind == "col":
        out_idx = lambda h, ai, r, s: (s[0], h, ai, r, 0)
    elif kind == "row":
        out_idx = lambda h, ai, r, s: (h, ai, s[0], r, 0)
    else:
        out_idx = lambda h, ai, r, s: (ai, s[0], h, r, 0)

    def body(s_ref, x_ref, o_ref):
        o_ref[...] = x_ref[...].astype(BF16)

    grid_spec = pltpu.PrefetchScalarGridSpec(
        num_scalar_prefetch=1, grid=(2, a, rh // tr),
        in_specs=[pl.BlockSpec((None, None, tr, cdim), lambda h, ai, r, s: (h, ai, r, 0))],
        out_specs=pl.BlockSpec((None, None, None, tr, cdim), out_idx))
    return pl.pallas_call(
        body, grid_spec=grid_spec, out_shape=jax.ShapeDtypeStruct(out_shape, BF16),
        compiler_params=_cparams(_PAR, _PAR, _PAR), name=name)(shard, src)


def _gather_weights(bufs, kinds, *, name):
    nt = len(bufs)

    def body(*refs):
        out_refs = refs[nt:2 * nt]
        send_sems, recv_sems = refs[2 * nt:]
        x, y, c = _place()
        sibling = (x, y, 1 - c)
        nbr_x, nbr_y = (1 - x, y, c), (x, 1 - y, c)
        s_me, s_x, s_y, s_d = 2 * x + y, 2 * (1 - x) + y, 2 * x + (1 - y), 2 * (1 - x) + (1 - y)

        def at(k, s, h):
            return out_refs[k].at[_gather_dst(kinds[k])(s, h)]

        def part(ref, k, q):
            rows = bufs[k].shape[3] // 2
            return ref.at[:, pl.ds(q * rows, rows), :]

        def copy(ref, k, col, to):
            return _remote(ref, ref, send_sems.at[k, col], recv_sems.at[k, col], to)

        started = []

        def start(cp):
            cp.start()
            started.append(cp)

        for k in range(nt):
            start(copy(at(k, s_me, c), k, 0, nbr_x))
            start(copy(at(k, s_me, c), k, 1, nbr_y))
        for k in range(nt):
            got_y, got_x = at(k, s_y, c), at(k, s_x, c)
            copy(got_y, k, 1, nbr_y).wait_recv()
            start(copy(part(got_y, k, 0), k, 2, nbr_x))
            start(copy(got_y, k, 5, sibling))
            copy(got_x, k, 0, nbr_x).wait_recv()
            start(copy(part(got_x, k, 1), k, 3, nbr_y))
            start(copy(got_x, k, 4, sibling))
        for k in range(nt):
            got_d = at(k, s_d, c)
            for q in (0, 1):
                copy(part(got_d, k, q), k, 2 + q, sibling).wait_recv()
                start(copy(part(got_d, k, q), k, 6 + q, sibling))
        for k in range(nt):
            copy(at(k, s_x, 1 - c), k, 4, sibling).wait_recv()
            copy(at(k, s_y, 1 - c), k, 5, sibling).wait_recv()
            for q in (0, 1):
                copy(part(at(k, s_d, 1 - c), k, q), k, 6 + q, sibling).wait_recv()
        for cp in started:
            cp.wait_send()

    return pl.pallas_call(
        body, out_shape=[jax.ShapeDtypeStruct(b.shape, BF16) for b in bufs],
        in_specs=[_ANY] * nt, out_specs=[_ANY] * nt, input_output_aliases={k: k for k in range(nt)},
        scratch_shapes=[pltpu.SemaphoreType.DMA((nt, 8)), pltpu.SemaphoreType.DMA((nt, 8))],
        name=name)(*bufs)


def _pair_exchange(gs, layers_of, *, name):
    n, nt = len(gs), len(layers_of)

    def body(*refs):
        g_refs, out_refs = refs[:n], refs[n:n + nt]
        send_sems, recv_sems = refs[n + nt:]
        x, y, c = _place()
        sibling = (x, y, 1 - c)
        copies = []
        for t, ks in enumerate(layers_of):
            for l, k in enumerate(ks):
                cp = _remote(g_refs[k].at[:, 1 - c], out_refs[t].at[l], send_sems.at[k], recv_sems.at[k], sibling)
                cp.start()
                copies.append(cp)
        for cp in copies:
            cp.wait()

    out_shape = [jax.ShapeDtypeStruct((len(ks), NSHARD) + gs[ks[0]].shape[2:], gs[ks[0]].dtype)
                 for ks in layers_of]
    return pl.pallas_call(
        body, out_shape=out_shape, in_specs=[_ANY] * n, out_specs=[_ANY] * nt,
        scratch_shapes=[pltpu.SemaphoreType.DMA((n,)), pltpu.SemaphoreType.DMA((n,))],
        name=name)(*gs)


def _pair_sum(g, r1, s_acc, layer, half, *, name):
    _, _, rh, cdim = g.shape
    tr = _pick(rh, (256, 128, 176, 64, 32, 16))

    def body(half_ref, g_ref, r_ref, s_in_ref, o_ref):
        o_ref[...] = (g_ref[...].astype(F32) + r_ref[...].astype(F32)).astype(BF16)

    grid_spec = pltpu.PrefetchScalarGridSpec(
        num_scalar_prefetch=1, grid=(NSHARD, rh // tr),
        in_specs=[pl.BlockSpec((None, None, tr, cdim), lambda s, r, hf: (s, hf[0], r, 0)),
                  pl.BlockSpec((None, None, tr, cdim), lambda s, r, hf: (layer, s, r, 0)),
                  _ANY],
        out_specs=pl.BlockSpec((None, None, tr, cdim), lambda s, r, hf: (layer, s, r, 0)))
    return pl.pallas_call(
        body, grid_spec=grid_spec, out_shape=jax.ShapeDtypeStruct(s_acc.shape, BF16),
        input_output_aliases={3: 0},
        compiler_params=_cparams(_PAR, _PAR), name=name)(half, g, r1, s_acc)


def _chip_exchange(ss, *, name):
    nt = len(ss)

    def body(*refs):
        s_refs, out_refs, stage_refs = refs[:nt], refs[nt:2 * nt], refs[2 * nt:3 * nt]
        send_sems, recv_sems = refs[3 * nt:]
        x, y, c = _place()
        nbr_x, nbr_y = (1 - x, y, c), (x, 1 - y, c)
        s_x, s_y, s_d = 2 * (1 - x) + y, 2 * x + (1 - y), 2 * (1 - x) + (1 - y)

        def part(ref, t, q):
            rows = ss[t].shape[2] // 2
            return ref.at[:, pl.ds(q * rows, rows), :]

        def copy(src, dst, t, col, to):
            return _remote(src, dst, send_sems.at[t, col], recv_sems.at[t, col], to)

        started = []

        def start(cp):
            cp.start()
            started.append(cp)

        for t in range(nt):
            start(copy(s_refs[t].at[:, s_x], out_refs[t].at[0], t, 0, nbr_x))
            start(copy(s_refs[t].at[:, s_y], out_refs[t].at[1], t, 1, nbr_y))
            start(copy(part(s_refs[t].at[:, s_d], t, 0), stage_refs[t].at[0], t, 2, nbr_x))
            start(copy(part(s_refs[t].at[:, s_d], t, 1), stage_refs[t].at[1], t, 3, nbr_y))
        for t in range(nt):
            st0, st1 = stage_refs[t].at[0], stage_refs[t].at[1]
            copy(st0, st0, t, 2, nbr_x).wait_recv()
            start(copy(st0, part(out_refs[t].at[2], t, 0), t, 4, nbr_y))
            copy(st1, st1, t, 3, nbr_y).wait_recv()
            start(copy(st1, part(out_refs[t].at[2], t, 1), t, 5, nbr_x))
        for t in range(nt):
            copy(out_refs[t].at[0], out_refs[t].at[0], t, 0, nbr_x).wait_recv()
            copy(out_refs[t].at[1], out_refs[t].at[1], t, 1, nbr_y).wait_recv()
            for q in (0, 1):
                got = part(out_refs[t].at[2], t, q)
                copy(got, got, t, 4 + q, nbr_x).wait_recv()
        for cp in started:
            cp.wait_send()

    out_shape = [jax.ShapeDtypeStruct((3, s.shape[0]) + s.shape[2:], BF16) for s in ss]
    out_shape += [jax.ShapeDtypeStruct((2, s.shape[0], s.shape[2] // 2, s.shape[3]), BF16) for s in ss]
    return pl.pallas_call(
        body, out_shape=out_shape, in_specs=[_ANY] * nt, out_specs=[_ANY] * (2 * nt),
        scratch_shapes=[pltpu.SemaphoreType.DMA((nt, 6)), pltpu.SemaphoreType.DMA((nt, 6))],
        name=name)(*ss)[:nt]


def _chip_sum(s_t, r3_t, place, *, name):
    nl, _, rh, cdim = s_t.shape
    tr = _pick(rh, (256, 128, 176, 64, 32, 16))

    def body(pz, s_ref, r_ref, o_ref):
        acc = s_ref[...].astype(F32) + r_ref[0].astype(F32)
        o_ref[...] = (acc + r_ref[1].astype(F32)) + r_ref[2].astype(F32)

    grid_spec = pltpu.PrefetchScalarGridSpec(
        num_scalar_prefetch=1, grid=(nl, rh // tr),
        in_specs=[pl.BlockSpec((None, None, tr, cdim), lambda l, r, pz: (l, pz[0], r, 0)),
                  pl.BlockSpec((3, None, tr, cdim), lambda l, r, pz: (0, l, r, 0))],
        out_specs=pl.BlockSpec((None, None, tr, cdim), lambda l, r, pz: (l, pz[1], r, 0)))
    return pl.pallas_call(
        body, grid_spec=grid_spec, out_shape=jax.ShapeDtypeStruct((nl, 2, rh, cdim), F32),
        compiler_params=_cparams(_PAR, _PAR), name=name)(place, s_t, r3_t)


def _join_halves(reds, *, name):
    nt = len(reds)

    def body(*refs):
        out_refs = refs[nt:2 * nt]
        send_sems, recv_sems = refs[2 * nt:]
        x, y, c = _place()
        sibling = (x, y, 1 - c)
        copies = []
        for t in range(nt):
            cp = _remote(out_refs[t].at[:, c], out_refs[t].at[:, c], send_sems.at[t], recv_sems.at[t], sibling)
            cp.start()
            copies.append(cp)
        for t, cp in enumerate(copies):
            cp.wait_send()
            got = out_refs[t].at[:, 1 - c]
            _remote(got, got, send_sems.at[t], recv_sems.at[t], sibling).wait_recv()

    return pl.pallas_call(
        body, out_shape=[jax.ShapeDtypeStruct(r.shape, F32) for r in reds],
        in_specs=[_ANY] * nt, out_specs=[_ANY] * nt, input_output_aliases={t: t for t in range(nt)},
        scratch_shapes=[pltpu.SemaphoreType.DMA((nt,)), pltpu.SemaphoreType.DMA((nt,))],
        name=name)(*reds)


def _sum_devices(g, *, name):
    _, r, cdim = g.shape
    tr = _pick(r, (512, 256, 128, 64, 32, 16, 8))

    def body(g_ref, o_ref):
        acc = g_ref[0]
        for e in range(1, NDEV):
            acc = acc + g_ref[e]
        o_ref[...] = acc

    return pl.pallas_call(
        body, grid=(r // tr,), in_specs=[pl.BlockSpec((NDEV, tr, cdim), lambda i: (0, i, 0))],
        out_specs=pl.BlockSpec((tr, cdim), lambda i: (i, 0)),
        out_shape=jax.ShapeDtypeStruct((r, cdim), F32),
        compiler_params=_cparams(_PAR), name=name)(g)


def _mod_fwd(c_all, mod_w, mod_b_cols, *, name):
    nl, d, n = mod_w.shape
    ne = c_all.shape[0]
    tn = _pick(n, (768, 512, 384, 256, 128))

    def body(c_ref, w_ref, b_ref, o_ref):
        cv = c_ref[...]
        act = (cv * _sigmoid(cv)).astype(BF16)
        o_ref[...] = jnp.dot(act, w_ref[...].astype(BF16), preferred_element_type=F32) + b_ref[...]

    return pl.pallas_call(
        body, grid=(nl, n // tn),
        in_specs=[pl.BlockSpec((ne, d), lambda i, j: (0, 0)),
                  pl.BlockSpec((None, d, tn), lambda i, j: (i, 0, j)),
                  pl.BlockSpec((None, 1, tn), lambda i, j: (i, 0, j))],
        out_specs=pl.BlockSpec((None, ne, tn), lambda i, j: (i, 0, j)),
        out_shape=jax.ShapeDtypeStruct((nl, ne, n), F32),
        compiler_params=_cparams(_PAR, _PAR), name=name)(c_all, mod_w, mod_b_cols)


def _adam_math(w, g, m, v):
    m2 = ADAM_B1 * m + (1.0 - ADAM_B1) * g
    v2 = ADAM_B2 * v + (1.0 - ADAM_B2) * (g * g)
    m_hat = m2 / (1.0 - ADAM_B1 ** ADAM_STEP)
    v_hat = v2 / (1.0 - ADAM_B2 ** ADAM_STEP)
    delta = -ADAM_LR * (m_hat / (jnp.sqrt(v_hat) + ADAM_EPS) + ADAM_WD * w)
    return delta, m2, v2


def _adamw(w, g, m, v, *, name):
    rows, cdim = w.shape
    tr = _pick(rows, tuple(t for t in (512, 256, 128, 64, 32, 16, 8) if t * cdim <= 256 * 1024))

    def body(w_ref, g_ref, m_ref, v_ref, d_ref, mo_ref, vo_ref):
        d_ref[...], mo_ref[...], vo_ref[...] = _adam_math(w_ref[...], g_ref[...], m_ref[...], v_ref[...])

    spec = pl.BlockSpec((tr, cdim), lambda i: (i, 0))
    return pl.pallas_call(
        body, grid=(rows // tr,), in_specs=[spec] * 4, out_specs=[spec] * 3,
        out_shape=[jax.ShapeDtypeStruct((rows, cdim), F32)] * 3,
        compiler_params=_cparams(_PAR), name=name)(w, g, m, v)


def _mod_w_update(c_t, dmod, w, m, v, *, name):
    nl, d, n = w.shape
    ne = c_t.shape[1]
    tr = _pick(d, (128, 64, 32, 16, 8))

    def body(c_ref, dm_ref, w_ref, m_ref, v_ref, g_ref, d_ref, mo_ref, vo_ref):
        cv = c_ref[...]
        act = cv * _sigmoid(cv)
        dm = dm_ref[...]
        g = act[:, 0:1] * dm[0:1, :]
        for e in range(1, ne):
            g = g + act[:, e:e + 1] * dm[e:e + 1, :]
        g_ref[...] = g
        d_ref[...], mo_ref[...], vo_ref[...] = _adam_math(w_ref[...], g, m_ref[...], v_ref[...])

    big = pl.BlockSpec((None, tr, n), lambda i, r: (i, r, 0))
    return pl.pallas_call(
        body, grid=(nl, d // tr),
        in_specs=[pl.BlockSpec((tr, ne), lambda i, r: (r, 0)),
                  pl.BlockSpec((None, ne, n), lambda i, r: (i, 0, 0)), big, big, big],
        out_specs=[big] * 4, out_shape=[jax.ShapeDtypeStruct((nl, d, n), F32)] * 4,
        compiler_params=_cparams(_PAR, _PAR), name=name)(c_t, dmod, w, m, v)


PACK_ROWS = 256


def _pack(arrs):
    flat = jnp.concatenate([a.reshape(-1) for a in arrs])
    tile = PACK_ROWS * LANES
    pad = (-flat.shape[0]) % tile
    return jnp.pad(flat, (0, pad)).reshape(-1, LANES)


def _unpack(packed, shapes, lead=()):
    flat = packed.reshape(lead + (-1,))
    out, off = [], 0
    for shp in shapes:
        size = 1
        for s in shp:
            size *= s
        out.append(flat[..., off:off + size].reshape(lead + tuple(shp)))
        off += size
    return out


SMALL_SHARD_AXIS = {"a_pw1_b": 1, "a_dw_w": 2, "a_dw_b": 1, "a_ln_g": 1, "a_ln_b": 1, "a_pw2_b": 1,
                    "c_conv_w": 2, "f_dw_w": 2}
SMALL_REPLICATED = ("norm_pre_mix", "norm_post_mix", "norm_pre_ffn", "norm_post_ffn",
                    "b_group_b", "b_scale", "f_dw_b")
WEIGHT_ORDER = ("mod_w", "mod_b", "norm_pre_mix", "norm_post_mix", "norm_pre_ffn", "norm_post_ffn",
                "a_pw1_w", "a_pw1_b", "a_dw_w", "a_dw_b", "a_ln_g", "a_ln_b", "a_pw2_w", "a_pw2_b",
                "b_group_w", "b_group_b", "b_scale", "c_in_w", "c_conv_w", "c_out_w",
                "f_up_w", "f_dw_w", "f_dw_b", "f_down_w")


def _as_layers_rows_cols(name, w):
    if name == "b_group_w":
        return w.reshape(w.shape[1], w.shape[2], w.shape[3])
    return w


def _step(x, c, loss_target, w, m, v):
    xi, yi, ci = _place()
    shard = 2 * xi + yi
    example = 4 * xi + 2 * yi + ci
    d = x.shape[-1]

    small_names = tuple(SMALL_SHARD_AXIS)
    gathered0 = _all_gather8(_pack([c] + [w[k] for k in small_names]), name="gather_small")
    parts = _unpack(gathered0, [c.shape] + [w[k].shape for k in small_names], lead=(NDEV,))
    c_all = parts[0].reshape(NDEV, d)
    p = {}
    for k, part in zip(small_names, parts[1:]):
        p[k] = jnp.concatenate([part[2 * s] for s in range(NSHARD)], axis=SMALL_SHARD_AXIS[k])
    for k in SMALL_REPLICATED:
        p[k] = w[k]

    ncol = w["mod_w"].shape[2]
    mod_b_cols = lax.dynamic_slice_in_dim(w["mod_b"], shard * ncol, ncol, axis=1).reshape(DEPTH, 1, ncol)
    mod_part = _mod_fwd(c_all, w["mod_w"], mod_b_cols, name="mod_fwd")
    gathered1 = _all_gather8(mod_part.reshape(DEPTH * NDEV, ncol), name="gather_mod")
    mod_all = gathered1.reshape(NSHARD, 2, DEPTH, NDEV, ncol)[:, 0]
    mod_mine = lax.dynamic_index_in_dim(mod_all, example, axis=2, keepdims=False)
    mod = jnp.transpose(mod_mine, (1, 0, 2)).reshape(DEPTH, 6, d)

    shard_arr = shard.reshape(1).astype(jnp.int32)
    bufs, kinds = [], []
    for k in BIG:
        wk = _as_layers_rows_cols(k, w[k])
        nl, r, cdim = wk.shape
        if nl >= 2:
            a, rh = nl // 2, r
        else:
            a, rh = 1, r // 2
        if k in COL_SHARDED:
            kind, out_shape = "col", (NSHARD, 2, a, rh, cdim)
        elif nl >= 2:
            kind, out_shape = "row", (2, a, NSHARD, rh, cdim)
        else:
            kind, out_shape = "row1", (1, NSHARD, 2, rh, cdim)
        kinds.append(kind)
        bufs.append(_cast_into_gathered(wk.reshape(2, a, rh, cdim), kind, out_shape, shard_arr, name=f"cast_{k}"))
    full = _gather_weights(bufs, kinds, name="gather_weights")
    for k, f in zip(BIG, full):
        nl, r, cdim = _as_layers_rows_cols(k, w[k]).shape
        if k in COL_SHARDED:
            p[k] = f.reshape(NSHARD, nl, r, cdim)
        elif k == "b_group_w":
            p[k] = f.reshape(1, nl, NSHARD * r, cdim)
        else:
            p[k] = f.reshape(nl, NSHARD * r, cdim)

    loss, grad_x, dmod, small, big = _local_step(x[0], loss_target[0], mod, p)

    gs, layers_of = [], []
    for k in BIG:
        ks = []
        for g in big[k]:
            if k == "b_group_w":
                ng, rr, cc = g.shape
                g = jnp.transpose(g.reshape(ng, NSHARD, rr // NSHARD, cc), (1, 0, 2, 3)).reshape(NSHARD, -1, cc)
            elif k not in COL_SHARDED:
                g = g.reshape(NSHARD, g.shape[0] // NSHARD, g.shape[1])
            ks.append(len(gs))
            gs.append(g.reshape(NSHARD, 2, g.shape[1] // 2, g.shape[2]))
        layers_of.append(ks)
    half = ci.reshape(1).astype(jnp.int32)
    place = jnp.stack([shard, ci]).astype(jnp.int32)
    r1 = _pair_exchange(gs, layers_of, name="grad_pair_exchange")
    ss = []
    for k, ks, r1_t in zip(BIG, layers_of, r1):
        s_t = lax.empty(r1_t.shape, BF16)
        for l, i in enumerate(ks):
            s_t = _pair_sum(gs[i], r1_t, s_t, l, half, name=f"grad_pair_sum_{k}_{l}")
        ss.append(s_t)
    r3 = _chip_exchange(ss, name="grad_chip_exchange")
    reds = [_chip_sum(s_t, r3_t, place, name=f"grad_chip_sum_{k}") for k, s_t, r3_t in zip(BIG, ss, r3)]
    joined = _join_halves(reds, name="grad_join_halves")
    grads = {k: j.reshape(w[k].shape) for k, j in zip(BIG, joined)}

    rep_names = SMALL_REPLICATED
    small_list = [small[k] for k in rep_names] + [small[k] for k in small_names] + [dmod]
    gathered2 = _all_gather8(_pack(small_list), name="gather_small_grads")
    summed = _sum_devices(gathered2, name="sum_small_grads")
    shapes = [s.shape for s in small_list]
    sums = _unpack(summed, shapes)
    for k, s in zip(rep_names, sums[:len(rep_names)]):
        grads[k] = s
    for k, s in zip(small_names, sums[len(rep_names):-1]):
        ax = SMALL_SHARD_AXIS[k]
        grads[k] = lax.dynamic_slice_in_dim(s, shard * w[k].shape[ax], w[k].shape[ax], axis=ax)
    grads["mod_b"] = sums[-1].reshape(w["mod_b"].shape)
    dmod_all = _unpack(gathered2, shapes, lead=(NDEV,))[-1].reshape(NDEV, DEPTH, NSHARD, ncol)
    dmod_cols = jnp.transpose(lax.dynamic_index_in_dim(dmod_all, shard, axis=2, keepdims=False), (1, 0, 2))

    delta, new_m, new_v = {}, {}, {}
    grads["mod_w"], delta["mod_w"], new_m["mod_w"], new_v["mod_w"] = _mod_w_update(
        c_all.T, dmod_cols, w["mod_w"], m["mod_w"], v["mod_w"], name="mod_w_update")
    for k in BIG:
        cdim = w[k].shape[-1]
        outs = _adamw(*[t.reshape(-1, cdim) for t in (w[k], grads[k], m[k], v[k])], name=f"adamw_{k}")
        delta[k], new_m[k], new_v[k] = [o.reshape(w[k].shape) for o in outs]
    rest = ("mod_b",) + rep_names + small_names
    packs = [_pack([t[k] for k in rest]) for t in (w, grads, m, v)]
    outs = _adamw(*packs, name="adamw_small")
    rest_shapes = [w[k].shape for k in rest]
    for dst, o in zip((delta, new_m, new_v), outs):
        for k, t in zip(rest, _unpack(o, rest_shapes)):
            dst[k] = t

    loss_all = lax.psum(loss[0, 0], ("x", "y", "c"))
    return (loss_all, grad_x[None], *[grads[k] for k in WEIGHT_ORDER], *[delta[k] for k in WEIGHT_ORDER],
            *[new_m[k] for k in WEIGHT_ORDER], *[new_v[k] for k in WEIGHT_ORDER])


def kernel(x, c, mod_w, mod_b, norm_pre_mix, norm_post_mix, norm_pre_ffn, norm_post_ffn, a_pw1_w, a_pw1_b, a_dw_w, a_dw_b, a_ln_g, a_ln_b, a_pw2_w, a_pw2_b, b_group_w, b_group_b, b_scale, c_in_w, c_conv_w, c_out_w, f_up_w, f_dw_w, f_dw_b, f_down_w, loss_target, m_mod_w, m_mod_b, m_norm_pre_mix, m_norm_post_mix, m_norm_pre_ffn, m_norm_post_ffn, m_a_pw1_w, m_a_pw1_b, m_a_dw_w, m_a_dw_b, m_a_ln_g, m_a_ln_b, m_a_pw2_w, m_a_pw2_b, m_b_group_w, m_b_group_b, m_b_scale, m_c_in_w, m_c_conv_w, m_c_out_w, m_f_up_w, m_f_dw_w, m_f_dw_b, m_f_down_w, v_mod_w, v_mod_b, v_norm_pre_mix, v_norm_post_mix, v_norm_pre_ffn, v_norm_post_ffn, v_a_pw1_w, v_a_pw1_b, v_a_dw_w, v_a_dw_b, v_a_ln_g, v_a_ln_b, v_a_pw2_w, v_a_pw2_b, v_b_group_w, v_b_group_b, v_b_scale, v_c_in_w, v_c_conv_w, v_c_out_w, v_f_up_w, v_f_dw_w, v_f_dw_b, v_f_down_w):
    given = dict(locals())
    w = {k: given[k] for k in WEIGHT_ORDER}
    m = {k: given["m_" + k] for k in WEIGHT_ORDER}
    v = {k: given["v_" + k] for k in WEIGHT_ORDER}
    return _step(x, c, loss_target, w, m, v)
```
